```python
import jax, jax.numpy as jnp
from jax import lax
import numpy as np

D_MODEL = 1024
BATCH = 8
SEQ = 8192
DEPTH = 4

HEAD_DIM = 64
N_Q_HEADS = D_MODEL // (2 * HEAD_DIM)
N_KV_HEADS = max(1, N_Q_HEADS // 4)
GQA_GROUP = N_Q_HEADS // N_KV_HEADS
WINDOW = 128
ATTN_BLOCK = 128
GM_HEADS = N_Q_HEADS
GM_HEAD_DIM = HEAD_DIM
CHUNK = 128
ATTN_W = N_Q_HEADS * HEAD_DIM
KV_W = N_KV_HEADS * HEAD_DIM
GM_W = GM_HEADS * GM_HEAD_DIM
D_MIX = ATTN_W + GM_W
D_IN = ATTN_W + 2 * KV_W + 2 * GM_W
D_FF = ((8 * D_MODEL + 3 * 256 - 1) // (3 * 256)) * 256
PLE_DIM = 256
NORM_EPS = 1e-6
NEG_BIG = -1e30

kernel_name = "hymba_swa_gmlp_hybrid"


def rmsnorm(x, g):
    xf = x.astype(jnp.float32)
    y = xf * lax.rsqrt(jnp.mean(xf * xf, axis=-1, keepdims=True) + NORM_EPS)
    return (y * g.astype(jnp.float32)).astype(x.dtype)


def layernorm(x, g, b):
    xf = x.astype(jnp.float32)
    mu = jnp.mean(xf, axis=-1, keepdims=True)
    xc = xf - mu
    y = xc * lax.rsqrt(jnp.mean(xc * xc, axis=-1, keepdims=True) + NORM_EPS)
    return (y * g.astype(jnp.float32) + b.astype(jnp.float32)).astype(x.dtype)


def alibi_slopes(n_heads):
    return jnp.exp2(-8.0 * (jnp.arange(n_heads, dtype=jnp.float32) + 1.0) / n_heads)


def sliding_window_attention(q, k, v, sinks):
    B, S = q.shape[0], q.shape[1]
    nb = S // ATTN_BLOCK
    qb = q.reshape(B, nb, ATTN_BLOCK, N_KV_HEADS, GQA_GROUP, HEAD_DIM)
    kb = k.reshape(B, nb, ATTN_BLOCK, N_KV_HEADS, HEAD_DIM)
    vb = v.reshape(B, nb, ATTN_BLOCK, N_KV_HEADS, HEAD_DIM)
    pad = ((0, 0), (1, 0), (0, 0), (0, 0), (0, 0))
    kk = jnp.concatenate([jnp.pad(kb[:, :-1], pad), kb], axis=2)
    vv = jnp.concatenate([jnp.pad(vb[:, :-1], pad), vb], axis=2)
    scale = HEAD_DIM ** -0.5
    s = jnp.einsum('bnqkgd,bnskd->bnkgqs', qb, kk,
                   preferred_element_type=jnp.float32) * scale
    qi = jnp.arange(ATTN_BLOCK)[:, None]
    kj = jnp.arange(2 * ATTN_BLOCK)[None, :]
    dist = qi + ATTN_BLOCK - kj
    band = (dist >= 0) & (dist < WINDOW)
    blk = jnp.arange(nb)[:, None, None]
    valid = band[None] & ((blk > 0) | (kj >= ATTN_BLOCK)[None])
    slopes = alibi_slopes(N_Q_HEADS).reshape(N_KV_HEADS, GQA_GROUP)
    bias = -slopes[:, :, None, None] * dist.astype(jnp.float32)
    s = jnp.where(valid[None, :, None, None], s + bias[None, None], NEG_BIG)
    sink = sinks.astype(jnp.float32).reshape(N_KV_HEADS, GQA_GROUP)[None, None, :, :, None, None]
    m = jnp.maximum(jnp.max(s, axis=-1, keepdims=True), sink)
    e = jnp.exp(s - m)
    pr = e / (jnp.sum(e, axis=-1, keepdims=True) + jnp.exp(sink - m))
    o = jnp.einsum('bnkgqs,bnskd->bnqkgd', pr.astype(vv.dtype), vv)
    return o.reshape(B, S, ATTN_W)


def chunked_spatial_gating(zu, zv, ln_g, ln_b, ws, bs):
    B, S = zu.shape[0], zu.shape[1]
    nc = S // CHUNK
    zv = layernorm(zv, ln_g, ln_b)
    vh = zv.reshape(B, nc, CHUNK, GM_HEADS, GM_HEAD_DIM)
    causal = jnp.tril(jnp.ones((CHUNK, CHUNK), dtype=bool))
    w = jnp.where(causal[None], ws, jnp.zeros_like(ws))
    mixed = jnp.einsum('hts,bnshc->bnthc', w, vh) + bs.T[None, None, :, :, None]
    return zu * mixed.reshape(B, S, GM_W)


def _fwd_setup_inputs(seed: int = 0) -> dict:
    key = jax.random.key(seed)
    ks = jax.random.split(key, 24)
    f32 = jnp.float32
    nrm = lambda k, shape, s: jax.random.normal(k, shape, f32) * s
    gain = lambda k, n: 1.0 + 0.05 * jax.random.normal(k, (DEPTH, n), f32)
    return {
        "x": nrm(ks[0], (BATCH, SEQ, D_MODEL), 1.0),
        "p": nrm(ks[1], (DEPTH, BATCH, SEQ, PLE_DIM), 1.0),
        "ln_mix_pre": gain(ks[2], D_MODEL),
        "w_in": nrm(ks[3], (DEPTH, D_MODEL, D_IN), D_MODEL ** -0.5),
        "attn_sinks": nrm(ks[4], (DEPTH, N_Q_HEADS), 1.0),
        "gm_ln_g": gain(ks[5], GM_W),
        "gm_ln_b": nrm(ks[6], (DEPTH, GM_W), 0.01),
        "gm_ws": nrm(ks[7], (DEPTH, GM_HEADS, CHUNK, CHUNK), CHUNK ** -0.5),
        "gm_bs": 1.0 + nrm(ks[8], (DEPTH, GM_HEADS, CHUNK), 0.01),
        "g_attn_out": gain(ks[9], ATTN_W),
        "g_gm_out": gain(ks[10], GM_W),
        "w_out": nrm(ks[11], (DEPTH, D_MIX, D_MODEL), D_MIX ** -0.5),
        "ln_mix_post": gain(ks[12], D_MODEL),
        "ln_ffn_pre": gain(ks[13], D_MODEL),
        "w_ffn_gate": nrm(ks[14], (DEPTH, D_MODEL, D_FF), D_MODEL ** -0.5),
        "w_ffn_up": nrm(ks[15], (DEPTH, D_MODEL, D_FF), D_MODEL ** -0.5),
        "w_ffn_down": nrm(ks[16], (DEPTH, D_FF, D_MODEL), D_FF ** -0.5),
        "ln_ffn_post": gain(ks[17], D_MODEL),
        "w_ple": nrm(ks[18], (DEPTH, PLE_DIM, D_MODEL), PLE_DIM ** -0.5),
        "ln_ple_gate": gain(ks[19], D_MODEL),
        "w_ple_gate": nrm(ks[20], (DEPTH, D_MODEL, D_MODEL), D_MODEL ** -0.5),
    }


def _fwd_reference(x, p, ln_mix_pre, w_in, attn_sinks, gm_ln_g, gm_ln_b, gm_ws, gm_bs,
              g_attn_out, g_gm_out, w_out, ln_mix_post, ln_ffn_pre, w_ffn_gate,
              w_ffn_up, w_ffn_down, ln_ffn_post, w_ple, ln_ple_gate, w_ple_gate):
    h = x
    splits = [ATTN_W, ATTN_W + KV_W, ATTN_W + 2 * KV_W, ATTN_W + 2 * KV_W + GM_W]
    for i in range(DEPTH):
        a = rmsnorm(h, ln_mix_pre[i])
        z = a @ w_in[i]
        q, k, v, zu, zv = jnp.split(z, splits, axis=-1)
        attn = sliding_window_attention(q, k, v, attn_sinks[i])
        gm = chunked_spatial_gating(jax.nn.gelu(zu), jax.nn.gelu(zv),
                                    gm_ln_g[i], gm_ln_b[i], gm_ws[i], gm_bs[i])
        heads = jnp.concatenate([rmsnorm(attn, g_attn_out[i]),
                                 rmsnorm(gm, g_gm_out[i])], axis=-1)
        h = h + rmsnorm(heads @ w_out[i], ln_mix_post[i])
        f = rmsnorm(h, ln_ffn_pre[i])
        f = (jax.nn.silu(f @ w_ffn_gate[i]) * (f @ w_ffn_up[i])) @ w_ffn_down[i]
        h = h + rmsnorm(f, ln_ffn_post[i])
        gate = jax.nn.sigmoid(rmsnorm(h, ln_ple_gate[i]) @ w_ple_gate[i])
        h = h + (p[i] @ w_ple[i]) * gate
    return h


import jax as _jax
import jax.numpy as _jnp

TWIN_FORMAT = 'train_step'
FWD_PARAMS = ['x', 'p', 'ln_mix_pre', 'w_in', 'attn_sinks', 'gm_ln_g', 'gm_ln_b', 'gm_ws', 'gm_bs', 'g_attn_out', 'g_gm_out', 'w_out', 'ln_mix_post', 'ln_ffn_pre', 'w_ffn_gate', 'w_ffn_up', 'w_ffn_down', 'ln_ffn_post', 'w_ple', 'ln_ple_gate', 'w_ple_gate']
TWIN_WEIGHTS = ['ln_mix_pre', 'w_in', 'attn_sinks', 'gm_ln_g', 'gm_ln_b', 'gm_ws', 'gm_bs', 'g_attn_out', 'g_gm_out', 'w_out', 'ln_mix_post', 'ln_ffn_pre', 'w_ffn_gate', 'w_ffn_up', 'w_ffn_down', 'ln_ffn_post', 'w_ple', 'ln_ple_gate', 'w_ple_gate']
TWIN_DIFF_INPUT = 'x'
TWIN_INPUTS = ['x', 'p', 'ln_mix_pre', 'w_in', 'attn_sinks', 'gm_ln_g', 'gm_ln_b', 'gm_ws', 'gm_bs', 'g_attn_out', 'g_gm_out', 'w_out', 'ln_mix_post', 'ln_ffn_pre', 'w_ffn_gate', 'w_ffn_up', 'w_ffn_down', 'ln_ffn_post', 'w_ple', 'ln_ple_gate', 'w_ple_gate', 'loss_target', 'm_ln_mix_pre', 'm_w_in', 'm_attn_sinks', 'm_gm_ln_g', 'm_gm_ln_b', 'm_gm_ws', 'm_gm_bs', 'm_g_attn_out', 'm_g_gm_out', 'm_w_out', 'm_ln_mix_post', 'm_ln_ffn_pre', 'm_w_ffn_gate', 'm_w_ffn_up', 'm_w_ffn_down', 'm_ln_ffn_post', 'm_w_ple', 'm_ln_ple_gate', 'm_w_ple_gate', 'v_ln_mix_pre', 'v_w_in', 'v_attn_sinks', 'v_gm_ln_g', 'v_gm_ln_b', 'v_gm_ws', 'v_gm_bs', 'v_g_attn_out', 'v_g_gm_out', 'v_w_out', 'v_ln_mix_post', 'v_ln_ffn_pre', 'v_w_ffn_gate', 'v_w_ffn_up', 'v_w_ffn_down', 'v_ln_ffn_post', 'v_w_ple', 'v_ln_ple_gate', 'v_w_ple_gate']
TWIN_OUTPUTS = ['loss', 'grad_x', 'grad_ln_mix_pre', 'grad_w_in', 'grad_attn_sinks', 'grad_gm_ln_g', 'grad_gm_ln_b', 'grad_gm_ws', 'grad_gm_bs', 'grad_g_attn_out', 'grad_g_gm_out', 'grad_w_out', 'grad_ln_mix_post', 'grad_ln_ffn_pre', 'grad_w_ffn_gate', 'grad_w_ffn_up', 'grad_w_ffn_down', 'grad_ln_ffn_post', 'grad_w_ple', 'grad_ln_ple_gate', 'grad_w_ple_gate', 'delta_ln_mix_pre', 'delta_w_in', 'delta_attn_sinks', 'delta_gm_ln_g', 'delta_gm_ln_b', 'delta_gm_ws', 'delta_gm_bs', 'delta_g_attn_out', 'delta_g_gm_out', 'delta_w_out', 'delta_ln_mix_post', 'delta_ln_ffn_pre', 'delta_w_ffn_gate', 'delta_w_ffn_up', 'delta_w_ffn_down', 'delta_ln_ffn_post', 'delta_w_ple', 'delta_ln_ple_gate', 'delta_w_ple_gate', 'new_m_ln_mix_pre', 'new_m_w_in', 'new_m_attn_sinks', 'new_m_gm_ln_g', 'new_m_gm_ln_b', 'new_m_gm_ws', 'new_m_gm_bs', 'new_m_g_attn_out', 'new_m_g_gm_out', 'new_m_w_out', 'new_m_ln_mix_post', 'new_m_ln_ffn_pre', 'new_m_w_ffn_gate', 'new_m_w_ffn_up', 'new_m_w_ffn_down', 'new_m_ln_ffn_post', 'new_m_w_ple', 'new_m_ln_ple_gate', 'new_m_w_ple_gate', 'new_v_ln_mix_pre', 'new_v_w_in', 'new_v_attn_sinks', 'new_v_gm_ln_g', 'new_v_gm_ln_b', 'new_v_gm_ws', 'new_v_gm_bs', 'new_v_g_attn_out', 'new_v_g_gm_out', 'new_v_w_out', 'new_v_ln_mix_post', 'new_v_ln_ffn_pre', 'new_v_w_ffn_gate', 'new_v_w_ffn_up', 'new_v_w_ffn_down', 'new_v_ln_ffn_post', 'new_v_w_ple', 'new_v_ln_ple_gate', 'new_v_w_ple_gate']
TWIN_LEAF_KINDS = {'loss': 'loss', 'grad_x': 'grad_x', 'grad_ln_mix_pre': 'grad_w', 'grad_w_in': 'grad_w', 'grad_attn_sinks': 'grad_w', 'grad_gm_ln_g': 'grad_w', 'grad_gm_ln_b': 'grad_w', 'grad_gm_ws': 'grad_w', 'grad_gm_bs': 'grad_w', 'grad_g_attn_out': 'grad_w', 'grad_g_gm_out': 'grad_w', 'grad_w_out': 'grad_w', 'grad_ln_mix_post': 'grad_w', 'grad_ln_ffn_pre': 'grad_w', 'grad_w_ffn_gate': 'grad_w', 'grad_w_ffn_up': 'grad_w', 'grad_w_ffn_down': 'grad_w', 'grad_ln_ffn_post': 'grad_w', 'grad_w_ple': 'grad_w', 'grad_ln_ple_gate': 'grad_w', 'grad_w_ple_gate': 'grad_w', 'delta_ln_mix_pre': 'delta_w', 'delta_w_in': 'delta_w', 'delta_attn_sinks': 'delta_w', 'delta_gm_ln_g': 'delta_w', 'delta_gm_ln_b': 'delta_w', 'delta_gm_ws': 'delta_w', 'delta_gm_bs': 'delta_w', 'delta_g_attn_out': 'delta_w', 'delta_g_gm_out': 'delta_w', 'delta_w_out': 'delta_w', 'delta_ln_mix_post': 'delta_w', 'delta_ln_ffn_pre': 'delta_w', 'delta_w_ffn_gate': 'delta_w', 'delta_w_ffn_up': 'delta_w', 'delta_w_ffn_down': 'delta_w', 'delta_ln_ffn_post': 'delta_w', 'delta_w_ple': 'delta_w', 'delta_ln_ple_gate': 'delta_w', 'delta_w_ple_gate': 'delta_w', 'new_m_ln_mix_pre': 'new_m', 'new_m_w_in': 'new_m', 'new_m_attn_sinks': 'new_m', 'new_m_gm_ln_g': 'new_m', 'new_m_gm_ln_b': 'new_m', 'new_m_gm_ws': 'new_m', 'new_m_gm_bs': 'new_m', 'new_m_g_attn_out': 'new_m', 'new_m_g_gm_out': 'new_m', 'new_m_w_out': 'new_m', 'new_m_ln_mix_post': 'new_m', 'new_m_ln_ffn_pre': 'new_m', 'new_m_w_ffn_gate': 'new_m', 'new_m_w_ffn_up': 'new_m', 'new_m_w_ffn_down': 'new_m', 'new_m_ln_ffn_post': 'new_m', 'new_m_w_ple': 'new_m', 'new_m_ln_ple_gate': 'new_m', 'new_m_w_ple_gate': 'new_m', 'new_v_ln_mix_pre': 'new_v', 'new_v_w_in': 'new_v', 'new_v_attn_sinks': 'new_v', 'new_v_gm_ln_g': 'new_v', 'new_v_gm_ln_b': 'new_v', 'new_v_gm_ws': 'new_v', 'new_v_gm_bs': 'new_v', 'new_v_g_attn_out': 'new_v', 'new_v_g_gm_out': 'new_v', 'new_v_w_out': 'new_v', 'new_v_ln_mix_post': 'new_v', 'new_v_ln_ffn_pre': 'new_v', 'new_v_w_ffn_gate': 'new_v', 'new_v_w_ffn_up': 'new_v', 'new_v_w_ffn_down': 'new_v', 'new_v_ln_ffn_post': 'new_v', 'new_v_w_ple': 'new_v', 'new_v_ln_ple_gate': 'new_v', 'new_v_w_ple_gate': 'new_v'}


def _forward(args):
    return _fwd_reference(*[args[k] for k in FWD_PARAMS])


def _output_shape():
    def fwd():
        inp = _fwd_setup_inputs(0)
        return _fwd_reference(*[inp[k] for k in FWD_PARAMS])
    out = _jax.eval_shape(fwd)
    return out.shape, out.dtype

N_MICROBATCH = 1
ADAM_LR = 0.001
ADAM_B1 = 0.9
ADAM_B2 = 0.999
ADAM_EPS = 1e-08
ADAM_WD = 0.01
ADAM_STEP = 10
PER_EXAMPLE_BATCH_AXIS = {'x': 0, 'p': 1, 'loss_target': 0}
SHARED_INPUTS = []
_WEIGHT_DTYPES = {'ln_mix_pre': _jnp.float32, 'w_in': _jnp.float32, 'attn_sinks': _jnp.float32, 'gm_ln_g': _jnp.float32, 'gm_ln_b': _jnp.float32, 'gm_ws': _jnp.float32, 'gm_bs': _jnp.float32, 'g_attn_out': _jnp.float32, 'g_gm_out': _jnp.float32, 'w_out': _jnp.float32, 'ln_mix_post': _jnp.float32, 'ln_ffn_pre': _jnp.float32, 'w_ffn_gate': _jnp.float32, 'w_ffn_up': _jnp.float32, 'w_ffn_down': _jnp.float32, 'ln_ffn_post': _jnp.float32, 'w_ple': _jnp.float32, 'ln_ple_gate': _jnp.float32, 'w_ple_gate': _jnp.float32}
MOMENT_SCALE = {'ln_mix_pre': 1.628246e+01, 'w_in': 1.198130e+01, 'attn_sinks': 1.452848e+01, 'gm_ln_g': 1.133564e+00, 'gm_ln_b': 1.391946e+00, 'gm_ws': 6.832384e-01, 'gm_bs': 1.220182e+00, 'g_attn_out': 2.520702e+01, 'g_gm_out': 2.801200e+01, 'w_out': 2.612240e+01, 'ln_mix_post': 7.044127e+01, 'ln_ffn_pre': 7.346773e+00, 'w_ffn_gate': 2.000904e+00, 'w_ffn_up': 3.739753e+00, 'w_ffn_down': 6.329390e+00, 'ln_ffn_post': 6.439445e+01, 'w_ple': 1.047671e+00, 'ln_ple_gate': 2.197045e+00, 'w_ple_gate': 1.135757e+00}


def _to_microbatches(a, axis):
    t = _jnp.moveaxis(a, axis, 0)
    t = t.reshape((N_MICROBATCH, t.shape[0] // N_MICROBATCH) + t.shape[1:])
    return _jnp.moveaxis(t, 1, axis + 1)


def setup_inputs(seed: int = 0) -> dict:
    inp = _fwd_setup_inputs(seed)
    key = _jax.random.fold_in(_jax.random.key(seed), 7919)
    shape, _ = _output_shape()
    out = dict(inp)
    out["loss_target"] = _jax.random.normal(_jax.random.fold_in(key, 0), shape, _jnp.float32)
    for i, name in enumerate(TWIN_WEIGHTS):
        w = inp[name].astype(_jnp.float32)
        if MOMENT_SCALE is None:
            s = _jnp.sqrt(_jnp.mean(_jnp.square(w)) + 1e-30)
        else:
            s = MOMENT_SCALE[name]
        km, kv = _jax.random.split(_jax.random.fold_in(key, i + 1))
        out[name] = w
        out["m_" + name] = s * _jax.random.normal(km, w.shape, _jnp.float32)
        out["v_" + name] = (s * s) * _jax.random.uniform(kv, w.shape, _jnp.float32, 0.5, 1.5)
    if N_MICROBATCH > 1:
        for name, axis in PER_EXAMPLE_BATCH_AXIS.items():
            out[name] = _to_microbatches(out[name], axis)
    return {'x': out['x'], 'p': out['p'], 'ln_mix_pre': out['ln_mix_pre'], 'w_in': out['w_in'], 'attn_sinks': out['attn_sinks'], 'gm_ln_g': out['gm_ln_g'], 'gm_ln_b': out['gm_ln_b'], 'gm_ws': out['gm_ws'], 'gm_bs': out['gm_bs'], 'g_attn_out': out['g_attn_out'], 'g_gm_out': out['g_gm_out'], 'w_out': out['w_out'], 'ln_mix_post': out['ln_mix_post'], 'ln_ffn_pre': out['ln_ffn_pre'], 'w_ffn_gate': out['w_ffn_gate'], 'w_ffn_up': out['w_ffn_up'], 'w_ffn_down': out['w_ffn_down'], 'ln_ffn_post': out['ln_ffn_post'], 'w_ple': out['w_ple'], 'ln_ple_gate': out['ln_ple_gate'], 'w_ple_gate': out['w_ple_gate'], 'loss_target': out['loss_target'], 'm_ln_mix_pre': out['m_ln_mix_pre'], 'm_w_in': out['m_w_in'], 'm_attn_sinks': out['m_attn_sinks'], 'm_gm_ln_g': out['m_gm_ln_g'], 'm_gm_ln_b': out['m_gm_ln_b'], 'm_gm_ws': out['m_gm_ws'], 'm_gm_bs': out['m_gm_bs'], 'm_g_attn_out': out['m_g_attn_out'], 'm_g_gm_out': out['m_g_gm_out'], 'm_w_out': out['m_w_out'], 'm_ln_mix_post': out['m_ln_mix_post'], 'm_ln_ffn_pre': out['m_ln_ffn_pre'], 'm_w_ffn_gate': out['m_w_ffn_gate'], 'm_w_ffn_up': out['m_w_ffn_up'], 'm_w_ffn_down': out['m_w_ffn_down'], 'm_ln_ffn_post': out['m_ln_ffn_post'], 'm_w_ple': out['m_w_ple'], 'm_ln_ple_gate': out['m_ln_ple_gate'], 'm_w_ple_gate': out['m_w_ple_gate'], 'v_ln_mix_pre': out['v_ln_mix_pre'], 'v_w_in': out['v_w_in'], 'v_attn_sinks': out['v_attn_sinks'], 'v_gm_ln_g': out['v_gm_ln_g'], 'v_gm_ln_b': out['v_gm_ln_b'], 'v_gm_ws': out['v_gm_ws'], 'v_gm_bs': out['v_gm_bs'], 'v_g_attn_out': out['v_g_attn_out'], 'v_g_gm_out': out['v_g_gm_out'], 'v_w_out': out['v_w_out'], 'v_ln_mix_post': out['v_ln_mix_post'], 'v_ln_ffn_pre': out['v_ln_ffn_pre'], 'v_w_ffn_gate': out['v_w_ffn_gate'], 'v_w_ffn_up': out['v_w_ffn_up'], 'v_w_ffn_down': out['v_w_ffn_down'], 'v_ln_ffn_post': out['v_ln_ffn_post'], 'v_w_ple': out['v_w_ple'], 'v_ln_ple_gate': out['v_ln_ple_gate'], 'v_w_ple_gate': out['v_w_ple_gate']}


def _loss(weights, diff, rest, loss_target):
    with _jax.named_scope("forward"):
        args = {**rest, TWIN_DIFF_INPUT: diff, **{k: w.astype(_WEIGHT_DTYPES[k]) for k, w in weights.items()}}
        y = _forward(args)
    with _jax.named_scope("loss_head"):
        err = _jnp.square(y.astype(_jnp.float32) - loss_target)
        return 0.5 * _jnp.sum(_jnp.mean(err, axis=-1)) if err.ndim else 0.5 * err


def _adamw(w, g, m, v):
    m = ADAM_B1 * m + (1.0 - ADAM_B1) * g
    v = ADAM_B2 * v + (1.0 - ADAM_B2) * _jnp.square(g)
    m_hat = m / (1.0 - ADAM_B1 ** ADAM_STEP)
    v_hat = v / (1.0 - ADAM_B2 ** ADAM_STEP)
    delta = -ADAM_LR * (m_hat / (_jnp.sqrt(v_hat) + ADAM_EPS) + ADAM_WD * w)
    return delta, m, v


def reference(x, p, ln_mix_pre, w_in, attn_sinks, gm_ln_g, gm_ln_b, gm_ws, gm_bs, g_attn_out, g_gm_out, w_out, ln_mix_post, ln_ffn_pre, w_ffn_gate, w_ffn_up, w_ffn_down, ln_ffn_post, w_ple, ln_ple_gate, w_ple_gate, loss_target, m_ln_mix_pre, m_w_in, m_attn_sinks, m_gm_ln_g, m_gm_ln_b, m_gm_ws, m_gm_bs, m_g_attn_out, m_g_gm_out, m_w_out, m_ln_mix_post, m_ln_ffn_pre, m_w_ffn_gate, m_w_ffn_up, m_w_ffn_down, m_ln_ffn_post, m_w_ple, m_ln_ple_gate, m_w_ple_gate, v_ln_mix_pre, v_w_in, v_attn_sinks, v_gm_ln_g, v_gm_ln_b, v_gm_ws, v_gm_bs, v_g_attn_out, v_g_gm_out, v_w_out, v_ln_mix_post, v_ln_ffn_pre, v_w_ffn_gate, v_w_ffn_up, v_w_ffn_down, v_ln_ffn_post, v_w_ple, v_ln_ple_gate, v_w_ple_gate):
    given = dict(x=x, p=p, ln_mix_pre=ln_mix_pre, w_in=w_in, attn_sinks=attn_sinks, gm_ln_g=gm_ln_g, gm_ln_b=gm_ln_b, gm_ws=gm_ws, gm_bs=gm_bs, g_attn_out=g_attn_out, g_gm_out=g_gm_out, w_out=w_out, ln_mix_post=ln_mix_post, ln_ffn_pre=ln_ffn_pre, w_ffn_gate=w_ffn_gate, w_ffn_up=w_ffn_up, w_ffn_down=w_ffn_down, ln_ffn_post=ln_ffn_post, w_ple=w_ple, ln_ple_gate=ln_ple_gate, w_ple_gate=w_ple_gate, loss_target=loss_target, m_ln_mix_pre=m_ln_mix_pre, m_w_in=m_w_in, m_attn_sinks=m_attn_sinks, m_gm_ln_g=m_gm_ln_g, m_gm_ln_b=m_gm_ln_b, m_gm_ws=m_gm_ws, m_gm_bs=m_gm_bs, m_g_attn_out=m_g_attn_out, m_g_gm_out=m_g_gm_out, m_w_out=m_w_out, m_ln_mix_post=m_ln_mix_post, m_ln_ffn_pre=m_ln_ffn_pre, m_w_ffn_gate=m_w_ffn_gate, m_w_ffn_up=m_w_ffn_up, m_w_ffn_down=m_w_ffn_down, m_ln_ffn_post=m_ln_ffn_post, m_w_ple=m_w_ple, m_ln_ple_gate=m_ln_ple_gate, m_w_ple_gate=m_w_ple_gate, v_ln_mix_pre=v_ln_mix_pre, v_w_in=v_w_in, v_attn_sinks=v_attn_sinks, v_gm_ln_g=v_gm_ln_g, v_gm_ln_b=v_gm_ln_b, v_gm_ws=v_gm_ws, v_gm_bs=v_gm_bs, v_g_attn_out=v_g_attn_out, v_g_gm_out=v_g_gm_out, v_w_out=v_w_out, v_ln_mix_post=v_ln_mix_post, v_ln_ffn_pre=v_ln_ffn_pre, v_w_ffn_gate=v_w_ffn_gate, v_w_ffn_up=v_w_ffn_up, v_w_ffn_down=v_w_ffn_down, v_ln_ffn_post=v_ln_ffn_post, v_w_ple=v_w_ple, v_ln_ple_gate=v_ln_ple_gate, v_w_ple_gate=v_w_ple_gate)
    weights = {n: given[n] for n in TWIN_WEIGHTS}
    shared = {n: given[n] for n in SHARED_INPUTS}
    per_example = {n: given[n] for n in ['x', 'p']}
    grad_fn = _jax.value_and_grad(_loss, argnums=(0, 1))

    def one_microbatch(ex, loss_target):
        ex = dict(ex)
        diff = ex.pop(TWIN_DIFF_INPUT)
        return grad_fn(weights, diff, {**shared, **ex}, loss_target)

    if N_MICROBATCH == 1:
        loss, (grad_w, grad_x) = one_microbatch(per_example, given["loss_target"])
    else:
        def body(carry, xs):
            loss_sum, grad_sum = carry
            l_k, (gw_k, gx_k) = one_microbatch(xs[0], xs[1])
            with _jax.named_scope("update"):
                return (loss_sum + l_k, _jax.tree.map(_jnp.add, grad_sum, gw_k)), gx_k

        init = (_jnp.zeros((), _jnp.float32), _jax.tree.map(_jnp.zeros_like, weights))
        (loss, grad_w), grad_x = _jax.lax.scan(body, init, (per_example, given["loss_target"]))
    with _jax.named_scope("update"):
        delta_w, new_m, new_v = {}, {}, {}
        for n in TWIN_WEIGHTS:
            delta_w[n], new_m[n], new_v[n] = _adamw(weights[n], grad_w[n], given["m_" + n], given["v_" + n])
    return (loss, grad_x, *[grad_w[n] for n in TWIN_WEIGHTS], *[delta_w[n] for n in TWIN_WEIGHTS],
            *[new_m[n] for n in TWIN_WEIGHTS], *[new_v[n] for n in TWIN_WEIGHTS])
```

```python
import functools
import math

import jax
import jax.numpy as jnp
from jax import lax
from jax.experimental import pallas as pl
from jax.experimental.pallas import tpu as pltpu

F32 = jnp.float32
BF16 = jnp.bfloat16

D_MODEL = 1024
HEAD_DIM = 64
N_Q_HEADS = 8
BLK = 128
ATTN_W = 512
KV_W = 128
GM_W = 512
D_IN = ATTN_W + 2 * KV_W + 2 * GM_W
D_FF = 2816
PLE_DIM = 256
DEPTH = 4
NORM_EPS = 1e-6
NEG_BIG = -1e30
N_CHIPS = 4
N_DEV = 8

ADAM_LR = 0.001
ADAM_B1 = 0.9
ADAM_B2 = 0.999
ADAM_EPS = 1e-08
ADAM_WD = 0.01
ADAM_STEP = 10

VMEM_LIMIT_BYTES = 56 * 1024 * 1024
LANES = 128
GELU_C0 = math.sqrt(2.0 / math.pi)
GELU_C1 = 0.044715
ALIBI_SLOPES = tuple(2.0 ** (-8.0 * (h + 1.0) / N_Q_HEADS) for h in range(N_Q_HEADS))

WEIGHTS = ['ln_mix_pre', 'w_in', 'attn_sinks', 'gm_ln_g', 'gm_ln_b', 'gm_ws', 'gm_bs', 'g_attn_out',
           'g_gm_out', 'w_out', 'ln_mix_post', 'ln_ffn_pre', 'w_ffn_gate', 'w_ffn_up', 'w_ffn_down',
           'ln_ffn_post', 'w_ple', 'ln_ple_gate', 'w_ple_gate']
BIG = {'w_in': 2, 'w_out': 1, 'w_ffn_gate': 2, 'w_ffn_up': 2, 'w_ffn_down': 1, 'w_ple': 2, 'w_ple_gate': 1}
BIG_NAMES = list(BIG)
SMALL_NAMES = [n for n in WEIGHTS if n not in BIG]
SMALL_PAD = 1024
SMALL_ROW_TILE = 512


def _nt(a, b):
    return lax.dot_general(a, b, (((1,), (1,)), ((), ())), preferred_element_type=F32)


def _tn(a, b):
    return lax.dot_general(a, b, (((0,), (0,)), ((), ())), preferred_element_type=F32)


def _mm(a, b):
    return jnp.dot(a, b, preferred_element_type=F32)


def _rms(x, g):
    r = lax.rsqrt(jnp.mean(x * x, axis=-1, keepdims=True) + NORM_EPS)
    return x * r * g


def _rms_bwd(dy, x, g):
    r = lax.rsqrt(jnp.mean(x * x, axis=-1, keepdims=True) + NORM_EPS)
    xh = x * r
    dg = jnp.sum(dy * xh, axis=0, keepdims=True)
    dxh = dy * g
    dx = r * (dxh - xh * jnp.mean(dxh * xh, axis=-1, keepdims=True))
    return dx, dg


def _gelu(x):
    return 0.5 * x * (1.0 + jnp.tanh(GELU_C0 * (x + GELU_C1 * x * x * x)))


def _gelu_grad(x):
    t = jnp.tanh(GELU_C0 * (x + GELU_C1 * x * x * x))
    return 0.5 * (1.0 + t) + 0.5 * x * (1.0 - t * t) * GELU_C0 * (1.0 + 3.0 * GELU_C1 * x * x)


def _sigmoid(x):
    return 1.0 / (1.0 + jnp.exp(-x))


def _rows(tm, n):
    return pl.BlockSpec((tm, n), lambda i: (i, 0))


def _whole(shape):
    return pl.BlockSpec(shape, lambda i: (0,) * len(shape))


def _accumulate(ref, val):
    @pl.when(pl.program_id(0) == 0)
    def _():
        ref[...] = jnp.zeros_like(ref)

    ref[...] += val


def _params(n_axes=1):
    return pltpu.CompilerParams(dimension_semantics=("arbitrary",) * n_axes,
                                vmem_limit_bytes=VMEM_LIMIT_BYTES)


def _sds(shape, dtype):
    return jax.ShapeDtypeStruct(shape, dtype)


def _tile(t, want):
    return min(t, want)


def _f1_norm_in(h, g, w):
    t = h.shape[0]
    tm = _tile(t, 512)

    def body(h_ref, g_ref, w_ref, z_ref, a_ref):
        a = _rms(h_ref[...], g_ref[...]).astype(BF16)
        a_ref[...] = a
        z_ref[...] = _mm(a, w_ref[...])

    return pl.pallas_call(
        body, name="f1_norm_in", grid=(t // tm,),
        in_specs=[_rows(tm, D_MODEL), _whole((1, D_MODEL)), _whole((D_MODEL, D_IN))],
        out_specs=[_rows(tm, D_IN), _rows(tm, D_MODEL)],
        out_shape=[_sds((t, D_IN), F32), _sds((t, D_MODEL), BF16)],
        compiler_params=_params())(h, g, w)


def _attn_geometry(n):
    ti = lax.broadcasted_iota(jnp.int32, (BLK, 2 * BLK), 0)
    ji = lax.broadcasted_iota(jnp.int32, (BLK, 2 * BLK), 1)
    dist = ti + BLK - ji
    valid = (dist >= 0) & (dist < BLK) & ((n > 0) | (ji >= BLK))
    return valid, dist.astype(F32)


def _attn_probs(qm, kuse, valid, distf, slope, sink):
    sc = _nt(qm, kuse) * (HEAD_DIM ** -0.5)
    sc = jnp.where(valid, sc - slope * distf, NEG_BIG)
    m = jnp.maximum(jnp.max(sc, axis=1, keepdims=True), sink)
    e = jnp.exp(sc - m)
    es = jnp.exp(sink - m)
    inv = 1.0 / (jnp.sum(e, axis=1, keepdims=True) + es)
    return e * inv, es * inv


def _kv_window(z_ref, kp_ref, vp_ref):
    kcat = jnp.concatenate([kp_ref[...], z_ref[:, ATTN_W:ATTN_W + KV_W]], axis=0)
    vcat = jnp.concatenate([vp_ref[...], z_ref[:, ATTN_W + KV_W:ATTN_W + 2 * KV_W]], axis=0)
    kswap = pltpu.roll(kcat, HEAD_DIM, 1)
    vswap = pltpu.roll(vcat, HEAD_DIM, 1)
    return kcat.astype(BF16), kswap.astype(BF16), vcat, vswap


def _gm_forward(z_ref, lng_ref, lnb_ref):
    zu = z_ref[:, ATTN_W + 2 * KV_W:ATTN_W + 2 * KV_W + GM_W]
    zv = z_ref[:, ATTN_W + 2 * KV_W + GM_W:D_IN]
    u = _gelu(zu)
    gv = _gelu(zv)
    xc = gv - jnp.mean(gv, axis=-1, keepdims=True)
    rstd = lax.rsqrt(jnp.mean(xc * xc, axis=-1, keepdims=True) + NORM_EPS)
    xhat = xc * rstd
    vn = xhat * lng_ref[...] + lnb_ref[...]
    return zu, zv, u, xhat, rstd, vn


def _tril_w(ws_ref, h):
    ti = lax.broadcasted_iota(jnp.int32, (BLK, BLK), 0)
    si = lax.broadcasted_iota(jnp.int32, (BLK, BLK), 1)
    causal = si <= ti
    return jnp.where(causal, ws_ref[h], 0.0).astype(BF16), causal


def _gm_mixed(vn, ws_ref, bs_ref, lo, hi):
    slabs = []
    for s in range(GM_W // LANES):
        vs = vn[:, s * LANES:(s + 1) * LANES]
        w0, _ = _tril_w(ws_ref, 2 * s)
        w1, _ = _tril_w(ws_ref, 2 * s + 1)
        mixed = (_mm(w0, jnp.where(lo, vs, 0.0).astype(BF16))
                 + _mm(w1, jnp.where(hi, vs, 0.0).astype(BF16))
                 + bs_ref[:, s * LANES:(s + 1) * LANES])
        slabs.append(mixed)
    return slabs


def _block_specs_z(nb):
    prev = lambda i: (jnp.maximum(i - 1, 0), ATTN_W // KV_W)
    prev_v = lambda i: (jnp.maximum(i - 1, 0), ATTN_W // KV_W + 1)
    return [_rows(BLK, D_IN), pl.BlockSpec((BLK, KV_W), prev), pl.BlockSpec((BLK, KV_W), prev_v)]


def _f2_attn_gm(z, sinks, ln_g, ln_b, ws, bs_full):
    t = z.shape[0]
    nb = t // BLK

    def body(z_ref, kp_ref, vp_ref, sink_ref, lng_ref, lnb_ref, ws_ref, bs_ref, am_ref):
        n = pl.program_id(0)
        lane = lax.broadcasted_iota(jnp.int32, (1, LANES), 1)
        lo = lane < HEAD_DIM
        hi = lane >= HEAD_DIM
        valid, distf = _attn_geometry(n)
        kc, ks, vcat, vswap = _kv_window(z_ref, kp_ref, vp_ref)
        for s in range(ATTN_W // LANES):
            kv_head = s // 2
            qs = z_ref[:, s * LANES:(s + 1) * LANES]
            o = jnp.zeros((BLK, LANES), F32)
            for half in range(2):
                hq = 2 * s + half
                mask = lo if half == 0 else hi
                same = kv_head == half
                qm = jnp.where(mask, qs, 0.0).astype(BF16)
                pr, _ = _attn_probs(qm, kc if same else ks, valid, distf, ALIBI_SLOPES[hq], sink_ref[hq])
                vm = jnp.where(mask, vcat if same else vswap, 0.0).astype(BF16)
                o = o + _mm(pr.astype(BF16), vm)
            am_ref[:, s * LANES:(s + 1) * LANES] = o
        _, _, u, _, _, vn = _gm_forward(z_ref, lng_ref, lnb_ref)
        mixed = _gm_mixed(vn, ws_ref, bs_ref, lo, hi)
        for s in range(GM_W // LANES):
            am_ref[:, ATTN_W + s * LANES:ATTN_W + (s + 1) * LANES] = u[:, s * LANES:(s + 1) * LANES] * mixed[s]

    return pl.pallas_call(
        body, name="f2_attn_gm", grid=(nb,),
        in_specs=_block_specs_z(nb) + [
            pl.BlockSpec(memory_space=pltpu.SMEM), _whole((1, GM_W)), _whole((1, GM_W)),
            _whole((N_Q_HEADS, BLK, BLK)), _whole((BLK, GM_W))],
        out_specs=_rows(BLK, ATTN_W + GM_W),
        out_shape=_sds((t, ATTN_W + GM_W), F32),
        compiler_params=_params())(z, z, z, sinks, ln_g, ln_b, ws, bs_full)


def _f3_mix_out(am, h, ga, gg, w, gpost):
    t = h.shape[0]
    tm = _tile(t, 512)

    def body(am_ref, h_ref, ga_ref, gg_ref, w_ref, gp_ref, heads_ref, mix_ref, h1_ref):
        heads = jnp.concatenate([_rms(am_ref[:, :ATTN_W], ga_ref[...]),
                                 _rms(am_ref[:, ATTN_W:], gg_ref[...])], axis=1).astype(BF16)
        heads_ref[...] = heads
        mix = _mm(heads, w_ref[...])
        mix_ref[...] = mix
        h1_ref[...] = h_ref[...] + _rms(mix, gp_ref[...])

    return pl.pallas_call(
        body, name="f3_mix_out", grid=(t // tm,),
        in_specs=[_rows(tm, D_MODEL), _rows(tm, D_MODEL), _whole((1, ATTN_W)), _whole((1, GM_W)),
                  _whole((D_MODEL, D_MODEL)), _whole((1, D_MODEL))],
        out_specs=[_rows(tm, D_MODEL)] * 3,
        out_shape=[_sds((t, D_MODEL), BF16), _sds((t, D_MODEL), F32), _sds((t, D_MODEL), F32)],
        compiler_params=_params())(am, h, ga, gg, w, gpost)


def _f4a_ffn_in(h1, gf, wg, wu):
    t = h1.shape[0]
    tm = _tile(t, 256)

    def body(h_ref, g_ref, wg_ref, wu_ref, f_ref, gt_ref, up_ref):
        f = _rms(h_ref[...], g_ref[...]).astype(BF16)
        f_ref[...] = f
        gt_ref[...] = _mm(f, wg_ref[...])
        up_ref[...] = _mm(f, wu_ref[...])

    return pl.pallas_call(
        body, name="f4a_ffn_in", grid=(t // tm,),
        in_specs=[_rows(tm, D_MODEL), _whole((1, D_MODEL)), _whole((D_MODEL, D_FF)), _whole((D_MODEL, D_FF))],
        out_specs=[_rows(tm, D_MODEL), _rows(tm, D_FF), _rows(tm, D_FF)],
        out_shape=[_sds((t, D_MODEL), BF16), _sds((t, D_FF), F32), _sds((t, D_FF), F32)],
        compiler_params=_params())(h1, gf, wg, wu)


def _f4b_ffn_out(gt, up, wd, h1, gfp):
    t = h1.shape[0]
    tm = _tile(t, 256)

    def body(gt_ref, up_ref, wd_ref, h_ref, g_ref, dn_ref, h2_ref):
        gt = gt_ref[...]
        act = (gt * _sigmoid(gt) * up_ref[...]).astype(BF16)
        dn = _mm(act, wd_ref[...])
        dn_ref[...] = dn
        h2_ref[...] = h_ref[...] + _rms(dn, g_ref[...])

    return pl.pallas_call(
        body, name="f4b_ffn_out", grid=(t // tm,),
        in_specs=[_rows(tm, D_FF), _rows(tm, D_FF), _whole((D_FF, D_MODEL)), _rows(tm, D_MODEL),
                  _whole((1, D_MODEL))],
        out_specs=[_rows(tm, D_MODEL)] * 2,
        out_shape=[_sds((t, D_MODEL), F32)] * 2,
        compiler_params=_params())(gt, up, wd, h1, gfp)


def _f5_ple(h2, gpl, wpg, p, wple):
    t = h2.shape[0]
    tm = _tile(t, 512)

    def body(h_ref, g_ref, wpg_ref, p_ref, wple_ref, r_ref, pg_ref, pe_ref, h3_ref):
        h = h_ref[...]
        r = _rms(h, g_ref[...]).astype(BF16)
        r_ref[...] = r
        pg = _mm(r, wpg_ref[...])
        pe = _mm(p_ref[...].astype(BF16), wple_ref[...])
        pg_ref[...] = pg
        pe_ref[...] = pe
        h3_ref[...] = h + pe * _sigmoid(pg)

    return pl.pallas_call(
        body, name="f5_ple", grid=(t // tm,),
        in_specs=[_rows(tm, D_MODEL), _whole((1, D_MODEL)), _whole((D_MODEL, D_MODEL)), _rows(tm, PLE_DIM),
                  _whole((PLE_DIM, D_MODEL))],
        out_specs=[_rows(tm, D_MODEL)] * 4,
        out_shape=[_sds((t, D_MODEL), BF16)] + [_sds((t, D_MODEL), F32)] * 3,
        compiler_params=_params())(h2, gpl, wpg, p, wple)


def _loss_head(y, target):
    t = y.shape[0]
    tm = _tile(t, 512)

    def body(y_ref, t_ref, sq_ref, dy_ref):
        err = y_ref[...] - t_ref[...]
        dy_ref[...] = err * (1.0 / D_MODEL)
        _accumulate(sq_ref, jnp.sum(err * err, keepdims=True))

    return pl.pallas_call(
        body, name="loss_head", grid=(t // tm,),
        in_specs=[_rows(tm, D_MODEL)] * 2,
        out_specs=[_whole((1, LANES)), _rows(tm, D_MODEL)],
        out_shape=[_sds((1, LANES), F32), _sds((t, D_MODEL), F32)],
        compiler_params=_params())(y, target)


def _b5_ple(dh3, h2, pg, pe, gpl, wpg):
    t = h2.shape[0]
    tm = _tile(t, 512)

    def body(dh_ref, h_ref, pg_ref, pe_ref, g_ref, w_ref, dpe_ref, dpg_ref, dh2_ref, dg_ref):
        dh = dh_ref[...]
        s = _sigmoid(pg_ref[...])
        dpe_ref[...] = (dh * s).astype(BF16)
        dpg = (dh * pe_ref[...] * s * (1.0 - s)).astype(BF16)
        dpg_ref[...] = dpg
        dx, dg = _rms_bwd(_nt(dpg, w_ref[...]), h_ref[...], g_ref[...])
        dh2_ref[...] = dh + dx
        _accumulate(dg_ref, dg)

    return pl.pallas_call(
        body, name="b5_ple", grid=(t // tm,),
        in_specs=[_rows(tm, D_MODEL)] * 4 + [_whole((1, D_MODEL)), _whole((D_MODEL, D_MODEL))],
        out_specs=[_rows(tm, D_MODEL)] * 3 + [_whole((1, D_MODEL))],
        out_shape=[_sds((t, D_MODEL), BF16)] * 2 + [_sds((t, D_MODEL), F32), _sds((1, D_MODEL), F32)],
        compiler_params=_params())(dh3, h2, pg, pe, gpl, wpg)


def _b4a_ffn_out(dh2, dn, gfp, wd, gt, up):
    t = dh2.shape[0]
    tm = _tile(t, 256)

    def body(dh_ref, dn_ref, g_ref, wd_ref, gt_ref, up_ref, ddn_ref, act_ref, dgt_ref, dup_ref, dg_ref):
        ddn, dg = _rms_bwd(dh_ref[...], dn_ref[...], g_ref[...])
        _accumulate(dg_ref, dg)
        ddn = ddn.astype(BF16)
        ddn_ref[...] = ddn
        dact = _nt(ddn, wd_ref[...])
        gt = gt_ref[...]
        up = up_ref[...]
        sg = _sigmoid(gt)
        silu = gt * sg
        act_ref[...] = (silu * up).astype(BF16)
        dup_ref[...] = (dact * silu).astype(BF16)
        dgt_ref[...] = (dact * up * (sg * (1.0 + gt * (1.0 - sg)))).astype(BF16)

    return pl.pallas_call(
        body, name="b4a_ffn_out", grid=(t // tm,),
        in_specs=[_rows(tm, D_MODEL), _rows(tm, D_MODEL), _whole((1, D_MODEL)), _whole((D_FF, D_MODEL)),
                  _rows(tm, D_FF), _rows(tm, D_FF)],
        out_specs=[_rows(tm, D_MODEL), _rows(tm, D_FF), _rows(tm, D_FF), _rows(tm, D_FF), _whole((1, D_MODEL))],
        out_shape=[_sds((t, D_MODEL), BF16)] + [_sds((t, D_FF), BF16)] * 3 + [_sds((1, D_MODEL), F32)],
        compiler_params=_params())(dh2, dn, gfp, wd, gt, up)


def _b4b_ffn_in(dgt, dup, wg, wu, h1, gf, dh2):
    t = h1.shape[0]
    tm = _tile(t, 512)

    def body(dgt_ref, dup_ref, wg_ref, wu_ref, h_ref, g_ref, dh_ref, dh1_ref, dg_ref):
        df = _nt(dgt_ref[...], wg_ref[...]) + _nt(dup_ref[...], wu_ref[...])
        dx, dg = _rms_bwd(df, h_ref[...], g_ref[...])
        dh1_ref[...] = dh_ref[...] + dx
        _accumulate(dg_ref, dg)

    return pl.pallas_call(
        body, name="b4b_ffn_in", grid=(t // tm,),
        in_specs=[_rows(tm, D_FF), _rows(tm, D_FF), _whole((D_MODEL, D_FF)), _whole((D_MODEL, D_FF)),
                  _rows(tm, D_MODEL), _whole((1, D_MODEL)), _rows(tm, D_MODEL)],
        out_specs=[_rows(tm, D_MODEL), _whole((1, D_MODEL))],
        out_shape=[_sds((t, D_MODEL), F32), _sds((1, D_MODEL), F32)],
        compiler_params=_params())(dgt, dup, wg, wu, h1, gf, dh2)


def _b3_mix_out(dh1, mix, gpost, w, am, ga, gg):
    t = dh1.shape[0]
    tm = _tile(t, 512)

    def body(dh_ref, mix_ref, gp_ref, w_ref, am_ref, ga_ref, gg_ref, dmix_ref, dam_ref, dgp_ref, dga_ref, dgg_ref):
        dmix, dgp = _rms_bwd(dh_ref[...], mix_ref[...], gp_ref[...])
        _accumulate(dgp_ref, dgp)
        dmix = dmix.astype(BF16)
        dmix_ref[...] = dmix
        dheads = _nt(dmix, w_ref[...])
        dat, dga = _rms_bwd(dheads[:, :ATTN_W], am_ref[:, :ATTN_W], ga_ref[...])
        dgm, dgg = _rms_bwd(dheads[:, ATTN_W:], am_ref[:, ATTN_W:], gg_ref[...])
        dam_ref[:, :ATTN_W] = dat
        dam_ref[:, ATTN_W:] = dgm
        _accumulate(dga_ref, dga)
        _accumulate(dgg_ref, dgg)

    return pl.pallas_call(
        body, name="b3_mix_out", grid=(t // tm,),
        in_specs=[_rows(tm, D_MODEL), _rows(tm, D_MODEL), _whole((1, D_MODEL)), _whole((D_MODEL, D_MODEL)),
                  _rows(tm, D_MODEL), _whole((1, ATTN_W)), _whole((1, GM_W))],
        out_specs=[_rows(tm, D_MODEL), _rows(tm, D_MODEL), _whole((1, D_MODEL)), _whole((1, ATTN_W)),
                   _whole((1, GM_W))],
        out_shape=[_sds((t, D_MODEL), BF16), _sds((t, D_MODEL), F32), _sds((1, D_MODEL), F32),
                   _sds((1, ATTN_W), F32), _sds((1, GM_W), F32)],
        compiler_params=_params())(dh1, mix, gpost, w, am, ga, gg)


def _b2_attn_gm(dam, z, sinks, ln_g, ln_b, ws, bs_full):
    t = z.shape[0]
    nb = t // BLK

    def body(dam_ref, z_ref, kp_ref, vp_ref, sink_ref, lng_ref, lnb_ref, ws_ref, bs_ref,
             dzq_ref, dkv_ref, dzuv_ref, dsink_ref, dlng_ref, dlnb_ref, dws_ref, dbs_ref):
        n = pl.program_id(0)
        lane = lax.broadcasted_iota(jnp.int32, (1, LANES), 1)
        lo = lane < HEAD_DIM
        hi = lane >= HEAD_DIM
        valid, distf = _attn_geometry(n)
        kc, ks, vcat, vswap = _kv_window(z_ref, kp_ref, vp_ref)
        vc = vcat.astype(BF16)
        vs_ = vswap.astype(BF16)
        dk_acc = jnp.zeros((2 * BLK, LANES), F32)
        dv_acc = jnp.zeros((2 * BLK, LANES), F32)
        dsink = jnp.zeros((1, LANES), F32)
        for s in range(ATTN_W // LANES):
            kv_head = s // 2
            qs = z_ref[:, s * LANES:(s + 1) * LANES]
            dos = dam_ref[:, s * LANES:(s + 1) * LANES]
            dq = jnp.zeros((BLK, LANES), F32)
            for half in range(2):
                hq = 2 * s + half
                mask = lo if half == 0 else hi
                same = kv_head == half
                kuse = kc if same else ks
                vuse = vc if same else vs_
                qm = jnp.where(mask, qs, 0.0).astype(BF16)
                dom = jnp.where(mask, dos, 0.0).astype(BF16)
                pr, ps = _attn_probs(qm, kuse, valid, distf, ALIBI_SLOPES[hq], sink_ref[hq])
                dpr = _nt(dom, vuse)
                row = jnp.sum(dpr * pr, axis=1, keepdims=True)
                ds = (pr * (dpr - row) * (HEAD_DIM ** -0.5)).astype(BF16)
                dsink = dsink + jnp.where(lane == hq, -jnp.sum(ps * row, keepdims=True), 0.0)
                dq = dq + jnp.where(mask, _mm(ds, kuse), 0.0)
                dkc = _tn(ds, qm)
                dvc = _tn(pr.astype(BF16), dom)
                if not same:
                    dkc = pltpu.roll(dkc, HEAD_DIM, 1)
                    dvc = pltpu.roll(dvc, HEAD_DIM, 1)
                dk_acc = dk_acc + dkc
                dv_acc = dv_acc + dvc
            dzq_ref[:, s * LANES:(s + 1) * LANES] = dq.astype(BF16)
        cur = pl.multiple_of(n * BLK, BLK)
        dkv_ref[pl.ds(cur, BLK), 0:KV_W] = dk_acc[BLK:, :]
        dkv_ref[pl.ds(cur, BLK), KV_W:2 * KV_W] = dv_acc[BLK:, :]

        @pl.when(n > 0)
        def _():
            prv = pl.multiple_of((n - 1) * BLK, BLK)
            dkv_ref[pl.ds(prv, BLK), 0:KV_W] += dk_acc[:BLK, :]
            dkv_ref[pl.ds(prv, BLK), KV_W:2 * KV_W] += dv_acc[:BLK, :]

        _accumulate(dsink_ref, dsink)

        zu, zv, u, xhat, rstd, vn = _gm_forward(z_ref, lng_ref, lnb_ref)
        mixed = _gm_mixed(vn, ws_ref, bs_ref, lo, hi)

        @pl.when(n == 0)
        def _():
            dws_ref[...] = jnp.zeros_like(dws_ref)

        dbs = jnp.zeros((BLK, LANES), F32)
        dvn_slabs = []
        du_slabs = []
        for s in range(GM_W // LANES):
            dgm = dam_ref[:, ATTN_W + s * LANES:ATTN_W + (s + 1) * LANES]
            du_slabs.append(dgm * mixed[s])
            dmx = dgm * u[:, s * LANES:(s + 1) * LANES]
            vsb = vn[:, s * LANES:(s + 1) * LANES].astype(BF16)
            dvn = jnp.zeros((BLK, LANES), F32)
            for half in range(2):
                h = 2 * s + half
                mask = lo if half == 0 else hi
                dmm = jnp.where(mask, dmx, 0.0)
                dbs = dbs + jnp.where(lane == h, jnp.sum(dmm, axis=1, keepdims=True), 0.0)
                dmm = dmm.astype(BF16)
                wt, causal = _tril_w(ws_ref, h)
                dvn = dvn + jnp.where(mask, _tn(wt, dmm), 0.0)
                dws_ref[h] += jnp.where(causal, _nt(dmm, vsb), 0.0)
            dvn_slabs.append(dvn)
        _accumulate(dbs_ref, dbs)
        dvn = jnp.concatenate(dvn_slabs, axis=1)
        du = jnp.concatenate(du_slabs, axis=1)
        _accumulate(dlnb_ref, jnp.sum(dvn, axis=0, keepdims=True))
        _accumulate(dlng_ref, jnp.sum(dvn * xhat, axis=0, keepdims=True))
        dxh = dvn * lng_ref[...]
        dgv = rstd * (dxh - jnp.mean(dxh, axis=-1, keepdims=True)
                      - xhat * jnp.mean(dxh * xhat, axis=-1, keepdims=True))
        dzuv_ref[:, :GM_W] = (du * _gelu_grad(zu)).astype(BF16)
        dzuv_ref[:, GM_W:] = (dgv * _gelu_grad(zv)).astype(BF16)

    return pl.pallas_call(
        body, name="b2_attn_gm", grid=(nb,),
        in_specs=[_rows(BLK, ATTN_W + GM_W)] + _block_specs_z(nb) + [
            pl.BlockSpec(memory_space=pltpu.SMEM), _whole((1, GM_W)), _whole((1, GM_W)),
            _whole((N_Q_HEADS, BLK, BLK)), _whole((BLK, GM_W))],
        out_specs=[_rows(BLK, ATTN_W), _whole((t, 2 * KV_W)), _rows(BLK, 2 * GM_W), _whole((1, LANES)),
                   _whole((1, GM_W)), _whole((1, GM_W)), _whole((N_Q_HEADS, BLK, BLK)), _whole((BLK, LANES))],
        out_shape=[_sds((t, ATTN_W), BF16), _sds((t, 2 * KV_W), F32), _sds((t, 2 * GM_W), BF16),
                   _sds((1, LANES), F32), _sds((1, GM_W), F32), _sds((1, GM_W), F32),
                   _sds((N_Q_HEADS, BLK, BLK), F32), _sds((BLK, LANES), F32)],
        compiler_params=_params())(dam, z, z, z, sinks, ln_g, ln_b, ws, bs_full)


def _b1_norm_in(dzq, dkv, dzuv, w, h, g, dh1):
    t = h.shape[0]
    tm = _tile(t, 512)

    def body(dzq_ref, dkv_ref, dzuv_ref, w_ref, h_ref, g_ref, dh_ref, dz_ref, dh0_ref, dg_ref):
        dz = jnp.concatenate([dzq_ref[...], dkv_ref[...].astype(BF16), dzuv_ref[...]], axis=1)
        dz_ref[...] = dz
        dx, dg = _rms_bwd(_nt(dz, w_ref[...]), h_ref[...], g_ref[...])
        dh0_ref[...] = dh_ref[...] + dx
        _accumulate(dg_ref, dg)

    return pl.pallas_call(
        body, name="b1_norm_in", grid=(t // tm,),
        in_specs=[_rows(tm, ATTN_W), _rows(tm, 2 * KV_W), _rows(tm, 2 * GM_W), _whole((D_MODEL, D_IN)),
                  _rows(tm, D_MODEL), _whole((1, D_MODEL)), _rows(tm, D_MODEL)],
        out_specs=[_rows(tm, D_IN), _rows(tm, D_MODEL), _whole((1, D_MODEL))],
        out_shape=[_sds((t, D_IN), BF16), _sds((t, D_MODEL), F32), _sds((1, D_MODEL), F32)],
        compiler_params=_params())(dzq, dkv, dzuv, w, h, g, dh1)


def _weight_grad(x, dy, name):
    t, k = x.shape
    n = dy.shape[1]
    tm = _tile(t, 512)

    def body(x_ref, dy_ref, dw_ref):
        _accumulate(dw_ref, _tn(x_ref[...].astype(BF16), dy_ref[...]))

    return pl.pallas_call(
        body, name=name, grid=(t // tm,),
        in_specs=[_rows(tm, k), _rows(tm, n)],
        out_specs=_whole((k, n)),
        out_shape=_sds((k, n), F32),
        compiler_params=_params())(x, dy)


_ANY = pl.BlockSpec(memory_space=pl.ANY)


def _mesh_pos():
    return lax.axis_index("x"), lax.axis_index("y"), lax.axis_index("c")


def _other_chips(x, y):
    return [(1 - x, y), (x, 1 - y), (1 - x, 1 - y)]


def _chip_exchange(arrs, scatter, name):
    k = len(arrs)

    def body(*refs):
        ins, outs = refs[:k], refs[k:2 * k]
        send_sems, recv_sems, local_sems = refs[2 * k:]
        x, y, c = _mesh_pos()
        me = 2 * x + y
        peers = _other_chips(x, y)
        copies = []
        for i in range(k):
            mine = ins[i].at[me] if scatter else ins[i]
            cp = pltpu.make_async_copy(mine, outs[i].at[me], local_sems.at[i])
            cp.start()
            copies.append(cp)
        sends = []
        for i in range(k):
            for j, (px, py) in enumerate(peers):
                src = ins[i].at[2 * px + py] if scatter else ins[i]
                cp = pltpu.make_async_remote_copy(
                    src_ref=src, dst_ref=outs[i].at[me], send_sem=send_sems.at[i * 3 + j],
                    recv_sem=recv_sems.at[i * 3 + j], device_id=(px, py, c), device_id_type=pl.DeviceIdType.MESH)
                cp.start()
                sends.append(cp)
        for i in range(k):
            for j, (px, py) in enumerate(peers):
                src = ins[i].at[me] if scatter else ins[i]
                pltpu.make_async_remote_copy(
                    src_ref=src, dst_ref=outs[i].at[2 * px + py], send_sem=send_sems.at[i * 3 + j],
                    recv_sem=recv_sems.at[i * 3 + j], device_id=(px, py, c),
                    device_id_type=pl.DeviceIdType.MESH).wait_recv()
        for cp in sends:
            cp.wait_send()
        for cp in copies:
            cp.wait()

    out_shape = [_sds(a.shape if scatter else (N_CHIPS,) + a.shape, a.dtype) for a in arrs]
    return pl.pallas_call(
        body, name=name, in_specs=[_ANY] * k, out_specs=[_ANY] * k, out_shape=out_shape,
        scratch_shapes=[pltpu.SemaphoreType.DMA((3 * k,)), pltpu.SemaphoreType.DMA((3 * k,)),
                        pltpu.SemaphoreType.DMA((k,))])(*arrs)


def _sibling_exchange(arrs, name):
    k = len(arrs)

    def body(*refs):
        ins, outs = refs[:k], refs[k:2 * k]
        send_sems, recv_sems = refs[2 * k:]
        x, y, c = _mesh_pos()
        cps = []
        for i in range(k):
            cp = pltpu.make_async_remote_copy(
                src_ref=ins[i], dst_ref=outs[i], send_sem=send_sems.at[i], recv_sem=recv_sems.at[i],
                device_id=(x, y, 1 - c), device_id_type=pl.DeviceIdType.MESH)
            cp.start()
            cps.append(cp)
        for cp in cps:
            cp.wait()

    return pl.pallas_call(
        body, name=name, in_specs=[_ANY] * k, out_specs=[_ANY] * k,
        out_shape=[_sds(a.shape, a.dtype) for a in arrs],
        scratch_shapes=[pltpu.SemaphoreType.DMA((k,)), pltpu.SemaphoreType.DMA((k,))])(*arrs)


def _all_gather_devices(a, name):
    def body(a_ref, out_ref, send_sems, recv_sems, local_sem):
        x, y, c = _mesh_pos()
        me = 4 * x + 2 * y + c
        mine = pltpu.make_async_copy(a_ref, out_ref.at[me], local_sem)
        mine.start()
        sends = []
        peers = []
        for j in range(1, N_DEV):
            fx, fy, fc = (j >> 2) & 1, (j >> 1) & 1, j & 1
            px = jnp.where(fx == 1, 1 - x, x)
            py = jnp.where(fy == 1, 1 - y, y)
            pc = jnp.where(fc == 1, 1 - c, c)
            peers.append((px, py, pc))
            cp = pltpu.make_async_remote_copy(
                src_ref=a_ref, dst_ref=out_ref.at[me], send_sem=send_sems.at[j - 1], recv_sem=recv_sems.at[j - 1],
                device_id=(px, py, pc), device_id_type=pl.DeviceIdType.MESH)
            cp.start()
            sends.append(cp)
        for j, (px, py, pc) in enumerate(peers):
            pltpu.make_async_remote_copy(
                src_ref=a_ref, dst_ref=out_ref.at[4 * px + 2 * py + pc], send_sem=send_sems.at[j],
                recv_sem=recv_sems.at[j], device_id=(px, py, pc), device_id_type=pl.DeviceIdType.MESH).wait_recv()
        for cp in sends:
            cp.wait_send()
        mine.wait()

    return pl.pallas_call(
        body, name=name, in_specs=[_ANY], out_specs=_ANY, out_shape=_sds((N_DEV,) + a.shape, a.dtype),
        scratch_shapes=[pltpu.SemaphoreType.DMA((N_DEV - 1,)), pltpu.SemaphoreType.DMA((N_DEV - 1,)),
                        pltpu.SemaphoreType.DMA])(a)


def _adamw(w, g, m, v):
    m = ADAM_B1 * m + (1.0 - ADAM_B1) * g
    v = ADAM_B2 * v + (1.0 - ADAM_B2) * (g * g)
    m_hat = m / (1.0 - ADAM_B1 ** ADAM_STEP)
    v_hat = v / (1.0 - ADAM_B2 ** ADAM_STEP)
    delta = -ADAM_LR * (m_hat / (jnp.sqrt(v_hat) + ADAM_EPS) + ADAM_WD * w)
    return delta, m, v


def _row_tile(rows, cols, n_arrays):
    budget = VMEM_LIMIT_BYTES // 4
    padded = -(-cols // LANES) * LANES
    tr = max(8, budget // (2 * n_arrays * padded * 4))
    tr = min(rows, 1 << (tr.bit_length() - 1))
    while rows % tr:
        tr //= 2
    return tr


def _sum_chips(landing, name):
    _, r, c = landing.shape
    tr = _row_tile(r, c, 5)

    def body(l_ref, s_ref):
        s_ref[...] = ((l_ref[0] + l_ref[1]) + l_ref[2]) + l_ref[3]

    return pl.pallas_call(
        body, name=name, grid=(r // tr,),
        in_specs=[pl.BlockSpec((N_CHIPS, tr, c), lambda i: (0, i, 0))],
        out_specs=_rows(tr, c), out_shape=_sds((r, c), F32),
        compiler_params=_params())(landing)


def _adamw_big(mine, sibling, w, m, v, name):
    r, c = w.shape
    tr = _row_tile(r, c, 9)

    def body(a_ref, b_ref, w_ref, m_ref, v_ref, g_out, d_out, m_out, v_out):
        g = a_ref[...] + b_ref[...]
        g_out[...] = g
        d_out[...], m_out[...], v_out[...] = _adamw(w_ref[...], g, m_ref[...], v_ref[...])

    return pl.pallas_call(
        body, name=name, grid=(r // tr,),
        in_specs=[_rows(tr, c)] * 5, out_specs=[_rows(tr, c)] * 4, out_shape=[_sds((r, c), F32)] * 4,
        compiler_params=_params())(mine, sibling, w, m, v)


def _adamw_small(gathered, w, m, v):
    r = w.shape[0]
    tr = SMALL_ROW_TILE

    def body(a_ref, w_ref, m_ref, v_ref, g_out, d_out, m_out, v_out):
        g = a_ref[0]
        for d in range(1, N_DEV):
            g = g + a_ref[d]
        g_out[...] = g
        d_out[...], m_out[...], v_out[...] = _adamw(w_ref[...], g, m_ref[...], v_ref[...])

    return pl.pallas_call(
        body, name="adamw_small", grid=(r // tr,),
        in_specs=[pl.BlockSpec((N_DEV, tr, LANES), lambda i: (0, i, 0))] + [_rows(tr, LANES)] * 3,
        out_specs=[_rows(tr, LANES)] * 4, out_shape=[_sds((r, LANES), F32)] * 4,
        compiler_params=_params())(gathered, w, m, v)


def _to_shard_major(full, axis):
    d, k, n = full.shape
    if axis == 2:
        return full.reshape(d, k, N_CHIPS, n // N_CHIPS).transpose(2, 0, 1, 3)
    return full.reshape(d, N_CHIPS, k // N_CHIPS, n).transpose(1, 0, 2, 3)


def _from_shard_major(parts, axis):
    _, d, k, n = parts.shape
    if axis == 2:
        return parts.transpose(1, 2, 0, 3).reshape(d, k, N_CHIPS * n)
    return parts.transpose(1, 0, 2, 3).reshape(d, N_CHIPS * k, n)


def _pack_small(params):
    pieces = []
    for name in SMALL_NAMES:
        flat = params[name].reshape(-1)
        pad = -flat.shape[0] % SMALL_PAD
        pieces.append(jnp.pad(flat, (0, pad)))
    flat = jnp.concatenate(pieces)
    pad = -flat.shape[0] % (SMALL_ROW_TILE * LANES)
    return jnp.pad(flat, (0, pad)).reshape(-1, LANES)


def _unpack_small(packed, like):
    flat = packed.reshape(-1)
    out, off = {}, 0
    for name in SMALL_NAMES:
        size = like[name].size
        out[name] = flat[off:off + size].reshape(like[name].shape)
        off += size + (-size % SMALL_PAD)
    return out


def kernel(x, p, ln_mix_pre, w_in, attn_sinks, gm_ln_g, gm_ln_b, gm_ws, gm_bs, g_attn_out, g_gm_out, w_out, ln_mix_post, ln_ffn_pre, w_ffn_gate, w_ffn_up, w_ffn_down, ln_ffn_post, w_ple, ln_ple_gate, w_ple_gate, loss_target, m_ln_mix_pre, m_w_in, m_attn_sinks, m_gm_ln_g, m_gm_ln_b, m_gm_ws, m_gm_bs, m_g_attn_out, m_g_gm_out, m_w_out, m_ln_mix_post, m_ln_ffn_pre, m_w_ffn_gate, m_w_ffn_up, m_w_ffn_down, m_ln_ffn_post, m_w_ple, m_ln_ple_gate, m_w_ple_gate, v_ln_mix_pre, v_w_in, v_attn_sinks, v_gm_ln_g, v_gm_ln_b, v_gm_ws, v_gm_bs, v_g_attn_out, v_g_gm_out, v_w_out, v_ln_mix_post, v_ln_ffn_pre, v_w_ffn_gate, v_w_ffn_up, v_w_ffn_down, v_ln_ffn_post, v_w_ple, v_ln_ple_gate, v_w_ple_gate):
    given = dict(locals())
    wts = {n: given[n] for n in WEIGHTS}
    mom = {n: given["m_" + n] for n in WEIGHTS}
    var = {n: given["v_" + n] for n in WEIGHTS}
    depth = w_in.shape[0]
    h = x[0]
    target = loss_target[0]

    gathered = _chip_exchange([wts[n].astype(BF16) for n in BIG_NAMES], scatter=False, name="gather_weights")
    full = {n: _from_shard_major(g, BIG[n]) for n, g in zip(BIG_NAMES, gathered)}

    row = lambda a, i: a[i][None, :]
    bs_full = [jnp.repeat(gm_bs[i].T, HEAD_DIM, axis=1) for i in range(depth)]

    saved = []
    for i in range(depth):
        z, a = _f1_norm_in(h, row(ln_mix_pre, i), full['w_in'][i])
        am = _f2_attn_gm(z, attn_sinks[i], row(gm_ln_g, i), row(gm_ln_b, i), gm_ws[i], bs_full[i])
        heads, mix, h1 = _f3_mix_out(am, h, row(g_attn_out, i), row(g_gm_out, i), full['w_out'][i],
                                     row(ln_mix_post, i))
        f, gt, up = _f4a_ffn_in(h1, row(ln_ffn_pre, i), full['w_ffn_gate'][i], full['w_ffn_up'][i])
        dn, h2 = _f4b_ffn_out(gt, up, full['w_ffn_down'][i], h1, row(ln_ffn_post, i))
        r, pg, pe, h3 = _f5_ple(h2, row(ln_ple_gate, i), full['w_ple_gate'][i], p[i, 0], full['w_ple'][i])
        saved.append(dict(h=h, z=z, a=a, am=am, heads=heads, mix=mix, h1=h1, f=f, gt=gt, up=up, dn=dn, h2=h2,
                          r=r, pg=pg, pe=pe))
        h = h3

    sq, dh = _loss_head(h, target)
    loss = lax.psum(0.5 / D_MODEL * sq[0, 0], ("x", "y", "c"))

    big_grads = {n: [None] * depth for n in BIG_NAMES}
    small_grads = {n: [None] * depth for n in SMALL_NAMES}
    for i in reversed(range(depth)):
        s = saved[i]
        dpe, dpg, dh2, dg = _b5_ple(dh, s['h2'], s['pg'], s['pe'], row(ln_ple_gate, i), full['w_ple_gate'][i])
        small_grads['ln_ple_gate'][i] = dg[0]
        big_grads['w_ple'][i] = _weight_grad(p[i, 0], dpe, "dw_ple")
        big_grads['w_ple_gate'][i] = _weight_grad(s['r'], dpg, "dw_ple_gate")

        ddn, act, dgt, dup, dg = _b4a_ffn_out(dh2, s['dn'], row(ln_ffn_post, i), full['w_ffn_down'][i], s['gt'], s['up'])
        small_grads['ln_ffn_post'][i] = dg[0]
        big_grads['w_ffn_down'][i] = _weight_grad(act, ddn, "dw_ffn_down")
        big_grads['w_ffn_gate'][i] = _weight_grad(s['f'], dgt, "dw_ffn_gate")
        big_grads['w_ffn_up'][i] = _weight_grad(s['f'], dup, "dw_ffn_up")
        dh1, dg = _b4b_ffn_in(dgt, dup, full['w_ffn_gate'][i], full['w_ffn_up'][i], s['h1'], row(ln_ffn_pre, i), dh2)
        small_grads['ln_ffn_pre'][i] = dg[0]

        dmix, dam, dgp, dga, dgg = _b3_mix_out(dh1, s['mix'], row(ln_mix_post, i), full['w_out'][i], s['am'],
                                               row(g_attn_out, i), row(g_gm_out, i))
        small_grads['ln_mix_post'][i] = dgp[0]
        small_grads['g_attn_out'][i] = dga[0]
        small_grads['g_gm_out'][i] = dgg[0]
        big_grads['w_out'][i] = _weight_grad(s['heads'], dmix, "dw_out")

        dzq, dkv, dzuv, dsink, dlng, dlnb, dws, dbs = _b2_attn_gm(
            dam, s['z'], attn_sinks[i], row(gm_ln_g, i), row(gm_ln_b, i), gm_ws[i], bs_full[i])
        small_grads['attn_sinks'][i] = dsink[0, :N_Q_HEADS]
        small_grads['gm_ln_g'][i] = dlng[0]
        small_grads['gm_ln_b'][i] = dlnb[0]
        small_grads['gm_ws'][i] = dws
        small_grads['gm_bs'][i] = dbs[:, :N_Q_HEADS].T

        dz, dh, dg = _b1_norm_in(dzq, dkv, dzuv, full['w_in'][i], s['h'], row(ln_mix_pre, i), dh1)
        small_grads['ln_mix_pre'][i] = dg[0]
        big_grads['w_in'][i] = _weight_grad(s['a'], dz, "dw_in")
    grad_x = dh[None]

    parts = [_to_shard_major(jnp.stack(big_grads[n]), BIG[n]) for n in BIG_NAMES]
    landing = _chip_exchange(parts, scatter=True, name="scatter_grads")
    flat2 = lambda a: a.reshape(-1, a.shape[-1])
    partial = [_sum_chips(l.reshape(N_CHIPS, -1, l.shape[-1]), "sum_chips_" + n) for n, l in zip(BIG_NAMES, landing)]
    sibling = _sibling_exchange(partial, name="sibling_grads")
    out = {k: {} for k in ("grad", "delta", "m", "v")}
    for n, mine, sib in zip(BIG_NAMES, partial, sibling):
        res = _adamw_big(mine, sib, flat2(wts[n]), flat2(mom[n]), flat2(var[n]), "adamw_" + n)
        for k, a in zip(("grad", "delta", "m", "v"), res):
            out[k][n] = a.reshape(wts[n].shape)

    small = {n: jnp.stack(small_grads[n]) for n in SMALL_NAMES}
    gathered_small = _all_gather_devices(_pack_small(small), name="gather_small_grads")
    res = _adamw_small(gathered_small, _pack_small(wts), _pack_small(mom), _pack_small(var))
    for k, a in zip(("grad", "delta", "m", "v"), res):
        out[k].update(_unpack_small(a, wts))

    return (loss, grad_x, *[out["grad"][n] for n in WEIGHTS], *[out["delta"][n] for n in WEIGHTS],
            *[out["m"][n] for n in WEIGHTS], *[out["v"][n] for n in WEIGHTS])
```

```python
import functools
import math

import jax
import jax.numpy as jnp
from jax import lax
from jax.experimental import pallas as pl
from jax.experimental.pallas import tpu as pltpu

F32 = jnp.float32
BF16 = jnp.bfloat16

D_MODEL = 1024
HEAD_DIM = 64
N_Q_HEADS = 8
BLK = 128
ATTN_W = 512
KV_W = 128
GM_W = 512
D_IN = ATTN_W + 2 * KV_W + 2 * GM_W
D_FF = 2816
PLE_DIM = 256
DEPTH = 4
NORM_EPS = 1e-6
NEG_BIG = -1e30
N_CHIPS = 4
N_DEV = 8

ADAM_LR = 0.001
ADAM_B1 = 0.9
ADAM_B2 = 0.999
ADAM_EPS = 1e-08
ADAM_WD = 0.01
ADAM_STEP = 10

VMEM_LIMIT_BYTES = 56 * 1024 * 1024
LANES = 128
GELU_C0 = math.sqrt(2.0 / math.pi)
GELU_C1 = 0.044715
ALIBI_SLOPES = tuple(2.0 ** (-8.0 * (h + 1.0) / N_Q_HEADS) for h in range(N_Q_HEADS))

WEIGHTS = ['ln_mix_pre', 'w_in', 'attn_sinks', 'gm_ln_g', 'gm_ln_b', 'gm_ws', 'gm_bs', 'g_attn_out',
           'g_gm_out', 'w_out', 'ln_mix_post', 'ln_ffn_pre', 'w_ffn_gate', 'w_ffn_up', 'w_ffn_down',
           'ln_ffn_post', 'w_ple', 'ln_ple_gate', 'w_ple_gate']
BIG = {'w_in': 2, 'w_out': 1, 'w_ffn_gate': 2, 'w_ffn_up': 2, 'w_ffn_down': 1, 'w_ple': 2, 'w_ple_gate': 1}
BIG_NAMES = list(BIG)
SMALL_NAMES = [n for n in WEIGHTS if n not in BIG]
SMALL_PAD = 1024


def _nt(a, b):
    return lax.dot_general(a, b, (((1,), (1,)), ((), ())), preferred_element_type=F32)


def _tn(a, b):
    return lax.dot_general(a, b, (((0,), (0,)), ((), ())), preferred_element_type=F32)


def _mm(a, b):
    return jnp.dot(a, b, preferred_element_type=F32)


def _rms(x, g):
    r = lax.rsqrt(jnp.mean(x * x, axis=-1, keepdims=True) + NORM_EPS)
    return x * r * g


def _rms_bwd(dy, x, g):
    r = lax.rsqrt(jnp.mean(x * x, axis=-1, keepdims=True) + NORM_EPS)
    xh = x * r
    dg = jnp.sum(dy * xh, axis=0, keepdims=True)
    dxh = dy * g
    dx = r * (dxh - xh * jnp.mean(dxh * xh, axis=-1, keepdims=True))
    return dx, dg


def _gelu(x):
    return 0.5 * x * (1.0 + jnp.tanh(GELU_C0 * (x + GELU_C1 * x * x * x)))


def _gelu_grad(x):
    t = jnp.tanh(GELU_C0 * (x + GELU_C1 * x * x * x))
    return 0.5 * (1.0 + t) + 0.5 * x * (1.0 - t * t) * GELU_C0 * (1.0 + 3.0 * GELU_C1 * x * x)


def _sigmoid(x):
    return 1.0 / (1.0 + jnp.exp(-x))


def _rows(tm, n):
    return pl.BlockSpec((tm, n), lambda i: (i, 0))


def _whole(shape):
    return pl.BlockSpec(shape, lambda i: (0,) * len(shape))


def _accumulate(ref, val):
    @pl.when(pl.program_id(0) == 0)
    def _():
        ref[...] = jnp.zeros_like(ref)

    ref[...] += val


def _params(n_axes=1):
    return pltpu.CompilerParams(dimension_semantics=("arbitrary",) * n_axes,
                                vmem_limit_bytes=VMEM_LIMIT_BYTES)


def _sds(shape, dtype):
    return jax.ShapeDtypeStruct(shape, dtype)


def _tile(t, want):
    return min(t, want)


def _f1_norm_in(h, g, w):
    t = h.shape[0]
    tm = _tile(t, 512)

    def body(h_ref, g_ref, w_ref, z_ref, a_ref):
        a = _rms(h_ref[...], g_ref[...]).astype(BF16)
        a_ref[...] = a
        z_ref[...] = _mm(a, w_ref[...])

    return pl.pallas_call(
        body, name="f1_norm_in", grid=(t // tm,),
        in_specs=[_rows(tm, D_MODEL), _whole((1, D_MODEL)), _whole((D_MODEL, D_IN))],
        out_specs=[_rows(tm, D_IN), _rows(tm, D_MODEL)],
        out_shape=[_sds((t, D_IN), F32), _sds((t, D_MODEL), BF16)],
        compiler_params=_params())(h, g, w)


def _attn_geometry(n):
    ti = lax.broadcasted_iota(jnp.int32, (BLK, 2 * BLK), 0)
    ji = lax.broadcasted_iota(jnp.int32, (BLK, 2 * BLK), 1)
    dist = ti + BLK - ji
    valid = (dist >= 0) & (dist < BLK) & ((n > 0) | (ji >= BLK))
    return valid, dist.astype(F32)


def _attn_probs(qm, kuse, valid, distf, slope, sink):
    sc = _nt(qm, kuse) * (HEAD_DIM ** -0.5)
    sc = jnp.where(valid, sc - slope * distf, NEG_BIG)
    m = jnp.maximum(jnp.max(sc, axis=1, keepdims=True), sink)
    e = jnp.exp(sc - m)
    es = jnp.exp(sink - m)
    inv = 1.0 / (jnp.sum(e, axis=1, keepdims=True) + es)
    return e * inv, es * inv


def _kv_window(z_ref, kp_ref, vp_ref):
    kcat = jnp.concatenate([kp_ref[...], z_ref[:, ATTN_W:ATTN_W + KV_W]], axis=0)
    vcat = jnp.concatenate([vp_ref[...], z_ref[:, ATTN_W + KV_W:ATTN_W + 2 * KV_W]], axis=0)
    kswap = pltpu.roll(kcat, HEAD_DIM, 1)
    vswap = pltpu.roll(vcat, HEAD_DIM, 1)
    return kcat.astype(BF16), kswap.astype(BF16), vcat, vswap


def _gm_forward(z_ref, lng_ref, lnb_ref):
    zu = z_ref[:, ATTN_W + 2 * KV_W:ATTN_W + 2 * KV_W + GM_W]
    zv = z_ref[:, ATTN_W + 2 * KV_W + GM_W:D_IN]
    u = _gelu(zu)
    gv = _gelu(zv)
    xc = gv - jnp.mean(gv, axis=-1, keepdims=True)
    rstd = lax.rsqrt(jnp.mean(xc * xc, axis=-1, keepdims=True) + NORM_EPS)
    xhat = xc * rstd
    vn = xhat * lng_ref[...] + lnb_ref[...]
    return zu, zv, u, xhat, rstd, vn


def _tril_w(ws_ref, h):
    ti = lax.broadcasted_iota(jnp.int32, (BLK, BLK), 0)
    si = lax.broadcasted_iota(jnp.int32, (BLK, BLK), 1)
    causal = si <= ti
    return jnp.where(causal, ws_ref[h], 0.0).astype(BF16), causal


def _gm_mixed(vn, ws_ref, bs_ref, lo, hi):
    slabs = []
    for s in range(GM_W // LANES):
        vs = vn[:, s * LANES:(s + 1) * LANES]
        w0, _ = _tril_w(ws_ref, 2 * s)
        w1, _ = _tril_w(ws_ref, 2 * s + 1)
        mixed = (_mm(w0, jnp.where(lo, vs, 0.0).astype(BF16))
                 + _mm(w1, jnp.where(hi, vs, 0.0).astype(BF16))
                 + bs_ref[:, s * LANES:(s + 1) * LANES])
        slabs.append(mixed)
    return slabs


def _block_specs_z(nb):
    prev = lambda i: (jnp.maximum(i - 1, 0), ATTN_W // KV_W)
    prev_v = lambda i: (jnp.maximum(i - 1, 0), ATTN_W // KV_W + 1)
    return [_rows(BLK, D_IN), pl.BlockSpec((BLK, KV_W), prev), pl.BlockSpec((BLK, KV_W), prev_v)]


def _f2_attn_gm(z, sinks, ln_g, ln_b, ws, bs_full):
    t = z.shape[0]
    nb = t // BLK

    def body(z_ref, kp_ref, vp_ref, sink_ref, lng_ref, lnb_ref, ws_ref, bs_ref, am_ref):
        n = pl.program_id(0)
        lane = lax.broadcasted_iota(jnp.int32, (1, LANES), 1)
        lo = lane < HEAD_DIM
        hi = lane >= HEAD_DIM
        valid, distf = _attn_geometry(n)
        kc, ks, vcat, vswap = _kv_window(z_ref, kp_ref, vp_ref)
        for s in range(ATTN_W // LANES):
            kv_head = s // 2
            qs = z_ref[:, s * LANES:(s + 1) * LANES]
            o = jnp.zeros((BLK, LANES), F32)
            for half in range(2):
                hq = 2 * s + half
                mask = lo if half == 0 else hi
                same = kv_head == half
                qm = jnp.where(mask, qs, 0.0).astype(BF16)
                pr, _ = _attn_probs(qm, kc if same else ks, valid, distf, ALIBI_SLOPES[hq], sink_ref[hq])
                vm = jnp.where(mask, vcat if same else vswap, 0.0).astype(BF16)
                o = o + _mm(pr.astype(BF16), vm)
            am_ref[:, s * LANES:(s + 1) * LANES] = o
        _, _, u, _, _, vn = _gm_forward(z_ref, lng_ref, lnb_ref)
        mixed = _gm_mixed(vn, ws_ref, bs_ref, lo, hi)
        for s in range(GM_W // LANES):
            am_ref[:, ATTN_W + s * LANES:ATTN_W + (s + 1) * LANES] = u[:, s * LANES:(s + 1) * LANES] * mixed[s]

    return pl.pallas_call(
        body, name="f2_attn_gm", grid=(nb,),
        in_specs=_block_specs_z(nb) + [
            pl.BlockSpec(memory_space=pltpu.SMEM), _whole((1, GM_W)), _whole((1, GM_W)),
            _whole((N_Q_HEADS, BLK, BLK)), _whole((BLK, GM_W))],
        out_specs=_rows(BLK, ATTN_W + GM_W),
        out_shape=_sds((t, ATTN_W + GM_W), F32),
        compiler_params=_params())(z, z, z, sinks, ln_g, ln_b, ws, bs_full)


def _f3_mix_out(am, h, ga, gg, w, gpost):
    t = h.shape[0]
    tm = _tile(t, 512)

    def body(am_ref, h_ref, ga_ref, gg_ref, w_ref, gp_ref, heads_ref, mix_ref, h1_ref):
        heads = jnp.concatenate([_rms(am_ref[:, :ATTN_W], ga_ref[...]),
                                 _rms(am_ref[:, ATTN_W:], gg_ref[...])], axis=1).astype(BF16)
        heads_ref[...] = heads
        mix = _mm(heads, w_ref[...])
        mix_ref[...] = mix
        h1_ref[...] = h_ref[...] + _rms(mix, gp_ref[...])

    return pl.pallas_call(
        body, name="f3_mix_out", grid=(t // tm,),
        in_specs=[_rows(tm, D_MODEL), _rows(tm, D_MODEL), _whole((1, ATTN_W)), _whole((1, GM_W)),
                  _whole((D_MODEL, D_MODEL)), _whole((1, D_MODEL))],
        out_specs=[_rows(tm, D_MODEL)] * 3,
        out_shape=[_sds((t, D_MODEL), BF16), _sds((t, D_MODEL), F32), _sds((t, D_MODEL), F32)],
        compiler_params=_params())(am, h, ga, gg, w, gpost)


def _f4a_ffn_in(h1, gf, wg, wu):
    t = h1.shape[0]
    tm = _tile(t, 256)

    def body(h_ref, g_ref, wg_ref, wu_ref, f_ref, gt_ref, up_ref):
        f = _rms(h_ref[...], g_ref[...]).astype(BF16)
        f_ref[...] = f
        gt_ref[...] = _mm(f, wg_ref[...])
        up_ref[...] = _mm(f, wu_ref[...])

    return pl.pallas_call(
        body, name="f4a_ffn_in", grid=(t // tm,),
        in_specs=[_rows(tm, D_MODEL), _whole((1, D_MODEL)), _whole((D_MODEL, D_FF)), _whole((D_MODEL, D_FF))],
        out_specs=[_rows(tm, D_MODEL), _rows(tm, D_FF), _rows(tm, D_FF)],
        out_shape=[_sds((t, D_MODEL), BF16), _sds((t, D_FF), F32), _sds((t, D_FF), F32)],
        compiler_params=_params())(h1, gf, wg, wu)


def _f4b_ffn_out(gt, up, wd, h1, gfp):
    t = h1.shape[0]
    tm = _tile(t, 256)

    def body(gt_ref, up_ref, wd_ref, h_ref, g_ref, dn_ref, h2_ref):
        gt = gt_ref[...]
        act = (gt * _sigmoid(gt) * up_ref[...]).astype(BF16)
        dn = _mm(act, wd_ref[...])
        dn_ref[...] = dn
        h2_ref[...] = h_ref[...] + _rms(dn, g_ref[...])

    return pl.pallas_call(
        body, name="f4b_ffn_out", grid=(t // tm,),
        in_specs=[_rows(tm, D_FF), _rows(tm, D_FF), _whole((D_FF, D_MODEL)), _rows(tm, D_MODEL),
                  _whole((1, D_MODEL))],
        out_specs=[_rows(tm, D_MODEL)] * 2,
        out_shape=[_sds((t, D_MODEL), F32)] * 2,
        compiler_params=_params())(gt, up, wd, h1, gfp)


def _f5_ple(h2, gpl, wpg, p, wple):
    t = h2.shape[0]
    tm = _tile(t, 512)

    def body(h_ref, g_ref, wpg_ref, p_ref, wple_ref, r_ref, pg_ref, pe_ref, h3_ref):
        h = h_ref[...]
        r = _rms(h, g_ref[...]).astype(BF16)
        r_ref[...] = r
        pg = _mm(r, wpg_ref[...])
        pe = _mm(p_ref[...].astype(BF16), wple_ref[...])
        pg_ref[...] = pg
        pe_ref[...] = pe
        h3_ref[...] = h + pe * _sigmoid(pg)

    return pl.pallas_call(
        body, name="f5_ple", grid=(t // tm,),
        in_specs=[_rows(tm, D_MODEL), _whole((1, D_MODEL)), _whole((D_MODEL, D_MODEL)), _rows(tm, PLE_DIM),
                  _whole((PLE_DIM, D_MODEL))],
        out_specs=[_rows(tm, D_MODEL)] * 4,
        out_shape=[_sds((t, D_MODEL), BF16)] + [_sds((t, D_MODEL), F32)] * 3,
        compiler_params=_params())(h2, gpl, wpg, p, wple)


def _loss_head(y, target):
    t = y.shape[0]
    tm = _tile(t, 512)

    def body(y_ref, t_ref, sq_ref, dy_ref):
        err = y_ref[...] - t_ref[...]
        dy_ref[...] = err * (1.0 / D_MODEL)
        _accumulate(sq_ref, jnp.sum(err * err, keepdims=True))

    return pl.pallas_call(
        body, name="loss_head", grid=(t // tm,),
        in_specs=[_rows(tm, D_MODEL)] * 2,
        out_specs=[_whole((1, LANES)), _rows(tm, D_MODEL)],
        out_shape=[_sds((1, LANES), F32), _sds((t, D_MODEL), F32)],
        compiler_params=_params())(y, target)


def _b5_ple(dh3, h2, pg, pe, gpl, wpg):
    t = h2.shape[0]
    tm = _tile(t, 512)

    def body(dh_ref, h_ref, pg_ref, pe_ref, g_ref, w_ref, dpe_ref, dpg_ref, dh2_ref, dg_ref):
        dh = dh_ref[...]
        s = _sigmoid(pg_ref[...])
        dpe_ref[...] = (dh * s).astype(BF16)
        dpg = (dh * pe_ref[...] * s * (1.0 - s)).astype(BF16)
        dpg_ref[...] = dpg
        dx, dg = _rms_bwd(_nt(dpg, w_ref[...]), h_ref[...], g_ref[...])
        dh2_ref[...] = dh + dx
        _accumulate(dg_ref, dg)

    return pl.pallas_call(
        body, name="b5_ple", grid=(t // tm,),
        in_specs=[_rows(tm, D_MODEL)] * 4 + [_whole((1, D_MODEL)), _whole((D_MODEL, D_MODEL))],
        out_specs=[_rows(tm, D_MODEL)] * 3 + [_whole((1, D_MODEL))],
        out_shape=[_sds((t, D_MODEL), BF16)] * 2 + [_sds((t, D_MODEL), F32), _sds((1, D_MODEL), F32)],
        compiler_params=_params())(dh3, h2, pg, pe, gpl, wpg)


def _b4a_ffn_out(dh2, dn, gfp, wd, gt, up):
    t = dh2.shape[0]
    tm = _tile(t, 256)

    def body(dh_ref, dn_ref, g_ref, wd_ref, gt_ref, up_ref, ddn_ref, act_ref, dgt_ref, dup_ref, dg_ref):
        ddn, dg = _rms_bwd(dh_ref[...], dn_ref[...], g_ref[...])
        _accumulate(dg_ref, dg)
        ddn = ddn.astype(BF16)
        ddn_ref[...] = ddn
        dact = _nt(ddn, wd_ref[...])
        gt = gt_ref[...]
        up = up_ref[...]
        sg = _sigmoid(gt)
        silu = gt * sg
        act_ref[...] = (silu * up).astype(BF16)
        dup_ref[...] = (dact * silu).astype(BF16)
        dgt_ref[...] = (dact * up * (sg * (1.0 + gt * (1.0 - sg)))).astype(BF16)

    return pl.pallas_call(
        body, name="b4a_ffn_out", grid=(t // tm,),
        in_specs=[_rows(tm, D_MODEL), _rows(tm, D_MODEL), _whole((1, D_MODEL)), _whole((D_FF, D_MODEL)),
                  _rows(tm, D_FF), _rows(tm, D_FF)],
        out_specs=[_rows(tm, D_MODEL), _rows(tm, D_FF), _rows(tm, D_FF), _rows(tm, D_FF), _whole((1, D_MODEL))],
        out_shape=[_sds((t, D_MODEL), BF16)] + [_sds((t, D_FF), BF16)] * 3 + [_sds((1, D_MODEL), F32)],
        compiler_params=_params())(dh2, dn, gfp, wd, gt, up)


def _b4b_ffn_in(dgt, dup, wg, wu, h1, gf, dh2):
    t = h1.shape[0]
    tm = _tile(t, 512)

    def body(dgt_ref, dup_ref, wg_ref, wu_ref, h_ref, g_ref, dh_ref, dh1_ref, dg_ref):
        df = _nt(dgt_ref[...], wg_ref[...]) + _nt(dup_ref[...], wu_ref[...])
        dx, dg = _rms_bwd(df, h_ref[...], g_ref[...])
        dh1_ref[...] = dh_ref[...] + dx
        _accumulate(dg_ref, dg)

    return pl.pallas_call(
        body, name="b4b_ffn_in", grid=(t // tm,),
        in_specs=[_rows(tm, D_FF), _rows(tm, D_FF), _whole((D_MODEL, D_FF)), _whole((D_MODEL, D_FF)),
                  _rows(tm, D_MODEL), _whole((1, D_MODEL)), _rows(tm, D_MODEL)],
        out_specs=[_rows(tm, D_MODEL), _whole((1, D_MODEL))],
        out_shape=[_sds((t, D_MODEL), F32), _sds((1, D_MODEL), F32)],
        compiler_params=_params())(dgt, dup, wg, wu, h1, gf, dh2)


def _b3_mix_out(dh1, mix, gpost, w, am, ga, gg):
    t = dh1.shape[0]
    tm = _tile(t, 512)

    def body(dh_ref, mix_ref, gp_ref, w_ref, am_ref, ga_ref, gg_ref, dmix_ref, dam_ref, dgp_ref, dga_ref, dgg_ref):
        dmix, dgp = _rms_bwd(dh_ref[...], mix_ref[...], gp_ref[...])
        _accumulate(dgp_ref, dgp)
        dmix = dmix.astype(BF16)
        dmix_ref[...] = dmix
        dheads = _nt(dmix, w_ref[...])
        dat, dga = _rms_bwd(dheads[:, :ATTN_W], am_ref[:, :ATTN_W], ga_ref[...])
        dgm, dgg = _rms_bwd(dheads[:, ATTN_W:], am_ref[:, ATTN_W:], gg_ref[...])
        dam_ref[:, :ATTN_W] = dat
        dam_ref[:, ATTN_W:] = dgm
        _accumulate(dga_ref, dga)
        _accumulate(dgg_ref, dgg)

    return pl.pallas_call(
        body, name="b3_mix_out", grid=(t // tm,),
        in_specs=[_rows(tm, D_MODEL), _rows(tm, D_MODEL), _whole((1, D_MODEL)), _whole((D_MODEL, D_MODEL)),
                  _rows(tm, D_MODEL), _whole((1, ATTN_W)), _whole((1, GM_W))],
        out_specs=[_rows(tm, D_MODEL), _rows(tm, D_MODEL), _whole((1, D_MODEL)), _whole((1, ATTN_W)),
                   _whole((1, GM_W))],
        out_shape=[_sds((t, D_MODEL), BF16), _sds((t, D_MODEL), F32), _sds((1, D_MODEL), F32),
                   _sds((1, ATTN_W), F32), _sds((1, GM_W), F32)],
        compiler_params=_params())(dh1, mix, gpost, w, am, ga, gg)


def _b2_attn_gm(dam, z, sinks, ln_g, ln_b, ws, bs_full):
    t = z.shape[0]
    nb = t // BLK

    def body(dam_ref, z_ref, kp_ref, vp_ref, sink_ref, lng_ref, lnb_ref, ws_ref, bs_ref,
             dzq_ref, dkv_ref, dzuv_ref, dsink_ref, dlng_ref, dlnb_ref, dws_ref, dbs_ref):
        n = pl.program_id(0)
        lane = lax.broadcasted_iota(jnp.int32, (1, LANES), 1)
        lo = lane < HEAD_DIM
        hi = lane >= HEAD_DIM
        valid, distf = _attn_geometry(n)
        kc, ks, vcat, vswap = _kv_window(z_ref, kp_ref, vp_ref)
        vc = vcat.astype(BF16)
        vs_ = vswap.astype(BF16)
        dk_acc = jnp.zeros((2 * BLK, LANES), F32)
        dv_acc = jnp.zeros((2 * BLK, LANES), F32)
        dsink = jnp.zeros((1, LANES), F32)
        for s in range(ATTN_W // LANES):
            kv_head = s // 2
            qs = z_ref[:, s * LANES:(s + 1) * LANES]
            dos = dam_ref[:, s * LANES:(s + 1) * LANES]
            dq = jnp.zeros((BLK, LANES), F32)
            for half in range(2):
                hq = 2 * s + half
                mask = lo if half == 0 else hi
                same = kv_head == half
                kuse = kc if same else ks
                vuse = vc if same else vs_
                qm = jnp.where(mask, qs, 0.0).astype(BF16)
                dom = jnp.where(mask, dos, 0.0).astype(BF16)
                pr, ps = _attn_probs(qm, kuse, valid, distf, ALIBI_SLOPES[hq], sink_ref[hq])
                dpr = _nt(dom, vuse)
                row = jnp.sum(dpr * pr, axis=1, keepdims=True)
                ds = (pr * (dpr - row) * (HEAD_DIM ** -0.5)).astype(BF16)
                dsink = dsink + jnp.where(lane == hq, -jnp.sum(ps * row, keepdims=True), 0.0)
                dq = dq + jnp.where(mask, _mm(ds, kuse), 0.0)
                dkc = _tn(ds, qm)
                dvc = _tn(pr.astype(BF16), dom)
                if not same:
                    dkc = pltpu.roll(dkc, HEAD_DIM, 1)
                    dvc = pltpu.roll(dvc, HEAD_DIM, 1)
                dk_acc = dk_acc + dkc
                dv_acc = dv_acc + dvc
            dzq_ref[:, s * LANES:(s + 1) * LANES] = dq.astype(BF16)
        cur = pl.multiple_of(n * BLK, BLK)
        dkv_ref[pl.ds(cur, BLK), 0:KV_W] = dk_acc[BLK:, :]
        dkv_ref[pl.ds(cur, BLK), KV_W:2 * KV_W] = dv_acc[BLK:, :]

        @pl.when(n > 0)
        def _():
            prv = pl.multiple_of((n - 1) * BLK, BLK)
            dkv_ref[pl.ds(prv, BLK), 0:KV_W] += dk_acc[:BLK, :]
            dkv_ref[pl.ds(prv, BLK), KV_W:2 * KV_W] += dv_acc[:BLK, :]

        _accumulate(dsink_ref, dsink)

        zu, zv, u, xhat, rstd, vn = _gm_forward(z_ref, lng_ref, lnb_ref)
        mixed = _gm_mixed(vn, ws_ref, bs_ref, lo, hi)

        @pl.when(n == 0)
        def _():
            dws_ref[...] = jnp.zeros_like(dws_ref)

        dbs = jnp.zeros((BLK, LANES), F32)
        dvn_slabs = []
        du_slabs = []
        for s in range(GM_W // LANES):
            dgm = dam_ref[:, ATTN_W + s * LANES:ATTN_W + (s + 1) * LANES]
            du_slabs.append(dgm * mixed[s])
            dmx = dgm * u[:, s * LANES:(s + 1) * LANES]
            vsb = vn[:, s * LANES:(s + 1) * LANES].astype(BF16)
            dvn = jnp.zeros((BLK, LANES), F32)
            for half in range(2):
                h = 2 * s + half
                mask = lo if half == 0 else hi
                dmm = jnp.where(mask, dmx, 0.0)
                dbs = dbs + jnp.where(lane == h, jnp.sum(dmm, axis=1, keepdims=True), 0.0)
                dmm = dmm.astype(BF16)
                wt, causal = _tril_w(ws_ref, h)
                dvn = dvn + jnp.where(mask, _tn(wt, dmm), 0.0)
                dws_ref[h] += jnp.where(causal, _nt(dmm, vsb), 0.0)
            dvn_slabs.append(dvn)
        _accumulate(dbs_ref, dbs)
        dvn = jnp.concatenate(dvn_slabs, axis=1)
        du = jnp.concatenate(du_slabs, axis=1)
        _accumulate(dlnb_ref, jnp.sum(dvn, axis=0, keepdims=True))
        _accumulate(dlng_ref, jnp.sum(dvn * xhat, axis=0, keepdims=True))
        dxh = dvn * lng_ref[...]
        dgv = rstd * (dxh - jnp.mean(dxh, axis=-1, keepdims=True)
                      - xhat * jnp.mean(dxh * xhat, axis=-1, keepdims=True))
        dzuv_ref[:, :GM_W] = (du * _gelu_grad(zu)).astype(BF16)
        dzuv_ref[:, GM_W:] = (dgv * _gelu_grad(zv)).astype(BF16)

    return pl.pallas_call(
        body, name="b2_attn_gm", grid=(nb,),
        in_specs=[_rows(BLK, ATTN_W + GM_W)] + _block_specs_z(nb) + [
            pl.BlockSpec(memory_space=pltpu.SMEM), _whole((1, GM_W)), _whole((1, GM_W)),
            _whole((N_Q_HEADS, BLK, BLK)), _whole((BLK, GM_W))],
        out_specs=[_rows(BLK, ATTN_W), _whole((t, 2 * KV_W)), _rows(BLK, 2 * GM_W), _whole((1, LANES)),
                   _whole((1, GM_W)), _whole((1, GM_W)), _whole((N_Q_HEADS, BLK, BLK)), _whole((BLK, LANES))],
        out_shape=[_sds((t, ATTN_W), BF16), _sds((t, 2 * KV_W), F32), _sds((t, 2 * GM_W), BF16),
                   _sds((1, LANES), F32), _sds((1, GM_W), F32), _sds((1, GM_W), F32),
                   _sds((N_Q_HEADS, BLK, BLK), F32), _sds((BLK, LANES), F32)],
        compiler_params=_params())(dam, z, z, z, sinks, ln_g, ln_b, ws, bs_full)


def _b1_norm_in(dzq, dkv, dzuv, w, h, g, dh1):
    t = h.shape[0]
    tm = _tile(t, 512)

    def body(dzq_ref, dkv_ref, dzuv_ref, w_ref, h_ref, g_ref, dh_ref, dz_ref, dh0_ref, dg_ref):
        dz = jnp.concatenate([dzq_ref[...], dkv_ref[...].astype(BF16), dzuv_ref[...]], axis=1)
        dz_ref[...] = dz
        dx, dg = _rms_bwd(_nt(dz, w_ref[...]), h_ref[...], g_ref[...])
        dh0_ref[...] = dh_ref[...] + dx
        _accumulate(dg_ref, dg)

    return pl.pallas_call(
        body, name="b1_norm_in", grid=(t // tm,),
        in_specs=[_rows(tm, ATTN_W), _rows(tm, 2 * KV_W), _rows(tm, 2 * GM_W), _whole((D_MODEL, D_IN)),
                  _rows(tm, D_MODEL), _whole((1, D_MODEL)), _rows(tm, D_MODEL)],
        out_specs=[_rows(tm, D_IN), _rows(tm, D_MODEL), _whole((1, D_MODEL))],
        out_shape=[_sds((t, D_IN), BF16), _sds((t, D_MODEL), F32), _sds((1, D_MODEL), F32)],
        compiler_params=_params())(dzq, dkv, dzuv, w, h, g, dh1)


def _weight_grad(x, dy, name):
    t, k = x.shape
    n = dy.shape[1]
    tm = _tile(t, 512)

    def body(x_ref, dy_ref, dw_ref):
        _accumulate(dw_ref, _tn(x_ref[...].astype(BF16), dy_ref[...]))

    return pl.pallas_call(
        body, name=name, grid=(t // tm,),
        in_specs=[_rows(tm, k), _rows(tm, n)],
        out_specs=_whole((k, n)),
        out_shape=_sds((k, n), F32),
        compiler_params=_params())(x, dy)


_ANY = pl.BlockSpec(memory_space=pl.ANY)
_HBM = pl.BlockSpec(memory_space=pltpu.HBM)
_SEM = pl.BlockSpec(memory_space=pltpu.SEMAPHORE)
_EFFECT = pltpu.SideEffectType.DATAFLOW_SIDE_EFFECTING


def _mesh_pos():
    return lax.axis_index("x"), lax.axis_index("y"), lax.axis_index("c")


def _other_chips(x, y):
    return [(1 - x, y), (x, 1 - y), (1 - x, 1 - y)]


def _peers(mode, x, y, c):
    if mode == "devices":
        peers = []
        for j in range(1, N_DEV):
            px = 1 - x if (j >> 2) & 1 else x
            py = 1 - y if (j >> 1) & 1 else y
            pc = 1 - c if j & 1 else c
            peers.append(((px, py, pc), 4 * px + 2 * py + pc))
        return peers, 4 * x + 2 * y + c
    return [((px, py, c), 2 * px + py) for px, py in _other_chips(x, y)], 2 * x + y


def _descriptors(mode, srcs, lands, send_sems, recv_sems):
    x, y, c = _mesh_pos()
    peers, me = _peers(mode, x, y, c)
    scatter = mode == "scatter"
    outgoing, incoming = [], []
    for i, (src, land) in enumerate(zip(srcs, lands)):
        for j, (dev, slot) in enumerate(peers):
            sem = i * len(peers) + j
            common = dict(send_sem=send_sems.at[sem], recv_sem=recv_sems.at[sem], device_id=dev,
                          device_id_type=pl.DeviceIdType.MESH)
            outgoing.append(pltpu.make_async_remote_copy(
                src_ref=src.at[slot] if scatter else src, dst_ref=land.at[me], **common))
            incoming.append(pltpu.make_async_remote_copy(
                src_ref=src.at[me] if scatter else src, dst_ref=land.at[slot], **common))
    return outgoing, incoming


def _n_sems(mode, k):
    return k * ((N_DEV if mode == "devices" else N_CHIPS) - 1)


def _exchange_start(srcs, lands, mode, name, after=None):
    k = len(srcs)
    arrs = [*srcs, *lands]

    def body(*refs):
        skip = 1 if after is not None else 0
        send_sems, recv_sems = refs[2 * k + skip], refs[2 * k + skip + 1]
        outgoing, _ = _descriptors(mode, refs[:k], refs[k:2 * k], send_sems, recv_sems)
        for cp in outgoing:
            cp.start()
        refs[-1][...] = jnp.zeros_like(refs[-1])

    operands = [pltpu.with_memory_space_constraint(a, pltpu.HBM) for a in arrs]
    in_specs = [_HBM] * (2 * k)
    if after is not None:
        operands.append(after)
        in_specs.append(_ANY)
    sems = pltpu.SemaphoreType.DMA((_n_sems(mode, k),))
    res = pl.pallas_call(
        body, name=name, in_specs=in_specs,
        out_shape=(sems, sems, *[pltpu.HBM(a.shape, a.dtype) for a in arrs], _sds((8, LANES), F32)),
        out_specs=(_SEM, _SEM, *[_HBM] * (2 * k), pl.BlockSpec(memory_space=pltpu.VMEM)),
        input_output_aliases={i: 2 + i for i in range(2 * k)},
        compiler_params=pltpu.CompilerParams(has_side_effects=_EFFECT))(*operands)
    return dict(mode=mode, send=res[0], recv=res[1], srcs=res[2:2 + k], lands=res[2 + k:2 + 2 * k], token=res[-1])


def _exchange_wait(flight, name, after):
    mode, k = flight["mode"], len(flight["srcs"])
    arrs = [*flight["srcs"], *flight["lands"]]

    def body(*refs):
        outgoing, incoming = _descriptors(mode, refs[:k], refs[k:2 * k], refs[2 * k], refs[2 * k + 1])
        for cp in outgoing:
            cp.wait_send()
        for cp in incoming:
            cp.wait_recv()

    res = pl.pallas_call(
        body, name=name, in_specs=[_HBM] * (2 * k) + [_SEM, _SEM, _ANY],
        out_shape=tuple(pltpu.HBM(a.shape, a.dtype) for a in arrs), out_specs=tuple([_HBM] * (2 * k)),
        input_output_aliases={i: i for i in range(2 * k)},
        compiler_params=pltpu.CompilerParams(has_side_effects=_EFFECT))(*arrs, flight["send"], flight["recv"], after)
    return res[:k], res[k:]


def _sibling_exchange(arrs, name):
    k = len(arrs)

    def body(*refs):
        ins, outs = refs[:k], refs[k:2 * k]
        send_sems, recv_sems = refs[2 * k:]
        x, y, c = _mesh_pos()
        cps = []
        for i in range(k):
            cp = pltpu.make_async_remote_copy(
                src_ref=ins[i], dst_ref=outs[i], send_sem=send_sems.at[i], recv_sem=recv_sems.at[i],
                device_id=(x, y, 1 - c), device_id_type=pl.DeviceIdType.MESH)
            cp.start()
            cps.append(cp)
        for cp in cps:
            cp.wait()

    return pl.pallas_call(
        body, name=name, in_specs=[_ANY] * k, out_specs=[_ANY] * k,
        out_shape=[_sds(a.shape, a.dtype) for a in arrs],
        scratch_shapes=[pltpu.SemaphoreType.DMA((k,)), pltpu.SemaphoreType.DMA((k,))])(*arrs)


def _adamw(w, g, m, v):
    m = ADAM_B1 * m + (1.0 - ADAM_B1) * g
    v = ADAM_B2 * v + (1.0 - ADAM_B2) * (g * g)
    m_hat = m / (1.0 - ADAM_B1 ** ADAM_STEP)
    v_hat = v / (1.0 - ADAM_B2 ** ADAM_STEP)
    delta = -ADAM_LR * (m_hat / (jnp.sqrt(v_hat) + ADAM_EPS) + ADAM_WD * w)
    return delta, m, v


def _row_tile(rows, cols, n_arrays):
    budget = VMEM_LIMIT_BYTES // 4
    padded = -(-cols // LANES) * LANES
    tr = max(8, budget // (2 * n_arrays * padded * 4))
    tr = min(rows, 1 << (tr.bit_length() - 1))
    while rows % tr:
        tr //= 2
    return tr


def _sum_chips(landing, name):
    _, r, c = landing.shape
    tr = _row_tile(r, c, 4)

    def body(l_ref, s_ref):
        acc = l_ref[0].astype(F32)
        for s in range(1, N_CHIPS):
            acc = acc + l_ref[s].astype(F32)
        s_ref[...] = acc

    return pl.pallas_call(
        body, name=name, grid=(r // tr,),
        in_specs=[pl.BlockSpec((N_CHIPS, tr, c), lambda i: (0, i, 0))],
        out_specs=_rows(tr, c), out_shape=_sds((r, c), F32),
        compiler_params=_params())(landing)


def _adamw_big(mine, sibling, w, m, v, prev, layer, name):
    _, r, c = w.shape
    tr = _row_tile(r, c, 9)
    stacked = pl.BlockSpec((None, tr, c), lambda i: (layer, i, 0))

    def body(a_ref, b_ref, w_ref, m_ref, v_ref, *rest):
        g_out, d_out, m_out, v_out = rest[-4:]
        g = a_ref[...] + b_ref[...]
        g_out[...] = g
        d_out[...], m_out[...], v_out[...] = _adamw(w_ref[...], g, m_ref[...], v_ref[...])

    prev = list(prev) if prev is not None else []
    return pl.pallas_call(
        body, name=name, grid=(r // tr,),
        in_specs=[_rows(tr, c)] * 2 + [stacked] * 3 + [_ANY] * len(prev),
        out_specs=[stacked] * 4, out_shape=[_sds(w.shape, F32)] * 4,
        input_output_aliases={5 + j: j for j in range(len(prev))},
        compiler_params=_params())(mine, sibling, w, m, v, *prev)


def _adamw_small(gathered, w, m, v):
    r = w.shape[0]

    def body(a_ref, w_ref, m_ref, v_ref, g_out, d_out, m_out, v_out):
        g = a_ref[0]
        for d in range(1, N_DEV):
            g = g + a_ref[d]
        g_out[...] = g
        d_out[...], m_out[...], v_out[...] = _adamw(w_ref[...], g, m_ref[...], v_ref[...])

    return pl.pallas_call(
        body, name="adamw_small", grid=(1,),
        in_specs=[_whole((N_DEV, r, LANES))] + [_whole((r, LANES))] * 3,
        out_specs=[_whole((r, LANES))] * 4, out_shape=[_sds((r, LANES), F32)] * 4,
        compiler_params=_params())(gathered, w, m, v)


def _to_shard_major(full, axis):
    k, n = full.shape
    if axis == 2:
        return full.reshape(k, N_CHIPS, n // N_CHIPS).transpose(1, 0, 2)
    return full.reshape(N_CHIPS, k // N_CHIPS, n)


def _from_shard_major(parts, axis):
    _, k, n = parts.shape
    if axis == 2:
        return parts.transpose(1, 0, 2).reshape(k, N_CHIPS * n)
    return parts.reshape(N_CHIPS * k, n)


def _own_slot_filled(parts, slot):
    ids = lax.broadcasted_iota(jnp.int32, (parts.shape[0],) + (1,) * (parts.ndim - 1), 0)
    return jnp.where(ids == slot, parts, jnp.zeros_like(parts))


def _pack_small(params):
    pieces = []
    for name in SMALL_NAMES:
        flat = params[name].reshape(-1)
        pieces.append(jnp.pad(flat, (0, -flat.shape[0] % SMALL_PAD)))
    return jnp.concatenate(pieces).reshape(-1, LANES)


def _unpack_small(packed, like):
    flat = packed.reshape(-1)
    out, off = {}, 0
    for name in SMALL_NAMES:
        size = like[name].size
        out[name] = flat[off:off + size].reshape(like[name].shape)
        off += size + (-size % SMALL_PAD)
    return out


def kernel(x, p, ln_mix_pre, w_in, attn_sinks, gm_ln_g, gm_ln_b, gm_ws, gm_bs, g_attn_out, g_gm_out, w_out, ln_mix_post, ln_ffn_pre, w_ffn_gate, w_ffn_up, w_ffn_down, ln_ffn_post, w_ple, ln_ple_gate, w_ple_gate, loss_target, m_ln_mix_pre, m_w_in, m_attn_sinks, m_gm_ln_g, m_gm_ln_b, m_gm_ws, m_gm_bs, m_g_attn_out, m_g_gm_out, m_w_out, m_ln_mix_post, m_ln_ffn_pre, m_w_ffn_gate, m_w_ffn_up, m_w_ffn_down, m_ln_ffn_post, m_w_ple, m_ln_ple_gate, m_w_ple_gate, v_ln_mix_pre, v_w_in, v_attn_sinks, v_gm_ln_g, v_gm_ln_b, v_gm_ws, v_gm_bs, v_g_attn_out, v_g_gm_out, v_w_out, v_ln_mix_post, v_ln_ffn_pre, v_w_ffn_gate, v_w_ffn_up, v_w_ffn_down, v_ln_ffn_post, v_w_ple, v_ln_ple_gate, v_w_ple_gate):
    given = dict(locals())
    wts = {n: given[n] for n in WEIGHTS}
    mom = {n: given["m_" + n] for n in WEIGHTS}
    var = {n: given["v_" + n] for n in WEIGHTS}
    depth = w_in.shape[0]
    h = x[0]
    target = loss_target[0]
    chip = 2 * lax.axis_index("x") + lax.axis_index("y")
    device = 2 * chip + lax.axis_index("c")
    row = lambda a, i: a[i][None, :]
    bs_full = [jnp.repeat(gm_bs[i].T, HEAD_DIM, axis=1) for i in range(depth)]
    kinds = ("grad", "delta", "m", "v")

    def start_gather(i, after):
        shards = [wts[n][i].astype(BF16) for n in BIG_NAMES]
        lands = [_own_slot_filled(jnp.broadcast_to(s, (N_CHIPS,) + s.shape), chip) for s in shards]
        return _exchange_start(shards, lands, "gather", f"gather_weights_start_{i}", after)

    def finish_gather(flight, i, after):
        _, lands = _exchange_wait(flight, f"gather_weights_wait_{i}", after)
        return {n: _from_shard_major(l, BIG[n]) for n, l in zip(BIG_NAMES, lands)}

    flight = start_gather(0, None)
    full = [finish_gather(flight, 0, flight["token"])] + [None] * (depth - 1)
    saved = []
    for i in range(depth):
        g_in = row(ln_mix_pre, i)
        if i + 1 < depth:
            flight = start_gather(i + 1, full[i]['w_in'])
            g_in = g_in + flight["token"][0:1, 0:1]
        w = full[i]
        z, a = _f1_norm_in(h, g_in, w['w_in'])
        am = _f2_attn_gm(z, attn_sinks[i], row(gm_ln_g, i), row(gm_ln_b, i), gm_ws[i], bs_full[i])
        heads, mix, h1 = _f3_mix_out(am, h, row(g_attn_out, i), row(g_gm_out, i), w['w_out'], row(ln_mix_post, i))
        f, gt, up = _f4a_ffn_in(h1, row(ln_ffn_pre, i), w['w_ffn_gate'], w['w_ffn_up'])
        dn, h2 = _f4b_ffn_out(gt, up, w['w_ffn_down'], h1, row(ln_ffn_post, i))
        r, pg, pe, h3 = _f5_ple(h2, row(ln_ple_gate, i), w['w_ple_gate'], p[i, 0], w['w_ple'])
        saved.append(dict(h=h, z=z, a=a, am=am, heads=heads, mix=mix, h1=h1, f=f, gt=gt, up=up, dn=dn, h2=h2,
                          r=r, pg=pg, pe=pe))
        h = h3
        if i + 1 < depth:
            full[i + 1] = finish_gather(flight, i + 1, h3)

    sq, dh = _loss_head(h, target)
    loss = lax.psum(0.5 / D_MODEL * sq[0, 0], ("x", "y", "c"))

    chain = {n: None for n in BIG_NAMES}
    small_out = [None] * depth

    def start_exchanges(i, dws, small, after):
        parts = [_to_shard_major(dws[n], BIG[n]).astype(BF16) for n in BIG_NAMES]
        big = _exchange_start(parts, [_own_slot_filled(q, chip) for q in parts], "scatter",
                              f"scatter_grads_start_{i}", after)
        packed = _pack_small(small)
        land = _own_slot_filled(jnp.broadcast_to(packed, (N_DEV,) + packed.shape), device)
        sm = _exchange_start([packed], [land], "devices", f"gather_small_grads_start_{i}", after)
        return i, big, sm

    def finish_exchanges(pending, after):
        i, big, sm = pending
        _, lands = _exchange_wait(big, f"scatter_grads_wait_{i}", after)
        partial = [_sum_chips(l, "sum_chips_" + n) for n, l in zip(BIG_NAMES, lands)]
        sibling = _sibling_exchange(partial, name="sibling_grads")
        for n, mine, sib in zip(BIG_NAMES, partial, sibling):
            chain[n] = _adamw_big(mine, sib, wts[n], mom[n], var[n], chain[n], i, "adamw_" + n)
        _, (gathered,) = _exchange_wait(sm, f"gather_small_grads_wait_{i}", after)
        layer = lambda d: _pack_small({n: d[n][i] for n in SMALL_NAMES})
        small_out[i] = _adamw_small(gathered, layer(wts), layer(mom), layer(var))
        return lands[0]

    pending, done = None, None
    for i in reversed(range(depth)):
        s, w = saved[i], full[i]
        dws, small = {}, {}
        dpe, dpg, dh2, dg = _b5_ple(dh, s['h2'], s['pg'], s['pe'], row(ln_ple_gate, i), w['w_ple_gate'])
        small['ln_ple_gate'] = dg[0]
        dws['w_ple'] = _weight_grad(p[i, 0], dpe, "dw_ple")
        dws['w_ple_gate'] = _weight_grad(s['r'], dpg, "dw_ple_gate")

        ddn, act, dgt, dup, dg = _b4a_ffn_out(dh2, s['dn'], row(ln_ffn_post, i), w['w_ffn_down'], s['gt'], s['up'])
        small['ln_ffn_post'] = dg[0]
        dws['w_ffn_down'] = _weight_grad(act, ddn, "dw_ffn_down")
        dws['w_ffn_gate'] = _weight_grad(s['f'], dgt, "dw_ffn_gate")
        dws['w_ffn_up'] = _weight_grad(s['f'], dup, "dw_ffn_up")
        dh1, dg = _b4b_ffn_in(dgt, dup, w['w_ffn_gate'], w['w_ffn_up'], s['h1'], row(ln_ffn_pre, i), dh2)
        small['ln_ffn_pre'] = dg[0]

        dmix, dam, dgp, dga, dgg = _b3_mix_out(dh1, s['mix'], row(ln_mix_post, i), w['w_out'], s['am'],
                                               row(g_attn_out, i), row(g_gm_out, i))
        small['ln_mix_post'] = dgp[0]
        small['g_attn_out'] = dga[0]
        small['g_gm_out'] = dgg[0]
        dws['w_out'] = _weight_grad(s['heads'], dmix, "dw_out")

        dzq, dkv, dzuv, dsink, dlng, dlnb, dgws, dbs = _b2_attn_gm(
            dam, s['z'], attn_sinks[i], row(gm_ln_g, i), row(gm_ln_b, i), gm_ws[i], bs_full[i])
        small['attn_sinks'] = dsink[0, :N_Q_HEADS]
        small['gm_ln_g'] = dlng[0]
        small['gm_ln_b'] = dlnb[0]
        small['gm_ws'] = dgws
        small['gm_bs'] = dbs[:, :N_Q_HEADS].T

        dz, dh, dg = _b1_norm_in(dzq, dkv, dzuv, w['w_in'], s['h'], row(ln_mix_pre, i), dh1)
        small['ln_mix_pre'] = dg[0]
        dws['w_in'] = _weight_grad(s['a'], dz, "dw_in")

        if pending is not None:
            done = finish_exchanges(pending, dh)
        pending = start_exchanges(i, dws, small, done)
    finish_exchanges(pending, dh)
    grad_x = dh[None]

    out = {k: {n: chain[n][j] for n in BIG_NAMES} for j, k in enumerate(kinds)}
    for j, k in enumerate(kinds):
        layers = [_unpack_small(small_out[i][j], {n: wts[n][i] for n in SMALL_NAMES}) for i in range(depth)]
        out[k].update({n: jnp.stack([l[n] for l in layers]) for n in SMALL_NAMES})

    return (loss, grad_x, *[out["grad"][n] for n in WEIGHTS], *[out["delta"][n] for n in WEIGHTS],
            *[out["m"][n] for n in WEIGHTS], *[out["v"][n] for n in WEIGHTS])
```

```python
import functools
import math

import jax
import jax.numpy as jnp
from jax import lax
from jax.experimental import pallas as pl
from jax.experimental.pallas import tpu as pltpu

F32 = jnp.float32
BF16 = jnp.bfloat16

D_MODEL = 1024
HEAD_DIM = 64
N_Q_HEADS = 8
BLK = 128
ATTN_W = 512
KV_W = 128
GM_W = 512
D_IN = ATTN_W + 2 * KV_W + 2 * GM_W
D_FF = 2816
PLE_DIM = 256
DEPTH = 4
NORM_EPS = 1e-6
NEG_BIG = -1e30
N_CHIPS = 4
N_DEV = 8

ADAM_LR = 0.001
ADAM_B1 = 0.9
ADAM_B2 = 0.999
ADAM_EPS = 1e-08
ADAM_WD = 0.01
ADAM_STEP = 10

VMEM_LIMIT_BYTES = 56 * 1024 * 1024
LANES = 128
GELU_C0 = math.sqrt(2.0 / math.pi)
GELU_C1 = 0.044715
ALIBI_SLOPES = tuple(2.0 ** (-8.0 * (h + 1.0) / N_Q_HEADS) for h in range(N_Q_HEADS))

WEIGHTS = ['ln_mix_pre', 'w_in', 'attn_sinks', 'gm_ln_g', 'gm_ln_b', 'gm_ws', 'gm_bs', 'g_attn_out',
           'g_gm_out', 'w_out', 'ln_mix_post', 'ln_ffn_pre', 'w_ffn_gate', 'w_ffn_up', 'w_ffn_down',
           'ln_ffn_post', 'w_ple', 'ln_ple_gate', 'w_ple_gate']
BIG = {'w_in': 2, 'w_out': 1, 'w_ffn_gate': 2, 'w_ffn_up': 2, 'w_ffn_down': 1, 'w_ple': 2, 'w_ple_gate': 1}
BIG_NAMES = list(BIG)
SMALL_NAMES = [n for n in WEIGHTS if n not in BIG]
SMALL_PAD = 1024
GATHER_GROUPS = (('w_in',), ('w_out', 'w_ffn_gate', 'w_ffn_up'), ('w_ffn_down', 'w_ple_gate', 'w_ple'))
SCATTER_GROUPS = (('w_ple', 'w_ple_gate', 'w_ffn_down', 'w_ffn_gate', 'w_ffn_up'), ('w_out',), ('w_in',))
SMALL_LATE = ('ln_mix_pre',)
SMALL_EARLY = tuple(n for n in SMALL_NAMES if n not in SMALL_LATE)


def _nt(a, b):
    return lax.dot_general(a, b, (((1,), (1,)), ((), ())), preferred_element_type=F32)


def _tn(a, b):
    return lax.dot_general(a, b, (((0,), (0,)), ((), ())), preferred_element_type=F32)


def _mm(a, b):
    return jnp.dot(a, b, preferred_element_type=F32)


def _rms(x, g):
    r = lax.rsqrt(jnp.mean(x * x, axis=-1, keepdims=True) + NORM_EPS)
    return x * r * g


def _rms_bwd(dy, x, g):
    r = lax.rsqrt(jnp.mean(x * x, axis=-1, keepdims=True) + NORM_EPS)
    xh = x * r
    dg = jnp.sum(dy * xh, axis=0, keepdims=True)
    dxh = dy * g
    dx = r * (dxh - xh * jnp.mean(dxh * xh, axis=-1, keepdims=True))
    return dx, dg


def _gelu(x):
    return 0.5 * x * (1.0 + jnp.tanh(GELU_C0 * (x + GELU_C1 * x * x * x)))


def _gelu_grad(x):
    t = jnp.tanh(GELU_C0 * (x + GELU_C1 * x * x * x))
    return 0.5 * (1.0 + t) + 0.5 * x * (1.0 - t * t) * GELU_C0 * (1.0 + 3.0 * GELU_C1 * x * x)


def _sigmoid(x):
    return 1.0 / (1.0 + jnp.exp(-x))


def _rows(tm, n):
    return pl.BlockSpec((tm, n), lambda i: (i, 0))


def _whole(shape):
    return pl.BlockSpec(shape, lambda i: (0,) * len(shape))


def _accumulate(ref, val):
    @pl.when(pl.program_id(0) == 0)
    def _():
        ref[...] = jnp.zeros_like(ref)

    ref[...] += val


def _params(n_axes=1):
    return pltpu.CompilerParams(dimension_semantics=("arbitrary",) * n_axes,
                                vmem_limit_bytes=VMEM_LIMIT_BYTES)


def _sds(shape, dtype):
    return jax.ShapeDtypeStruct(shape, dtype)


def _tile(t, want):
    return min(t, want)


def _f1_norm_in(h, g, w):
    t = h.shape[0]
    tm = _tile(t, 512)

    def body(h_ref, g_ref, w_ref, z_ref, a_ref):
        a = _rms(h_ref[...], g_ref[...]).astype(BF16)
        a_ref[...] = a
        z_ref[...] = _mm(a, w_ref[...])

    return pl.pallas_call(
        body, name="f1_norm_in", grid=(t // tm,),
        in_specs=[_rows(tm, D_MODEL), _whole((1, D_MODEL)), _whole((D_MODEL, D_IN))],
        out_specs=[_rows(tm, D_IN), _rows(tm, D_MODEL)],
        out_shape=[_sds((t, D_IN), F32), _sds((t, D_MODEL), BF16)],
        compiler_params=_params())(h, g, w)


def _attn_geometry(n):
    ti = lax.broadcasted_iota(jnp.int32, (BLK, 2 * BLK), 0)
    ji = lax.broadcasted_iota(jnp.int32, (BLK, 2 * BLK), 1)
    dist = ti + BLK - ji
    valid = (dist >= 0) & (dist < BLK) & ((n > 0) | (ji >= BLK))
    return valid, dist.astype(F32)


def _attn_probs(qm, kuse, valid, distf, slope, sink):
    sc = _nt(qm, kuse) * (HEAD_DIM ** -0.5)
    sc = jnp.where(valid, sc - slope * distf, NEG_BIG)
    m = jnp.maximum(jnp.max(sc, axis=1, keepdims=True), sink)
    e = jnp.exp(sc - m)
    es = jnp.exp(sink - m)
    inv = 1.0 / (jnp.sum(e, axis=1, keepdims=True) + es)
    return e * inv, es * inv


def _kv_window(z_ref, kp_ref, vp_ref):
    kcat = jnp.concatenate([kp_ref[...], z_ref[:, ATTN_W:ATTN_W + KV_W]], axis=0)
    vcat = jnp.concatenate([vp_ref[...], z_ref[:, ATTN_W + KV_W:ATTN_W + 2 * KV_W]], axis=0)
    kswap = pltpu.roll(kcat, HEAD_DIM, 1)
    vswap = pltpu.roll(vcat, HEAD_DIM, 1)
    return kcat.astype(BF16), kswap.astype(BF16), vcat, vswap


def _gm_forward(z_ref, lng_ref, lnb_ref):
    zu = z_ref[:, ATTN_W + 2 * KV_W:ATTN_W + 2 * KV_W + GM_W]
    zv = z_ref[:, ATTN_W + 2 * KV_W + GM_W:D_IN]
    u = _gelu(zu)
    gv = _gelu(zv)
    xc = gv - jnp.mean(gv, axis=-1, keepdims=True)
    rstd = lax.rsqrt(jnp.mean(xc * xc, axis=-1, keepdims=True) + NORM_EPS)
    xhat = xc * rstd
    vn = xhat * lng_ref[...] + lnb_ref[...]
    return zu, zv, u, xhat, rstd, vn


def _tril_w(ws_ref, h):
    ti = lax.broadcasted_iota(jnp.int32, (BLK, BLK), 0)
    si = lax.broadcasted_iota(jnp.int32, (BLK, BLK), 1)
    causal = si <= ti
    return jnp.where(causal, ws_ref[h], 0.0).astype(BF16), causal


def _gm_mixed(vn, ws_ref, bs_ref, lo, hi):
    slabs = []
    for s in range(GM_W // LANES):
        vs = vn[:, s * LANES:(s + 1) * LANES]
        w0, _ = _tril_w(ws_ref, 2 * s)
        w1, _ = _tril_w(ws_ref, 2 * s + 1)
        mixed = (_mm(w0, jnp.where(lo, vs, 0.0).astype(BF16))
                 + _mm(w1, jnp.where(hi, vs, 0.0).astype(BF16))
                 + bs_ref[:, s * LANES:(s + 1) * LANES])
        slabs.append(mixed)
    return slabs


def _block_specs_z(nb):
    prev = lambda i: (jnp.maximum(i - 1, 0), ATTN_W // KV_W)
    prev_v = lambda i: (jnp.maximum(i - 1, 0), ATTN_W // KV_W + 1)
    return [_rows(BLK, D_IN), pl.BlockSpec((BLK, KV_W), prev), pl.BlockSpec((BLK, KV_W), prev_v)]


def _f2_attn_gm(z, sinks, ln_g, ln_b, ws, bs_full):
    t = z.shape[0]
    nb = t // BLK

    def body(z_ref, kp_ref, vp_ref, sink_ref, lng_ref, lnb_ref, ws_ref, bs_ref, am_ref):
        n = pl.program_id(0)
        lane = lax.broadcasted_iota(jnp.int32, (1, LANES), 1)
        lo = lane < HEAD_DIM
        hi = lane >= HEAD_DIM
        valid, distf = _attn_geometry(n)
        kc, ks, vcat, vswap = _kv_window(z_ref, kp_ref, vp_ref)
        for s in range(ATTN_W // LANES):
            kv_head = s // 2
            qs = z_ref[:, s * LANES:(s + 1) * LANES]
            o = jnp.zeros((BLK, LANES), F32)
            for half in range(2):
                hq = 2 * s + half
                mask = lo if half == 0 else hi
                same = kv_head == half
                qm = jnp.where(mask, qs, 0.0).astype(BF16)
                pr, _ = _attn_probs(qm, kc if same else ks, valid, distf, ALIBI_SLOPES[hq], sink_ref[hq])
                vm = jnp.where(mask, vcat if same else vswap, 0.0).astype(BF16)
                o = o + _mm(pr.astype(BF16), vm)
            am_ref[:, s * LANES:(s + 1) * LANES] = o
        _, _, u, _, _, vn = _gm_forward(z_ref, lng_ref, lnb_ref)
        mixed = _gm_mixed(vn, ws_ref, bs_ref, lo, hi)
        for s in range(GM_W // LANES):
            am_ref[:, ATTN_W + s * LANES:ATTN_W + (s + 1) * LANES] = u[:, s * LANES:(s + 1) * LANES] * mixed[s]

    return pl.pallas_call(
        body, name="f2_attn_gm", grid=(nb,),
        in_specs=_block_specs_z(nb) + [
            pl.BlockSpec(memory_space=pltpu.SMEM), _whole((1, GM_W)), _whole((1, GM_W)),
            _whole((N_Q_HEADS, BLK, BLK)), _whole((BLK, GM_W))],
        out_specs=_rows(BLK, ATTN_W + GM_W),
        out_shape=_sds((t, ATTN_W + GM_W), F32),
        compiler_params=_params())(z, z, z, sinks, ln_g, ln_b, ws, bs_full)


def _f3_mix_out(am, h, ga, gg, w, gpost):
    t = h.shape[0]
    tm = _tile(t, 512)

    def body(am_ref, h_ref, ga_ref, gg_ref, w_ref, gp_ref, heads_ref, mix_ref, h1_ref):
        heads = jnp.concatenate([_rms(am_ref[:, :ATTN_W], ga_ref[...]),
                                 _rms(am_ref[:, ATTN_W:], gg_ref[...])], axis=1).astype(BF16)
        heads_ref[...] = heads
        mix = _mm(heads, w_ref[...])
        mix_ref[...] = mix
        h1_ref[...] = h_ref[...] + _rms(mix, gp_ref[...])

    return pl.pallas_call(
        body, name="f3_mix_out", grid=(t // tm,),
        in_specs=[_rows(tm, D_MODEL), _rows(tm, D_MODEL), _whole((1, ATTN_W)), _whole((1, GM_W)),
                  _whole((D_MODEL, D_MODEL)), _whole((1, D_MODEL))],
        out_specs=[_rows(tm, D_MODEL)] * 3,
        out_shape=[_sds((t, D_MODEL), BF16), _sds((t, D_MODEL), F32), _sds((t, D_MODEL), F32)],
        compiler_params=_params())(am, h, ga, gg, w, gpost)


def _f4a_ffn_in(h1, gf, wg, wu):
    t = h1.shape[0]
    tm = _tile(t, 256)

    def body(h_ref, g_ref, wg_ref, wu_ref, f_ref, gt_ref, up_ref):
        f = _rms(h_ref[...], g_ref[...]).astype(BF16)
        f_ref[...] = f
        gt_ref[...] = _mm(f, wg_ref[...])
        up_ref[...] = _mm(f, wu_ref[...])

    return pl.pallas_call(
        body, name="f4a_ffn_in", grid=(t // tm,),
        in_specs=[_rows(tm, D_MODEL), _whole((1, D_MODEL)), _whole((D_MODEL, D_FF)), _whole((D_MODEL, D_FF))],
        out_specs=[_rows(tm, D_MODEL), _rows(tm, D_FF), _rows(tm, D_FF)],
        out_shape=[_sds((t, D_MODEL), BF16), _sds((t, D_FF), F32), _sds((t, D_FF), F32)],
        compiler_params=_params())(h1, gf, wg, wu)


def _f4b_ffn_out(gt, up, wd, h1, gfp):
    t = h1.shape[0]
    tm = _tile(t, 256)

    def body(gt_ref, up_ref, wd_ref, h_ref, g_ref, dn_ref, h2_ref):
        gt = gt_ref[...]
        act = (gt * _sigmoid(gt) * up_ref[...]).astype(BF16)
        dn = _mm(act, wd_ref[...])
        dn_ref[...] = dn
        h2_ref[...] = h_ref[...] + _rms(dn, g_ref[...])

    return pl.pallas_call(
        body, name="f4b_ffn_out", grid=(t // tm,),
        in_specs=[_rows(tm, D_FF), _rows(tm, D_FF), _whole((D_FF, D_MODEL)), _rows(tm, D_MODEL),
                  _whole((1, D_MODEL))],
        out_specs=[_rows(tm, D_MODEL)] * 2,
        out_shape=[_sds((t, D_MODEL), F32)] * 2,
        compiler_params=_params())(gt, up, wd, h1, gfp)


def _f5_ple(h2, gpl, wpg, p, wple):
    t = h2.shape[0]
    tm = _tile(t, 512)

    def body(h_ref, g_ref, wpg_ref, p_ref, wple_ref, r_ref, pg_ref, pe_ref, h3_ref):
        h = h_ref[...]
        r = _rms(h, g_ref[...]).astype(BF16)
        r_ref[...] = r
        pg = _mm(r, wpg_ref[...])
        pe = _mm(p_ref[...].astype(BF16), wple_ref[...])
        pg_ref[...] = pg
        pe_ref[...] = pe
        h3_ref[...] = h + pe * _sigmoid(pg)

    return pl.pallas_call(
        body, name="f5_ple", grid=(t // tm,),
        in_specs=[_rows(tm, D_MODEL), _whole((1, D_MODEL)), _whole((D_MODEL, D_MODEL)), _rows(tm, PLE_DIM),
                  _whole((PLE_DIM, D_MODEL))],
        out_specs=[_rows(tm, D_MODEL)] * 4,
        out_shape=[_sds((t, D_MODEL), BF16)] + [_sds((t, D_MODEL), F32)] * 3,
        compiler_params=_params())(h2, gpl, wpg, p, wple)


def _loss_head(y, target):
    t = y.shape[0]
    tm = _tile(t, 512)

    def body(y_ref, t_ref, sq_ref, dy_ref):
        err = y_ref[...] - t_ref[...]
        dy_ref[...] = err * (1.0 / D_MODEL)
        _accumulate(sq_ref, jnp.sum(err * err, keepdims=True))

    return pl.pallas_call(
        body, name="loss_head", grid=(t // tm,),
        in_specs=[_rows(tm, D_MODEL)] * 2,
        out_specs=[_whole((1, LANES)), _rows(tm, D_MODEL)],
        out_shape=[_sds((1, LANES), F32), _sds((t, D_MODEL), F32)],
        compiler_params=_params())(y, target)


def _b5_ple(dh3, h2, pg, pe, gpl, wpg):
    t = h2.shape[0]
    tm = _tile(t, 512)

    def body(dh_ref, h_ref, pg_ref, pe_ref, g_ref, w_ref, dpe_ref, dpg_ref, dh2_ref, dg_ref):
        dh = dh_ref[...]
        s = _sigmoid(pg_ref[...])
        dpe_ref[...] = (dh * s).astype(BF16)
        dpg = (dh * pe_ref[...] * s * (1.0 - s)).astype(BF16)
        dpg_ref[...] = dpg
        dx, dg = _rms_bwd(_nt(dpg, w_ref[...]), h_ref[...], g_ref[...])
        dh2_ref[...] = dh + dx
        _accumulate(dg_ref, dg)

    return pl.pallas_call(
        body, name="b5_ple", grid=(t // tm,),
        in_specs=[_rows(tm, D_MODEL)] * 4 + [_whole((1, D_MODEL)), _whole((D_MODEL, D_MODEL))],
        out_specs=[_rows(tm, D_MODEL)] * 3 + [_whole((1, D_MODEL))],
        out_shape=[_sds((t, D_MODEL), BF16)] * 2 + [_sds((t, D_MODEL), F32), _sds((1, D_MODEL), F32)],
        compiler_params=_params())(dh3, h2, pg, pe, gpl, wpg)


def _b4a_ffn_out(dh2, dn, gfp, wd, gt, up):
    t = dh2.shape[0]
    tm = _tile(t, 256)

    def body(dh_ref, dn_ref, g_ref, wd_ref, gt_ref, up_ref, ddn_ref, act_ref, dgt_ref, dup_ref, dg_ref):
        ddn, dg = _rms_bwd(dh_ref[...], dn_ref[...], g_ref[...])
        _accumulate(dg_ref, dg)
        ddn = ddn.astype(BF16)
        ddn_ref[...] = ddn
        dact = _nt(ddn, wd_ref[...])
        gt = gt_ref[...]
        up = up_ref[...]
        sg = _sigmoid(gt)
        silu = gt * sg
        act_ref[...] = (silu * up).astype(BF16)
        dup_ref[...] = (dact * silu).astype(BF16)
        dgt_ref[...] = (dact * up * (sg * (1.0 + gt * (1.0 - sg)))).astype(BF16)

    return pl.pallas_call(
        body, name="b4a_ffn_out", grid=(t // tm,),
        in_specs=[_rows(tm, D_MODEL), _rows(tm, D_MODEL), _whole((1, D_MODEL)), _whole((D_FF, D_MODEL)),
                  _rows(tm, D_FF), _rows(tm, D_FF)],
        out_specs=[_rows(tm, D_MODEL), _rows(tm, D_FF), _rows(tm, D_FF), _rows(tm, D_FF), _whole((1, D_MODEL))],
        out_shape=[_sds((t, D_MODEL), BF16)] + [_sds((t, D_FF), BF16)] * 3 + [_sds((1, D_MODEL), F32)],
        compiler_params=_params())(dh2, dn, gfp, wd, gt, up)


def _b4b_ffn_in(dgt, dup, wg, wu, h1, gf, dh2):
    t = h1.shape[0]
    tm = _tile(t, 512)

    def body(dgt_ref, dup_ref, wg_ref, wu_ref, h_ref, g_ref, dh_ref, dh1_ref, dg_ref):
        df = _nt(dgt_ref[...], wg_ref[...]) + _nt(dup_ref[...], wu_ref[...])
        dx, dg = _rms_bwd(df, h_ref[...], g_ref[...])
        dh1_ref[...] = dh_ref[...] + dx
        _accumulate(dg_ref, dg)

    return pl.pallas_call(
        body, name="b4b_ffn_in", grid=(t // tm,),
        in_specs=[_rows(tm, D_FF), _rows(tm, D_FF), _whole((D_MODEL, D_FF)), _whole((D_MODEL, D_FF)),
                  _rows(tm, D_MODEL), _whole((1, D_MODEL)), _rows(tm, D_MODEL)],
        out_specs=[_rows(tm, D_MODEL), _whole((1, D_MODEL))],
        out_shape=[_sds((t, D_MODEL), F32), _sds((1, D_MODEL), F32)],
        compiler_params=_params())(dgt, dup, wg, wu, h1, gf, dh2)


def _b3_mix_out(dh1, mix, gpost, w, am, ga, gg):
    t = dh1.shape[0]
    tm = _tile(t, 512)

    def body(dh_ref, mix_ref, gp_ref, w_ref, am_ref, ga_ref, gg_ref, dmix_ref, dam_ref, dgp_ref, dga_ref, dgg_ref):
        dmix, dgp = _rms_bwd(dh_ref[...], mix_ref[...], gp_ref[...])
        _accumulate(dgp_ref, dgp)
        dmix = dmix.astype(BF16)
        dmix_ref[...] = dmix
        dheads = _nt(dmix, w_ref[...])
        dat, dga = _rms_bwd(dheads[:, :ATTN_W], am_ref[:, :ATTN_W], ga_ref[...])
        dgm, dgg = _rms_bwd(dheads[:, ATTN_W:], am_ref[:, ATTN_W:], gg_ref[...])
        dam_ref[:, :ATTN_W] = dat
        dam_ref[:, ATTN_W:] = dgm
        _accumulate(dga_ref, dga)
        _accumulate(dgg_ref, dgg)

    return pl.pallas_call(
        body, name="b3_mix_out", grid=(t // tm,),
        in_specs=[_rows(tm, D_MODEL), _rows(tm, D_MODEL), _whole((1, D_MODEL)), _whole((D_MODEL, D_MODEL)),
                  _rows(tm, D_MODEL), _whole((1, ATTN_W)), _whole((1, GM_W))],
        out_specs=[_rows(tm, D_MODEL), _rows(tm, D_MODEL), _whole((1, D_MODEL)), _whole((1, ATTN_W)),
                   _whole((1, GM_W))],
        out_shape=[_sds((t, D_MODEL), BF16), _sds((t, D_MODEL), F32), _sds((1, D_MODEL), F32),
                   _sds((1, ATTN_W), F32), _sds((1, GM_W), F32)],
        compiler_params=_params())(dh1, mix, gpost, w, am, ga, gg)


def _b2_attn_gm(dam, z, sinks, ln_g, ln_b, ws, bs_full):
    t = z.shape[0]
    nb = t // BLK

    def body(dam_ref, z_ref, kp_ref, vp_ref, sink_ref, lng_ref, lnb_ref, ws_ref, bs_ref,
             dzq_ref, dkv_ref, dzuv_ref, dsink_ref, dlng_ref, dlnb_ref, dws_ref, dbs_ref):
        n = pl.program_id(0)
        lane = lax.broadcasted_iota(jnp.int32, (1, LANES), 1)
        lo = lane < HEAD_DIM
        hi = lane >= HEAD_DIM
        valid, distf = _attn_geometry(n)
        kc, ks, vcat, vswap = _kv_window(z_ref, kp_ref, vp_ref)
        vc = vcat.astype(BF16)
        vs_ = vswap.astype(BF16)
        dk_acc = jnp.zeros((2 * BLK, LANES), F32)
        dv_acc = jnp.zeros((2 * BLK, LANES), F32)
        dsink = jnp.zeros((1, LANES), F32)
        for s in range(ATTN_W // LANES):
            kv_head = s // 2
            qs = z_ref[:, s * LANES:(s + 1) * LANES]
            dos = dam_ref[:, s * LANES:(s + 1) * LANES]
            dq = jnp.zeros((BLK, LANES), F32)
            for half in range(2):
                hq = 2 * s + half
                mask = lo if half == 0 else hi
                same = kv_head == half
                kuse = kc if same else ks
                vuse = vc if same else vs_
                qm = jnp.where(mask, qs, 0.0).astype(BF16)
                dom = jnp.where(mask, dos, 0.0).astype(BF16)
                pr, ps = _attn_probs(qm, kuse, valid, distf, ALIBI_SLOPES[hq], sink_ref[hq])
                dpr = _nt(dom, vuse)
                row = jnp.sum(dpr * pr, axis=1, keepdims=True)
                ds = (pr * (dpr - row) * (HEAD_DIM ** -0.5)).astype(BF16)
                dsink = dsink + jnp.where(lane == hq, -jnp.sum(ps * row, keepdims=True), 0.0)
                dq = dq + jnp.where(mask, _mm(ds, kuse), 0.0)
                dkc = _tn(ds, qm)
                dvc = _tn(pr.astype(BF16), dom)
                if not same:
                    dkc = pltpu.roll(dkc, HEAD_DIM, 1)
                    dvc = pltpu.roll(dvc, HEAD_DIM, 1)
                dk_acc = dk_acc + dkc
                dv_acc = dv_acc + dvc
            dzq_ref[:, s * LANES:(s + 1) * LANES] = dq.astype(BF16)
        cur = pl.multiple_of(n * BLK, BLK)
        dkv_ref[pl.ds(cur, BLK), 0:KV_W] = dk_acc[BLK:, :]
        dkv_ref[pl.ds(cur, BLK), KV_W:2 * KV_W] = dv_acc[BLK:, :]

        @pl.when(n > 0)
        def _():
            prv = pl.multiple_of((n - 1) * BLK, BLK)
            dkv_ref[pl.ds(prv, BLK), 0:KV_W] += dk_acc[:BLK, :]
            dkv_ref[pl.ds(prv, BLK), KV_W:2 * KV_W] += dv_acc[:BLK, :]

        _accumulate(dsink_ref, dsink)

        zu, zv, u, xhat, rstd, vn = _gm_forward(z_ref, lng_ref, lnb_ref)
        mixed = _gm_mixed(vn, ws_ref, bs_ref, lo, hi)

        @pl.when(n == 0)
        def _():
            dws_ref[...] = jnp.zeros_like(dws_ref)

        dbs = jnp.zeros((BLK, LANES), F32)
        dvn_slabs = []
        du_slabs = []
        for s in range(GM_W // LANES):
            dgm = dam_ref[:, ATTN_W + s * LANES:ATTN_W + (s + 1) * LANES]
            du_slabs.append(dgm * mixed[s])
            dmx = dgm * u[:, s * LANES:(s + 1) * LANES]
            vsb = vn[:, s * LANES:(s + 1) * LANES].astype(BF16)
            dvn = jnp.zeros((BLK, LANES), F32)
            for half in range(2):
                h = 2 * s + half
                mask = lo if half == 0 else hi
                dmm = jnp.where(mask, dmx, 0.0)
                dbs = dbs + jnp.where(lane == h, jnp.sum(dmm, axis=1, keepdims=True), 0.0)
                dmm = dmm.astype(BF16)
                wt, causal = _tril_w(ws_ref, h)
                dvn = dvn + jnp.where(mask, _tn(wt, dmm), 0.0)
                dws_ref[h] += jnp.where(causal, _nt(dmm, vsb), 0.0)
            dvn_slabs.append(dvn)
        _accumulate(dbs_ref, dbs)
        dvn = jnp.concatenate(dvn_slabs, axis=1)
        du = jnp.concatenate(du_slabs, axis=1)
        _accumulate(dlnb_ref, jnp.sum(dvn, axis=0, keepdims=True))
        _accumulate(dlng_ref, jnp.sum(dvn * xhat, axis=0, keepdims=True))
        dxh = dvn * lng_ref[...]
        dgv = rstd * (dxh - jnp.mean(dxh, axis=-1, keepdims=True)
                      - xhat * jnp.mean(dxh * xhat, axis=-1, keepdims=True))
        dzuv_ref[:, :GM_W] = (du * _gelu_grad(zu)).astype(BF16)
        dzuv_ref[:, GM_W:] = (dgv * _gelu_grad(zv)).astype(BF16)

    return pl.pallas_call(
        body, name="b2_attn_gm", grid=(nb,),
        in_specs=[_rows(BLK, ATTN_W + GM_W)] + _block_specs_z(nb) + [
            pl.BlockSpec(memory_space=pltpu.SMEM), _whole((1, GM_W)), _whole((1, GM_W)),
            _whole((N_Q_HEADS, BLK, BLK)), _whole((BLK, GM_W))],
        out_specs=[_rows(BLK, ATTN_W), _whole((t, 2 * KV_W)), _rows(BLK, 2 * GM_W), _whole((1, LANES)),
                   _whole((1, GM_W)), _whole((1, GM_W)), _whole((N_Q_HEADS, BLK, BLK)), _whole((BLK, LANES))],
        out_shape=[_sds((t, ATTN_W), BF16), _sds((t, 2 * KV_W), F32), _sds((t, 2 * GM_W), BF16),
                   _sds((1, LANES), F32), _sds((1, GM_W), F32), _sds((1, GM_W), F32),
                   _sds((N_Q_HEADS, BLK, BLK), F32), _sds((BLK, LANES), F32)],
        compiler_params=_params())(dam, z, z, z, sinks, ln_g, ln_b, ws, bs_full)


def _b1_norm_in(dzq, dkv, dzuv, w, h, g, dh1):
    t = h.shape[0]
    tm = _tile(t, 512)

    def body(dzq_ref, dkv_ref, dzuv_ref, w_ref, h_ref, g_ref, dh_ref, dz_ref, dh0_ref, dg_ref):
        dz = jnp.concatenate([dzq_ref[...], dkv_ref[...].astype(BF16), dzuv_ref[...]], axis=1)
        dz_ref[...] = dz
        dx, dg = _rms_bwd(_nt(dz, w_ref[...]), h_ref[...], g_ref[...])
        dh0_ref[...] = dh_ref[...] + dx
        _accumulate(dg_ref, dg)

    return pl.pallas_call(
        body, name="b1_norm_in", grid=(t // tm,),
        in_specs=[_rows(tm, ATTN_W), _rows(tm, 2 * KV_W), _rows(tm, 2 * GM_W), _whole((D_MODEL, D_IN)),
                  _rows(tm, D_MODEL), _whole((1, D_MODEL)), _rows(tm, D_MODEL)],
        out_specs=[_rows(tm, D_IN), _rows(tm, D_MODEL), _whole((1, D_MODEL))],
        out_shape=[_sds((t, D_IN), BF16), _sds((t, D_MODEL), F32), _sds((1, D_MODEL), F32)],
        compiler_params=_params())(dzq, dkv, dzuv, w, h, g, dh1)


def _weight_grad(x, dy, name):
    t, k = x.shape
    n = dy.shape[1]
    tm = _tile(t, 512)

    def body(x_ref, dy_ref, dw_ref):
        _accumulate(dw_ref, _tn(x_ref[...].astype(BF16), dy_ref[...]))

    return pl.pallas_call(
        body, name=name, grid=(t // tm,),
        in_specs=[_rows(tm, k), _rows(tm, n)],
        out_specs=_whole((k, n)),
        out_shape=_sds((k, n), F32),
        compiler_params=_params())(x, dy)


_ANY = pl.BlockSpec(memory_space=pl.ANY)
_HBM = pl.BlockSpec(memory_space=pltpu.HBM)
_SEM = pl.BlockSpec(memory_space=pltpu.SEMAPHORE)
_EFFECT = pltpu.SideEffectType.DATAFLOW_SIDE_EFFECTING


def _mesh_pos():
    return lax.axis_index("x"), lax.axis_index("y"), lax.axis_index("c")


def _other_chips(x, y):
    return [(1 - x, y), (x, 1 - y), (1 - x, 1 - y)]


def _peers(mode, x, y, c):
    if mode == "devices":
        peers = []
        for j in range(1, N_DEV):
            px = 1 - x if (j >> 2) & 1 else x
            py = 1 - y if (j >> 1) & 1 else y
            pc = 1 - c if j & 1 else c
            peers.append(((px, py, pc), 4 * px + 2 * py + pc))
        return peers, 4 * x + 2 * y + c
    return [((px, py, c), 2 * px + py) for px, py in _other_chips(x, y)], 2 * x + y


def _descriptors(mode, srcs, lands, send_sems, recv_sems, with_incoming=True):
    x, y, c = _mesh_pos()
    peers, me = _peers(mode, x, y, c)
    scatter = mode == "scatter"
    outgoing, incoming = [], []
    for i, (src, land) in enumerate(zip(srcs, lands)):
        for j, (dev, slot) in enumerate(peers):
            sem = i * len(peers) + j
            common = dict(send_sem=send_sems.at[sem], recv_sem=recv_sems.at[sem], device_id=dev,
                          device_id_type=pl.DeviceIdType.MESH)
            outgoing.append(pltpu.make_async_remote_copy(
                src_ref=src.at[slot] if scatter else src, dst_ref=land.at[me], **common))
            if with_incoming:
                incoming.append(pltpu.make_async_remote_copy(
                    src_ref=src.at[me] if scatter else src, dst_ref=land.at[slot], **common))
    return outgoing, incoming


def _n_sems(mode, k):
    return k * ((N_DEV if mode == "devices" else N_CHIPS) - 1)


def _exchange_start(srcs, lands, mode, name, after=None):
    k = len(srcs)
    arrs = [*srcs, *lands]

    def body(*refs):
        skip = 1 if after is not None else 0
        send_sems, recv_sems = refs[2 * k + skip], refs[2 * k + skip + 1]
        outgoing, _ = _descriptors(mode, refs[:k], refs[k:2 * k], send_sems, recv_sems, with_incoming=False)
        for cp in outgoing:
            cp.start()
        refs[-1][...] = jnp.zeros_like(refs[-1])

    operands = [pltpu.with_memory_space_constraint(a, pltpu.HBM) for a in arrs]
    in_specs = [_HBM] * (2 * k)
    if after is not None:
        operands.append(after)
        in_specs.append(_ANY)
    sems = pltpu.SemaphoreType.DMA((_n_sems(mode, k),))
    res = pl.pallas_call(
        body, name=name, in_specs=in_specs,
        out_shape=(sems, sems, *[pltpu.HBM(a.shape, a.dtype) for a in arrs], _sds((8, LANES), F32)),
        out_specs=(_SEM, _SEM, *[_HBM] * (2 * k), pl.BlockSpec(memory_space=pltpu.VMEM)),
        input_output_aliases={i: 2 + i for i in range(2 * k)},
        compiler_params=pltpu.CompilerParams(has_side_effects=_EFFECT))(*operands)
    return dict(mode=mode, send=res[0], recv=res[1], srcs=res[2:2 + k], lands=res[2 + k:2 + 2 * k], token=res[-1])


def _exchange_wait(flight, name, after):
    mode, k = flight["mode"], len(flight["srcs"])
    arrs = [*flight["srcs"], *flight["lands"]]

    def body(*refs):
        outgoing, incoming = _descriptors(mode, refs[:k], refs[k:2 * k], refs[2 * k], refs[2 * k + 1])
        for cp in outgoing:
            cp.wait_send()
        for cp in incoming:
            cp.wait_recv()

    res = pl.pallas_call(
        body, name=name, in_specs=[_HBM] * (2 * k) + [_SEM, _SEM, _ANY],
        out_shape=tuple(pltpu.HBM(a.shape, a.dtype) for a in arrs), out_specs=tuple([_HBM] * (2 * k)),
        input_output_aliases={i: i for i in range(2 * k)},
        compiler_params=pltpu.CompilerParams(has_side_effects=_EFFECT))(*arrs, flight["send"], flight["recv"], after)
    return res[:k], res[k:]


def _sibling_exchange(arrs, name):
    k = len(arrs)

    def body(*refs):
        ins, outs = refs[:k], refs[k:2 * k]
        send_sems, recv_sems = refs[2 * k:]
        x, y, c = _mesh_pos()
        cps = []
        for i in range(k):
            cp = pltpu.make_async_remote_copy(
                src_ref=ins[i], dst_ref=outs[i], send_sem=send_sems.at[i], recv_sem=recv_sems.at[i],
                device_id=(x, y, 1 - c), device_id_type=pl.DeviceIdType.MESH)
            cp.start()
            cps.append(cp)
        for cp in cps:
            cp.wait()

    return pl.pallas_call(
        body, name=name, in_specs=[_ANY] * k, out_specs=[_ANY] * k,
        out_shape=[_sds(a.shape, a.dtype) for a in arrs],
        scratch_shapes=[pltpu.SemaphoreType.DMA((k,)), pltpu.SemaphoreType.DMA((k,))])(*arrs)


def _adamw(w, g, m, v):
    m = ADAM_B1 * m + (1.0 - ADAM_B1) * g
    v = ADAM_B2 * v + (1.0 - ADAM_B2) * (g * g)
    m_hat = m / (1.0 - ADAM_B1 ** ADAM_STEP)
    v_hat = v / (1.0 - ADAM_B2 ** ADAM_STEP)
    delta = -ADAM_LR * (m_hat / (jnp.sqrt(v_hat) + ADAM_EPS) + ADAM_WD * w)
    return delta, m, v


def _row_tile(rows, cols, n_arrays):
    budget = VMEM_LIMIT_BYTES // 4
    padded = -(-cols // LANES) * LANES
    tr = max(8, budget // (2 * n_arrays * padded * 4))
    tr = min(rows, 1 << (tr.bit_length() - 1))
    while rows % tr:
        tr //= 2
    return tr


def _sum_chips(landing, name):
    _, r, c = landing.shape
    tr = _row_tile(r, c, 4)

    def body(l_ref, s_ref):
        acc = l_ref[0].astype(F32)
        for s in range(1, N_CHIPS):
            acc = acc + l_ref[s].astype(F32)
        s_ref[...] = acc

    return pl.pallas_call(
        body, name=name, grid=(r // tr,),
        in_specs=[pl.BlockSpec((N_CHIPS, tr, c), lambda i: (0, i, 0))],
        out_specs=_rows(tr, c), out_shape=_sds((r, c), F32),
        compiler_params=_params())(landing)


def _adamw_big(mine, sibling, w, m, v, prev, layer, name):
    _, r, c = w.shape
    tr = _row_tile(r, c, 9)
    stacked = pl.BlockSpec((None, tr, c), lambda i: (layer, i, 0))

    def body(a_ref, b_ref, w_ref, m_ref, v_ref, *rest):
        g_out, d_out, m_out, v_out = rest[-4:]
        g = a_ref[...] + b_ref[...]
        g_out[...] = g
        d_out[...], m_out[...], v_out[...] = _adamw(w_ref[...], g, m_ref[...], v_ref[...])

    prev = list(prev) if prev is not None else []
    return pl.pallas_call(
        body, name=name, grid=(r // tr,),
        in_specs=[_rows(tr, c)] * 2 + [stacked] * 3 + [_ANY] * len(prev),
        out_specs=[stacked] * 4, out_shape=[_sds(w.shape, F32)] * 4,
        input_output_aliases={5 + j: j for j in range(len(prev))},
        compiler_params=_params())(mine, sibling, w, m, v, *prev)


def _adamw_small(gathered, w, m, v):
    r = w.shape[0]

    def body(a_ref, w_ref, m_ref, v_ref, g_out, d_out, m_out, v_out):
        g = a_ref[0]
        for d in range(1, N_DEV):
            g = g + a_ref[d]
        g_out[...] = g
        d_out[...], m_out[...], v_out[...] = _adamw(w_ref[...], g, m_ref[...], v_ref[...])

    return pl.pallas_call(
        body, name="adamw_small", grid=(1,),
        in_specs=[_whole((N_DEV, r, LANES))] + [_whole((r, LANES))] * 3,
        out_specs=[_whole((r, LANES))] * 4, out_shape=[_sds((r, LANES), F32)] * 4,
        compiler_params=_params())(gathered, w, m, v)


def _to_shard_major(full, axis):
    k, n = full.shape
    if axis == 2:
        return full.reshape(k, N_CHIPS, n // N_CHIPS).transpose(1, 0, 2)
    return full.reshape(N_CHIPS, k // N_CHIPS, n)


def _from_shard_major(parts, axis):
    _, k, n = parts.shape
    if axis == 2:
        return parts.transpose(1, 0, 2).reshape(k, N_CHIPS * n)
    return parts.reshape(N_CHIPS * k, n)


def _own_slot_filled(parts, slot):
    ids = lax.broadcasted_iota(jnp.int32, (parts.shape[0],) + (1,) * (parts.ndim - 1), 0)
    return jnp.where(ids == slot, parts, jnp.zeros_like(parts))


def _pack_small(params, names):
    pieces = []
    for name in names:
        flat = params[name].reshape(-1)
        pieces.append(jnp.pad(flat, (0, -flat.shape[0] % SMALL_PAD)))
    return jnp.concatenate(pieces).reshape(-1, LANES)


def _unpack_small(packed, like, names):
    flat = packed.reshape(-1)
    out, off = {}, 0
    for name in names:
        size = like[name].size
        out[name] = flat[off:off + size].reshape(like[name].shape)
        off += size + (-size % SMALL_PAD)
    return out


def kernel(x, p, ln_mix_pre, w_in, attn_sinks, gm_ln_g, gm_ln_b, gm_ws, gm_bs, g_attn_out, g_gm_out, w_out, ln_mix_post, ln_ffn_pre, w_ffn_gate, w_ffn_up, w_ffn_down, ln_ffn_post, w_ple, ln_ple_gate, w_ple_gate, loss_target, m_ln_mix_pre, m_w_in, m_attn_sinks, m_gm_ln_g, m_gm_ln_b, m_gm_ws, m_gm_bs, m_g_attn_out, m_g_gm_out, m_w_out, m_ln_mix_post, m_ln_ffn_pre, m_w_ffn_gate, m_w_ffn_up, m_w_ffn_down, m_ln_ffn_post, m_w_ple, m_ln_ple_gate, m_w_ple_gate, v_ln_mix_pre, v_w_in, v_attn_sinks, v_gm_ln_g, v_gm_ln_b, v_gm_ws, v_gm_bs, v_g_attn_out, v_g_gm_out, v_w_out, v_ln_mix_post, v_ln_ffn_pre, v_w_ffn_gate, v_w_ffn_up, v_w_ffn_down, v_ln_ffn_post, v_w_ple, v_ln_ple_gate, v_w_ple_gate):
    given = dict(locals())
    wts = {n: given[n] for n in WEIGHTS}
    mom = {n: given["m_" + n] for n in WEIGHTS}
    var = {n: given["v_" + n] for n in WEIGHTS}
    depth = w_in.shape[0]
    h = x[0]
    target = loss_target[0]
    chip = 2 * lax.axis_index("x") + lax.axis_index("y")
    device = 2 * chip + lax.axis_index("c")
    row = lambda a, i: a[i][None, :]
    bs_full = [jnp.repeat(gm_bs[i].T, HEAD_DIM, axis=1) for i in range(depth)]
    kinds = ("grad", "delta", "m", "v")

    zero = lambda flight: flight["token"][0:1, 0:1]

    def start_gather(i, names, after, tag):
        shards = [wts[n][i].astype(BF16) for n in names]
        lands = [_own_slot_filled(jnp.broadcast_to(s, (N_CHIPS,) + s.shape), chip) for s in shards]
        return _exchange_start(shards, lands, "gather", f"gather_weights_start_{i}{tag}", after)

    def finish_gather(flight, names, i, after, tag):
        _, lands = _exchange_wait(flight, f"gather_weights_wait_{i}{tag}", after)
        return {n: _from_shard_major(l, BIG[n]) for n, l in zip(names, lands)}

    first, after = [], None
    for k, names in enumerate(GATHER_GROUPS):
        first.append(start_gather(0, names, after, "abc"[k]))
        after = first[-1]["token"]
    full = [finish_gather(first[0], GATHER_GROUPS[0], 0, after, "a")] + [None] * (depth - 1)
    saved = []
    for i in range(depth):
        w = full[i]
        g_in = row(ln_mix_pre, i)
        if i + 1 < depth:
            flight = start_gather(i + 1, BIG_NAMES, w['w_in'], "")
            g_in = g_in + zero(flight)
        z, a = _f1_norm_in(h, g_in, w['w_in'])
        am = _f2_attn_gm(z, attn_sinks[i], row(gm_ln_g, i), row(gm_ln_b, i), gm_ws[i], bs_full[i])
        if i == 0:
            w.update(finish_gather(first[1], GATHER_GROUPS[1], 0, am, "b"))
        heads, mix, h1 = _f3_mix_out(am, h, row(g_attn_out, i), row(g_gm_out, i), w['w_out'], row(ln_mix_post, i))
        f, gt, up = _f4a_ffn_in(h1, row(ln_ffn_pre, i), w['w_ffn_gate'], w['w_ffn_up'])
        if i == 0:
            w.update(finish_gather(first[2], GATHER_GROUPS[2], 0, gt, "c"))
        dn, h2 = _f4b_ffn_out(gt, up, w['w_ffn_down'], h1, row(ln_ffn_post, i))
        r, pg, pe, h3 = _f5_ple(h2, row(ln_ple_gate, i), w['w_ple_gate'], p[i, 0], w['w_ple'])
        saved.append(dict(h=h, z=z, a=a, am=am, heads=heads, mix=mix, h1=h1, f=f, gt=gt, up=up, dn=dn, h2=h2,
                          r=r, pg=pg, pe=pe))
        h = h3
        if i + 1 < depth:
            full[i + 1] = finish_gather(flight, BIG_NAMES, i + 1, h3, "")

    sq, dh = _loss_head(h, target)
    loss = lax.psum(0.5 / D_MODEL * sq[0, 0], ("x", "y", "c"))

    chain = {n: None for n in BIG_NAMES}
    small_out = [{k: {} for k in kinds} for _ in range(depth)]

    def start_scatter(i, names, dws, after, tag):
        parts = [_to_shard_major(dws[n], BIG[n]).astype(BF16) for n in names]
        return _exchange_start(parts, [_own_slot_filled(q, chip) for q in parts], "scatter",
                               f"scatter_grads_start_{i}{tag}", after)

    def finish_scatter(i, names, flight, after, tag):
        _, lands = _exchange_wait(flight, f"scatter_grads_wait_{i}{tag}", after)
        partial = [_sum_chips(l, "sum_chips_" + n) for n, l in zip(names, lands)]
        sibling = _sibling_exchange(partial, name="sibling_grads")
        for n, mine, sib in zip(names, partial, sibling):
            chain[n] = _adamw_big(mine, sib, wts[n], mom[n], var[n], chain[n], i, "adamw_" + n)
        return lands[0]

    def start_small(i, names, small, after, tag):
        packed = _pack_small(small, names)
        land = _own_slot_filled(jnp.broadcast_to(packed, (N_DEV,) + packed.shape), device)
        return _exchange_start([packed], [land], "devices", f"gather_small_grads_start_{i}{tag}", after)

    def finish_small(i, names, flight, after, tag):
        _, (gathered,) = _exchange_wait(flight, f"gather_small_grads_wait_{i}{tag}", after)
        layer = lambda d: _pack_small({n: d[n][i] for n in names}, names)
        res = _adamw_small(gathered, layer(wts), layer(mom), layer(var))
        for k, a in zip(kinds, res):
            small_out[i][k].update(_unpack_small(a, {n: wts[n][i] for n in names}, names))
        return gathered

    pending, done, behind = [], None, None
    for i in reversed(range(depth)):
        s, w = saved[i], full[i]
        last = i == 0
        dws, small = {}, {}
        gain = row(ln_ple_gate, i)
        if behind is not None:
            gain = gain + behind
        dpe, dpg, dh2, dg = _b5_ple(dh, s['h2'], s['pg'], s['pe'], gain, w['w_ple_gate'])
        small['ln_ple_gate'] = dg[0]
        dws['w_ple'] = _weight_grad(p[i, 0], dpe, "dw_ple")
        dws['w_ple_gate'] = _weight_grad(s['r'], dpg, "dw_ple_gate")

        ddn, act, dgt, dup, dg = _b4a_ffn_out(dh2, s['dn'], row(ln_ffn_post, i), w['w_ffn_down'], s['gt'], s['up'])
        small['ln_ffn_post'] = dg[0]
        dws['w_ffn_down'] = _weight_grad(act, ddn, "dw_ffn_down")
        dws['w_ffn_gate'] = _weight_grad(s['f'], dgt, "dw_ffn_gate")
        dws['w_ffn_up'] = _weight_grad(s['f'], dup, "dw_ffn_up")
        gain = row(ln_ffn_pre, i)
        if last:
            flight_a = start_scatter(i, SCATTER_GROUPS[0], dws, None, "a")
            gain = gain + zero(flight_a)
        dh1, dg = _b4b_ffn_in(dgt, dup, w['w_ffn_gate'], w['w_ffn_up'], s['h1'], gain, dh2)
        small['ln_ffn_pre'] = dg[0]

        dmix, dam, dgp, dga, dgg = _b3_mix_out(dh1, s['mix'], row(ln_mix_post, i), w['w_out'], s['am'],
                                               row(g_attn_out, i), row(g_gm_out, i))
        small['ln_mix_post'] = dgp[0]
        small['g_attn_out'] = dga[0]
        small['g_gm_out'] = dgg[0]
        dws['w_out'] = _weight_grad(s['heads'], dmix, "dw_out")
        gain = row(gm_ln_g, i)
        if last:
            flight_b = start_scatter(i, SCATTER_GROUPS[1], dws, flight_a["token"], "b")
            gain = gain + zero(flight_b)

        dzq, dkv, dzuv, dsink, dlng, dlnb, dgws, dbs = _b2_attn_gm(
            dam, s['z'], attn_sinks[i], gain, row(gm_ln_b, i), gm_ws[i], bs_full[i])
        small['attn_sinks'] = dsink[0, :N_Q_HEADS]
        small['gm_ln_g'] = dlng[0]
        small['gm_ln_b'] = dlnb[0]
        small['gm_ws'] = dgws
        small['gm_bs'] = dbs[:, :N_Q_HEADS].T
        gain = row(ln_mix_pre, i)
        if last:
            flight_s = start_small(i, SMALL_EARLY, small, flight_b["token"], "a")
            gain = gain + zero(flight_s)

        dz, dh, dg = _b1_norm_in(dzq, dkv, dzuv, w['w_in'], s['h'], gain, dh1)
        small['ln_mix_pre'] = dg[0]
        for finish in pending:
            done = finish(dh)
        pending = []
        if last:
            done = finish_scatter(i, SCATTER_GROUPS[0], flight_a, dz, "a")
            done = finish_scatter(i, SCATTER_GROUPS[1], flight_b, done, "b")
        dws['w_in'] = _weight_grad(s['a'], dz, "dw_in")
        if last:
            flight_c = start_scatter(i, SCATTER_GROUPS[2], dws, done, "c")
            flight_t = start_small(i, SMALL_LATE, small, flight_c["token"], "b")
            finish_small(i, SMALL_EARLY, flight_s, flight_t["token"], "a")
            finish_scatter(i, SCATTER_GROUPS[2], flight_c, flight_t["token"], "c")
            finish_small(i, SMALL_LATE, flight_t, flight_t["token"], "b")
        else:
            flight_c = start_scatter(i, BIG_NAMES, dws, done, "")
            flight_s = start_small(i, SMALL_NAMES, small, flight_c["token"], "")
            behind = zero(flight_s)
            pending = [functools.partial(finish_scatter, i, BIG_NAMES, flight_c, tag=""),
                       functools.partial(finish_small, i, SMALL_NAMES, flight_s, tag="")]
    grad_x = dh[None]

    out = {k: {n: chain[n][j] for n in BIG_NAMES} for j, k in enumerate(kinds)}
    for k in kinds:
        out[k].update({n: jnp.stack([small_out[i][k][n] for i in range(depth)]) for n in SMALL_NAMES})

    return (loss, grad_x, *[out["grad"][n] for n in WEIGHTS], *[out["delta"][n] for n in WEIGHTS],
            *[out["m"][n] for n in WEIGHTS], *[out["v"][n] for n in WEIGHTS])
```

```python
import functools
import math

import jax
import jax.numpy as jnp
from jax import lax
from jax.experimental import pallas as pl
from jax.experimental.pallas import tpu as pltpu

F32 = jnp.float32
BF16 = jnp.bfloat16

D_MODEL = 1024
HEAD_DIM = 64
N_Q_HEADS = 8
BLK = 128
ATTN_W = 512
KV_W = 128
GM_W = 512
D_IN = ATTN_W + 2 * KV_W + 2 * GM_W
D_FF = 2816
PLE_DIM = 256
DEPTH = 4
NORM_EPS = 1e-6
NEG_BIG = -1e30
N_CHIPS = 4
N_DEV = 8

ADAM_LR = 0.001
ADAM_B1 = 0.9
ADAM_B2 = 0.999
ADAM_EPS = 1e-08
ADAM_WD = 0.01
ADAM_STEP = 10

VMEM_LIMIT_BYTES = 56 * 1024 * 1024
LANES = 128
GELU_C0 = math.sqrt(2.0 / math.pi)
GELU_C1 = 0.044715
ALIBI_SLOPES = tuple(2.0 ** (-8.0 * (h + 1.0) / N_Q_HEADS) for h in range(N_Q_HEADS))

WEIGHTS = ['ln_mix_pre', 'w_in', 'attn_sinks', 'gm_ln_g', 'gm_ln_b', 'gm_ws', 'gm_bs', 'g_attn_out',
           'g_gm_out', 'w_out', 'ln_mix_post', 'ln_ffn_pre', 'w_ffn_gate', 'w_ffn_up', 'w_ffn_down',
           'ln_ffn_post', 'w_ple', 'ln_ple_gate', 'w_ple_gate']
BIG = {'w_in': 2, 'w_out': 1, 'w_ffn_gate': 2, 'w_ffn_up': 2, 'w_ffn_down': 1, 'w_ple': 2, 'w_ple_gate': 1}
BIG_NAMES = list(BIG)
TRANSPOSED = ('w_in', 'w_ffn_gate', 'w_ffn_up')
SMALL_NAMES = [n for n in WEIGHTS if n not in BIG]
SMALL_PAD = 1024
GATHER_GROUPS = (('w_in',), ('w_out', 'w_ffn_gate', 'w_ffn_up'), ('w_ffn_down', 'w_ple_gate', 'w_ple'))
SCATTER_GROUPS = (('w_ple', 'w_ple_gate', 'w_ffn_down', 'w_ffn_gate', 'w_ffn_up'), ('w_out',), ('w_in',))
SMALL_LATE = ('ln_mix_pre',)
SMALL_EARLY = tuple(n for n in SMALL_NAMES if n not in SMALL_LATE)


def _nt(a, b):
    return lax.dot_general(a, b, (((1,), (1,)), ((), ())), preferred_element_type=F32)


def _tn(a, b):
    return lax.dot_general(a, b, (((0,), (0,)), ((), ())), preferred_element_type=F32)


def _mm(a, b):
    return jnp.dot(a, b, preferred_element_type=F32)


def _rms(x, g):
    r = lax.rsqrt(jnp.mean(x * x, axis=-1, keepdims=True) + NORM_EPS)
    return x * r * g


def _rms_bwd(dy, x, g):
    r = lax.rsqrt(jnp.mean(x * x, axis=-1, keepdims=True) + NORM_EPS)
    xh = x * r
    dg = jnp.sum(dy * xh, axis=0, keepdims=True)
    dxh = dy * g
    dx = r * (dxh - xh * jnp.mean(dxh * xh, axis=-1, keepdims=True))
    return dx, dg


def _gelu(x):
    return 0.5 * x * (1.0 + jnp.tanh(GELU_C0 * (x + GELU_C1 * x * x * x)))


def _gelu_grad(x):
    t = jnp.tanh(GELU_C0 * (x + GELU_C1 * x * x * x))
    return 0.5 * (1.0 + t) + 0.5 * x * (1.0 - t * t) * GELU_C0 * (1.0 + 3.0 * GELU_C1 * x * x)


def _sigmoid(x):
    return 1.0 / (1.0 + jnp.exp(-x))


def _rows(tm, n):
    return pl.BlockSpec((tm, n), lambda i: (i, 0))


def _whole(shape):
    return pl.BlockSpec(shape, lambda i: (0,) * len(shape))


def _accumulate(ref, val):
    @pl.when(pl.program_id(0) == 0)
    def _():
        ref[...] = jnp.zeros_like(ref)

    ref[...] += val


def _params(n_axes=1):
    return pltpu.CompilerParams(dimension_semantics=("arbitrary",) * n_axes,
                                vmem_limit_bytes=VMEM_LIMIT_BYTES)


def _sds(shape, dtype):
    return jax.ShapeDtypeStruct(shape, dtype)


def _tile(t, want):
    return min(t, want)


def _f1_norm_in(h, g, w_t):
    t = h.shape[0]
    tm = _tile(t, 512)

    def body(h_ref, g_ref, w_ref, z_ref, a_ref):
        a = _rms(h_ref[...], g_ref[...]).astype(BF16)
        a_ref[...] = a
        z_ref[...] = _nt(a, w_ref[...])

    return pl.pallas_call(
        body, name="f1_norm_in", grid=(t // tm,),
        in_specs=[_rows(tm, D_MODEL), _whole((1, D_MODEL)), _whole((D_IN, D_MODEL))],
        out_specs=[_rows(tm, D_IN), _rows(tm, D_MODEL)],
        out_shape=[_sds((t, D_IN), F32), _sds((t, D_MODEL), BF16)],
        compiler_params=_params())(h, g, w_t)


def _attn_geometry(n):
    ti = lax.broadcasted_iota(jnp.int32, (BLK, 2 * BLK), 0)
    ji = lax.broadcasted_iota(jnp.int32, (BLK, 2 * BLK), 1)
    dist = ti + BLK - ji
    valid = (dist >= 0) & (dist < BLK) & ((n > 0) | (ji >= BLK))
    return valid, dist.astype(F32)


def _attn_probs(qm, kuse, valid, distf, slope, sink):
    sc = _nt(qm, kuse) * (HEAD_DIM ** -0.5)
    sc = jnp.where(valid, sc - slope * distf, NEG_BIG)
    m = jnp.maximum(jnp.max(sc, axis=1, keepdims=True), sink)
    e = jnp.exp(sc - m)
    es = jnp.exp(sink - m)
    inv = 1.0 / (jnp.sum(e, axis=1, keepdims=True) + es)
    return e * inv, es * inv


def _kv_window(z_ref, kp_ref, vp_ref):
    kcat = jnp.concatenate([kp_ref[...], z_ref[:, ATTN_W:ATTN_W + KV_W]], axis=0)
    vcat = jnp.concatenate([vp_ref[...], z_ref[:, ATTN_W + KV_W:ATTN_W + 2 * KV_W]], axis=0)
    kswap = pltpu.roll(kcat, HEAD_DIM, 1)
    vswap = pltpu.roll(vcat, HEAD_DIM, 1)
    return kcat.astype(BF16), kswap.astype(BF16), vcat, vswap


def _gm_forward(z_ref, lng_ref, lnb_ref):
    zu = z_ref[:, ATTN_W + 2 * KV_W:ATTN_W + 2 * KV_W + GM_W]
    zv = z_ref[:, ATTN_W + 2 * KV_W + GM_W:D_IN]
    u = _gelu(zu)
    gv = _gelu(zv)
    xc = gv - jnp.mean(gv, axis=-1, keepdims=True)
    rstd = lax.rsqrt(jnp.mean(xc * xc, axis=-1, keepdims=True) + NORM_EPS)
    xhat = xc * rstd
    vn = xhat * lng_ref[...] + lnb_ref[...]
    return zu, zv, u, xhat, rstd, vn


def _tril_w(ws_ref, h):
    ti = lax.broadcasted_iota(jnp.int32, (BLK, BLK), 0)
    si = lax.broadcasted_iota(jnp.int32, (BLK, BLK), 1)
    causal = si <= ti
    return jnp.where(causal, ws_ref[h], 0.0).astype(BF16), causal


def _gm_mixed(vn, ws_ref, bs_ref, lo, hi):
    slabs = []
    for s in range(GM_W // LANES):
        vs = vn[:, s * LANES:(s + 1) * LANES]
        w0, _ = _tril_w(ws_ref, 2 * s)
        w1, _ = _tril_w(ws_ref, 2 * s + 1)
        mixed = (_mm(w0, jnp.where(lo, vs, 0.0).astype(BF16))
                 + _mm(w1, jnp.where(hi, vs, 0.0).astype(BF16))
                 + bs_ref[:, s * LANES:(s + 1) * LANES])
        slabs.append(mixed)
    return slabs


def _block_specs_z(nb):
    prev = lambda i: (jnp.maximum(i - 1, 0), ATTN_W // KV_W)
    prev_v = lambda i: (jnp.maximum(i - 1, 0), ATTN_W // KV_W + 1)
    return [_rows(BLK, D_IN), pl.BlockSpec((BLK, KV_W), prev), pl.BlockSpec((BLK, KV_W), prev_v)]


def _f2_attn_gm(z, sinks, ln_g, ln_b, ws, bs_full):
    t = z.shape[0]
    nb = t // BLK

    def body(z_ref, kp_ref, vp_ref, sink_ref, lng_ref, lnb_ref, ws_ref, bs_ref, am_ref):
        n = pl.program_id(0)
        lane = lax.broadcasted_iota(jnp.int32, (1, LANES), 1)
        lo = lane < HEAD_DIM
        hi = lane >= HEAD_DIM
        valid, distf = _attn_geometry(n)
        kc, ks, vcat, vswap = _kv_window(z_ref, kp_ref, vp_ref)
        for s in range(ATTN_W // LANES):
            kv_head = s // 2
            qs = z_ref[:, s * LANES:(s + 1) * LANES]
            o = jnp.zeros((BLK, LANES), F32)
            for half in range(2):
                hq = 2 * s + half
                mask = lo if half == 0 else hi
                same = kv_head == half
                qm = jnp.where(mask, qs, 0.0).astype(BF16)
                pr, _ = _attn_probs(qm, kc if same else ks, valid, distf, ALIBI_SLOPES[hq], sink_ref[hq])
                vm = jnp.where(mask, vcat if same else vswap, 0.0).astype(BF16)
                o = o + _mm(pr.astype(BF16), vm)
            am_ref[:, s * LANES:(s + 1) * LANES] = o
        _, _, u, _, _, vn = _gm_forward(z_ref, lng_ref, lnb_ref)
        mixed = _gm_mixed(vn, ws_ref, bs_ref, lo, hi)
        for s in range(GM_W // LANES):
            am_ref[:, ATTN_W + s * LANES:ATTN_W + (s + 1) * LANES] = u[:, s * LANES:(s + 1) * LANES] * mixed[s]

    return pl.pallas_call(
        body, name="f2_attn_gm", grid=(nb,),
        in_specs=_block_specs_z(nb) + [
            pl.BlockSpec(memory_space=pltpu.SMEM), _whole((1, GM_W)), _whole((1, GM_W)),
            _whole((N_Q_HEADS, BLK, BLK)), _whole((BLK, GM_W))],
        out_specs=_rows(BLK, ATTN_W + GM_W),
        out_shape=_sds((t, ATTN_W + GM_W), F32),
        compiler_params=_params())(z, z, z, sinks, ln_g, ln_b, ws, bs_full)


def _f3_mix_out(am, h, ga, gg, w, gpost):
    t = h.shape[0]
    tm = _tile(t, 512)

    def body(am_ref, h_ref, ga_ref, gg_ref, w_ref, gp_ref, heads_ref, mix_ref, h1_ref):
        heads = jnp.concatenate([_rms(am_ref[:, :ATTN_W], ga_ref[...]),
                                 _rms(am_ref[:, ATTN_W:], gg_ref[...])], axis=1).astype(BF16)
        heads_ref[...] = heads
        mix = _mm(heads, w_ref[...])
        mix_ref[...] = mix
        h1_ref[...] = h_ref[...] + _rms(mix, gp_ref[...])

    return pl.pallas_call(
        body, name="f3_mix_out", grid=(t // tm,),
        in_specs=[_rows(tm, D_MODEL), _rows(tm, D_MODEL), _whole((1, ATTN_W)), _whole((1, GM_W)),
                  _whole((D_MODEL, D_MODEL)), _whole((1, D_MODEL))],
        out_specs=[_rows(tm, D_MODEL)] * 3,
        out_shape=[_sds((t, D_MODEL), BF16), _sds((t, D_MODEL), F32), _sds((t, D_MODEL), F32)],
        compiler_params=_params())(am, h, ga, gg, w, gpost)


def _f4a_ffn_in(h1, gf, wg_t, wu_t):
    t = h1.shape[0]
    tm = _tile(t, 256)

    def body(h_ref, g_ref, wg_ref, wu_ref, f_ref, gt_ref, up_ref):
        f = _rms(h_ref[...], g_ref[...]).astype(BF16)
        f_ref[...] = f
        gt_ref[...] = _nt(f, wg_ref[...])
        up_ref[...] = _nt(f, wu_ref[...])

    return pl.pallas_call(
        body, name="f4a_ffn_in", grid=(t // tm,),
        in_specs=[_rows(tm, D_MODEL), _whole((1, D_MODEL)), _whole((D_FF, D_MODEL)), _whole((D_FF, D_MODEL))],
        out_specs=[_rows(tm, D_MODEL), _rows(tm, D_FF), _rows(tm, D_FF)],
        out_shape=[_sds((t, D_MODEL), BF16), _sds((t, D_FF), F32), _sds((t, D_FF), F32)],
        compiler_params=_params())(h1, gf, wg_t, wu_t)


def _f4b_ffn_out(gt, up, wd, h1, gfp):
    t = h1.shape[0]
    tm = _tile(t, 256)

    def body(gt_ref, up_ref, wd_ref, h_ref, g_ref, dn_ref, h2_ref):
        gt = gt_ref[...]
        act = (gt * _sigmoid(gt) * up_ref[...]).astype(BF16)
        dn = _mm(act, wd_ref[...])
        dn_ref[...] = dn
        h2_ref[...] = h_ref[...] + _rms(dn, g_ref[...])

    return pl.pallas_call(
        body, name="f4b_ffn_out", grid=(t // tm,),
        in_specs=[_rows(tm, D_FF), _rows(tm, D_FF), _whole((D_FF, D_MODEL)), _rows(tm, D_MODEL),
                  _whole((1, D_MODEL))],
        out_specs=[_rows(tm, D_MODEL)] * 2,
        out_shape=[_sds((t, D_MODEL), F32)] * 2,
        compiler_params=_params())(gt, up, wd, h1, gfp)


def _f5_ple(h2, gpl, wpg, p, wple):
    t = h2.shape[0]
    tm = _tile(t, 512)

    def body(h_ref, g_ref, wpg_ref, p_ref, wple_ref, r_ref, pg_ref, pe_ref, h3_ref):
        h = h_ref[...]
        r = _rms(h, g_ref[...]).astype(BF16)
        r_ref[...] = r
        pg = _mm(r, wpg_ref[...])
        pe = _mm(p_ref[...].astype(BF16), wple_ref[...])
        pg_ref[...] = pg
        pe_ref[...] = pe
        h3_ref[...] = h + pe * _sigmoid(pg)

    return pl.pallas_call(
        body, name="f5_ple", grid=(t // tm,),
        in_specs=[_rows(tm, D_MODEL), _whole((1, D_MODEL)), _whole((D_MODEL, D_MODEL)), _rows(tm, PLE_DIM),
                  _whole((PLE_DIM, D_MODEL))],
        out_specs=[_rows(tm, D_MODEL)] * 4,
        out_shape=[_sds((t, D_MODEL), BF16)] + [_sds((t, D_MODEL), F32)] * 3,
        compiler_params=_params())(h2, gpl, wpg, p, wple)


def _loss_head(y, target):
    t = y.shape[0]
    tm = _tile(t, 512)

    def body(y_ref, t_ref, sq_ref, dy_ref):
        err = y_ref[...] - t_ref[...]
        dy_ref[...] = err * (1.0 / D_MODEL)
        _accumulate(sq_ref, jnp.sum(err * err, keepdims=True))

    return pl.pallas_call(
        body, name="loss_head", grid=(t // tm,),
        in_specs=[_rows(tm, D_MODEL)] * 2,
        out_specs=[_whole((1, LANES)), _rows(tm, D_MODEL)],
        out_shape=[_sds((1, LANES), F32), _sds((t, D_MODEL), F32)],
        compiler_params=_params())(y, target)


def _b5_ple(dh3, h2, pg, pe, gpl, wpg):
    t = h2.shape[0]
    tm = _tile(t, 512)

    def body(dh_ref, h_ref, pg_ref, pe_ref, g_ref, w_ref, dpe_ref, dpg_ref, dh2_ref, dg_ref):
        dh = dh_ref[...]
        s = _sigmoid(pg_ref[...])
        dpe_ref[...] = (dh * s).astype(BF16)
        dpg = (dh * pe_ref[...] * s * (1.0 - s)).astype(BF16)
        dpg_ref[...] = dpg
        dx, dg = _rms_bwd(_nt(dpg, w_ref[...]), h_ref[...], g_ref[...])
        dh2_ref[...] = dh + dx
        _accumulate(dg_ref, dg)

    return pl.pallas_call(
        body, name="b5_ple", grid=(t // tm,),
        in_specs=[_rows(tm, D_MODEL)] * 4 + [_whole((1, D_MODEL)), _whole((D_MODEL, D_MODEL))],
        out_specs=[_rows(tm, D_MODEL)] * 3 + [_whole((1, D_MODEL))],
        out_shape=[_sds((t, D_MODEL), BF16)] * 2 + [_sds((t, D_MODEL), F32), _sds((1, D_MODEL), F32)],
        compiler_params=_params())(dh3, h2, pg, pe, gpl, wpg)


def _b4a_ffn_out(dh2, dn, gfp, wd, gt, up):
    t = dh2.shape[0]
    tm = _tile(t, 256)

    def body(dh_ref, dn_ref, g_ref, wd_ref, gt_ref, up_ref, ddn_ref, act_ref, dgt_ref, dup_ref, dg_ref):
        ddn, dg = _rms_bwd(dh_ref[...], dn_ref[...], g_ref[...])
        _accumulate(dg_ref, dg)
        ddn = ddn.astype(BF16)
        ddn_ref[...] = ddn
        dact = _nt(ddn, wd_ref[...])
        gt = gt_ref[...]
        up = up_ref[...]
        sg = _sigmoid(gt)
        silu = gt * sg
        act_ref[...] = (silu * up).astype(BF16)
        dup_ref[...] = (dact * silu).astype(BF16)
        dgt_ref[...] = (dact * up * (sg * (1.0 + gt * (1.0 - sg)))).astype(BF16)

    return pl.pallas_call(
        body, name="b4a_ffn_out", grid=(t // tm,),
        in_specs=[_rows(tm, D_MODEL), _rows(tm, D_MODEL), _whole((1, D_MODEL)), _whole((D_FF, D_MODEL)),
                  _rows(tm, D_FF), _rows(tm, D_FF)],
        out_specs=[_rows(tm, D_MODEL), _rows(tm, D_FF), _rows(tm, D_FF), _rows(tm, D_FF), _whole((1, D_MODEL))],
        out_shape=[_sds((t, D_MODEL), BF16)] + [_sds((t, D_FF), BF16)] * 3 + [_sds((1, D_MODEL), F32)],
        compiler_params=_params())(dh2, dn, gfp, wd, gt, up)


def _b4b_ffn_in(dgt, dup, wg_t, wu_t, h1, gf, dh2):
    t = h1.shape[0]
    tm = _tile(t, 512)

    def body(dgt_ref, dup_ref, wg_ref, wu_ref, h_ref, g_ref, dh_ref, dh1_ref, dg_ref):
        df = _mm(dgt_ref[...], wg_ref[...]) + _mm(dup_ref[...], wu_ref[...])
        dx, dg = _rms_bwd(df, h_ref[...], g_ref[...])
        dh1_ref[...] = dh_ref[...] + dx
        _accumulate(dg_ref, dg)

    return pl.pallas_call(
        body, name="b4b_ffn_in", grid=(t // tm,),
        in_specs=[_rows(tm, D_FF), _rows(tm, D_FF), _whole((D_FF, D_MODEL)), _whole((D_FF, D_MODEL)),
                  _rows(tm, D_MODEL), _whole((1, D_MODEL)), _rows(tm, D_MODEL)],
        out_specs=[_rows(tm, D_MODEL), _whole((1, D_MODEL))],
        out_shape=[_sds((t, D_MODEL), F32), _sds((1, D_MODEL), F32)],
        compiler_params=_params())(dgt, dup, wg_t, wu_t, h1, gf, dh2)


def _b3_mix_out(dh1, mix, gpost, w, am, ga, gg):
    t = dh1.shape[0]
    tm = _tile(t, 512)

    def body(dh_ref, mix_ref, gp_ref, w_ref, am_ref, ga_ref, gg_ref, dmix_ref, dam_ref, dgp_ref, dga_ref, dgg_ref):
        dmix, dgp = _rms_bwd(dh_ref[...], mix_ref[...], gp_ref[...])
        _accumulate(dgp_ref, dgp)
        dmix = dmix.astype(BF16)
        dmix_ref[...] = dmix
        dheads = _nt(dmix, w_ref[...])
        dat, dga = _rms_bwd(dheads[:, :ATTN_W], am_ref[:, :ATTN_W], ga_ref[...])
        dgm, dgg = _rms_bwd(dheads[:, ATTN_W:], am_ref[:, ATTN_W:], gg_ref[...])
        dam_ref[:, :ATTN_W] = dat
        dam_ref[:, ATTN_W:] = dgm
        _accumulate(dga_ref, dga)
        _accumulate(dgg_ref, dgg)

    return pl.pallas_call(
        body, name="b3_mix_out", grid=(t // tm,),
        in_specs=[_rows(tm, D_MODEL), _rows(tm, D_MODEL), _whole((1, D_MODEL)), _whole((D_MODEL, D_MODEL)),
                  _rows(tm, D_MODEL), _whole((1, ATTN_W)), _whole((1, GM_W))],
        out_specs=[_rows(tm, D_MODEL), _rows(tm, D_MODEL), _whole((1, D_MODEL)), _whole((1, ATTN_W)),
                   _whole((1, GM_W))],
        out_shape=[_sds((t, D_MODEL), BF16), _sds((t, D_MODEL), F32), _sds((1, D_MODEL), F32),
                   _sds((1, ATTN_W), F32), _sds((1, GM_W), F32)],
        compiler_params=_params())(dh1, mix, gpost, w, am, ga, gg)


def _b2_attn_gm(dam, z, sinks, ln_g, ln_b, ws, bs_full):
    t = z.shape[0]
    nb = t // BLK

    def body(dam_ref, z_ref, kp_ref, vp_ref, sink_ref, lng_ref, lnb_ref, ws_ref, bs_ref,
             dzq_ref, dkv_ref, dzuv_ref, dsink_ref, dlng_ref, dlnb_ref, dws_ref, dbs_ref):
        n = pl.program_id(0)
        lane = lax.broadcasted_iota(jnp.int32, (1, LANES), 1)
        lo = lane < HEAD_DIM
        hi = lane >= HEAD_DIM
        valid, distf = _attn_geometry(n)
        kc, ks, vcat, vswap = _kv_window(z_ref, kp_ref, vp_ref)
        vc = vcat.astype(BF16)
        vs_ = vswap.astype(BF16)
        dk_acc = jnp.zeros((2 * BLK, LANES), F32)
        dv_acc = jnp.zeros((2 * BLK, LANES), F32)
        dsink = jnp.zeros((1, LANES), F32)
        for s in range(ATTN_W // LANES):
            kv_head = s // 2
            qs = z_ref[:, s * LANES:(s + 1) * LANES]
            dos = dam_ref[:, s * LANES:(s + 1) * LANES]
            dq = jnp.zeros((BLK, LANES), F32)
            for half in range(2):
                hq = 2 * s + half
                mask = lo if half == 0 else hi
                same = kv_head == half
                kuse = kc if same else ks
                vuse = vc if same else vs_
                qm = jnp.where(mask, qs, 0.0).astype(BF16)
                dom = jnp.where(mask, dos, 0.0).astype(BF16)
                pr, ps = _attn_probs(qm, kuse, valid, distf, ALIBI_SLOPES[hq], sink_ref[hq])
                dpr = _nt(dom, vuse)
                row = jnp.sum(dpr * pr, axis=1, keepdims=True)
                ds = (pr * (dpr - row) * (HEAD_DIM ** -0.5)).astype(BF16)
                dsink = dsink + jnp.where(lane == hq, -jnp.sum(ps * row, keepdims=True), 0.0)
                dq = dq + jnp.where(mask, _mm(ds, kuse), 0.0)
                dkc = _tn(ds, qm)
                dvc = _tn(pr.astype(BF16), dom)
                if not same:
                    dkc = pltpu.roll(dkc, HEAD_DIM, 1)
                    dvc = pltpu.roll(dvc, HEAD_DIM, 1)
                dk_acc = dk_acc + dkc
                dv_acc = dv_acc + dvc
            dzq_ref[:, s * LANES:(s + 1) * LANES] = dq.astype(BF16)
        cur = pl.multiple_of(n * BLK, BLK)
        dkv_ref[pl.ds(cur, BLK), 0:KV_W] = dk_acc[BLK:, :]
        dkv_ref[pl.ds(cur, BLK), KV_W:2 * KV_W] = dv_acc[BLK:, :]

        @pl.when(n > 0)
        def _():
            prv = pl.multiple_of((n - 1) * BLK, BLK)
            dkv_ref[pl.ds(prv, BLK), 0:KV_W] += dk_acc[:BLK, :]
            dkv_ref[pl.ds(prv, BLK), KV_W:2 * KV_W] += dv_acc[:BLK, :]

        _accumulate(dsink_ref, dsink)

        zu, zv, u, xhat, rstd, vn = _gm_forward(z_ref, lng_ref, lnb_ref)
        mixed = _gm_mixed(vn, ws_ref, bs_ref, lo, hi)

        @pl.when(n == 0)
        def _():
            dws_ref[...] = jnp.zeros_like(dws_ref)

        dbs = jnp.zeros((BLK, LANES), F32)
        dvn_slabs = []
        du_slabs = []
        for s in range(GM_W // LANES):
            dgm = dam_ref[:, ATTN_W + s * LANES:ATTN_W + (s + 1) * LANES]
            du_slabs.append(dgm * mixed[s])
            dmx = dgm * u[:, s * LANES:(s + 1) * LANES]
            vsb = vn[:, s * LANES:(s + 1) * LANES].astype(BF16)
            dvn = jnp.zeros((BLK, LANES), F32)
            for half in range(2):
                h = 2 * s + half
                mask = lo if half == 0 else hi
                dmm = jnp.where(mask, dmx, 0.0)
                dbs = dbs + jnp.where(lane == h, jnp.sum(dmm, axis=1, keepdims=True), 0.0)
                dmm = dmm.astype(BF16)
                wt, causal = _tril_w(ws_ref, h)
                dvn = dvn + jnp.where(mask, _tn(wt, dmm), 0.0)
                dws_ref[h] += jnp.where(causal, _nt(dmm, vsb), 0.0)
            dvn_slabs.append(dvn)
        _accumulate(dbs_ref, dbs)
        dvn = jnp.concatenate(dvn_slabs, axis=1)
        du = jnp.concatenate(du_slabs, axis=1)
        _accumulate(dlnb_ref, jnp.sum(dvn, axis=0, keepdims=True))
        _accumulate(dlng_ref, jnp.sum(dvn * xhat, axis=0, keepdims=True))
        dxh = dvn * lng_ref[...]
        dgv = rstd * (dxh - jnp.mean(dxh, axis=-1, keepdims=True)
                      - xhat * jnp.mean(dxh * xhat, axis=-1, keepdims=True))
        dzuv_ref[:, :GM_W] = (du * _gelu_grad(zu)).astype(BF16)
        dzuv_ref[:, GM_W:] = (dgv * _gelu_grad(zv)).astype(BF16)

    return pl.pallas_call(
        body, name="b2_attn_gm", grid=(nb,),
        in_specs=[_rows(BLK, ATTN_W + GM_W)] + _block_specs_z(nb) + [
            pl.BlockSpec(memory_space=pltpu.SMEM), _whole((1, GM_W)), _whole((1, GM_W)),
            _whole((N_Q_HEADS, BLK, BLK)), _whole((BLK, GM_W))],
        out_specs=[_rows(BLK, ATTN_W), _whole((t, 2 * KV_W)), _rows(BLK, 2 * GM_W), _whole((1, LANES)),
                   _whole((1, GM_W)), _whole((1, GM_W)), _whole((N_Q_HEADS, BLK, BLK)), _whole((BLK, LANES))],
        out_shape=[_sds((t, ATTN_W), BF16), _sds((t, 2 * KV_W), F32), _sds((t, 2 * GM_W), BF16),
                   _sds((1, LANES), F32), _sds((1, GM_W), F32), _sds((1, GM_W), F32),
                   _sds((N_Q_HEADS, BLK, BLK), F32), _sds((BLK, LANES), F32)],
        compiler_params=_params())(dam, z, z, z, sinks, ln_g, ln_b, ws, bs_full)


def _b1_norm_in(dzq, dkv, dzuv, w_t, h, g, dh1):
    t = h.shape[0]
    tm = _tile(t, 512)

    def body(dzq_ref, dkv_ref, dzuv_ref, w_ref, h_ref, g_ref, dh_ref, dz_ref, dh0_ref, dg_ref):
        dz = jnp.concatenate([dzq_ref[...], dkv_ref[...].astype(BF16), dzuv_ref[...]], axis=1)
        dz_ref[...] = dz
        dx, dg = _rms_bwd(_mm(dz, w_ref[...]), h_ref[...], g_ref[...])
        dh0_ref[...] = dh_ref[...] + dx
        _accumulate(dg_ref, dg)

    return pl.pallas_call(
        body, name="b1_norm_in", grid=(t // tm,),
        in_specs=[_rows(tm, ATTN_W), _rows(tm, 2 * KV_W), _rows(tm, 2 * GM_W), _whole((D_IN, D_MODEL)),
                  _rows(tm, D_MODEL), _whole((1, D_MODEL)), _rows(tm, D_MODEL)],
        out_specs=[_rows(tm, D_IN), _rows(tm, D_MODEL), _whole((1, D_MODEL))],
        out_shape=[_sds((t, D_IN), BF16), _sds((t, D_MODEL), F32), _sds((1, D_MODEL), F32)],
        compiler_params=_params())(dzq, dkv, dzuv, w_t, h, g, dh1)


def _weight_grad(x, dy, name):
    t, k = x.shape
    n = dy.shape[1]
    tm = _tile(t, 1024)
    steps = t // tm

    def body(x_ref, dy_ref, dw_ref, acc_ref):
        i = pl.program_id(0)

        @pl.when(i == 0)
        def _():
            acc_ref[...] = _tn(x_ref[...].astype(BF16), dy_ref[...])

        @pl.when(i > 0)
        def _():
            acc_ref[...] += _tn(x_ref[...].astype(BF16), dy_ref[...])

        @pl.when(i == steps - 1)
        def _():
            dw_ref[...] = acc_ref[...].astype(BF16)

    return pl.pallas_call(
        body, name=name, grid=(steps,),
        in_specs=[_rows(tm, k), _rows(tm, n)],
        out_specs=_whole((k, n)),
        out_shape=_sds((k, n), BF16),
        scratch_shapes=[pltpu.VMEM((k, n), F32)],
        compiler_params=_params())(x, dy)


_ANY = pl.BlockSpec(memory_space=pl.ANY)
_HBM = pl.BlockSpec(memory_space=pltpu.HBM)
_SEM = pl.BlockSpec(memory_space=pltpu.SEMAPHORE)
_EFFECT = pltpu.SideEffectType.DATAFLOW_SIDE_EFFECTING


def _mesh_pos():
    return lax.axis_index("x"), lax.axis_index("y"), lax.axis_index("c")


def _other_chips(x, y):
    return [(1 - x, y), (x, 1 - y), (1 - x, 1 - y)]


def _peers(mode, x, y, c):
    if mode == "devices":
        peers = []
        for j in range(1, N_DEV):
            px = 1 - x if (j >> 2) & 1 else x
            py = 1 - y if (j >> 1) & 1 else y
            pc = 1 - c if j & 1 else c
            peers.append(((px, py, pc), 4 * px + 2 * py + pc))
        return peers, 4 * x + 2 * y + c
    return [((px, py, c), 2 * px + py) for px, py in _other_chips(x, y)], 2 * x + y


def _descriptors(mode, srcs, lands, send_sems, recv_sems, with_incoming=True):
    x, y, c = _mesh_pos()
    peers, me = _peers(mode, x, y, c)
    scatter = mode == "scatter"
    outgoing, incoming = [], []
    for i, (src, land) in enumerate(zip(srcs, lands)):
        for j, (dev, slot) in enumerate(peers):
            sem = i * len(peers) + j
            common = dict(send_sem=send_sems.at[sem], recv_sem=recv_sems.at[sem], device_id=dev,
                          device_id_type=pl.DeviceIdType.MESH)
            outgoing.append(pltpu.make_async_remote_copy(
                src_ref=src.at[slot] if scatter else src, dst_ref=land.at[me], **common))
            if with_incoming:
                incoming.append(pltpu.make_async_remote_copy(
                    src_ref=src.at[me] if scatter else src, dst_ref=land.at[slot], **common))
    return outgoing, incoming


def _n_sems(mode, k):
    return k * ((N_DEV if mode == "devices" else N_CHIPS) - 1)


def _exchange_start(srcs, lands, mode, name, after=None):
    k = len(srcs)
    arrs = [*srcs, *lands]

    def body(*refs):
        skip = 1 if after is not None else 0
        send_sems, recv_sems = refs[2 * k + skip], refs[2 * k + skip + 1]
        outgoing, _ = _descriptors(mode, refs[:k], refs[k:2 * k], send_sems, recv_sems, with_incoming=False)
        for cp in outgoing:
            cp.start()
        refs[-1][...] = jnp.zeros_like(refs[-1])

    operands = [pltpu.with_memory_space_constraint(a, pltpu.HBM) for a in arrs]
    in_specs = [_HBM] * (2 * k)
    if after is not None:
        operands.append(after)
        in_specs.append(_ANY)
    sems = pltpu.SemaphoreType.DMA((_n_sems(mode, k),))
    res = pl.pallas_call(
        body, name=name, in_specs=in_specs,
        out_shape=(sems, sems, *[pltpu.HBM(a.shape, a.dtype) for a in arrs], _sds((8, LANES), F32)),
        out_specs=(_SEM, _SEM, *[_HBM] * (2 * k), pl.BlockSpec(memory_space=pltpu.VMEM)),
        input_output_aliases={i: 2 + i for i in range(2 * k)},
        compiler_params=pltpu.CompilerParams(has_side_effects=_EFFECT))(*operands)
    return dict(mode=mode, send=res[0], recv=res[1], srcs=res[2:2 + k], lands=res[2 + k:2 + 2 * k], token=res[-1])


def _exchange_wait(flight, name, after):
    mode, k = flight["mode"], len(flight["srcs"])
    arrs = [*flight["srcs"], *flight["lands"]]

    def body(*refs):
        outgoing, incoming = _descriptors(mode, refs[:k], refs[k:2 * k], refs[2 * k], refs[2 * k + 1])
        for cp in outgoing:
            cp.wait_send()
        for cp in incoming:
            cp.wait_recv()

    res = pl.pallas_call(
        body, name=name, in_specs=[_HBM] * (2 * k) + [_SEM, _SEM, _ANY],
        out_shape=tuple(pltpu.HBM(a.shape, a.dtype) for a in arrs), out_specs=tuple([_HBM] * (2 * k)),
        input_output_aliases={i: i for i in range(2 * k)},
        compiler_params=pltpu.CompilerParams(has_side_effects=_EFFECT))(*arrs, flight["send"], flight["recv"], after)
    return res[:k], res[k:]


def _sibling_exchange(arrs, name):
    k = len(arrs)

    def body(*refs):
        ins, outs = refs[:k], refs[k:2 * k]
        send_sems, recv_sems = refs[2 * k:]
        x, y, c = _mesh_pos()
        cps = []
        for i in range(k):
            cp = pltpu.make_async_remote_copy(
                src_ref=ins[i], dst_ref=outs[i], send_sem=send_sems.at[i], recv_sem=recv_sems.at[i],
                device_id=(x, y, 1 - c), device_id_type=pl.DeviceIdType.MESH)
            cp.start()
            cps.append(cp)
        for cp in cps:
            cp.wait()

    return pl.pallas_call(
        body, name=name, in_specs=[_ANY] * k, out_specs=[_ANY] * k,
        out_shape=[_sds(a.shape, a.dtype) for a in arrs],
        scratch_shapes=[pltpu.SemaphoreType.DMA((k,)), pltpu.SemaphoreType.DMA((k,))])(*arrs)


def _adamw(w, g, m, v):
    m = ADAM_B1 * m + (1.0 - ADAM_B1) * g
    v = ADAM_B2 * v + (1.0 - ADAM_B2) * (g * g)
    m_hat = m / (1.0 - ADAM_B1 ** ADAM_STEP)
    v_hat = v / (1.0 - ADAM_B2 ** ADAM_STEP)
    delta = -ADAM_LR * (m_hat / (jnp.sqrt(v_hat) + ADAM_EPS) + ADAM_WD * w)
    return delta, m, v


def _row_tile(rows, cols, n_arrays):
    budget = VMEM_LIMIT_BYTES // 4
    padded = -(-cols // LANES) * LANES
    tr = max(8, budget // (2 * n_arrays * padded * 4))
    tr = min(rows, 1 << (tr.bit_length() - 1))
    while rows % tr:
        tr //= 2
    return tr


def _sum_chips(landing, name):
    _, r, c = landing.shape
    tr = _row_tile(r, c, 4)

    def body(l_ref, s_ref):
        acc = l_ref[0].astype(F32)
        for s in range(1, N_CHIPS):
            acc = acc + l_ref[s].astype(F32)
        s_ref[...] = acc

    return pl.pallas_call(
        body, name=name, grid=(r // tr,),
        in_specs=[pl.BlockSpec((N_CHIPS, tr, c), lambda i: (0, i, 0))],
        out_specs=_rows(tr, c), out_shape=_sds((r, c), F32),
        compiler_params=_params())(landing)


def _adamw_big(mine, sibling, w, m, v, prev, layer, name):
    _, r, c = w.shape
    tr = _row_tile(r, c, 9)
    stacked = pl.BlockSpec((None, tr, c), lambda i: (layer, i, 0))

    def body(a_ref, b_ref, w_ref, m_ref, v_ref, *rest):
        g_out, d_out, m_out, v_out = rest[-4:]
        g = a_ref[...] + b_ref[...]
        g_out[...] = g
        d_out[...], m_out[...], v_out[...] = _adamw(w_ref[...], g, m_ref[...], v_ref[...])

    prev = list(prev) if prev is not None else []
    return pl.pallas_call(
        body, name=name, grid=(r // tr,),
        in_specs=[_rows(tr, c)] * 2 + [stacked] * 3 + [_ANY] * len(prev),
        out_specs=[stacked] * 4, out_shape=[_sds(w.shape, F32)] * 4,
        input_output_aliases={5 + j: j for j in range(len(prev))},
        compiler_params=_params())(mine, sibling, w, m, v, *prev)


def _adamw_small(gathered, w, m, v):
    r = w.shape[0]

    def body(a_ref, w_ref, m_ref, v_ref, g_out, d_out, m_out, v_out):
        g = a_ref[0]
        for d in range(1, N_DEV):
            g = g + a_ref[d]
        g_out[...] = g
        d_out[...], m_out[...], v_out[...] = _adamw(w_ref[...], g, m_ref[...], v_ref[...])

    return pl.pallas_call(
        body, name="adamw_small", grid=(1,),
        in_specs=[_whole((N_DEV, r, LANES))] + [_whole((r, LANES))] * 3,
        out_specs=[_whole((r, LANES))] * 4, out_shape=[_sds((r, LANES), F32)] * 4,
        compiler_params=_params())(gathered, w, m, v)


def _shard_view(name, stacked):
    return jnp.swapaxes(stacked, 1, 2) if name in TRANSPOSED else stacked


def _row_split(name):
    return name in TRANSPOSED or BIG[name] == 1


def _assemble(name, landed):
    _, r, c = landed.shape
    if _row_split(name):
        return landed.reshape(N_CHIPS * r, c)
    return landed.transpose(1, 0, 2).reshape(r, N_CHIPS * c)


def _split(name, whole):
    r, c = whole.shape
    if _row_split(name):
        return whole.reshape(N_CHIPS, r // N_CHIPS, c)
    return whole.reshape(r, N_CHIPS, c // N_CHIPS).transpose(1, 0, 2)


def _with_own_slot(landed, own, slot):
    return lax.dynamic_update_index_in_dim(landed, own, slot, 0)


def _pack_small(params, names):
    pieces = []
    for name in names:
        flat = params[name].reshape(-1)
        pieces.append(jnp.pad(flat, (0, -flat.shape[0] % SMALL_PAD)))
    return jnp.concatenate(pieces).reshape(-1, LANES)


def _unpack_small(packed, like, names):
    flat = packed.reshape(-1)
    out, off = {}, 0
    for name in names:
        size = like[name].size
        out[name] = flat[off:off + size].reshape(like[name].shape)
        off += size + (-size % SMALL_PAD)
    return out


def kernel(x, p, ln_mix_pre, w_in, attn_sinks, gm_ln_g, gm_ln_b, gm_ws, gm_bs, g_attn_out, g_gm_out, w_out, ln_mix_post, ln_ffn_pre, w_ffn_gate, w_ffn_up, w_ffn_down, ln_ffn_post, w_ple, ln_ple_gate, w_ple_gate, loss_target, m_ln_mix_pre, m_w_in, m_attn_sinks, m_gm_ln_g, m_gm_ln_b, m_gm_ws, m_gm_bs, m_g_attn_out, m_g_gm_out, m_w_out, m_ln_mix_post, m_ln_ffn_pre, m_w_ffn_gate, m_w_ffn_up, m_w_ffn_down, m_ln_ffn_post, m_w_ple, m_ln_ple_gate, m_w_ple_gate, v_ln_mix_pre, v_w_in, v_attn_sinks, v_gm_ln_g, v_gm_ln_b, v_gm_ws, v_gm_bs, v_g_attn_out, v_g_gm_out, v_w_out, v_ln_mix_post, v_ln_ffn_pre, v_w_ffn_gate, v_w_ffn_up, v_w_ffn_down, v_ln_ffn_post, v_w_ple, v_ln_ple_gate, v_w_ple_gate):
    given = dict(locals())
    wts = {n: given[n] for n in WEIGHTS}
    mom = {n: given["m_" + n] for n in WEIGHTS}
    var = {n: given["v_" + n] for n in WEIGHTS}
    depth = w_in.shape[0]
    h = x[0]
    target = loss_target[0]
    chip = 2 * lax.axis_index("x") + lax.axis_index("y")
    device = 2 * chip + lax.axis_index("c")
    row = lambda a, i: a[i][None, :]
    bs_full = [jnp.repeat(gm_bs[i].T, HEAD_DIM, axis=1) for i in range(depth)]
    kinds = ("grad", "delta", "m", "v")
    wview = {n: _shard_view(n, wts[n]) for n in BIG_NAMES}
    mview = {n: _shard_view(n, mom[n]) for n in BIG_NAMES}
    vview = {n: _shard_view(n, var[n]) for n in BIG_NAMES}

    zero = lambda flight: flight["token"][0:1, 0:1]

    def start_gather(i, names, after, tag):
        shards = [wview[n][i].astype(BF16) for n in names]
        lands = [lax.empty((N_CHIPS,) + s.shape, BF16) for s in shards]
        return _exchange_start(shards, lands, "gather", f"gather_weights_start_{i}{tag}", after)

    def finish_gather(flight, names, i, after, tag):
        shards, lands = _exchange_wait(flight, f"gather_weights_wait_{i}{tag}", after)
        return {n: _assemble(n, _with_own_slot(l, s, chip)) for n, s, l in zip(names, shards, lands)}

    first, after = [], None
    for k, names in enumerate(GATHER_GROUPS):
        first.append(start_gather(0, names, after, "abc"[k]))
        after = first[-1]["token"]
    full = [finish_gather(first[0], GATHER_GROUPS[0], 0, after, "a")] + [None] * (depth - 1)
    saved = []
    for i in range(depth):
        w = full[i]
        g_in = row(ln_mix_pre, i)
        if i + 1 < depth:
            flight = start_gather(i + 1, BIG_NAMES, w['w_in'], "")
            g_in = g_in + zero(flight)
        z, a = _f1_norm_in(h, g_in, w['w_in'])
        am = _f2_attn_gm(z, attn_sinks[i], row(gm_ln_g, i), row(gm_ln_b, i), gm_ws[i], bs_full[i])
        if i == 0:
            w.update(finish_gather(first[1], GATHER_GROUPS[1], 0, am, "b"))
        heads, mix, h1 = _f3_mix_out(am, h, row(g_attn_out, i), row(g_gm_out, i), w['w_out'], row(ln_mix_post, i))
        f, gt, up = _f4a_ffn_in(h1, row(ln_ffn_pre, i), w['w_ffn_gate'], w['w_ffn_up'])
        if i == 0:
            w.update(finish_gather(first[2], GATHER_GROUPS[2], 0, gt, "c"))
        dn, h2 = _f4b_ffn_out(gt, up, w['w_ffn_down'], h1, row(ln_ffn_post, i))
        r, pg, pe, h3 = _f5_ple(h2, row(ln_ple_gate, i), w['w_ple_gate'], p[i, 0], w['w_ple'])
        saved.append(dict(h=h, z=z, a=a, am=am, heads=heads, mix=mix, h1=h1, f=f, gt=gt, up=up, dn=dn, h2=h2,
                          r=r, pg=pg, pe=pe))
        h = h3
        if i + 1 < depth:
            full[i + 1] = finish_gather(flight, BIG_NAMES, i + 1, h3, "")

    sq, dh = _loss_head(h, target)
    loss = lax.psum(0.5 / D_MODEL * sq[0, 0], ("x", "y", "c"))

    chain = {n: None for n in BIG_NAMES}
    small_out = [{k: {} for k in kinds} for _ in range(depth)]

    def start_scatter(i, names, dws, after, tag):
        parts = [_split(n, dws[n]) for n in names]
        lands = [lax.empty(q.shape, BF16) for q in parts]
        return _exchange_start(parts, lands, "scatter", f"scatter_grads_start_{i}{tag}", after)

    def finish_scatter(i, names, flight, after, tag):
        parts, lands = _exchange_wait(flight, f"scatter_grads_wait_{i}{tag}", after)
        lands = [_with_own_slot(l, lax.dynamic_index_in_dim(q, chip, 0, keepdims=False), chip)
                 for q, l in zip(parts, lands)]
        partial = [_sum_chips(l, "sum_chips_" + n) for n, l in zip(names, lands)]
        sibling = _sibling_exchange(partial, name="sibling_grads")
        for n, mine, sib in zip(names, partial, sibling):
            chain[n] = _adamw_big(mine, sib, wview[n], mview[n], vview[n], chain[n], i, "adamw_" + n)
        return lands[0]

    def start_small(i, names, small, after, tag):
        packed = _pack_small(small, names)
        land = lax.empty((N_DEV,) + packed.shape, F32)
        return _exchange_start([packed], [land], "devices", f"gather_small_grads_start_{i}{tag}", after)

    def finish_small(i, names, flight, after, tag):
        (packed,), (gathered,) = _exchange_wait(flight, f"gather_small_grads_wait_{i}{tag}", after)
        gathered = _with_own_slot(gathered, packed, device)
        layer = lambda d: _pack_small({n: d[n][i] for n in names}, names)
        res = _adamw_small(gathered, layer(wts), layer(mom), layer(var))
        for k, a in zip(kinds, res):
            small_out[i][k].update(_unpack_small(a, {n: wts[n][i] for n in names}, names))
        return gathered

    pending, done, behind = [], None, None
    for i in reversed(range(depth)):
        s, w = saved[i], full[i]
        last = i == 0
        dws, small = {}, {}
        gain = row(ln_ple_gate, i)
        if behind is not None:
            gain = gain + behind
        dpe, dpg, dh2, dg = _b5_ple(dh, s['h2'], s['pg'], s['pe'], gain, w['w_ple_gate'])
        small['ln_ple_gate'] = dg[0]
        dws['w_ple'] = _weight_grad(p[i, 0], dpe, "dw_ple")
        dws['w_ple_gate'] = _weight_grad(s['r'], dpg, "dw_ple_gate")

        ddn, act, dgt, dup, dg = _b4a_ffn_out(dh2, s['dn'], row(ln_ffn_post, i), w['w_ffn_down'], s['gt'], s['up'])
        small['ln_ffn_post'] = dg[0]
        dws['w_ffn_down'] = _weight_grad(act, ddn, "dw_ffn_down")
        dws['w_ffn_gate'] = _weight_grad(dgt, s['f'], "dw_ffn_gate")
        dws['w_ffn_up'] = _weight_grad(dup, s['f'], "dw_ffn_up")
        gain = row(ln_ffn_pre, i)
        if last:
            flight_a = start_scatter(i, SCATTER_GROUPS[0], dws, None, "a")
            gain = gain + zero(flight_a)
        dh1, dg = _b4b_ffn_in(dgt, dup, w['w_ffn_gate'], w['w_ffn_up'], s['h1'], gain, dh2)
        small['ln_ffn_pre'] = dg[0]

        dmix, dam, dgp, dga, dgg = _b3_mix_out(dh1, s['mix'], row(ln_mix_post, i), w['w_out'], s['am'],
                                               row(g_attn_out, i), row(g_gm_out, i))
        small['ln_mix_post'] = dgp[0]
        small['g_attn_out'] = dga[0]
        small['g_gm_out'] = dgg[0]
        dws['w_out'] = _weight_grad(s['heads'], dmix, "dw_out")
        gain = row(gm_ln_g, i)
        if last:
            flight_b = start_scatter(i, SCATTER_GROUPS[1], dws, flight_a["token"], "b")
            gain = gain + zero(flight_b)

        dzq, dkv, dzuv, dsink, dlng, dlnb, dgws, dbs = _b2_attn_gm(
            dam, s['z'], attn_sinks[i], gain, row(gm_ln_b, i), gm_ws[i], bs_full[i])
        small['attn_sinks'] = dsink[0, :N_Q_HEADS]
        small['gm_ln_g'] = dlng[0]
        small['gm_ln_b'] = dlnb[0]
        small['gm_ws'] = dgws
        small['gm_bs'] = dbs[:, :N_Q_HEADS].T
        gain = row(ln_mix_pre, i)
        if last:
            flight_s = start_small(i, SMALL_EARLY, small, flight_b["token"], "a")
            gain = gain + zero(flight_s)

        dz, dh, dg = _b1_norm_in(dzq, dkv, dzuv, w['w_in'], s['h'], gain, dh1)
        small['ln_mix_pre'] = dg[0]
        for finish in pending:
            done = finish(dh)
        pending = []
        if last:
            done = finish_scatter(i, SCATTER_GROUPS[0], flight_a, dz, "a")
            done = finish_scatter(i, SCATTER_GROUPS[1], flight_b, done, "b")
        dws['w_in'] = _weight_grad(dz, s['a'], "dw_in")
        if last:
            flight_c = start_scatter(i, SCATTER_GROUPS[2], dws, done, "c")
            flight_t = start_small(i, SMALL_LATE, small, flight_c["token"], "b")
            finish_small(i, SMALL_EARLY, flight_s, flight_t["token"], "a")
            finish_scatter(i, SCATTER_GROUPS[2], flight_c, flight_t["token"], "c")
            finish_small(i, SMALL_LATE, flight_t, flight_t["token"], "b")
        else:
            flight_c = start_scatter(i, BIG_NAMES, dws, done, "")
            flight_s = start_small(i, SMALL_NAMES, small, flight_c["token"], "")
            behind = zero(flight_s)
            pending = [functools.partial(finish_scatter, i, BIG_NAMES, flight_c, tag=""),
                       functools.partial(finish_small, i, SMALL_NAMES, flight_s, tag="")]
    grad_x = dh[None]

    out = {k: {n: _shard_view(n, chain[n][j]) for n in BIG_NAMES} for j, k in enumerate(kinds)}
    for k in kinds:
        out[k].update({n: jnp.stack([small_out[i][k][n] for i in range(depth)]) for n in SMALL_NAMES})

    return (loss, grad_x, *[out["grad"][n] for n in WEIGHTS], *[out["delta"][n] for n in WEIGHTS],
            *[out["m"][n] for n in WEIGHTS], *[out["v"][n] for n in WEIGHTS])
```

```python
import functools
import math

import jax
import jax.numpy as jnp
from jax import lax
from jax.experimental import pallas as pl
from jax.experimental.pallas import tpu as pltpu

F32 = jnp.float32
BF16 = jnp.bfloat16

D_MODEL = 1024
HEAD_DIM = 64
N_Q_HEADS = 8
BLK = 128
ATTN_W = 512
KV_W = 128
GM_W = 512
D_IN = ATTN_W + 2 * KV_W + 2 * GM_W
D_FF = 2816
PLE_DIM = 256
DEPTH = 4
NORM_EPS = 1e-6
NEG_BIG = -1e30
N_CHIPS = 4
N_DEV = 8

ADAM_LR = 0.001
ADAM_B1 = 0.9
ADAM_B2 = 0.999
ADAM_EPS = 1e-08
ADAM_WD = 0.01
ADAM_STEP = 10

VMEM_LIMIT_BYTES = 56 * 1024 * 1024
LANES = 128
STRIP = 16
GELU_C0 = math.sqrt(2.0 / math.pi)
GELU_C1 = 0.044715
ALIBI_SLOPES = tuple(2.0 ** (-8.0 * (h + 1.0) / N_Q_HEADS) for h in range(N_Q_HEADS))

WEIGHTS = ['ln_mix_pre', 'w_in', 'attn_sinks', 'gm_ln_g', 'gm_ln_b', 'gm_ws', 'gm_bs', 'g_attn_out',
           'g_gm_out', 'w_out', 'ln_mix_post', 'ln_ffn_pre', 'w_ffn_gate', 'w_ffn_up', 'w_ffn_down',
           'ln_ffn_post', 'w_ple', 'ln_ple_gate', 'w_ple_gate']
BIG = {'w_in': 2, 'w_out': 1, 'w_ffn_gate': 2, 'w_ffn_up': 2, 'w_ffn_down': 1, 'w_ple': 2, 'w_ple_gate': 1}
BIG_NAMES = list(BIG)
TRANSPOSED = ('w_in', 'w_ffn_gate', 'w_ffn_up')
SMALL_NAMES = [n for n in WEIGHTS if n not in BIG]
SMALL_PAD = 1024
GATHER_GROUPS = (('w_in',), ('w_out', 'w_ffn_gate', 'w_ffn_up'), ('w_ffn_down', 'w_ple_gate', 'w_ple'))
SCATTER_GROUPS = (('w_ple', 'w_ple_gate', 'w_ffn_down', 'w_ffn_gate', 'w_ffn_up'), ('w_out',), ('w_in',))
SMALL_LATE = ('ln_mix_pre',)
SMALL_EARLY = tuple(n for n in SMALL_NAMES if n not in SMALL_LATE)


def _nt(a, b):
    return lax.dot_general(a, b, (((1,), (1,)), ((), ())), preferred_element_type=F32)


def _tn(a, b):
    return lax.dot_general(a, b, (((0,), (0,)), ((), ())), preferred_element_type=F32)


def _mm(a, b):
    return jnp.dot(a, b, preferred_element_type=F32)


def _rms(x, g):
    r = lax.rsqrt(jnp.mean(x * x, axis=-1, keepdims=True) + NORM_EPS)
    return x * r * g


def _rms_bwd(dy, x, g):
    r = lax.rsqrt(jnp.mean(x * x, axis=-1, keepdims=True) + NORM_EPS)
    xh = x * r
    dg = jnp.sum(dy * xh, axis=0, keepdims=True)
    dxh = dy * g
    dx = r * (dxh - xh * jnp.mean(dxh * xh, axis=-1, keepdims=True))
    return dx, dg


def _gelu(x):
    return 0.5 * x * (1.0 + jnp.tanh(GELU_C0 * (x + GELU_C1 * x * x * x)))


def _sigmoid(x):
    return 1.0 / (1.0 + jnp.exp(-x))


def _rows(tm, n):
    return pl.BlockSpec((tm, n), lambda i: (i, 0))


def _whole(shape):
    return pl.BlockSpec(shape, lambda i: (0,) * len(shape))


def _accumulate(ref, val):
    @pl.when(pl.program_id(0) == 0)
    def _():
        ref[...] = jnp.zeros_like(ref)

    ref[...] += val


def _params(n_axes=1):
    return pltpu.CompilerParams(dimension_semantics=("arbitrary",) * n_axes,
                                vmem_limit_bytes=VMEM_LIMIT_BYTES)


def _sds(shape, dtype):
    return jax.ShapeDtypeStruct(shape, dtype)


def _tile(t, want):
    return min(t, want)


def _f1_norm_in(h, g, w_t):
    t = h.shape[0]
    tm = _tile(t, 512)

    def body(h_ref, g_ref, w_ref, z_ref, a_ref):
        a = _rms(h_ref[...], g_ref[...]).astype(BF16)
        a_ref[...] = a
        z_ref[...] = _nt(a, w_ref[...])

    return pl.pallas_call(
        body, name="f1_norm_in", grid=(t // tm,),
        in_specs=[_rows(tm, D_MODEL), _whole((1, D_MODEL)), _whole((D_IN, D_MODEL))],
        out_specs=[_rows(tm, D_IN), _rows(tm, D_MODEL)],
        out_shape=[_sds((t, D_IN), F32), _sds((t, D_MODEL), BF16)],
        compiler_params=_params())(h, g, w_t)


def _alibi_bias():
    ti = jnp.arange(BLK)[:, None]
    ji = jnp.arange(2 * BLK)[None, :]
    dist = ti + BLK - ji
    band = (dist >= 0) & (dist < BLK)
    bias = -jnp.asarray(ALIBI_SLOPES, F32)[:, None, None] * dist.astype(F32)[None]
    return jnp.stack([jnp.where((band & (ji >= BLK))[None], bias, NEG_BIG), jnp.where(band[None], bias, NEG_BIG)])


def _strips():
    return [slice(r * STRIP, (r + 1) * STRIP) for r in range(BLK // STRIP)]


def _bias_spec(transposed=False):
    tile = (2 * BLK, BLK) if transposed else (BLK, 2 * BLK)
    return pl.BlockSpec((None, N_Q_HEADS) + tile, lambda i: (jnp.minimum(i, 1), 0, 0, 0))


def _softmax_strip(s_ref, bias_ref, hq, rows, sink):
    sc = s_ref[hq, rows, :] + bias_ref[hq, rows, :]
    m = jnp.maximum(jnp.max(sc, axis=1, keepdims=True), sink)
    e = jnp.exp(sc - m)
    es = jnp.exp(sink - m)
    inv = 1.0 / (jnp.sum(e, axis=1, keepdims=True) + es)
    return e * inv, es * inv


def _kv_window(z_ref, kp_ref, vp_ref):
    kcat = jnp.concatenate([kp_ref[...], z_ref[:, ATTN_W:ATTN_W + KV_W]], axis=0)
    vcat = jnp.concatenate([vp_ref[...], z_ref[:, ATTN_W + KV_W:ATTN_W + 2 * KV_W]], axis=0)
    kswap = pltpu.roll(kcat, HEAD_DIM, 1)
    vswap = pltpu.roll(vcat, HEAD_DIM, 1)
    return kcat.astype(BF16), kswap.astype(BF16), vcat, vswap


def _gelu_and_grad(x):
    t = jnp.tanh(GELU_C0 * (x + GELU_C1 * x * x * x))
    return 0.5 * x * (1.0 + t), 0.5 * (1.0 + t) + 0.5 * x * (1.0 - t * t) * GELU_C0 * (1.0 + 3.0 * GELU_C1 * x * x)


def _layernorm_strip(gv, lng_ref, lnb_ref):
    xc = gv - jnp.mean(gv, axis=-1, keepdims=True)
    rstd = lax.rsqrt(jnp.mean(xc * xc, axis=-1, keepdims=True) + NORM_EPS)
    xhat = xc * rstd
    return xhat * lng_ref[...] + lnb_ref[...], xhat, rstd


def _tril_w(ws_ref, h):
    ti = lax.broadcasted_iota(jnp.int32, (BLK, BLK), 0)
    si = lax.broadcasted_iota(jnp.int32, (BLK, BLK), 1)
    causal = si <= ti
    return jnp.where(causal, ws_ref[h], 0.0).astype(BF16), causal


def _gm_mixed(vn, ws_ref, bs_ref, lo, hi):
    slabs = []
    for s in range(GM_W // LANES):
        vs = vn[:, s * LANES:(s + 1) * LANES]
        w0, _ = _tril_w(ws_ref, 2 * s)
        w1, _ = _tril_w(ws_ref, 2 * s + 1)
        mixed = (_mm(w0, jnp.where(lo, vs, 0.0).astype(BF16))
                 + _mm(w1, jnp.where(hi, vs, 0.0).astype(BF16))
                 + bs_ref[:, s * LANES:(s + 1) * LANES])
        slabs.append(mixed)
    return slabs


def _block_specs_z(nb):
    prev = lambda i: (jnp.maximum(i - 1, 0), ATTN_W // KV_W)
    prev_v = lambda i: (jnp.maximum(i - 1, 0), ATTN_W // KV_W + 1)
    return [_rows(BLK, D_IN), pl.BlockSpec((BLK, KV_W), prev), pl.BlockSpec((BLK, KV_W), prev_v)]


def _heads():
    return [(2 * s + half, s, half, s // 2 == half) for s in range(ATTN_W // LANES) for half in range(2)]


def _f2_attn_gm(z, sinks, bias, ln_g, ln_b, ws, bs_full):
    t = z.shape[0]
    nb = t // BLK

    def body(z_ref, kp_ref, vp_ref, sink_ref, bias_ref, lng_ref, lnb_ref, ws_ref, bs_ref, am_ref,
             s_ref, p_ref, u_ref, vn_ref):
        lane = lax.broadcasted_iota(jnp.int32, (1, LANES), 1)
        lo = lane < HEAD_DIM
        hi = lane >= HEAD_DIM
        kc, ks, vcat, vswap = _kv_window(z_ref, kp_ref, vp_ref)
        for hq, s, half, same in _heads():
            qs = z_ref[:, s * LANES:(s + 1) * LANES] * (HEAD_DIM ** -0.5)
            qm = jnp.where(lo if half == 0 else hi, qs, 0.0).astype(BF16)
            s_ref[hq] = _nt(qm, kc if same else ks)
        for hq in range(N_Q_HEADS):
            for rows in _strips():
                pr, _ = _softmax_strip(s_ref, bias_ref, hq, rows, sink_ref[hq])
                p_ref[hq, rows, :] = pr.astype(BF16)
        for s in range(ATTN_W // LANES):
            o = jnp.zeros((BLK, LANES), F32)
            for hq, hs, half, same in _heads():
                if hs == s:
                    vm = jnp.where(lo if half == 0 else hi, vcat if same else vswap, 0.0).astype(BF16)
                    o = o + _mm(p_ref[hq], vm)
            am_ref[:, s * LANES:(s + 1) * LANES] = o

        for rows in _strips():
            u_ref[rows, :] = _gelu(z_ref[rows, ATTN_W + 2 * KV_W:ATTN_W + 2 * KV_W + GM_W])
            vn_ref[rows, :], _, _ = _layernorm_strip(_gelu(z_ref[rows, ATTN_W + 2 * KV_W + GM_W:D_IN]),
                                                     lng_ref, lnb_ref)
        mixed = _gm_mixed(vn_ref, ws_ref, bs_ref, lo, hi)
        for s in range(GM_W // LANES):
            am_ref[:, ATTN_W + s * LANES:ATTN_W + (s + 1) * LANES] = u_ref[:, s * LANES:(s + 1) * LANES] * mixed[s]

    return pl.pallas_call(
        body, name="f2_attn_gm", grid=(nb,),
        in_specs=_block_specs_z(nb) + [
            pl.BlockSpec(memory_space=pltpu.SMEM), _bias_spec(), _whole((1, GM_W)),
            _whole((1, GM_W)), _whole((N_Q_HEADS, BLK, BLK)), _whole((BLK, GM_W))],
        out_specs=_rows(BLK, ATTN_W + GM_W),
        out_shape=_sds((t, ATTN_W + GM_W), F32),
        scratch_shapes=[pltpu.VMEM((N_Q_HEADS, BLK, 2 * BLK), F32), pltpu.VMEM((N_Q_HEADS, BLK, 2 * BLK), BF16),
                        pltpu.VMEM((BLK, GM_W), F32), pltpu.VMEM((BLK, GM_W), F32)],
        compiler_params=_params())(z, z, z, sinks, bias, ln_g, ln_b, ws, bs_full)


def _f3_mix_out(am, h, ga, gg, w, gpost):
    t = h.shape[0]
    tm = _tile(t, 512)

    def body(am_ref, h_ref, ga_ref, gg_ref, w_ref, gp_ref, heads_ref, mix_ref, h1_ref):
        heads = jnp.concatenate([_rms(am_ref[:, :ATTN_W], ga_ref[...]),
                                 _rms(am_ref[:, ATTN_W:], gg_ref[...])], axis=1).astype(BF16)
        heads_ref[...] = heads
        mix = _mm(heads, w_ref[...])
        mix_ref[...] = mix
        h1_ref[...] = h_ref[...] + _rms(mix, gp_ref[...])

    return pl.pallas_call(
        body, name="f3_mix_out", grid=(t // tm,),
        in_specs=[_rows(tm, D_MODEL), _rows(tm, D_MODEL), _whole((1, ATTN_W)), _whole((1, GM_W)),
                  _whole((D_MODEL, D_MODEL)), _whole((1, D_MODEL))],
        out_specs=[_rows(tm, D_MODEL)] * 3,
        out_shape=[_sds((t, D_MODEL), BF16), _sds((t, D_MODEL), F32), _sds((t, D_MODEL), F32)],
        compiler_params=_params())(am, h, ga, gg, w, gpost)


def _f4a_ffn_in(h1, gf, wg_t, wu_t):
    t = h1.shape[0]
    tm = _tile(t, 256)

    def body(h_ref, g_ref, wg_ref, wu_ref, f_ref, gt_ref, up_ref):
        f = _rms(h_ref[...], g_ref[...]).astype(BF16)
        f_ref[...] = f
        gt_ref[...] = _nt(f, wg_ref[...])
        up_ref[...] = _nt(f, wu_ref[...])

    return pl.pallas_call(
        body, name="f4a_ffn_in", grid=(t // tm,),
        in_specs=[_rows(tm, D_MODEL), _whole((1, D_MODEL)), _whole((D_FF, D_MODEL)), _whole((D_FF, D_MODEL))],
        out_specs=[_rows(tm, D_MODEL), _rows(tm, D_FF), _rows(tm, D_FF)],
        out_shape=[_sds((t, D_MODEL), BF16), _sds((t, D_FF), F32), _sds((t, D_FF), F32)],
        compiler_params=_params())(h1, gf, wg_t, wu_t)


def _f4b_ffn_out(gt, up, wd, h1, gfp):
    t = h1.shape[0]
    tm = _tile(t, 256)

    def body(gt_ref, up_ref, wd_ref, h_ref, g_ref, dn_ref, h2_ref):
        gt = gt_ref[...]
        act = (gt * _sigmoid(gt) * up_ref[...]).astype(BF16)
        dn = _mm(act, wd_ref[...])
        dn_ref[...] = dn
        h2_ref[...] = h_ref[...] + _rms(dn, g_ref[...])

    return pl.pallas_call(
        body, name="f4b_ffn_out", grid=(t // tm,),
        in_specs=[_rows(tm, D_FF), _rows(tm, D_FF), _whole((D_FF, D_MODEL)), _rows(tm, D_MODEL),
                  _whole((1, D_MODEL))],
        out_specs=[_rows(tm, D_MODEL)] * 2,
        out_shape=[_sds((t, D_MODEL), F32)] * 2,
        compiler_params=_params())(gt, up, wd, h1, gfp)


def _f5_ple(h2, gpl, wpg, p, wple):
    t = h2.shape[0]
    tm = _tile(t, 512)

    def body(h_ref, g_ref, wpg_ref, p_ref, wple_ref, r_ref, pg_ref, pe_ref, h3_ref):
        h = h_ref[...]
        r = _rms(h, g_ref[...]).astype(BF16)
        r_ref[...] = r
        pg = _mm(r, wpg_ref[...])
        pe = _mm(p_ref[...].astype(BF16), wple_ref[...])
        pg_ref[...] = pg
        pe_ref[...] = pe
        h3_ref[...] = h + pe * _sigmoid(pg)

    return pl.pallas_call(
        body, name="f5_ple", grid=(t // tm,),
        in_specs=[_rows(tm, D_MODEL), _whole((1, D_MODEL)), _whole((D_MODEL, D_MODEL)), _rows(tm, PLE_DIM),
                  _whole((PLE_DIM, D_MODEL))],
        out_specs=[_rows(tm, D_MODEL)] * 4,
        out_shape=[_sds((t, D_MODEL), BF16)] + [_sds((t, D_MODEL), F32)] * 3,
        compiler_params=_params())(h2, gpl, wpg, p, wple)


def _loss_head(y, target):
    t = y.shape[0]
    tm = _tile(t, 512)

    def body(y_ref, t_ref, sq_ref, dy_ref):
        err = y_ref[...] - t_ref[...]
        dy_ref[...] = err * (1.0 / D_MODEL)
        _accumulate(sq_ref, jnp.sum(err * err, keepdims=True))

    return pl.pallas_call(
        body, name="loss_head", grid=(t // tm,),
        in_specs=[_rows(tm, D_MODEL)] * 2,
        out_specs=[_whole((1, LANES)), _rows(tm, D_MODEL)],
        out_shape=[_sds((1, LANES), F32), _sds((t, D_MODEL), F32)],
        compiler_params=_params())(y, target)


def _b5_ple(dh3, h2, pg, pe, gpl, wpg):
    t = h2.shape[0]
    tm = _tile(t, 512)

    def body(dh_ref, h_ref, pg_ref, pe_ref, g_ref, w_ref, dpe_ref, dpg_ref, dh2_ref, dg_ref):
        dh = dh_ref[...]
        s = _sigmoid(pg_ref[...])
        dpe_ref[...] = (dh * s).astype(BF16)
        dpg = (dh * pe_ref[...] * s * (1.0 - s)).astype(BF16)
        dpg_ref[...] = dpg
        dx, dg = _rms_bwd(_nt(dpg, w_ref[...]), h_ref[...], g_ref[...])
        dh2_ref[...] = dh + dx
        _accumulate(dg_ref, dg)

    return pl.pallas_call(
        body, name="b5_ple", grid=(t // tm,),
        in_specs=[_rows(tm, D_MODEL)] * 4 + [_whole((1, D_MODEL)), _whole((D_MODEL, D_MODEL))],
        out_specs=[_rows(tm, D_MODEL)] * 3 + [_whole((1, D_MODEL))],
        out_shape=[_sds((t, D_MODEL), BF16)] * 2 + [_sds((t, D_MODEL), F32), _sds((1, D_MODEL), F32)],
        compiler_params=_params())(dh3, h2, pg, pe, gpl, wpg)


def _b4a_ffn_out(dh2, dn, gfp, wd, gt, up):
    t = dh2.shape[0]
    tm = _tile(t, 256)

    def body(dh_ref, dn_ref, g_ref, wd_ref, gt_ref, up_ref, ddn_ref, act_ref, dgt_ref, dup_ref, dg_ref):
        ddn, dg = _rms_bwd(dh_ref[...], dn_ref[...], g_ref[...])
        _accumulate(dg_ref, dg)
        ddn = ddn.astype(BF16)
        ddn_ref[...] = ddn
        dact = _nt(ddn, wd_ref[...])
        gt = gt_ref[...]
        up = up_ref[...]
        sg = _sigmoid(gt)
        silu = gt * sg
        act_ref[...] = (silu * up).astype(BF16)
        dup_ref[...] = (dact * silu).astype(BF16)
        dgt_ref[...] = (dact * up * (sg * (1.0 + gt * (1.0 - sg)))).astype(BF16)

    return pl.pallas_call(
        body, name="b4a_ffn_out", grid=(t // tm,),
        in_specs=[_rows(tm, D_MODEL), _rows(tm, D_MODEL), _whole((1, D_MODEL)), _whole((D_FF, D_MODEL)),
                  _rows(tm, D_FF), _rows(tm, D_FF)],
        out_specs=[_rows(tm, D_MODEL), _rows(tm, D_FF), _rows(tm, D_FF), _rows(tm, D_FF), _whole((1, D_MODEL))],
        out_shape=[_sds((t, D_MODEL), BF16)] + [_sds((t, D_FF), BF16)] * 3 + [_sds((1, D_MODEL), F32)],
        compiler_params=_params())(dh2, dn, gfp, wd, gt, up)


def _b4b_ffn_in(dgt, dup, wg_t, wu_t, h1, gf, dh2):
    t = h1.shape[0]
    tm = _tile(t, 512)

    def body(dgt_ref, dup_ref, wg_ref, wu_ref, h_ref, g_ref, dh_ref, dh1_ref, dg_ref):
        df = _mm(dgt_ref[...], wg_ref[...]) + _mm(dup_ref[...], wu_ref[...])
        dx, dg = _rms_bwd(df, h_ref[...], g_ref[...])
        dh1_ref[...] = dh_ref[...] + dx
        _accumulate(dg_ref, dg)

    return pl.pallas_call(
        body, name="b4b_ffn_in", grid=(t // tm,),
        in_specs=[_rows(tm, D_FF), _rows(tm, D_FF), _whole((D_FF, D_MODEL)), _whole((D_FF, D_MODEL)),
                  _rows(tm, D_MODEL), _whole((1, D_MODEL)), _rows(tm, D_MODEL)],
        out_specs=[_rows(tm, D_MODEL), _whole((1, D_MODEL))],
        out_shape=[_sds((t, D_MODEL), F32), _sds((1, D_MODEL), F32)],
        compiler_params=_params())(dgt, dup, wg_t, wu_t, h1, gf, dh2)


def _b3_mix_out(dh1, mix, gpost, w, am, ga, gg):
    t = dh1.shape[0]
    tm = _tile(t, 512)

    def body(dh_ref, mix_ref, gp_ref, w_ref, am_ref, ga_ref, gg_ref, dmix_ref, dam_ref, dgp_ref, dga_ref, dgg_ref):
        dmix, dgp = _rms_bwd(dh_ref[...], mix_ref[...], gp_ref[...])
        _accumulate(dgp_ref, dgp)
        dmix = dmix.astype(BF16)
        dmix_ref[...] = dmix
        dheads = _nt(dmix, w_ref[...])
        dat, dga = _rms_bwd(dheads[:, :ATTN_W], am_ref[:, :ATTN_W], ga_ref[...])
        dgm, dgg = _rms_bwd(dheads[:, ATTN_W:], am_ref[:, ATTN_W:], gg_ref[...])
        dam_ref[:, :ATTN_W] = dat
        dam_ref[:, ATTN_W:] = dgm
        _accumulate(dga_ref, dga)
        _accumulate(dgg_ref, dgg)

    return pl.pallas_call(
        body, name="b3_mix_out", grid=(t // tm,),
        in_specs=[_rows(tm, D_MODEL), _rows(tm, D_MODEL), _whole((1, D_MODEL)), _whole((D_MODEL, D_MODEL)),
                  _rows(tm, D_MODEL), _whole((1, ATTN_W)), _whole((1, GM_W))],
        out_specs=[_rows(tm, D_MODEL), _rows(tm, D_MODEL), _whole((1, D_MODEL)), _whole((1, ATTN_W)),
                   _whole((1, GM_W))],
        out_shape=[_sds((t, D_MODEL), BF16), _sds((t, D_MODEL), F32), _sds((1, D_MODEL), F32),
                   _sds((1, ATTN_W), F32), _sds((1, GM_W), F32)],
        compiler_params=_params())(dh1, mix, gpost, w, am, ga, gg)


def _b2_attn_gm(dam, z, sinks, bias, ln_g, ln_b, ws, bs_full):
    t = z.shape[0]
    nb = t // BLK

    def body(dam_ref, z_ref, kp_ref, vp_ref, sink_ref, bias_ref, lng_ref, lnb_ref, ws_ref, wst_ref, bs_ref,
             dzq_ref, dkv_ref, dzuv_ref, dsink_ref, dlng_ref, dlnb_ref, dws_ref, dbs_ref,
             u_ref, du_ref, dv_ref, vn_ref, xhat_ref, rstd_ref, dvn_ref):
        n = pl.program_id(0)
        lane = lax.broadcasted_iota(jnp.int32, (1, LANES), 1)
        lo = lane < HEAD_DIM
        hi = lane >= HEAD_DIM
        sub = lax.broadcasted_iota(jnp.int32, (LANES, 1), 0)
        kc, ks, vcat, vswap = _kv_window(z_ref, kp_ref, vp_ref)
        vc = vcat.astype(BF16)
        vs_ = vswap.astype(BF16)
        kc_t = kc.T
        ks_t = ks.T

        dk_acc = jnp.zeros((2 * BLK, LANES), F32)
        dv_acc = jnp.zeros((2 * BLK, LANES), F32)
        dsink = jnp.zeros((1, LANES), F32)
        for s in range(ATTN_W // LANES):
            qs = z_ref[:, s * LANES:(s + 1) * LANES] * (HEAD_DIM ** -0.5)
            dos = dam_ref[:, s * LANES:(s + 1) * LANES]
            dq_t = jnp.zeros((LANES, BLK), F32)
            for hq, hs, half, same in _heads():
                if hs != s:
                    continue
                mask = lo if half == 0 else hi
                qm = jnp.where(mask, qs, 0.0).astype(BF16)
                dom = jnp.where(mask, dos, 0.0).astype(BF16)
                sink = sink_ref[hq]
                sc = _nt(kc if same else ks, qm) + bias_ref[hq]
                m = jnp.maximum(jnp.max(sc, axis=0, keepdims=True), sink)
                e = jnp.exp(sc - m)
                es = jnp.exp(sink - m)
                inv = 1.0 / (jnp.sum(e, axis=0, keepdims=True) + es)
                pr = e * inv
                dpr = _nt(vc if same else vs_, dom)
                row = jnp.sum(dpr * pr, axis=0, keepdims=True)
                ds = (pr * (dpr - row)).astype(BF16)
                dsink = dsink + jnp.where(lane == hq, -jnp.sum(es * inv * row, keepdims=True), 0.0)
                dkc = _mm(ds, qm)
                dvc = _mm(pr.astype(BF16), dom)
                if not same:
                    dkc = pltpu.roll(dkc, HEAD_DIM, 1)
                    dvc = pltpu.roll(dvc, HEAD_DIM, 1)
                dk_acc = dk_acc + dkc
                dv_acc = dv_acc + dvc
                in_half = (sub < HEAD_DIM) if half == 0 else (sub >= HEAD_DIM)
                dq_t = dq_t + jnp.where(in_half, _mm(kc_t if same else ks_t, ds), 0.0)
            dzq_ref[:, s * LANES:(s + 1) * LANES] = (dq_t.T * (HEAD_DIM ** -0.5)).astype(BF16)
        cur = pl.multiple_of(n * BLK, BLK)
        dkv_ref[pl.ds(cur, BLK), 0:KV_W] = dk_acc[BLK:, :]
        dkv_ref[pl.ds(cur, BLK), KV_W:2 * KV_W] = dv_acc[BLK:, :]

        @pl.when(n > 0)
        def _():
            prv = pl.multiple_of((n - 1) * BLK, BLK)
            dkv_ref[pl.ds(prv, BLK), 0:KV_W] += dk_acc[:BLK, :]
            dkv_ref[pl.ds(prv, BLK), KV_W:2 * KV_W] += dv_acc[:BLK, :]

        _accumulate(dsink_ref, dsink)

        for rows in _strips():
            u_ref[rows, :], du_ref[rows, :] = _gelu_and_grad(
                z_ref[rows, ATTN_W + 2 * KV_W:ATTN_W + 2 * KV_W + GM_W])
            gv, dv_ref[rows, :] = _gelu_and_grad(z_ref[rows, ATTN_W + 2 * KV_W + GM_W:D_IN])
            vn_ref[rows, :], xhat_ref[rows, :], rstd = _layernorm_strip(gv, lng_ref, lnb_ref)
            rstd_ref[rows, :] = jnp.broadcast_to(rstd, (STRIP, LANES))
        mixed = _gm_mixed(vn_ref, ws_ref, bs_ref, lo, hi)

        @pl.when(n == 0)
        def _():
            dws_ref[...] = jnp.zeros_like(dws_ref)

        dbs = jnp.zeros((BLK, LANES), F32)
        for s in range(GM_W // LANES):
            slab = slice(s * LANES, (s + 1) * LANES)
            dgm = dam_ref[:, ATTN_W + s * LANES:ATTN_W + (s + 1) * LANES]
            dzuv_ref[:, slab] = (dgm * mixed[s] * du_ref[:, slab]).astype(BF16)
            dmx = dgm * u_ref[:, slab]
            vsb = vn_ref[:, slab].astype(BF16)
            dvn = jnp.zeros((BLK, LANES), F32)
            for half in range(2):
                h = 2 * s + half
                mask = lo if half == 0 else hi
                dmm = jnp.where(mask, dmx, 0.0)
                dbs = dbs + jnp.where(lane == h, jnp.sum(dmm, axis=1, keepdims=True), 0.0)
                dmm = dmm.astype(BF16)
                _, causal = _tril_w(ws_ref, h)
                ti = lax.broadcasted_iota(jnp.int32, (BLK, BLK), 0)
                si = lax.broadcasted_iota(jnp.int32, (BLK, BLK), 1)
                wt_t = jnp.where(ti <= si, wst_ref[h], 0.0).astype(BF16)
                dvn = dvn + jnp.where(mask, _mm(wt_t, dmm), 0.0)
                dws_ref[h] += jnp.where(causal, _nt(dmm, vsb), 0.0)
            dvn_ref[:, slab] = dvn
        _accumulate(dbs_ref, dbs)
        dlnb = jnp.zeros((STRIP, GM_W), F32)
        dlng = jnp.zeros((STRIP, GM_W), F32)
        for rows in _strips():
            dvn = dvn_ref[rows, :]
            xhat = xhat_ref[rows, :]
            dlnb = dlnb + dvn
            dlng = dlng + dvn * xhat
            dxh = dvn * lng_ref[...]
            dgv = rstd_ref[rows, 0:1] * (dxh - jnp.mean(dxh, axis=-1, keepdims=True)
                                         - xhat * jnp.mean(dxh * xhat, axis=-1, keepdims=True))
            dzuv_ref[rows, GM_W:] = (dgv * dv_ref[rows, :]).astype(BF16)
        _accumulate(dlnb_ref, jnp.sum(dlnb, axis=0, keepdims=True))
        _accumulate(dlng_ref, jnp.sum(dlng, axis=0, keepdims=True))

    return pl.pallas_call(
        body, name="b2_attn_gm", grid=(nb,),
        in_specs=[_rows(BLK, ATTN_W + GM_W)] + _block_specs_z(nb) + [
            pl.BlockSpec(memory_space=pltpu.SMEM), _bias_spec(transposed=True), _whole((1, GM_W)),
            _whole((1, GM_W)), _whole((N_Q_HEADS, BLK, BLK)), _whole((N_Q_HEADS, BLK, BLK)), _whole((BLK, GM_W))],
        out_specs=[_rows(BLK, ATTN_W), _whole((t, 2 * KV_W)), _rows(BLK, 2 * GM_W), _whole((1, LANES)),
                   _whole((1, GM_W)), _whole((1, GM_W)), _whole((N_Q_HEADS, BLK, BLK)), _whole((BLK, LANES))],
        out_shape=[_sds((t, ATTN_W), BF16), _sds((t, 2 * KV_W), F32), _sds((t, 2 * GM_W), BF16),
                   _sds((1, LANES), F32), _sds((1, GM_W), F32), _sds((1, GM_W), F32),
                   _sds((N_Q_HEADS, BLK, BLK), F32), _sds((BLK, LANES), F32)],
        scratch_shapes=[pltpu.VMEM((BLK, GM_W), F32)] * 5
        + [pltpu.VMEM((BLK, LANES), F32), pltpu.VMEM((BLK, GM_W), F32)],
        compiler_params=_params())(dam, z, z, z, sinks, jnp.swapaxes(bias, 2, 3), ln_g, ln_b, ws,
                                   jnp.swapaxes(ws, 1, 2), bs_full)


def _b1_norm_in(dzq, dkv, dzuv, w_t, h, g, dh1):
    t = h.shape[0]
    tm = _tile(t, 512)

    def body(dzq_ref, dkv_ref, dzuv_ref, w_ref, h_ref, g_ref, dh_ref, dz_ref, dh0_ref, dg_ref):
        dz = jnp.concatenate([dzq_ref[...], dkv_ref[...].astype(BF16), dzuv_ref[...]], axis=1)
        dz_ref[...] = dz
        dx, dg = _rms_bwd(_mm(dz, w_ref[...]), h_ref[...], g_ref[...])
        dh0_ref[...] = dh_ref[...] + dx
        _accumulate(dg_ref, dg)

    return pl.pallas_call(
        body, name="b1_norm_in", grid=(t // tm,),
        in_specs=[_rows(tm, ATTN_W), _rows(tm, 2 * KV_W), _rows(tm, 2 * GM_W), _whole((D_IN, D_MODEL)),
                  _rows(tm, D_MODEL), _whole((1, D_MODEL)), _rows(tm, D_MODEL)],
        out_specs=[_rows(tm, D_IN), _rows(tm, D_MODEL), _whole((1, D_MODEL))],
        out_shape=[_sds((t, D_IN), BF16), _sds((t, D_MODEL), F32), _sds((1, D_MODEL), F32)],
        compiler_params=_params())(dzq, dkv, dzuv, w_t, h, g, dh1)


def _weight_grad(x, dy, name):
    t, k = x.shape
    n = dy.shape[1]
    tm = _tile(t, 1024)
    steps = t // tm

    def body(x_ref, dy_ref, dw_ref, acc_ref):
        i = pl.program_id(0)

        @pl.when(i == 0)
        def _():
            acc_ref[...] = _tn(x_ref[...].astype(BF16), dy_ref[...])

        @pl.when(i > 0)
        def _():
            acc_ref[...] += _tn(x_ref[...].astype(BF16), dy_ref[...])

        @pl.when(i == steps - 1)
        def _():
            dw_ref[...] = acc_ref[...].astype(BF16)

    return pl.pallas_call(
        body, name=name, grid=(steps,),
        in_specs=[_rows(tm, k), _rows(tm, n)],
        out_specs=_whole((k, n)),
        out_shape=_sds((k, n), BF16),
        scratch_shapes=[pltpu.VMEM((k, n), F32)],
        compiler_params=_params())(x, dy)


_ANY = pl.BlockSpec(memory_space=pl.ANY)
_HBM = pl.BlockSpec(memory_space=pltpu.HBM)
_SEM = pl.BlockSpec(memory_space=pltpu.SEMAPHORE)
_EFFECT = pltpu.SideEffectType.DATAFLOW_SIDE_EFFECTING


def _mesh_pos():
    return lax.axis_index("x"), lax.axis_index("y"), lax.axis_index("c")


def _other_chips(x, y):
    return [(1 - x, y), (x, 1 - y), (1 - x, 1 - y)]


def _peers(mode, x, y, c):
    if mode == "devices":
        peers = []
        for j in range(1, N_DEV):
            px = 1 - x if (j >> 2) & 1 else x
            py = 1 - y if (j >> 1) & 1 else y
            pc = 1 - c if j & 1 else c
            peers.append(((px, py, pc), 4 * px + 2 * py + pc))
        return peers, 4 * x + 2 * y + c
    return [((px, py, c), 2 * px + py) for px, py in _other_chips(x, y)], 2 * x + y


def _descriptors(mode, srcs, lands, send_sems, recv_sems, with_incoming=True):
    x, y, c = _mesh_pos()
    peers, me = _peers(mode, x, y, c)
    scatter = mode == "scatter"
    outgoing, incoming = [], []
    for i, (src, land) in enumerate(zip(srcs, lands)):
        for j, (dev, slot) in enumerate(peers):
            sem = i * len(peers) + j
            common = dict(send_sem=send_sems.at[sem], recv_sem=recv_sems.at[sem], device_id=dev,
                          device_id_type=pl.DeviceIdType.MESH)
            outgoing.append(pltpu.make_async_remote_copy(
                src_ref=src.at[slot] if scatter else src, dst_ref=land.at[me], **common))
            if with_incoming:
                incoming.append(pltpu.make_async_remote_copy(
                    src_ref=src.at[me] if scatter else src, dst_ref=land.at[slot], **common))
    return outgoing, incoming


def _n_sems(mode, k):
    return k * ((N_DEV if mode == "devices" else N_CHIPS) - 1)


def _exchange_start(srcs, lands, mode, name, after=None):
    k = len(srcs)
    arrs = [*srcs, *lands]

    def body(*refs):
        skip = 1 if after is not None else 0
        send_sems, recv_sems = refs[2 * k + skip], refs[2 * k + skip + 1]
        outgoing, _ = _descriptors(mode, refs[:k], refs[k:2 * k], send_sems, recv_sems, with_incoming=False)
        for cp in outgoing:
            cp.start()
        refs[-1][...] = jnp.zeros_like(refs[-1])

    operands = [pltpu.with_memory_space_constraint(a, pltpu.HBM) for a in arrs]
    in_specs = [_HBM] * (2 * k)
    if after is not None:
        operands.append(after)
        in_specs.append(_ANY)
    sems = pltpu.SemaphoreType.DMA((_n_sems(mode, k),))
    res = pl.pallas_call(
        body, name=name, in_specs=in_specs,
        out_shape=(sems, sems, *[pltpu.HBM(a.shape, a.dtype) for a in arrs], _sds((8, LANES), F32)),
        out_specs=(_SEM, _SEM, *[_HBM] * (2 * k), pl.BlockSpec(memory_space=pltpu.VMEM)),
        input_output_aliases={i: 2 + i for i in range(2 * k)},
        compiler_params=pltpu.CompilerParams(has_side_effects=_EFFECT))(*operands)
    return dict(mode=mode, send=res[0], recv=res[1], srcs=res[2:2 + k], lands=res[2 + k:2 + 2 * k], token=res[-1])


def _exchange_wait(flight, name, after):
    mode, k = flight["mode"], len(flight["srcs"])
    arrs = [*flight["srcs"], *flight["lands"]]

    def body(*refs):
        outgoing, incoming = _descriptors(mode, refs[:k], refs[k:2 * k], refs[2 * k], refs[2 * k + 1])
        for cp in outgoing:
            cp.wait_send()
        for cp in incoming:
            cp.wait_recv()

    res = pl.pallas_call(
        body, name=name, in_specs=[_HBM] * (2 * k) + [_SEM, _SEM, _ANY],
        out_shape=tuple(pltpu.HBM(a.shape, a.dtype) for a in arrs), out_specs=tuple([_HBM] * (2 * k)),
        input_output_aliases={i: i for i in range(2 * k)},
        compiler_params=pltpu.CompilerParams(has_side_effects=_EFFECT))(*arrs, flight["send"], flight["recv"], after)
    return res[:k], res[k:]


def _sibling_exchange(arrs, name):
    k = len(arrs)

    def body(*refs):
        ins, outs = refs[:k], refs[k:2 * k]
        send_sems, recv_sems = refs[2 * k:]
        x, y, c = _mesh_pos()
        cps = []
        for i in range(k):
            cp = pltpu.make_async_remote_copy(
                src_ref=ins[i], dst_ref=outs[i], send_sem=send_sems.at[i], recv_sem=recv_sems.at[i],
                device_id=(x, y, 1 - c), device_id_type=pl.DeviceIdType.MESH)
            cp.start()
            cps.append(cp)
        for cp in cps:
            cp.wait()

    return pl.pallas_call(
        body, name=name, in_specs=[_ANY] * k, out_specs=[_ANY] * k,
        out_shape=[_sds(a.shape, a.dtype) for a in arrs],
        scratch_shapes=[pltpu.SemaphoreType.DMA((k,)), pltpu.SemaphoreType.DMA((k,))])(*arrs)


def _adamw(w, g, m, v):
    m = ADAM_B1 * m + (1.0 - ADAM_B1) * g
    v = ADAM_B2 * v + (1.0 - ADAM_B2) * (g * g)
    m_hat = m / (1.0 - ADAM_B1 ** ADAM_STEP)
    v_hat = v / (1.0 - ADAM_B2 ** ADAM_STEP)
    delta = -ADAM_LR * (m_hat / (jnp.sqrt(v_hat) + ADAM_EPS) + ADAM_WD * w)
    return delta, m, v


def _row_tile(rows, cols, n_arrays):
    budget = VMEM_LIMIT_BYTES // 4
    padded = -(-cols // LANES) * LANES
    tr = max(8, budget // (2 * n_arrays * padded * 4))
    tr = min(rows, 1 << (tr.bit_length() - 1))
    while rows % tr:
        tr //= 2
    return tr


def _sum_chips(landing, name):
    _, r, c = landing.shape
    tr = _row_tile(r, c, 4)

    def body(l_ref, s_ref):
        acc = l_ref[0].astype(F32)
        for s in range(1, N_CHIPS):
            acc = acc + l_ref[s].astype(F32)
        s_ref[...] = acc

    return pl.pallas_call(
        body, name=name, grid=(r // tr,),
        in_specs=[pl.BlockSpec((N_CHIPS, tr, c), lambda i: (0, i, 0))],
        out_specs=_rows(tr, c), out_shape=_sds((r, c), F32),
        compiler_params=_params())(landing)


def _adamw_big(mine, sibling, w, m, v, prev, layer, name):
    _, r, c = w.shape
    tr = _row_tile(r, c, 9)
    stacked = pl.BlockSpec((None, tr, c), lambda i: (layer, i, 0))

    def body(a_ref, b_ref, w_ref, m_ref, v_ref, *rest):
        g_out, d_out, m_out, v_out = rest[-4:]
        g = a_ref[...] + b_ref[...]
        g_out[...] = g
        d_out[...], m_out[...], v_out[...] = _adamw(w_ref[...], g, m_ref[...], v_ref[...])

    prev = list(prev) if prev is not None else []
    return pl.pallas_call(
        body, name=name, grid=(r // tr,),
        in_specs=[_rows(tr, c)] * 2 + [stacked] * 3 + [_ANY] * len(prev),
        out_specs=[stacked] * 4, out_shape=[_sds(w.shape, F32)] * 4,
        input_output_aliases={5 + j: j for j in range(len(prev))},
        compiler_params=_params())(mine, sibling, w, m, v, *prev)


def _adamw_small(gathered, w, m, v):
    r = w.shape[0]

    def body(a_ref, w_ref, m_ref, v_ref, g_out, d_out, m_out, v_out):
        g = a_ref[0]
        for d in range(1, N_DEV):
            g = g + a_ref[d]
        g_out[...] = g
        d_out[...], m_out[...], v_out[...] = _adamw(w_ref[...], g, m_ref[...], v_ref[...])

    return pl.pallas_call(
        body, name="adamw_small", grid=(1,),
        in_specs=[_whole((N_DEV, r, LANES))] + [_whole((r, LANES))] * 3,
        out_specs=[_whole((r, LANES))] * 4, out_shape=[_sds((r, LANES), F32)] * 4,
        compiler_params=_params())(gathered, w, m, v)


def _shard_view(name, stacked):
    return jnp.swapaxes(stacked, 1, 2) if name in TRANSPOSED else stacked


def _row_split(name):
    return name in TRANSPOSED or BIG[name] == 1


def _assemble(name, landed):
    _, r, c = landed.shape
    if _row_split(name):
        return landed.reshape(N_CHIPS * r, c)
    return landed.transpose(1, 0, 2).reshape(r, N_CHIPS * c)


def _split(name, whole):
    r, c = whole.shape
    if _row_split(name):
        return whole.reshape(N_CHIPS, r // N_CHIPS, c)
    return whole.reshape(r, N_CHIPS, c // N_CHIPS).transpose(1, 0, 2)


def _with_own_slot(landed, own, slot):
    return lax.dynamic_update_index_in_dim(landed, own, slot, 0)


def _pack_small(params, names):
    pieces = []
    for name in names:
        flat = params[name].reshape(-1)
        pieces.append(jnp.pad(flat, (0, -flat.shape[0] % SMALL_PAD)))
    return jnp.concatenate(pieces).reshape(-1, LANES)


def _unpack_small(packed, like, names):
    flat = packed.reshape(-1)
    out, off = {}, 0
    for name in names:
        size = like[name].size
        out[name] = flat[off:off + size].reshape(like[name].shape)
        off += size + (-size % SMALL_PAD)
    return out


def kernel(x, p, ln_mix_pre, w_in, attn_sinks, gm_ln_g, gm_ln_b, gm_ws, gm_bs, g_attn_out, g_gm_out, w_out, ln_mix_post, ln_ffn_pre, w_ffn_gate, w_ffn_up, w_ffn_down, ln_ffn_post, w_ple, ln_ple_gate, w_ple_gate, loss_target, m_ln_mix_pre, m_w_in, m_attn_sinks, m_gm_ln_g, m_gm_ln_b, m_gm_ws, m_gm_bs, m_g_attn_out, m_g_gm_out, m_w_out, m_ln_mix_post, m_ln_ffn_pre, m_w_ffn_gate, m_w_ffn_up, m_w_ffn_down, m_ln_ffn_post, m_w_ple, m_ln_ple_gate, m_w_ple_gate, v_ln_mix_pre, v_w_in, v_attn_sinks, v_gm_ln_g, v_gm_ln_b, v_gm_ws, v_gm_bs, v_g_attn_out, v_g_gm_out, v_w_out, v_ln_mix_post, v_ln_ffn_pre, v_w_ffn_gate, v_w_ffn_up, v_w_ffn_down, v_ln_ffn_post, v_w_ple, v_ln_ple_gate, v_w_ple_gate):
    given = dict(locals())
    wts = {n: given[n] for n in WEIGHTS}
    mom = {n: given["m_" + n] for n in WEIGHTS}
    var = {n: given["v_" + n] for n in WEIGHTS}
    depth = w_in.shape[0]
    h = x[0]
    target = loss_target[0]
    chip = 2 * lax.axis_index("x") + lax.axis_index("y")
    device = 2 * chip + lax.axis_index("c")
    row = lambda a, i: a[i][None, :]
    bs_full = [jnp.repeat(gm_bs[i].T, HEAD_DIM, axis=1) for i in range(depth)]
    bias = _alibi_bias()
    kinds = ("grad", "delta", "m", "v")
    wview = {n: _shard_view(n, wts[n]) for n in BIG_NAMES}
    mview = {n: _shard_view(n, mom[n]) for n in BIG_NAMES}
    vview = {n: _shard_view(n, var[n]) for n in BIG_NAMES}

    zero = lambda flight: flight["token"][0:1, 0:1]

    def start_gather(i, names, after, tag):
        shards = [wview[n][i].astype(BF16) for n in names]
        lands = [lax.empty((N_CHIPS,) + s.shape, BF16) for s in shards]
        return _exchange_start(shards, lands, "gather", f"gather_weights_start_{i}{tag}", after)

    def finish_gather(flight, names, i, after, tag):
        shards, lands = _exchange_wait(flight, f"gather_weights_wait_{i}{tag}", after)
        return {n: _assemble(n, _with_own_slot(l, s, chip)) for n, s, l in zip(names, shards, lands)}

    first, after = [], None
    for k, names in enumerate(GATHER_GROUPS):
        first.append(start_gather(0, names, after, "abc"[k]))
        after = first[-1]["token"]
    full = [finish_gather(first[0], GATHER_GROUPS[0], 0, after, "a")] + [None] * (depth - 1)
    saved = []
    for i in range(depth):
        w = full[i]
        g_in = row(ln_mix_pre, i)
        if i + 1 < depth:
            flight = start_gather(i + 1, BIG_NAMES, w['w_in'], "")
            g_in = g_in + zero(flight)
        z, a = _f1_norm_in(h, g_in, w['w_in'])
        am = _f2_attn_gm(z, attn_sinks[i], bias, row(gm_ln_g, i), row(gm_ln_b, i), gm_ws[i], bs_full[i])
        if i == 0:
            w.update(finish_gather(first[1], GATHER_GROUPS[1], 0, am, "b"))
        heads, mix, h1 = _f3_mix_out(am, h, row(g_attn_out, i), row(g_gm_out, i), w['w_out'], row(ln_mix_post, i))
        f, gt, up = _f4a_ffn_in(h1, row(ln_ffn_pre, i), w['w_ffn_gate'], w['w_ffn_up'])
        if i == 0:
            w.update(finish_gather(first[2], GATHER_GROUPS[2], 0, gt, "c"))
        dn, h2 = _f4b_ffn_out(gt, up, w['w_ffn_down'], h1, row(ln_ffn_post, i))
        r, pg, pe, h3 = _f5_ple(h2, row(ln_ple_gate, i), w['w_ple_gate'], p[i, 0], w['w_ple'])
        saved.append(dict(h=h, z=z, a=a, am=am, heads=heads, mix=mix, h1=h1, f=f, gt=gt, up=up, dn=dn, h2=h2,
                          r=r, pg=pg, pe=pe))
        h = h3
        if i + 1 < depth:
            full[i + 1] = finish_gather(flight, BIG_NAMES, i + 1, h3, "")

    sq, dh = _loss_head(h, target)
    loss = lax.psum(0.5 / D_MODEL * sq[0, 0], ("x", "y", "c"))

    chain = {n: None for n in BIG_NAMES}
    small_out = [{k: {} for k in kinds} for _ in range(depth)]

    def start_scatter(i, names, dws, after, tag):
        parts = [_split(n, dws[n]) for n in names]
        lands = [lax.empty(q.shape, BF16) for q in parts]
        return _exchange_start(parts, lands, "scatter", f"scatter_grads_start_{i}{tag}", after)

    def finish_scatter(i, names, flight, after, tag):
        parts, lands = _exchange_wait(flight, f"scatter_grads_wait_{i}{tag}", after)
        lands = [_with_own_slot(l, lax.dynamic_index_in_dim(q, chip, 0, keepdims=False), chip)
                 for q, l in zip(parts, lands)]
        partial = [_sum_chips(l, "sum_chips_" + n) for n, l in zip(names, lands)]
        sibling = _sibling_exchange(partial, name="sibling_grads")
        for n, mine, sib in zip(names, partial, sibling):
            chain[n] = _adamw_big(mine, sib, wview[n], mview[n], vview[n], chain[n], i, "adamw_" + n)
        return lands[0]

    def start_small(i, names, small, after, tag):
        packed = _pack_small(small, names)
        land = lax.empty((N_DEV,) + packed.shape, F32)
        return _exchange_start([packed], [land], "devices", f"gather_small_grads_start_{i}{tag}", after)

    def finish_small(i, names, flight, after, tag):
        (packed,), (gathered,) = _exchange_wait(flight, f"gather_small_grads_wait_{i}{tag}", after)
        gathered = _with_own_slot(gathered, packed, device)
        layer = lambda d: _pack_small({n: d[n][i] for n in names}, names)
        res = _adamw_small(gathered, layer(wts), layer(mom), layer(var))
        for k, a in zip(kinds, res):
            small_out[i][k].update(_unpack_small(a, {n: wts[n][i] for n in names}, names))
        return gathered

    pending, done, behind = [], None, None
    for i in reversed(range(depth)):
        s, w = saved[i], full[i]
        last = i == 0
        dws, small = {}, {}
        gain = row(ln_ple_gate, i)
        if behind is not None:
            gain = gain + behind
        dpe, dpg, dh2, dg = _b5_ple(dh, s['h2'], s['pg'], s['pe'], gain, w['w_ple_gate'])
        small['ln_ple_gate'] = dg[0]
        dws['w_ple'] = _weight_grad(p[i, 0], dpe, "dw_ple")
        dws['w_ple_gate'] = _weight_grad(s['r'], dpg, "dw_ple_gate")

        ddn, act, dgt, dup, dg = _b4a_ffn_out(dh2, s['dn'], row(ln_ffn_post, i), w['w_ffn_down'], s['gt'], s['up'])
        small['ln_ffn_post'] = dg[0]
        dws['w_ffn_down'] = _weight_grad(act, ddn, "dw_ffn_down")
        dws['w_ffn_gate'] = _weight_grad(dgt, s['f'], "dw_ffn_gate")
        dws['w_ffn_up'] = _weight_grad(dup, s['f'], "dw_ffn_up")
        gain = row(ln_ffn_pre, i)
        if last:
            flight_a = start_scatter(i, SCATTER_GROUPS[0], dws, None, "a")
            gain = gain + zero(flight_a)
        dh1, dg = _b4b_ffn_in(dgt, dup, w['w_ffn_gate'], w['w_ffn_up'], s['h1'], gain, dh2)
        small['ln_ffn_pre'] = dg[0]

        dmix, dam, dgp, dga, dgg = _b3_mix_out(dh1, s['mix'], row(ln_mix_post, i), w['w_out'], s['am'],
                                               row(g_attn_out, i), row(g_gm_out, i))
        small['ln_mix_post'] = dgp[0]
        small['g_attn_out'] = dga[0]
        small['g_gm_out'] = dgg[0]
        dws['w_out'] = _weight_grad(s['heads'], dmix, "dw_out")
        gain = row(gm_ln_g, i)
        if last:
            flight_b = start_scatter(i, SCATTER_GROUPS[1], dws, flight_a["token"], "b")
            gain = gain + zero(flight_b)

        dzq, dkv, dzuv, dsink, dlng, dlnb, dgws, dbs = _b2_attn_gm(
            dam, s['z'], attn_sinks[i], bias, gain, row(gm_ln_b, i), gm_ws[i], bs_full[i])
        small['attn_sinks'] = dsink[0, :N_Q_HEADS]
        small['gm_ln_g'] = dlng[0]
        small['gm_ln_b'] = dlnb[0]
        small['gm_ws'] = dgws
        small['gm_bs'] = dbs[:, :N_Q_HEADS].T
        gain = row(ln_mix_pre, i)
        if last:
            flight_s = start_small(i, SMALL_EARLY, small, flight_b["token"], "a")
            gain = gain + zero(flight_s)

        dz, dh, dg = _b1_norm_in(dzq, dkv, dzuv, w['w_in'], s['h'], gain, dh1)
        small['ln_mix_pre'] = dg[0]
        for finish in pending:
            done = finish(dh)
        pending = []
        if last:
            done = finish_scatter(i, SCATTER_GROUPS[0], flight_a, dz, "a")
            done = finish_scatter(i, SCATTER_GROUPS[1], flight_b, done, "b")
        dws['w_in'] = _weight_grad(dz, s['a'], "dw_in")
        if last:
            flight_c = start_scatter(i, SCATTER_GROUPS[2], dws, done, "c")
            flight_t = start_small(i, SMALL_LATE, small, flight_c["token"], "b")
            finish_small(i, SMALL_EARLY, flight_s, flight_t["token"], "a")
            finish_scatter(i, SCATTER_GROUPS[2], flight_c, flight_t["token"], "c")
            finish_small(i, SMALL_LATE, flight_t, flight_t["token"], "b")
        else:
            flight_c = start_scatter(i, BIG_NAMES, dws, done, "")
            flight_s = start_small(i, SMALL_NAMES, small, flight_c["token"], "")
            behind = zero(flight_s)
            pending = [functools.partial(finish_scatter, i, BIG_NAMES, flight_c, tag=""),
                       functools.partial(finish_small, i, SMALL_NAMES, flight_s, tag="")]
    grad_x = dh[None]

    out = {k: {n: _shard_view(n, chain[n][j]) for n in BIG_NAMES} for j, k in enumerate(kinds)}
    for k in kinds:
        out[k].update({n: jnp.stack([small_out[i][k][n] for i in range(depth)]) for n in SMALL_NAMES})

    return (loss, grad_x, *[out["grad"][n] for n in WEIGHTS], *[out["delta"][n] for n in WEIGHTS],
            *[out["m"][n] for n in WEIGHTS], *[out["v"][n] for n in WEIGHTS])
```

```python
import functools
import math

import jax
import jax.numpy as jnp
from jax import lax
from jax.experimental import pallas as pl
from jax.experimental.pallas import tpu as pltpu

F32 = jnp.float32
BF16 = jnp.bfloat16

D_MODEL = 1024
HEAD_DIM = 64
N_Q_HEADS = 8
BLK = 128
ATTN_W = 512
KV_W = 128
GM_W = 512
D_IN = ATTN_W + 2 * KV_W + 2 * GM_W
D_FF = 2816
PLE_DIM = 256
DEPTH = 4
NORM_EPS = 1e-6
NEG_BIG = -1e30
N_CHIPS = 4
N_DEV = 8

ADAM_LR = 0.001
ADAM_B1 = 0.9
ADAM_B2 = 0.999
ADAM_EPS = 1e-08
ADAM_WD = 0.01
ADAM_STEP = 10

VMEM_LIMIT_BYTES = 56 * 1024 * 1024
LANES = 128
STRIP = 16
GELU_C0 = math.sqrt(2.0 / math.pi)
GELU_C1 = 0.044715
ALIBI_SLOPES = tuple(2.0 ** (-8.0 * (h + 1.0) / N_Q_HEADS) for h in range(N_Q_HEADS))

WEIGHTS = ['ln_mix_pre', 'w_in', 'attn_sinks', 'gm_ln_g', 'gm_ln_b', 'gm_ws', 'gm_bs', 'g_attn_out',
           'g_gm_out', 'w_out', 'ln_mix_post', 'ln_ffn_pre', 'w_ffn_gate', 'w_ffn_up', 'w_ffn_down',
           'ln_ffn_post', 'w_ple', 'ln_ple_gate', 'w_ple_gate']
BIG = {'w_in': 2, 'w_out': 1, 'w_ffn_gate': 2, 'w_ffn_up': 2, 'w_ffn_down': 1, 'w_ple': 2, 'w_ple_gate': 1}
BIG_NAMES = list(BIG)
TRANSPOSED = ('w_in', 'w_ffn_gate', 'w_ffn_up')
SMALL_NAMES = [n for n in WEIGHTS if n not in BIG]
SMALL_PAD = 1024
GATHER_GROUPS = (('w_in',), ('w_out', 'w_ffn_gate', 'w_ffn_up'), ('w_ffn_down', 'w_ple_gate', 'w_ple'))
SCATTER_GROUPS = (('w_ple', 'w_ple_gate', 'w_ffn_down', 'w_ffn_gate', 'w_ffn_up'), ('w_out',), ('w_in',))
SMALL_LATE = ('ln_mix_pre',)
SMALL_EARLY = tuple(n for n in SMALL_NAMES if n not in SMALL_LATE)


def _nt(a, b):
    return lax.dot_general(a, b, (((1,), (1,)), ((), ())), preferred_element_type=F32)


def _tn(a, b):
    return lax.dot_general(a, b, (((0,), (0,)), ((), ())), preferred_element_type=F32)


def _mm(a, b):
    return jnp.dot(a, b, preferred_element_type=F32)


def _rms(x, g):
    r = lax.rsqrt(jnp.mean(x * x, axis=-1, keepdims=True) + NORM_EPS)
    return x * r * g


def _rms_bwd(dy, x, g):
    r = lax.rsqrt(jnp.mean(x * x, axis=-1, keepdims=True) + NORM_EPS)
    xh = x * r
    dg = jnp.sum(dy * xh, axis=0, keepdims=True)
    dxh = dy * g
    dx = r * (dxh - xh * jnp.mean(dxh * xh, axis=-1, keepdims=True))
    return dx, dg


def _gelu(x):
    return 0.5 * x * (1.0 + jnp.tanh(GELU_C0 * (x + GELU_C1 * x * x * x)))


def _sigmoid(x):
    return 1.0 / (1.0 + jnp.exp(-x))


def _rows(tm, n):
    return pl.BlockSpec((tm, n), lambda i: (i, 0))


def _whole(shape):
    return pl.BlockSpec(shape, lambda i: (0,) * len(shape))


def _accumulate(ref, val):
    @pl.when(pl.program_id(0) == 0)
    def _():
        ref[...] = jnp.zeros_like(ref)

    ref[...] += val


def _params(n_axes=1):
    return pltpu.CompilerParams(dimension_semantics=("arbitrary",) * n_axes,
                                vmem_limit_bytes=VMEM_LIMIT_BYTES)


def _sds(shape, dtype):
    return jax.ShapeDtypeStruct(shape, dtype)


def _tile(t, want):
    return min(t, want)


def _f1_norm_in(h, g, w_t):
    t = h.shape[0]
    tm = _tile(t, 512)

    def body(h_ref, g_ref, w_ref, z_ref, a_ref):
        a = _rms(h_ref[...], g_ref[...]).astype(BF16)
        a_ref[...] = a
        z_ref[...] = _nt(a, w_ref[...])

    return pl.pallas_call(
        body, name="f1_norm_in", grid=(t // tm,),
        in_specs=[_rows(tm, D_MODEL), _whole((1, D_MODEL)), _whole((D_IN, D_MODEL))],
        out_specs=[_rows(tm, D_IN), _rows(tm, D_MODEL)],
        out_shape=[_sds((t, D_IN), F32), _sds((t, D_MODEL), BF16)],
        compiler_params=_params())(h, g, w_t)


def _alibi_bias():
    ti = jnp.arange(BLK)[:, None]
    ji = jnp.arange(2 * BLK)[None, :]
    dist = ti + BLK - ji
    band = (dist >= 0) & (dist < BLK)
    bias = -jnp.asarray(ALIBI_SLOPES, F32)[:, None, None] * dist.astype(F32)[None]
    return jnp.stack([jnp.where((band & (ji >= BLK))[None], bias, NEG_BIG), jnp.where(band[None], bias, NEG_BIG)])


def _strips():
    return [slice(r * STRIP, (r + 1) * STRIP) for r in range(BLK // STRIP)]


def _bias_spec(transposed=False):
    tile = (2 * BLK, BLK) if transposed else (BLK, 2 * BLK)
    return pl.BlockSpec((None, N_Q_HEADS) + tile, lambda i: (jnp.minimum(i, 1), 0, 0, 0))


def _softmax_strip(s_ref, bias_ref, hq, rows, sink):
    sc = s_ref[hq, rows, :] + bias_ref[hq, rows, :]
    m = jnp.maximum(jnp.max(sc, axis=1, keepdims=True), sink)
    e = jnp.exp(sc - m)
    es = jnp.exp(sink - m)
    inv = 1.0 / (jnp.sum(e, axis=1, keepdims=True) + es)
    return e * inv, es * inv


def _kv_window(z_ref, kp_ref, vp_ref):
    kcat = jnp.concatenate([kp_ref[...], z_ref[:, ATTN_W:ATTN_W + KV_W]], axis=0)
    vcat = jnp.concatenate([vp_ref[...], z_ref[:, ATTN_W + KV_W:ATTN_W + 2 * KV_W]], axis=0)
    kswap = pltpu.roll(kcat, HEAD_DIM, 1)
    vswap = pltpu.roll(vcat, HEAD_DIM, 1)
    return kcat.astype(BF16), kswap.astype(BF16), vcat, vswap


def _gelu_and_grad(x):
    t = jnp.tanh(GELU_C0 * (x + GELU_C1 * x * x * x))
    return 0.5 * x * (1.0 + t), 0.5 * (1.0 + t) + 0.5 * x * (1.0 - t * t) * GELU_C0 * (1.0 + 3.0 * GELU_C1 * x * x)


def _layernorm_strip(gv, lng_ref, lnb_ref):
    xc = gv - jnp.mean(gv, axis=-1, keepdims=True)
    rstd = lax.rsqrt(jnp.mean(xc * xc, axis=-1, keepdims=True) + NORM_EPS)
    xhat = xc * rstd
    return xhat * lng_ref[...] + lnb_ref[...], xhat, rstd


def _tril_w(ws_ref, h):
    ti = lax.broadcasted_iota(jnp.int32, (BLK, BLK), 0)
    si = lax.broadcasted_iota(jnp.int32, (BLK, BLK), 1)
    causal = si <= ti
    return jnp.where(causal, ws_ref[h], 0.0).astype(BF16), causal


def _gm_mixed(vn, ws_ref, bs_ref, lo, hi):
    slabs = []
    for s in range(GM_W // LANES):
        vs = vn[:, s * LANES:(s + 1) * LANES]
        w0, _ = _tril_w(ws_ref, 2 * s)
        w1, _ = _tril_w(ws_ref, 2 * s + 1)
        mixed = (_mm(w0, jnp.where(lo, vs, 0.0).astype(BF16))
                 + _mm(w1, jnp.where(hi, vs, 0.0).astype(BF16))
                 + bs_ref[:, s * LANES:(s + 1) * LANES])
        slabs.append(mixed)
    return slabs


def _block_specs_z(nb):
    prev = lambda i: (jnp.maximum(i - 1, 0), ATTN_W // KV_W)
    prev_v = lambda i: (jnp.maximum(i - 1, 0), ATTN_W // KV_W + 1)
    return [_rows(BLK, D_IN), pl.BlockSpec((BLK, KV_W), prev), pl.BlockSpec((BLK, KV_W), prev_v)]


def _heads():
    return [(2 * s + half, s, half, s // 2 == half) for s in range(ATTN_W // LANES) for half in range(2)]


def _f2_attn_gm(z, sinks, bias, ln_g, ln_b, ws, bs_full):
    t = z.shape[0]
    nb = t // BLK

    def body(z_ref, kp_ref, vp_ref, sink_ref, bias_ref, lng_ref, lnb_ref, ws_ref, bs_ref, am_ref,
             s_ref, p_ref, u_ref, vn_ref):
        lane = lax.broadcasted_iota(jnp.int32, (1, LANES), 1)
        lo = lane < HEAD_DIM
        hi = lane >= HEAD_DIM
        kc, ks, vcat, vswap = _kv_window(z_ref, kp_ref, vp_ref)
        for hq, s, half, same in _heads():
            qs = z_ref[:, s * LANES:(s + 1) * LANES] * (HEAD_DIM ** -0.5)
            qm = jnp.where(lo if half == 0 else hi, qs, 0.0).astype(BF16)
            s_ref[hq] = _nt(qm, kc if same else ks)
        for hq in range(N_Q_HEADS):
            for rows in _strips():
                pr, _ = _softmax_strip(s_ref, bias_ref, hq, rows, sink_ref[hq])
                p_ref[hq, rows, :] = pr.astype(BF16)
        for s in range(ATTN_W // LANES):
            o = jnp.zeros((BLK, LANES), F32)
            for hq, hs, half, same in _heads():
                if hs == s:
                    vm = jnp.where(lo if half == 0 else hi, vcat if same else vswap, 0.0).astype(BF16)
                    o = o + _mm(p_ref[hq], vm)
            am_ref[:, s * LANES:(s + 1) * LANES] = o.astype(BF16)

        for rows in _strips():
            u_ref[rows, :] = _gelu(z_ref[rows, ATTN_W + 2 * KV_W:ATTN_W + 2 * KV_W + GM_W])
            vn_ref[rows, :], _, _ = _layernorm_strip(_gelu(z_ref[rows, ATTN_W + 2 * KV_W + GM_W:D_IN]),
                                                     lng_ref, lnb_ref)
        mixed = _gm_mixed(vn_ref, ws_ref, bs_ref, lo, hi)
        for s in range(GM_W // LANES):
            am_ref[:, ATTN_W + s * LANES:ATTN_W + (s + 1) * LANES] = (
                u_ref[:, s * LANES:(s + 1) * LANES] * mixed[s]).astype(BF16)

    return pl.pallas_call(
        body, name="f2_attn_gm", grid=(nb,),
        in_specs=_block_specs_z(nb) + [
            pl.BlockSpec(memory_space=pltpu.SMEM), _bias_spec(), _whole((1, GM_W)),
            _whole((1, GM_W)), _whole((N_Q_HEADS, BLK, BLK)), _whole((BLK, GM_W))],
        out_specs=_rows(BLK, ATTN_W + GM_W),
        out_shape=_sds((t, ATTN_W + GM_W), BF16),
        scratch_shapes=[pltpu.VMEM((N_Q_HEADS, BLK, 2 * BLK), F32), pltpu.VMEM((N_Q_HEADS, BLK, 2 * BLK), BF16),
                        pltpu.VMEM((BLK, GM_W), F32), pltpu.VMEM((BLK, GM_W), F32)],
        compiler_params=_params())(z, z, z, sinks, bias, ln_g, ln_b, ws, bs_full)


def _f3_mix_out(am, h, ga, gg, w, gpost):
    t = h.shape[0]
    tm = _tile(t, 512)

    def body(am_ref, h_ref, ga_ref, gg_ref, w_ref, gp_ref, heads_ref, mix_ref, h1_ref):
        heads = jnp.concatenate([_rms(am_ref[:, :ATTN_W].astype(F32), ga_ref[...]),
                                 _rms(am_ref[:, ATTN_W:].astype(F32), gg_ref[...])], axis=1).astype(BF16)
        heads_ref[...] = heads
        mix = _mm(heads, w_ref[...])
        mix_ref[...] = mix.astype(BF16)
        h1_ref[...] = h_ref[...] + _rms(mix, gp_ref[...])

    return pl.pallas_call(
        body, name="f3_mix_out", grid=(t // tm,),
        in_specs=[_rows(tm, D_MODEL), _rows(tm, D_MODEL), _whole((1, ATTN_W)), _whole((1, GM_W)),
                  _whole((D_MODEL, D_MODEL)), _whole((1, D_MODEL))],
        out_specs=[_rows(tm, D_MODEL)] * 3,
        out_shape=[_sds((t, D_MODEL), BF16), _sds((t, D_MODEL), BF16), _sds((t, D_MODEL), F32)],
        compiler_params=_params())(am, h, ga, gg, w, gpost)


def _f4a_ffn_in(h1, gf, wg_t, wu_t):
    t = h1.shape[0]
    tm = _tile(t, 256)

    def body(h_ref, g_ref, wg_ref, wu_ref, f_ref, gt_ref, up_ref):
        f = _rms(h_ref[...], g_ref[...]).astype(BF16)
        f_ref[...] = f
        gt_ref[...] = _nt(f, wg_ref[...]).astype(BF16)
        up_ref[...] = _nt(f, wu_ref[...]).astype(BF16)

    return pl.pallas_call(
        body, name="f4a_ffn_in", grid=(t // tm,),
        in_specs=[_rows(tm, D_MODEL), _whole((1, D_MODEL)), _whole((D_FF, D_MODEL)), _whole((D_FF, D_MODEL))],
        out_specs=[_rows(tm, D_MODEL), _rows(tm, D_FF), _rows(tm, D_FF)],
        out_shape=[_sds((t, D_MODEL), BF16), _sds((t, D_FF), BF16), _sds((t, D_FF), BF16)],
        compiler_params=_params())(h1, gf, wg_t, wu_t)


def _f4b_ffn_out(gt, up, wd, h1, gfp):
    t = h1.shape[0]
    tm = _tile(t, 256)

    def body(gt_ref, up_ref, wd_ref, h_ref, g_ref, dn_ref, h2_ref):
        gt = gt_ref[...].astype(F32)
        act = (gt * _sigmoid(gt) * up_ref[...].astype(F32)).astype(BF16)
        dn = _mm(act, wd_ref[...])
        dn_ref[...] = dn.astype(BF16)
        h2_ref[...] = h_ref[...] + _rms(dn, g_ref[...])

    return pl.pallas_call(
        body, name="f4b_ffn_out", grid=(t // tm,),
        in_specs=[_rows(tm, D_FF), _rows(tm, D_FF), _whole((D_FF, D_MODEL)), _rows(tm, D_MODEL),
                  _whole((1, D_MODEL))],
        out_specs=[_rows(tm, D_MODEL)] * 2,
        out_shape=[_sds((t, D_MODEL), BF16), _sds((t, D_MODEL), F32)],
        compiler_params=_params())(gt, up, wd, h1, gfp)


def _f5_ple(h2, gpl, wpg, p, wple):
    t = h2.shape[0]
    tm = _tile(t, 512)

    def body(h_ref, g_ref, wpg_ref, p_ref, wple_ref, r_ref, pg_ref, pe_ref, h3_ref):
        h = h_ref[...]
        r = _rms(h, g_ref[...]).astype(BF16)
        r_ref[...] = r
        pg = _mm(r, wpg_ref[...])
        pe = _mm(p_ref[...].astype(BF16), wple_ref[...])
        pg_ref[...] = pg.astype(BF16)
        pe_ref[...] = pe.astype(BF16)
        h3_ref[...] = h + pe * _sigmoid(pg)

    return pl.pallas_call(
        body, name="f5_ple", grid=(t // tm,),
        in_specs=[_rows(tm, D_MODEL), _whole((1, D_MODEL)), _whole((D_MODEL, D_MODEL)), _rows(tm, PLE_DIM),
                  _whole((PLE_DIM, D_MODEL))],
        out_specs=[_rows(tm, D_MODEL)] * 4,
        out_shape=[_sds((t, D_MODEL), BF16)] * 3 + [_sds((t, D_MODEL), F32)],
        compiler_params=_params())(h2, gpl, wpg, p, wple)


def _loss_head(y, target):
    t = y.shape[0]
    tm = _tile(t, 512)

    def body(y_ref, t_ref, sq_ref, dy_ref):
        err = y_ref[...] - t_ref[...]
        dy_ref[...] = err * (1.0 / D_MODEL)
        _accumulate(sq_ref, jnp.sum(err * err, keepdims=True))

    return pl.pallas_call(
        body, name="loss_head", grid=(t // tm,),
        in_specs=[_rows(tm, D_MODEL)] * 2,
        out_specs=[_whole((1, LANES)), _rows(tm, D_MODEL)],
        out_shape=[_sds((1, LANES), F32), _sds((t, D_MODEL), F32)],
        compiler_params=_params())(y, target)


def _b5_ple(dh3, h2, pg, pe, gpl, wpg):
    t = h2.shape[0]
    tm = _tile(t, 512)

    def body(dh_ref, h_ref, pg_ref, pe_ref, g_ref, w_ref, dpe_ref, dpg_ref, dh2_ref, dg_ref):
        dh = dh_ref[...]
        s = _sigmoid(pg_ref[...].astype(F32))
        dpe_ref[...] = (dh * s).astype(BF16)
        dpg = (dh * pe_ref[...].astype(F32) * s * (1.0 - s)).astype(BF16)
        dpg_ref[...] = dpg
        dx, dg = _rms_bwd(_nt(dpg, w_ref[...]), h_ref[...], g_ref[...])
        dh2_ref[...] = dh + dx
        _accumulate(dg_ref, dg)

    return pl.pallas_call(
        body, name="b5_ple", grid=(t // tm,),
        in_specs=[_rows(tm, D_MODEL)] * 4 + [_whole((1, D_MODEL)), _whole((D_MODEL, D_MODEL))],
        out_specs=[_rows(tm, D_MODEL)] * 3 + [_whole((1, D_MODEL))],
        out_shape=[_sds((t, D_MODEL), BF16)] * 2 + [_sds((t, D_MODEL), F32), _sds((1, D_MODEL), F32)],
        compiler_params=_params())(dh3, h2, pg, pe, gpl, wpg)


def _b4a_ffn_out(dh2, dn, gfp, wd, gt, up):
    t = dh2.shape[0]
    tm = _tile(t, 256)

    def body(dh_ref, dn_ref, g_ref, wd_ref, gt_ref, up_ref, ddn_ref, act_ref, dgt_ref, dup_ref, dg_ref):
        ddn, dg = _rms_bwd(dh_ref[...], dn_ref[...].astype(F32), g_ref[...])
        _accumulate(dg_ref, dg)
        ddn = ddn.astype(BF16)
        ddn_ref[...] = ddn
        dact = _nt(ddn, wd_ref[...])
        gt = gt_ref[...].astype(F32)
        up = up_ref[...].astype(F32)
        sg = _sigmoid(gt)
        silu = gt * sg
        act_ref[...] = (silu * up).astype(BF16)
        dup_ref[...] = (dact * silu).astype(BF16)
        dgt_ref[...] = (dact * up * (sg * (1.0 + gt * (1.0 - sg)))).astype(BF16)

    return pl.pallas_call(
        body, name="b4a_ffn_out", grid=(t // tm,),
        in_specs=[_rows(tm, D_MODEL), _rows(tm, D_MODEL), _whole((1, D_MODEL)), _whole((D_FF, D_MODEL)),
                  _rows(tm, D_FF), _rows(tm, D_FF)],
        out_specs=[_rows(tm, D_MODEL), _rows(tm, D_FF), _rows(tm, D_FF), _rows(tm, D_FF), _whole((1, D_MODEL))],
        out_shape=[_sds((t, D_MODEL), BF16)] + [_sds((t, D_FF), BF16)] * 3 + [_sds((1, D_MODEL), F32)],
        compiler_params=_params())(dh2, dn, gfp, wd, gt, up)


def _b4b_ffn_in(dgt, dup, wg_t, wu_t, h1, gf, dh2):
    t = h1.shape[0]
    tm = _tile(t, 512)

    def body(dgt_ref, dup_ref, wg_ref, wu_ref, h_ref, g_ref, dh_ref, dh1_ref, dg_ref):
        df = _mm(dgt_ref[...], wg_ref[...]) + _mm(dup_ref[...], wu_ref[...])
        dx, dg = _rms_bwd(df, h_ref[...], g_ref[...])
        dh1_ref[...] = dh_ref[...] + dx
        _accumulate(dg_ref, dg)

    return pl.pallas_call(
        body, name="b4b_ffn_in", grid=(t // tm,),
        in_specs=[_rows(tm, D_FF), _rows(tm, D_FF), _whole((D_FF, D_MODEL)), _whole((D_FF, D_MODEL)),
                  _rows(tm, D_MODEL), _whole((1, D_MODEL)), _rows(tm, D_MODEL)],
        out_specs=[_rows(tm, D_MODEL), _whole((1, D_MODEL))],
        out_shape=[_sds((t, D_MODEL), F32), _sds((1, D_MODEL), F32)],
        compiler_params=_params())(dgt, dup, wg_t, wu_t, h1, gf, dh2)


def _b3_mix_out(dh1, mix, gpost, w, am, ga, gg):
    t = dh1.shape[0]
    tm = _tile(t, 512)

    def body(dh_ref, mix_ref, gp_ref, w_ref, am_ref, ga_ref, gg_ref, dmix_ref, dam_ref, dgp_ref, dga_ref, dgg_ref):
        dmix, dgp = _rms_bwd(dh_ref[...], mix_ref[...].astype(F32), gp_ref[...])
        _accumulate(dgp_ref, dgp)
        dmix = dmix.astype(BF16)
        dmix_ref[...] = dmix
        dheads = _nt(dmix, w_ref[...])
        dat, dga = _rms_bwd(dheads[:, :ATTN_W], am_ref[:, :ATTN_W].astype(F32), ga_ref[...])
        dgm, dgg = _rms_bwd(dheads[:, ATTN_W:], am_ref[:, ATTN_W:].astype(F32), gg_ref[...])
        dam_ref[:, :ATTN_W] = dat.astype(BF16)
        dam_ref[:, ATTN_W:] = dgm.astype(BF16)
        _accumulate(dga_ref, dga)
        _accumulate(dgg_ref, dgg)

    return pl.pallas_call(
        body, name="b3_mix_out", grid=(t // tm,),
        in_specs=[_rows(tm, D_MODEL), _rows(tm, D_MODEL), _whole((1, D_MODEL)), _whole((D_MODEL, D_MODEL)),
                  _rows(tm, D_MODEL), _whole((1, ATTN_W)), _whole((1, GM_W))],
        out_specs=[_rows(tm, D_MODEL), _rows(tm, D_MODEL), _whole((1, D_MODEL)), _whole((1, ATTN_W)),
                   _whole((1, GM_W))],
        out_shape=[_sds((t, D_MODEL), BF16), _sds((t, D_MODEL), BF16), _sds((1, D_MODEL), F32),
                   _sds((1, ATTN_W), F32), _sds((1, GM_W), F32)],
        compiler_params=_params())(dh1, mix, gpost, w, am, ga, gg)


def _b2_attn_gm(dam, z, sinks, bias, ln_g, ln_b, ws, bs_full):
    t = z.shape[0]
    nb = t // BLK

    def body(dam_ref, z_ref, kp_ref, vp_ref, sink_ref, bias_ref, lng_ref, lnb_ref, ws_ref, wst_ref, bs_ref,
             dzq_ref, dkv_ref, dzuv_ref, dsink_ref, dlng_ref, dlnb_ref, dws_ref, dbs_ref,
             u_ref, du_ref, dv_ref, vn_ref, xhat_ref, rstd_ref, dvn_ref):
        n = pl.program_id(0)
        lane = lax.broadcasted_iota(jnp.int32, (1, LANES), 1)
        lo = lane < HEAD_DIM
        hi = lane >= HEAD_DIM
        sub = lax.broadcasted_iota(jnp.int32, (LANES, 1), 0)
        kc, ks, vcat, vswap = _kv_window(z_ref, kp_ref, vp_ref)
        vc = vcat.astype(BF16)
        vs_ = vswap.astype(BF16)
        kc_t = kc.T
        ks_t = ks.T

        dk_acc = jnp.zeros((2 * BLK, LANES), F32)
        dv_acc = jnp.zeros((2 * BLK, LANES), F32)
        dsink = jnp.zeros((1, LANES), F32)
        for s in range(ATTN_W // LANES):
            qs = z_ref[:, s * LANES:(s + 1) * LANES] * (HEAD_DIM ** -0.5)
            dos = dam_ref[:, s * LANES:(s + 1) * LANES]
            dq_t = jnp.zeros((LANES, BLK), F32)
            for hq, hs, half, same in _heads():
                if hs != s:
                    continue
                mask = lo if half == 0 else hi
                qm = jnp.where(mask, qs, 0.0).astype(BF16)
                dom = jnp.where(mask, dos, 0.0).astype(BF16)
                sink = sink_ref[hq]
                sc = _nt(kc if same else ks, qm) + bias_ref[hq]
                m = jnp.maximum(jnp.max(sc, axis=0, keepdims=True), sink)
                e = jnp.exp(sc - m)
                es = jnp.exp(sink - m)
                inv = 1.0 / (jnp.sum(e, axis=0, keepdims=True) + es)
                pr = e * inv
                dpr = _nt(vc if same else vs_, dom)
                row = jnp.sum(dpr * pr, axis=0, keepdims=True)
                ds = (pr * (dpr - row)).astype(BF16)
                dsink = dsink + jnp.where(lane == hq, -jnp.sum(es * inv * row, keepdims=True), 0.0)
                dkc = _mm(ds, qm)
                dvc = _mm(pr.astype(BF16), dom)
                if not same:
                    dkc = pltpu.roll(dkc, HEAD_DIM, 1)
                    dvc = pltpu.roll(dvc, HEAD_DIM, 1)
                dk_acc = dk_acc + dkc
                dv_acc = dv_acc + dvc
                in_half = (sub < HEAD_DIM) if half == 0 else (sub >= HEAD_DIM)
                dq_t = dq_t + jnp.where(in_half, _mm(kc_t if same else ks_t, ds), 0.0)
            dzq_ref[:, s * LANES:(s + 1) * LANES] = (dq_t.T * (HEAD_DIM ** -0.5)).astype(BF16)
        cur = pl.multiple_of(n * BLK, BLK)
        dkv_ref[pl.ds(cur, BLK), 0:KV_W] = dk_acc[BLK:, :]
        dkv_ref[pl.ds(cur, BLK), KV_W:2 * KV_W] = dv_acc[BLK:, :]

        @pl.when(n > 0)
        def _():
            prv = pl.multiple_of((n - 1) * BLK, BLK)
            dkv_ref[pl.ds(prv, BLK), 0:KV_W] += dk_acc[:BLK, :]
            dkv_ref[pl.ds(prv, BLK), KV_W:2 * KV_W] += dv_acc[:BLK, :]

        _accumulate(dsink_ref, dsink)

        for rows in _strips():
            u_ref[rows, :], du_ref[rows, :] = _gelu_and_grad(
                z_ref[rows, ATTN_W + 2 * KV_W:ATTN_W + 2 * KV_W + GM_W])
            gv, dv_ref[rows, :] = _gelu_and_grad(z_ref[rows, ATTN_W + 2 * KV_W + GM_W:D_IN])
            vn_ref[rows, :], xhat_ref[rows, :], rstd = _layernorm_strip(gv, lng_ref, lnb_ref)
            rstd_ref[rows, :] = jnp.broadcast_to(rstd, (STRIP, LANES))
        mixed = _gm_mixed(vn_ref, ws_ref, bs_ref, lo, hi)

        @pl.when(n == 0)
        def _():
            dws_ref[...] = jnp.zeros_like(dws_ref)

        dbs = jnp.zeros((BLK, LANES), F32)
        for s in range(GM_W // LANES):
            slab = slice(s * LANES, (s + 1) * LANES)
            dgm = dam_ref[:, ATTN_W + s * LANES:ATTN_W + (s + 1) * LANES]
            dzuv_ref[:, slab] = (dgm * mixed[s] * du_ref[:, slab]).astype(BF16)
            dmx = dgm * u_ref[:, slab]
            vsb = vn_ref[:, slab].astype(BF16)
            dvn = jnp.zeros((BLK, LANES), F32)
            for half in range(2):
                h = 2 * s + half
                mask = lo if half == 0 else hi
                dmm = jnp.where(mask, dmx, 0.0)
                dbs = dbs + jnp.where(lane == h, jnp.sum(dmm, axis=1, keepdims=True), 0.0)
                dmm = dmm.astype(BF16)
                _, causal = _tril_w(ws_ref, h)
                ti = lax.broadcasted_iota(jnp.int32, (BLK, BLK), 0)
                si = lax.broadcasted_iota(jnp.int32, (BLK, BLK), 1)
                wt_t = jnp.where(ti <= si, wst_ref[h], 0.0).astype(BF16)
                dvn = dvn + jnp.where(mask, _mm(wt_t, dmm), 0.0)
                dws_ref[h] += jnp.where(causal, _nt(dmm, vsb), 0.0)
            dvn_ref[:, slab] = dvn
        _accumulate(dbs_ref, dbs)
        dlnb = jnp.zeros((STRIP, GM_W), F32)
        dlng = jnp.zeros((STRIP, GM_W), F32)
        for rows in _strips():
            dvn = dvn_ref[rows, :]
            xhat = xhat_ref[rows, :]
            dlnb = dlnb + dvn
            dlng = dlng + dvn * xhat
            dxh = dvn * lng_ref[...]
            dgv = rstd_ref[rows, 0:1] * (dxh - jnp.mean(dxh, axis=-1, keepdims=True)
                                         - xhat * jnp.mean(dxh * xhat, axis=-1, keepdims=True))
            dzuv_ref[rows, GM_W:] = (dgv * dv_ref[rows, :]).astype(BF16)
        _accumulate(dlnb_ref, jnp.sum(dlnb, axis=0, keepdims=True))
        _accumulate(dlng_ref, jnp.sum(dlng, axis=0, keepdims=True))

    return pl.pallas_call(
        body, name="b2_attn_gm", grid=(nb,),
        in_specs=[_rows(BLK, ATTN_W + GM_W)] + _block_specs_z(nb) + [
            pl.BlockSpec(memory_space=pltpu.SMEM), _bias_spec(transposed=True), _whole((1, GM_W)),
            _whole((1, GM_W)), _whole((N_Q_HEADS, BLK, BLK)), _whole((N_Q_HEADS, BLK, BLK)), _whole((BLK, GM_W))],
        out_specs=[_rows(BLK, ATTN_W), _whole((t, 2 * KV_W)), _rows(BLK, 2 * GM_W), _whole((1, LANES)),
                   _whole((1, GM_W)), _whole((1, GM_W)), _whole((N_Q_HEADS, BLK, BLK)), _whole((BLK, LANES))],
        out_shape=[_sds((t, ATTN_W), BF16), _sds((t, 2 * KV_W), F32), _sds((t, 2 * GM_W), BF16),
                   _sds((1, LANES), F32), _sds((1, GM_W), F32), _sds((1, GM_W), F32),
                   _sds((N_Q_HEADS, BLK, BLK), F32), _sds((BLK, LANES), F32)],
        scratch_shapes=[pltpu.VMEM((BLK, GM_W), F32)] * 5
        + [pltpu.VMEM((BLK, LANES), F32), pltpu.VMEM((BLK, GM_W), F32)],
        compiler_params=_params())(dam, z, z, z, sinks, jnp.swapaxes(bias, 2, 3), ln_g, ln_b, ws,
                                   jnp.swapaxes(ws, 1, 2), bs_full)


def _b1_norm_in(dzq, dkv, dzuv, w_t, h, g, dh1):
    t = h.shape[0]
    tm = _tile(t, 512)

    def body(dzq_ref, dkv_ref, dzuv_ref, w_ref, h_ref, g_ref, dh_ref, dz_ref, dh0_ref, dg_ref):
        dz = jnp.concatenate([dzq_ref[...], dkv_ref[...].astype(BF16), dzuv_ref[...]], axis=1)
        dz_ref[...] = dz
        dx, dg = _rms_bwd(_mm(dz, w_ref[...]), h_ref[...], g_ref[...])
        dh0_ref[...] = dh_ref[...] + dx
        _accumulate(dg_ref, dg)

    return pl.pallas_call(
        body, name="b1_norm_in", grid=(t // tm,),
        in_specs=[_rows(tm, ATTN_W), _rows(tm, 2 * KV_W), _rows(tm, 2 * GM_W), _whole((D_IN, D_MODEL)),
                  _rows(tm, D_MODEL), _whole((1, D_MODEL)), _rows(tm, D_MODEL)],
        out_specs=[_rows(tm, D_IN), _rows(tm, D_MODEL), _whole((1, D_MODEL))],
        out_shape=[_sds((t, D_IN), BF16), _sds((t, D_MODEL), F32), _sds((1, D_MODEL), F32)],
        compiler_params=_params())(dzq, dkv, dzuv, w_t, h, g, dh1)


def _weight_grad(x, dy, name):
    t, k = x.shape
    n = dy.shape[1]
    tm = _tile(t, 1024)
    steps = t // tm

    def body(x_ref, dy_ref, dw_ref, acc_ref):
        i = pl.program_id(0)

        @pl.when(i == 0)
        def _():
            acc_ref[...] = _tn(x_ref[...].astype(BF16), dy_ref[...])

        @pl.when(i > 0)
        def _():
            acc_ref[...] += _tn(x_ref[...].astype(BF16), dy_ref[...])

        @pl.when(i == steps - 1)
        def _():
            dw_ref[...] = acc_ref[...].astype(BF16)

    return pl.pallas_call(
        body, name=name, grid=(steps,),
        in_specs=[_rows(tm, k), _rows(tm, n)],
        out_specs=_whole((k, n)),
        out_shape=_sds((k, n), BF16),
        scratch_shapes=[pltpu.VMEM((k, n), F32)],
        compiler_params=_params())(x, dy)


_ANY = pl.BlockSpec(memory_space=pl.ANY)
_HBM = pl.BlockSpec(memory_space=pltpu.HBM)
_SEM = pl.BlockSpec(memory_space=pltpu.SEMAPHORE)
_EFFECT = pltpu.SideEffectType.DATAFLOW_SIDE_EFFECTING


def _mesh_pos():
    return lax.axis_index("x"), lax.axis_index("y"), lax.axis_index("c")


def _other_chips(x, y):
    return [(1 - x, y), (x, 1 - y), (1 - x, 1 - y)]


def _peers(mode, x, y, c):
    if mode == "devices":
        peers = []
        for j in range(1, N_DEV):
            px = 1 - x if (j >> 2) & 1 else x
            py = 1 - y if (j >> 1) & 1 else y
            pc = 1 - c if j & 1 else c
            peers.append(((px, py, pc), 4 * px + 2 * py + pc))
        return peers, 4 * x + 2 * y + c
    return [((px, py, c), 2 * px + py) for px, py in _other_chips(x, y)], 2 * x + y


def _descriptors(mode, srcs, lands, send_sems, recv_sems, with_incoming=True):
    x, y, c = _mesh_pos()
    peers, me = _peers(mode, x, y, c)
    scatter = mode == "scatter"
    outgoing, incoming = [], []
    for i, (src, land) in enumerate(zip(srcs, lands)):
        for j, (dev, slot) in enumerate(peers):
            sem = i * len(peers) + j
            common = dict(send_sem=send_sems.at[sem], recv_sem=recv_sems.at[sem], device_id=dev,
                          device_id_type=pl.DeviceIdType.MESH)
            outgoing.append(pltpu.make_async_remote_copy(
                src_ref=src.at[slot] if scatter else src, dst_ref=land.at[me], **common))
            if with_incoming:
                incoming.append(pltpu.make_async_remote_copy(
                    src_ref=src.at[me] if scatter else src, dst_ref=land.at[slot], **common))
    return outgoing, incoming


def _n_sems(mode, k):
    return k * ((N_DEV if mode == "devices" else N_CHIPS) - 1)


def _exchange_start(srcs, lands, mode, name, after=None):
    k = len(srcs)
    arrs = [*srcs, *lands]

    def body(*refs):
        skip = 1 if after is not None else 0
        send_sems, recv_sems = refs[2 * k + skip], refs[2 * k + skip + 1]
        outgoing, _ = _descriptors(mode, refs[:k], refs[k:2 * k], send_sems, recv_sems, with_incoming=False)
        for cp in outgoing:
            cp.start()
        refs[-1][...] = jnp.zeros_like(refs[-1])

    operands = [pltpu.with_memory_space_constraint(a, pltpu.HBM) for a in arrs]
    in_specs = [_HBM] * (2 * k)
    if after is not None:
        operands.append(after)
        in_specs.append(_ANY)
    sems = pltpu.SemaphoreType.DMA((_n_sems(mode, k),))
    res = pl.pallas_call(
        body, name=name, in_specs=in_specs,
        out_shape=(sems, sems, *[pltpu.HBM(a.shape, a.dtype) for a in arrs], _sds((8, LANES), F32)),
        out_specs=(_SEM, _SEM, *[_HBM] * (2 * k), pl.BlockSpec(memory_space=pltpu.VMEM)),
        input_output_aliases={i: 2 + i for i in range(2 * k)},
        compiler_params=pltpu.CompilerParams(has_side_effects=_EFFECT))(*operands)
    return dict(mode=mode, send=res[0], recv=res[1], srcs=res[2:2 + k], lands=res[2 + k:2 + 2 * k], token=res[-1])


def _exchange_wait(flight, name, after):
    mode, k = flight["mode"], len(flight["srcs"])
    arrs = [*flight["srcs"], *flight["lands"]]

    def body(*refs):
        outgoing, incoming = _descriptors(mode, refs[:k], refs[k:2 * k], refs[2 * k], refs[2 * k + 1])
        for cp in outgoing:
            cp.wait_send()
        for cp in incoming:
            cp.wait_recv()

    res = pl.pallas_call(
        body, name=name, in_specs=[_HBM] * (2 * k) + [_SEM, _SEM, _ANY],
        out_shape=tuple(pltpu.HBM(a.shape, a.dtype) for a in arrs), out_specs=tuple([_HBM] * (2 * k)),
        input_output_aliases={i: i for i in range(2 * k)},
        compiler_params=pltpu.CompilerParams(has_side_effects=_EFFECT))(*arrs, flight["send"], flight["recv"], after)
    return res[:k], res[k:]


def _sibling_exchange(arrs, name):
    k = len(arrs)

    def body(*refs):
        ins, outs = refs[:k], refs[k:2 * k]
        send_sems, recv_sems = refs[2 * k:]
        x, y, c = _mesh_pos()
        cps = []
        for i in range(k):
            cp = pltpu.make_async_remote_copy(
                src_ref=ins[i], dst_ref=outs[i], send_sem=send_sems.at[i], recv_sem=recv_sems.at[i],
                device_id=(x, y, 1 - c), device_id_type=pl.DeviceIdType.MESH)
            cp.start()
            cps.append(cp)
        for cp in cps:
            cp.wait()

    return pl.pallas_call(
        body, name=name, in_specs=[_ANY] * k, out_specs=[_ANY] * k,
        out_shape=[_sds(a.shape, a.dtype) for a in arrs],
        scratch_shapes=[pltpu.SemaphoreType.DMA((k,)), pltpu.SemaphoreType.DMA((k,))])(*arrs)


def _adamw(w, g, m, v):
    m = ADAM_B1 * m + (1.0 - ADAM_B1) * g
    v = ADAM_B2 * v + (1.0 - ADAM_B2) * (g * g)
    m_hat = m / (1.0 - ADAM_B1 ** ADAM_STEP)
    v_hat = v / (1.0 - ADAM_B2 ** ADAM_STEP)
    delta = -ADAM_LR * (m_hat / (jnp.sqrt(v_hat) + ADAM_EPS) + ADAM_WD * w)
    return delta, m, v


def _row_tile(rows, cols, n_arrays):
    budget = VMEM_LIMIT_BYTES // 4
    padded = -(-cols // LANES) * LANES
    tr = max(8, budget // (2 * n_arrays * padded * 4))
    tr = min(rows, 1 << (tr.bit_length() - 1))
    while rows % tr:
        tr //= 2
    return tr


def _sum_chips(landing, name):
    _, r, c = landing.shape
    tr = _row_tile(r, c, 4)

    def body(l_ref, s_ref):
        acc = l_ref[0].astype(F32)
        for s in range(1, N_CHIPS):
            acc = acc + l_ref[s].astype(F32)
        s_ref[...] = acc

    return pl.pallas_call(
        body, name=name, grid=(r // tr,),
        in_specs=[pl.BlockSpec((N_CHIPS, tr, c), lambda i: (0, i, 0))],
        out_specs=_rows(tr, c), out_shape=_sds((r, c), F32),
        compiler_params=_params())(landing)


def _adamw_big(mine, sibling, w, m, v, prev, layer, name):
    _, r, c = w.shape
    tr = _row_tile(r, c, 9)
    stacked = pl.BlockSpec((None, tr, c), lambda i: (layer, i, 0))

    def body(a_ref, b_ref, w_ref, m_ref, v_ref, *rest):
        g_out, d_out, m_out, v_out = rest[-4:]
        g = a_ref[...] + b_ref[...]
        g_out[...] = g
        d_out[...], m_out[...], v_out[...] = _adamw(w_ref[...], g, m_ref[...], v_ref[...])

    prev = list(prev) if prev is not None else []
    return pl.pallas_call(
        body, name=name, grid=(r // tr,),
        in_specs=[_rows(tr, c)] * 2 + [stacked] * 3 + [_ANY] * len(prev),
        out_specs=[stacked] * 4, out_shape=[_sds(w.shape, F32)] * 4,
        input_output_aliases={5 + j: j for j in range(len(prev))},
        compiler_params=_params())(mine, sibling, w, m, v, *prev)


def _adamw_small(gathered, w, m, v):
    r = w.shape[0]

    def body(a_ref, w_ref, m_ref, v_ref, g_out, d_out, m_out, v_out):
        g = a_ref[0]
        for d in range(1, N_DEV):
            g = g + a_ref[d]
        g_out[...] = g
        d_out[...], m_out[...], v_out[...] = _adamw(w_ref[...], g, m_ref[...], v_ref[...])

    return pl.pallas_call(
        body, name="adamw_small", grid=(1,),
        in_specs=[_whole((N_DEV, r, LANES))] + [_whole((r, LANES))] * 3,
        out_specs=[_whole((r, LANES))] * 4, out_shape=[_sds((r, LANES), F32)] * 4,
        compiler_params=_params())(gathered, w, m, v)


def _shard_view(name, stacked):
    return jnp.swapaxes(stacked, 1, 2) if name in TRANSPOSED else stacked


def _row_split(name):
    return name in TRANSPOSED or BIG[name] == 1


def _assemble(name, landed):
    _, r, c = landed.shape
    if _row_split(name):
        return landed.reshape(N_CHIPS * r, c)
    return landed.transpose(1, 0, 2).reshape(r, N_CHIPS * c)


def _split(name, whole):
    r, c = whole.shape
    if _row_split(name):
        return whole.reshape(N_CHIPS, r // N_CHIPS, c)
    return whole.reshape(r, N_CHIPS, c // N_CHIPS).transpose(1, 0, 2)


def _with_own_slot(landed, own, slot):
    return lax.dynamic_update_index_in_dim(landed, own, slot, 0)


def _pack_small(params, names):
    pieces = []
    for name in names:
        flat = params[name].reshape(-1)
        pieces.append(jnp.pad(flat, (0, -flat.shape[0] % SMALL_PAD)))
    return jnp.concatenate(pieces).reshape(-1, LANES)


def _unpack_small(packed, like, names):
    flat = packed.reshape(-1)
    out, off = {}, 0
    for name in names:
        size = like[name].size
        out[name] = flat[off:off + size].reshape(like[name].shape)
        off += size + (-size % SMALL_PAD)
    return out


def kernel(x, p, ln_mix_pre, w_in, attn_sinks, gm_ln_g, gm_ln_b, gm_ws, gm_bs, g_attn_out, g_gm_out, w_out, ln_mix_post, ln_ffn_pre, w_ffn_gate, w_ffn_up, w_ffn_down, ln_ffn_post, w_ple, ln_ple_gate, w_ple_gate, loss_target, m_ln_mix_pre, m_w_in, m_attn_sinks, m_gm_ln_g, m_gm_ln_b, m_gm_ws, m_gm_bs, m_g_attn_out, m_g_gm_out, m_w_out, m_ln_mix_post, m_ln_ffn_pre, m_w_ffn_gate, m_w_ffn_up, m_w_ffn_down, m_ln_ffn_post, m_w_ple, m_ln_ple_gate, m_w_ple_gate, v_ln_mix_pre, v_w_in, v_attn_sinks, v_gm_ln_g, v_gm_ln_b, v_gm_ws, v_gm_bs, v_g_attn_out, v_g_gm_out, v_w_out, v_ln_mix_post, v_ln_ffn_pre, v_w_ffn_gate, v_w_ffn_up, v_w_ffn_down, v_ln_ffn_post, v_w_ple, v_ln_ple_gate, v_w_ple_gate):
    given = dict(locals())
    wts = {n: given[n] for n in WEIGHTS}
    mom = {n: given["m_" + n] for n in WEIGHTS}
    var = {n: given["v_" + n] for n in WEIGHTS}
    depth = w_in.shape[0]
    h = x[0]
    target = loss_target[0]
    chip = 2 * lax.axis_index("x") + lax.axis_index("y")
    device = 2 * chip + lax.axis_index("c")
    row = lambda a, i: a[i][None, :]
    bs_full = [jnp.repeat(gm_bs[i].T, HEAD_DIM, axis=1) for i in range(depth)]
    bias = _alibi_bias()
    kinds = ("grad", "delta", "m", "v")
    wview = {n: _shard_view(n, wts[n]) for n in BIG_NAMES}
    mview = {n: _shard_view(n, mom[n]) for n in BIG_NAMES}
    vview = {n: _shard_view(n, var[n]) for n in BIG_NAMES}

    zero = lambda flight: flight["token"][0:1, 0:1]

    def start_gather(i, names, after, tag):
        shards = [wview[n][i].astype(BF16) for n in names]
        lands = [lax.empty((N_CHIPS,) + s.shape, BF16) for s in shards]
        return _exchange_start(shards, lands, "gather", f"gather_weights_start_{i}{tag}", after)

    def finish_gather(flight, names, i, after, tag):
        shards, lands = _exchange_wait(flight, f"gather_weights_wait_{i}{tag}", after)
        return {n: _assemble(n, _with_own_slot(l, s, chip)) for n, s, l in zip(names, shards, lands)}

    first, after = [], None
    for k, names in enumerate(GATHER_GROUPS):
        first.append(start_gather(0, names, after, "abc"[k]))
        after = first[-1]["token"]
    full = [finish_gather(first[0], GATHER_GROUPS[0], 0, after, "a")] + [None] * (depth - 1)
    saved = []
    for i in range(depth):
        w = full[i]
        g_in = row(ln_mix_pre, i)
        if i + 1 < depth:
            flight = start_gather(i + 1, BIG_NAMES, w['w_in'], "")
            g_in = g_in + zero(flight)
        z, a = _f1_norm_in(h, g_in, w['w_in'])
        am = _f2_attn_gm(z, attn_sinks[i], bias, row(gm_ln_g, i), row(gm_ln_b, i), gm_ws[i], bs_full[i])
        if i == 0:
            w.update(finish_gather(first[1], GATHER_GROUPS[1], 0, am, "b"))
        heads, mix, h1 = _f3_mix_out(am, h, row(g_attn_out, i), row(g_gm_out, i), w['w_out'], row(ln_mix_post, i))
        f, gt, up = _f4a_ffn_in(h1, row(ln_ffn_pre, i), w['w_ffn_gate'], w['w_ffn_up'])
        if i == 0:
            w.update(finish_gather(first[2], GATHER_GROUPS[2], 0, gt, "c"))
        dn, h2 = _f4b_ffn_out(gt, up, w['w_ffn_down'], h1, row(ln_ffn_post, i))
        r, pg, pe, h3 = _f5_ple(h2, row(ln_ple_gate, i), w['w_ple_gate'], p[i, 0], w['w_ple'])
        saved.append(dict(h=h, z=z, a=a, am=am, heads=heads, mix=mix, h1=h1, f=f, gt=gt, up=up, dn=dn, h2=h2,
                          r=r, pg=pg, pe=pe))
        h = h3
        if i + 1 < depth:
            full[i + 1] = finish_gather(flight, BIG_NAMES, i + 1, h3, "")

    sq, dh = _loss_head(h, target)
    loss = lax.psum(0.5 / D_MODEL * sq[0, 0], ("x", "y", "c"))

    chain = {n: None for n in BIG_NAMES}
    small_out = [{k: {} for k in kinds} for _ in range(depth)]

    def start_scatter(i, names, dws, after, tag):
        parts = [_split(n, dws[n]) for n in names]
        lands = [lax.empty(q.shape, BF16) for q in parts]
        return _exchange_start(parts, lands, "scatter", f"scatter_grads_start_{i}{tag}", after)

    def finish_scatter(i, names, flight, after, tag):
        parts, lands = _exchange_wait(flight, f"scatter_grads_wait_{i}{tag}", after)
        lands = [_with_own_slot(l, lax.dynamic_index_in_dim(q, chip, 0, keepdims=False), chip)
                 for q, l in zip(parts, lands)]
        partial = [_sum_chips(l, "sum_chips_" + n) for n, l in zip(names, lands)]
        sibling = _sibling_exchange(partial, name="sibling_grads")
        for n, mine, sib in zip(names, partial, sibling):
            chain[n] = _adamw_big(mine, sib, wview[n], mview[n], vview[n], chain[n], i, "adamw_" + n)
        return lands[0]

    def start_small(i, names, small, after, tag):
        packed = _pack_small(small, names)
        land = lax.empty((N_DEV,) + packed.shape, F32)
        return _exchange_start([packed], [land], "devices", f"gather_small_grads_start_{i}{tag}", after)

    def finish_small(i, names, flight, after, tag):
        (packed,), (gathered,) = _exchange_wait(flight, f"gather_small_grads_wait_{i}{tag}", after)
        gathered = _with_own_slot(gathered, packed, device)
        layer = lambda d: _pack_small({n: d[n][i] for n in names}, names)
        res = _adamw_small(gathered, layer(wts), layer(mom), layer(var))
        for k, a in zip(kinds, res):
            small_out[i][k].update(_unpack_small(a, {n: wts[n][i] for n in names}, names))
        return gathered

    pending, done, behind = [], None, None
    for i in reversed(range(depth)):
        s, w = saved[i], full[i]
        last = i == 0
        dws, small = {}, {}
        gain = row(ln_ple_gate, i)
        if behind is not None:
            gain = gain + behind
        dpe, dpg, dh2, dg = _b5_ple(dh, s['h2'], s['pg'], s['pe'], gain, w['w_ple_gate'])
        small['ln_ple_gate'] = dg[0]
        dws['w_ple'] = _weight_grad(p[i, 0], dpe, "dw_ple")
        dws['w_ple_gate'] = _weight_grad(s['r'], dpg, "dw_ple_gate")

        ddn, act, dgt, dup, dg = _b4a_ffn_out(dh2, s['dn'], row(ln_ffn_post, i), w['w_ffn_down'], s['gt'], s['up'])
        small['ln_ffn_post'] = dg[0]
        dws['w_ffn_down'] = _weight_grad(act, ddn, "dw_ffn_down")
        dws['w_ffn_gate'] = _weight_grad(dgt, s['f'], "dw_ffn_gate")
        dws['w_ffn_up'] = _weight_grad(dup, s['f'], "dw_ffn_up")
        gain = row(ln_ffn_pre, i)
        if last:
            flight_a = start_scatter(i, SCATTER_GROUPS[0], dws, None, "a")
            gain = gain + zero(flight_a)
        dh1, dg = _b4b_ffn_in(dgt, dup, w['w_ffn_gate'], w['w_ffn_up'], s['h1'], gain, dh2)
        small['ln_ffn_pre'] = dg[0]

        dmix, dam, dgp, dga, dgg = _b3_mix_out(dh1, s['mix'], row(ln_mix_post, i), w['w_out'], s['am'],
                                               row(g_attn_out, i), row(g_gm_out, i))
        small['ln_mix_post'] = dgp[0]
        small['g_attn_out'] = dga[0]
        small['g_gm_out'] = dgg[0]
        dws['w_out'] = _weight_grad(s['heads'], dmix, "dw_out")
        gain = row(gm_ln_g, i)
        if last:
            flight_b = start_scatter(i, SCATTER_GROUPS[1], dws, flight_a["token"], "b")
            gain = gain + zero(flight_b)

        dzq, dkv, dzuv, dsink, dlng, dlnb, dgws, dbs = _b2_attn_gm(
            dam, s['z'], attn_sinks[i], bias, gain, row(gm_ln_b, i), gm_ws[i], bs_full[i])
        small['attn_sinks'] = dsink[0, :N_Q_HEADS]
        small['gm_ln_g'] = dlng[0]
        small['gm_ln_b'] = dlnb[0]
        small['gm_ws'] = dgws
        small['gm_bs'] = dbs[:, :N_Q_HEADS].T
        gain = row(ln_mix_pre, i)
        if last:
            flight_s = start_small(i, SMALL_EARLY, small, flight_b["token"], "a")
            gain = gain + zero(flight_s)

        dz, dh, dg = _b1_norm_in(dzq, dkv, dzuv, w['w_in'], s['h'], gain, dh1)
        small['ln_mix_pre'] = dg[0]
        for finish in pending:
            done = finish(dh)
        pending = []
        if last:
            done = finish_scatter(i, SCATTER_GROUPS[0], flight_a, dz, "a")
            done = finish_scatter(i, SCATTER_GROUPS[1], flight_b, done, "b")
        dws['w_in'] = _weight_grad(dz, s['a'], "dw_in")
        if last:
            flight_c = start_scatter(i, SCATTER_GROUPS[2], dws, done, "c")
            flight_t = start_small(i, SMALL_LATE, small, flight_c["token"], "b")
            finish_small(i, SMALL_EARLY, flight_s, flight_t["token"], "a")
            finish_scatter(i, SCATTER_GROUPS[2], flight_c, flight_t["token"], "c")
            finish_small(i, SMALL_LATE, flight_t, flight_t["token"], "b")
        else:
            flight_c = start_scatter(i, BIG_NAMES, dws, done, "")
            flight_s = start_small(i, SMALL_NAMES, small, flight_c["token"], "")
            behind = zero(flight_s)
            pending = [functools.partial(finish_scatter, i, BIG_NAMES, flight_c, tag=""),
                       functools.partial(finish_small, i, SMALL_NAMES, flight_s, tag="")]
    grad_x = dh[None]

    out = {k: {n: _shard_view(n, chain[n][j]) for n in BIG_NAMES} for j, k in enumerate(kinds)}
    for k in kinds:
        out[k].update({n: jnp.stack([small_out[i][k][n] for i in range(depth)]) for n in SMALL_NAMES})

    return (loss, grad_x, *[out["grad"][n] for n in WEIGHTS], *[out["delta"][n] for n in WEIGHTS],
            *[out["m"][n] for n in WEIGHTS], *[out["v"][n] for n in WEIGHTS])
```

```python
import functools
import math

import jax
import jax.numpy as jnp
from jax import lax
from jax.experimental import pallas as pl
from jax.experimental.pallas import tpu as pltpu

F32 = jnp.float32
BF16 = jnp.bfloat16

D_MODEL = 1024
HEAD_DIM = 64
N_Q_HEADS = 8
BLK = 128
ATTN_W = 512
KV_W = 128
GM_W = 512
D_IN = ATTN_W + 2 * KV_W + 2 * GM_W
D_FF = 2816
PLE_DIM = 256
DEPTH = 4
NORM_EPS = 1e-6
NEG_BIG = -1e30
N_CHIPS = 4
N_DEV = 8

ADAM_LR = 0.001
ADAM_B1 = 0.9
ADAM_B2 = 0.999
ADAM_EPS = 1e-08
ADAM_WD = 0.01
ADAM_STEP = 10

VMEM_LIMIT_BYTES = 56 * 1024 * 1024
LANES = 128
STRIP = 16
GELU_C0 = math.sqrt(2.0 / math.pi)
GELU_C1 = 0.044715
ALIBI_SLOPES = tuple(2.0 ** (-8.0 * (h + 1.0) / N_Q_HEADS) for h in range(N_Q_HEADS))

WEIGHTS = ['ln_mix_pre', 'w_in', 'attn_sinks', 'gm_ln_g', 'gm_ln_b', 'gm_ws', 'gm_bs', 'g_attn_out',
           'g_gm_out', 'w_out', 'ln_mix_post', 'ln_ffn_pre', 'w_ffn_gate', 'w_ffn_up', 'w_ffn_down',
           'ln_ffn_post', 'w_ple', 'ln_ple_gate', 'w_ple_gate']
BIG = {'w_in': 2, 'w_out': 1, 'w_ffn_gate': 2, 'w_ffn_up': 2, 'w_ffn_down': 1, 'w_ple': 2, 'w_ple_gate': 1}
BIG_NAMES = list(BIG)
TRANSPOSED = ('w_in', 'w_ffn_gate', 'w_ffn_up')
SMALL_NAMES = [n for n in WEIGHTS if n not in BIG]
SMALL_PAD = 1024
GATHER_GROUPS = (('w_in',), ('w_out', 'w_ffn_gate', 'w_ffn_up'), ('w_ffn_down', 'w_ple_gate', 'w_ple'))
SCATTER_GROUPS = (('w_ple', 'w_ple_gate', 'w_ffn_down', 'w_ffn_gate', 'w_ffn_up'), ('w_out',), ('w_in',))
SMALL_LATE = ('ln_mix_pre',)
SMALL_EARLY = tuple(n for n in SMALL_NAMES if n not in SMALL_LATE)


def _nt(a, b):
    return lax.dot_general(a, b, (((1,), (1,)), ((), ())), preferred_element_type=F32)


def _tn(a, b):
    return lax.dot_general(a, b, (((0,), (0,)), ((), ())), preferred_element_type=F32)


def _mm(a, b):
    return jnp.dot(a, b, preferred_element_type=F32)


def _rms(x, g):
    r = lax.rsqrt(jnp.mean(x * x, axis=-1, keepdims=True) + NORM_EPS)
    return x * r * g


def _rms_bwd(dy, x, g):
    r = lax.rsqrt(jnp.mean(x * x, axis=-1, keepdims=True) + NORM_EPS)
    xh = x * r
    dg = jnp.sum(dy * xh, axis=0, keepdims=True)
    dxh = dy * g
    dx = r * (dxh - xh * jnp.mean(dxh * xh, axis=-1, keepdims=True))
    return dx, dg


def _gelu(x):
    return 0.5 * x * (1.0 + jnp.tanh(GELU_C0 * (x + GELU_C1 * x * x * x)))


def _sigmoid(x):
    return 1.0 / (1.0 + jnp.exp(-x))


def _rows(tm, n):
    return pl.BlockSpec((tm, n), lambda i: (i, 0))


def _whole(shape):
    return pl.BlockSpec(shape, lambda i: (0,) * len(shape))


def _accumulate(ref, val):
    @pl.when(pl.program_id(0) == 0)
    def _():
        ref[...] = jnp.zeros_like(ref)

    ref[...] += val


def _params(n_axes=1):
    return pltpu.CompilerParams(dimension_semantics=("arbitrary",) * n_axes,
                                vmem_limit_bytes=VMEM_LIMIT_BYTES)


def _sds(shape, dtype):
    return jax.ShapeDtypeStruct(shape, dtype)


def _tile(t, want):
    return min(t, want)


def _f1_norm_in(h, g, w_t):
    t = h.shape[0]
    tm = _tile(t, 512)

    def body(h_ref, g_ref, w_ref, z_ref, a_ref):
        a = _rms(h_ref[...], g_ref[...]).astype(BF16)
        a_ref[...] = a
        z_ref[...] = _nt(a, w_ref[...])

    return pl.pallas_call(
        body, name="f1_norm_in", grid=(t // tm,),
        in_specs=[_rows(tm, D_MODEL), _whole((1, D_MODEL)), _whole((D_IN, D_MODEL))],
        out_specs=[_rows(tm, D_IN), _rows(tm, D_MODEL)],
        out_shape=[_sds((t, D_IN), F32), _sds((t, D_MODEL), BF16)],
        compiler_params=_params())(h, g, w_t)


def _alibi_bias():
    ti = jnp.arange(BLK)[:, None]
    ji = jnp.arange(2 * BLK)[None, :]
    dist = ti + BLK - ji
    band = (dist >= 0) & (dist < BLK)
    bias = -jnp.asarray(ALIBI_SLOPES, F32)[:, None, None] * dist.astype(F32)[None]
    return jnp.stack([jnp.where((band & (ji >= BLK))[None], bias, NEG_BIG), jnp.where(band[None], bias, NEG_BIG)])


def _strips():
    return [slice(r * STRIP, (r + 1) * STRIP) for r in range(BLK // STRIP)]


def _bias_spec(transposed=False):
    tile = (2 * BLK, BLK) if transposed else (BLK, 2 * BLK)
    return pl.BlockSpec((None, N_Q_HEADS) + tile, lambda i: (jnp.minimum(i, 1), 0, 0, 0))


def _softmax_strip(s_ref, bias_ref, hq, rows, sink):
    sc = s_ref[hq, rows, :] + bias_ref[hq, rows, :]
    m = jnp.maximum(jnp.max(sc, axis=1, keepdims=True), sink)
    e = jnp.exp(sc - m)
    es = jnp.exp(sink - m)
    inv = 1.0 / (jnp.sum(e, axis=1, keepdims=True) + es)
    return e * inv, es * inv


def _kv_window(z_ref, kp_ref, vp_ref):
    kcat = jnp.concatenate([kp_ref[...], z_ref[:, ATTN_W:ATTN_W + KV_W]], axis=0)
    vcat = jnp.concatenate([vp_ref[...], z_ref[:, ATTN_W + KV_W:ATTN_W + 2 * KV_W]], axis=0)
    kswap = pltpu.roll(kcat, HEAD_DIM, 1)
    vswap = pltpu.roll(vcat, HEAD_DIM, 1)
    return kcat.astype(BF16), kswap.astype(BF16), vcat, vswap


def _gelu_and_grad(x):
    t = jnp.tanh(GELU_C0 * (x + GELU_C1 * x * x * x))
    return 0.5 * x * (1.0 + t), 0.5 * (1.0 + t) + 0.5 * x * (1.0 - t * t) * GELU_C0 * (1.0 + 3.0 * GELU_C1 * x * x)


def _layernorm_strip(gv, lng_ref, lnb_ref):
    xc = gv - jnp.mean(gv, axis=-1, keepdims=True)
    rstd = lax.rsqrt(jnp.mean(xc * xc, axis=-1, keepdims=True) + NORM_EPS)
    xhat = xc * rstd
    return xhat * lng_ref[...] + lnb_ref[...], xhat, rstd


def _tril_w(ws_ref, h):
    ti = lax.broadcasted_iota(jnp.int32, (BLK, BLK), 0)
    si = lax.broadcasted_iota(jnp.int32, (BLK, BLK), 1)
    causal = si <= ti
    return jnp.where(causal, ws_ref[h], 0.0).astype(BF16), causal


def _gm_mixed(vn, ws_ref, bs_ref, lo, hi):
    slabs = []
    for s in range(GM_W // LANES):
        vs = vn[:, s * LANES:(s + 1) * LANES]
        w0, _ = _tril_w(ws_ref, 2 * s)
        w1, _ = _tril_w(ws_ref, 2 * s + 1)
        mixed = (_mm(w0, jnp.where(lo, vs, 0.0).astype(BF16))
                 + _mm(w1, jnp.where(hi, vs, 0.0).astype(BF16))
                 + bs_ref[:, s * LANES:(s + 1) * LANES])
        slabs.append(mixed)
    return slabs


def _block_specs_z(nb):
    prev = lambda i: (jnp.maximum(i - 1, 0), ATTN_W // KV_W)
    prev_v = lambda i: (jnp.maximum(i - 1, 0), ATTN_W // KV_W + 1)
    return [_rows(BLK, D_IN), pl.BlockSpec((BLK, KV_W), prev), pl.BlockSpec((BLK, KV_W), prev_v)]


def _heads():
    return [(2 * s + half, s, half, s // 2 == half) for s in range(ATTN_W // LANES) for half in range(2)]


def _f2_attn_gm(z, sinks, bias, ln_g, ln_b, ws, bs_full):
    t = z.shape[0]
    nb = t // BLK

    def body(z_ref, kp_ref, vp_ref, sink_ref, bias_ref, lng_ref, lnb_ref, ws_ref, bs_ref, am_ref,
             s_ref, p_ref, u_ref, vn_ref):
        lane = lax.broadcasted_iota(jnp.int32, (1, LANES), 1)
        lo = lane < HEAD_DIM
        hi = lane >= HEAD_DIM
        kc, ks, vcat, vswap = _kv_window(z_ref, kp_ref, vp_ref)
        for hq, s, half, same in _heads():
            qs = z_ref[:, s * LANES:(s + 1) * LANES] * (HEAD_DIM ** -0.5)
            qm = jnp.where(lo if half == 0 else hi, qs, 0.0).astype(BF16)
            s_ref[hq] = _nt(qm, kc if same else ks)
        for hq in range(N_Q_HEADS):
            for rows in _strips():
                pr, _ = _softmax_strip(s_ref, bias_ref, hq, rows, sink_ref[hq])
                p_ref[hq, rows, :] = pr.astype(BF16)
        for s in range(ATTN_W // LANES):
            o = jnp.zeros((BLK, LANES), F32)
            for hq, hs, half, same in _heads():
                if hs == s:
                    vm = jnp.where(lo if half == 0 else hi, vcat if same else vswap, 0.0).astype(BF16)
                    o = o + _mm(p_ref[hq], vm)
            am_ref[:, s * LANES:(s + 1) * LANES] = o.astype(BF16)

        for rows in _strips():
            u_ref[rows, :] = _gelu(z_ref[rows, ATTN_W + 2 * KV_W:ATTN_W + 2 * KV_W + GM_W])
            vn_ref[rows, :], _, _ = _layernorm_strip(_gelu(z_ref[rows, ATTN_W + 2 * KV_W + GM_W:D_IN]),
                                                     lng_ref, lnb_ref)
        mixed = _gm_mixed(vn_ref, ws_ref, bs_ref, lo, hi)
        for s in range(GM_W // LANES):
            am_ref[:, ATTN_W + s * LANES:ATTN_W + (s + 1) * LANES] = (
                u_ref[:, s * LANES:(s + 1) * LANES] * mixed[s]).astype(BF16)

    return pl.pallas_call(
        body, name="f2_attn_gm", grid=(nb,),
        in_specs=_block_specs_z(nb) + [
            pl.BlockSpec(memory_space=pltpu.SMEM), _bias_spec(), _whole((1, GM_W)),
            _whole((1, GM_W)), _whole((N_Q_HEADS, BLK, BLK)), _whole((BLK, GM_W))],
        out_specs=_rows(BLK, ATTN_W + GM_W),
        out_shape=_sds((t, ATTN_W + GM_W), BF16),
        scratch_shapes=[pltpu.VMEM((N_Q_HEADS, BLK, 2 * BLK), F32), pltpu.VMEM((N_Q_HEADS, BLK, 2 * BLK), BF16),
                        pltpu.VMEM((BLK, GM_W), F32), pltpu.VMEM((BLK, GM_W), F32)],
        compiler_params=_params())(z, z, z, sinks, bias, ln_g, ln_b, ws, bs_full)


def _f3_mix_out(am, h, ga, gg, w, gpost):
    t = h.shape[0]
    tm = _tile(t, 512)

    def body(am_ref, h_ref, ga_ref, gg_ref, w_ref, gp_ref, heads_ref, mix_ref, h1_ref):
        heads = jnp.concatenate([_rms(am_ref[:, :ATTN_W].astype(F32), ga_ref[...]),
                                 _rms(am_ref[:, ATTN_W:].astype(F32), gg_ref[...])], axis=1).astype(BF16)
        heads_ref[...] = heads
        mix = _mm(heads, w_ref[...])
        mix_ref[...] = mix.astype(BF16)
        h1_ref[...] = h_ref[...] + _rms(mix, gp_ref[...])

    return pl.pallas_call(
        body, name="f3_mix_out", grid=(t // tm,),
        in_specs=[_rows(tm, D_MODEL), _rows(tm, D_MODEL), _whole((1, ATTN_W)), _whole((1, GM_W)),
                  _whole((D_MODEL, D_MODEL)), _whole((1, D_MODEL))],
        out_specs=[_rows(tm, D_MODEL)] * 3,
        out_shape=[_sds((t, D_MODEL), BF16), _sds((t, D_MODEL), BF16), _sds((t, D_MODEL), F32)],
        compiler_params=_params())(am, h, ga, gg, w, gpost)


def _f4a_ffn_in(h1, gf, wg_t, wu_t):
    t = h1.shape[0]
    tm = _tile(t, 256)

    def body(h_ref, g_ref, wg_ref, wu_ref, f_ref, gt_ref, up_ref):
        f = _rms(h_ref[...], g_ref[...]).astype(BF16)
        f_ref[...] = f
        gt_ref[...] = _nt(f, wg_ref[...]).astype(BF16)
        up_ref[...] = _nt(f, wu_ref[...]).astype(BF16)

    return pl.pallas_call(
        body, name="f4a_ffn_in", grid=(t // tm,),
        in_specs=[_rows(tm, D_MODEL), _whole((1, D_MODEL)), _whole((D_FF, D_MODEL)), _whole((D_FF, D_MODEL))],
        out_specs=[_rows(tm, D_MODEL), _rows(tm, D_FF), _rows(tm, D_FF)],
        out_shape=[_sds((t, D_MODEL), BF16), _sds((t, D_FF), BF16), _sds((t, D_FF), BF16)],
        compiler_params=_params())(h1, gf, wg_t, wu_t)


def _f4b_ffn_out(gt, up, wd, h1, gfp):
    t = h1.shape[0]
    tm = _tile(t, 256)

    def body(gt_ref, up_ref, wd_ref, h_ref, g_ref, dn_ref, h2_ref):
        gt = gt_ref[...].astype(F32)
        act = (gt * _sigmoid(gt) * up_ref[...].astype(F32)).astype(BF16)
        dn = _mm(act, wd_ref[...])
        dn_ref[...] = dn.astype(BF16)
        h2_ref[...] = h_ref[...] + _rms(dn, g_ref[...])

    return pl.pallas_call(
        body, name="f4b_ffn_out", grid=(t // tm,),
        in_specs=[_rows(tm, D_FF), _rows(tm, D_FF), _whole((D_FF, D_MODEL)), _rows(tm, D_MODEL),
                  _whole((1, D_MODEL))],
        out_specs=[_rows(tm, D_MODEL)] * 2,
        out_shape=[_sds((t, D_MODEL), BF16), _sds((t, D_MODEL), F32)],
        compiler_params=_params())(gt, up, wd, h1, gfp)


def _f5_ple(h2, gpl, wpg, p, wple):
    t = h2.shape[0]
    tm = _tile(t, 512)

    def body(h_ref, g_ref, wpg_ref, p_ref, wple_ref, r_ref, pg_ref, pe_ref, h3_ref):
        h = h_ref[...]
        r = _rms(h, g_ref[...]).astype(BF16)
        r_ref[...] = r
        pg = _mm(r, wpg_ref[...])
        pe = _mm(p_ref[...].astype(BF16), wple_ref[...])
        pg_ref[...] = pg.astype(BF16)
        pe_ref[...] = pe.astype(BF16)
        h3_ref[...] = h + pe * _sigmoid(pg)

    return pl.pallas_call(
        body, name="f5_ple", grid=(t // tm,),
        in_specs=[_rows(tm, D_MODEL), _whole((1, D_MODEL)), _whole((D_MODEL, D_MODEL)), _rows(tm, PLE_DIM),
                  _whole((PLE_DIM, D_MODEL))],
        out_specs=[_rows(tm, D_MODEL)] * 4,
        out_shape=[_sds((t, D_MODEL), BF16)] * 3 + [_sds((t, D_MODEL), F32)],
        compiler_params=_params())(h2, gpl, wpg, p, wple)


def _loss_head(y, target):
    t = y.shape[0]
    tm = _tile(t, 512)

    def body(y_ref, t_ref, sq_ref, dy_ref):
        err = y_ref[...] - t_ref[...]
        dy_ref[...] = err * (1.0 / D_MODEL)
        _accumulate(sq_ref, jnp.sum(err * err, keepdims=True))

    return pl.pallas_call(
        body, name="loss_head", grid=(t // tm,),
        in_specs=[_rows(tm, D_MODEL)] * 2,
        out_specs=[_whole((1, LANES)), _rows(tm, D_MODEL)],
        out_shape=[_sds((1, LANES), F32), _sds((t, D_MODEL), F32)],
        compiler_params=_params())(y, target)


def _b5_ple(dh3, h2, pg, pe, gpl, wpg):
    t = h2.shape[0]
    tm = _tile(t, 512)

    def body(dh_ref, h_ref, pg_ref, pe_ref, g_ref, w_ref, dpe_ref, dpg_ref, dh2_ref, dg_ref):
        dh = dh_ref[...]
        s = _sigmoid(pg_ref[...].astype(F32))
        dpe_ref[...] = (dh * s).astype(BF16)
        dpg = (dh * pe_ref[...].astype(F32) * s * (1.0 - s)).astype(BF16)
        dpg_ref[...] = dpg
        dx, dg = _rms_bwd(_nt(dpg, w_ref[...]), h_ref[...], g_ref[...])
        dh2_ref[...] = dh + dx
        _accumulate(dg_ref, dg)

    return pl.pallas_call(
        body, name="b5_ple", grid=(t // tm,),
        in_specs=[_rows(tm, D_MODEL)] * 4 + [_whole((1, D_MODEL)), _whole((D_MODEL, D_MODEL))],
        out_specs=[_rows(tm, D_MODEL)] * 3 + [_whole((1, D_MODEL))],
        out_shape=[_sds((t, D_MODEL), BF16)] * 2 + [_sds((t, D_MODEL), F32), _sds((1, D_MODEL), F32)],
        compiler_params=_params())(dh3, h2, pg, pe, gpl, wpg)


def _b4a_ffn_out(dh2, dn, gfp, wd, gt, up):
    t = dh2.shape[0]
    tm = _tile(t, 256)

    def body(dh_ref, dn_ref, g_ref, wd_ref, gt_ref, up_ref, ddn_ref, act_ref, dgt_ref, dup_ref, dg_ref):
        ddn, dg = _rms_bwd(dh_ref[...], dn_ref[...].astype(F32), g_ref[...])
        _accumulate(dg_ref, dg)
        ddn = ddn.astype(BF16)
        ddn_ref[...] = ddn
        dact = _nt(ddn, wd_ref[...])
        gt = gt_ref[...].astype(F32)
        up = up_ref[...].astype(F32)
        sg = _sigmoid(gt)
        silu = gt * sg
        act_ref[...] = (silu * up).astype(BF16)
        dup_ref[...] = (dact * silu).astype(BF16)
        dgt_ref[...] = (dact * up * (sg * (1.0 + gt * (1.0 - sg)))).astype(BF16)

    return pl.pallas_call(
        body, name="b4a_ffn_out", grid=(t // tm,),
        in_specs=[_rows(tm, D_MODEL), _rows(tm, D_MODEL), _whole((1, D_MODEL)), _whole((D_FF, D_MODEL)),
                  _rows(tm, D_FF), _rows(tm, D_FF)],
        out_specs=[_rows(tm, D_MODEL), _rows(tm, D_FF), _rows(tm, D_FF), _rows(tm, D_FF), _whole((1, D_MODEL))],
        out_shape=[_sds((t, D_MODEL), BF16)] + [_sds((t, D_FF), BF16)] * 3 + [_sds((1, D_MODEL), F32)],
        compiler_params=_params())(dh2, dn, gfp, wd, gt, up)


def _b4b_ffn_in(dgt, dup, wg_t, wu_t, h1, gf, dh2):
    t = h1.shape[0]
    tm = _tile(t, 512)

    def body(dgt_ref, dup_ref, wg_ref, wu_ref, h_ref, g_ref, dh_ref, dh1_ref, dg_ref):
        df = _mm(dgt_ref[...], wg_ref[...]) + _mm(dup_ref[...], wu_ref[...])
        dx, dg = _rms_bwd(df, h_ref[...], g_ref[...])
        dh1_ref[...] = dh_ref[...] + dx
        _accumulate(dg_ref, dg)

    return pl.pallas_call(
        body, name="b4b_ffn_in", grid=(t // tm,),
        in_specs=[_rows(tm, D_FF), _rows(tm, D_FF), _whole((D_FF, D_MODEL)), _whole((D_FF, D_MODEL)),
                  _rows(tm, D_MODEL), _whole((1, D_MODEL)), _rows(tm, D_MODEL)],
        out_specs=[_rows(tm, D_MODEL), _whole((1, D_MODEL))],
        out_shape=[_sds((t, D_MODEL), F32), _sds((1, D_MODEL), F32)],
        compiler_params=_params())(dgt, dup, wg_t, wu_t, h1, gf, dh2)


def _b3_mix_out(dh1, mix, gpost, w, am, ga, gg):
    t = dh1.shape[0]
    tm = _tile(t, 512)

    def body(dh_ref, mix_ref, gp_ref, w_ref, am_ref, ga_ref, gg_ref, dmix_ref, dam_ref, dgp_ref, dga_ref, dgg_ref):
        dmix, dgp = _rms_bwd(dh_ref[...], mix_ref[...].astype(F32), gp_ref[...])
        _accumulate(dgp_ref, dgp)
        dmix = dmix.astype(BF16)
        dmix_ref[...] = dmix
        dheads = _nt(dmix, w_ref[...])
        dat, dga = _rms_bwd(dheads[:, :ATTN_W], am_ref[:, :ATTN_W].astype(F32), ga_ref[...])
        dgm, dgg = _rms_bwd(dheads[:, ATTN_W:], am_ref[:, ATTN_W:].astype(F32), gg_ref[...])
        dam_ref[:, :ATTN_W] = dat.astype(BF16)
        dam_ref[:, ATTN_W:] = dgm.astype(BF16)
        _accumulate(dga_ref, dga)
        _accumulate(dgg_ref, dgg)

    return pl.pallas_call(
        body, name="b3_mix_out", grid=(t // tm,),
        in_specs=[_rows(tm, D_MODEL), _rows(tm, D_MODEL), _whole((1, D_MODEL)), _whole((D_MODEL, D_MODEL)),
                  _rows(tm, D_MODEL), _whole((1, ATTN_W)), _whole((1, GM_W))],
        out_specs=[_rows(tm, D_MODEL), _rows(tm, D_MODEL), _whole((1, D_MODEL)), _whole((1, ATTN_W)),
                   _whole((1, GM_W))],
        out_shape=[_sds((t, D_MODEL), BF16), _sds((t, D_MODEL), BF16), _sds((1, D_MODEL), F32),
                   _sds((1, ATTN_W), F32), _sds((1, GM_W), F32)],
        compiler_params=_params())(dh1, mix, gpost, w, am, ga, gg)


def _b2_attn_gm(dam, z, sinks, bias, ln_g, ln_b, ws, bs_full):
    t = z.shape[0]
    nb = t // BLK

    def body(dam_ref, z_ref, kp_ref, vp_ref, sink_ref, bias_ref, lng_ref, lnb_ref, ws_ref, wst_ref, bs_ref,
             dzq_ref, dkv_ref, dzuv_ref, dsink_ref, dlng_ref, dlnb_ref, dws_ref, dbs_ref,
             sc_ref, dp_ref, p_ref, ds_ref, u_ref, du_ref, dv_ref, vn_ref, xhat_ref, rstd_ref, dvn_ref):
        n = pl.program_id(0)
        lane = lax.broadcasted_iota(jnp.int32, (1, LANES), 1)
        lo = lane < HEAD_DIM
        hi = lane >= HEAD_DIM
        sub = lax.broadcasted_iota(jnp.int32, (LANES, 1), 0)
        kc, ks, vcat, vswap = _kv_window(z_ref, kp_ref, vp_ref)
        vc = vcat.astype(BF16)
        vs_ = vswap.astype(BF16)
        kc_t = kc.T
        ks_t = ks.T

        def operands(s, half):
            mask = lo if half == 0 else hi
            qs = z_ref[:, s * LANES:(s + 1) * LANES] * (HEAD_DIM ** -0.5)
            qm = jnp.where(mask, qs, 0.0).astype(BF16)
            dom = jnp.where(mask, dam_ref[:, s * LANES:(s + 1) * LANES], 0.0).astype(BF16)
            return qm, dom

        for hq, s, half, same in _heads():
            qm, dom = operands(s, half)
            sc_ref[hq] = _nt(kc if same else ks, qm)
            dp_ref[hq] = _nt(vc if same else vs_, dom)
        dsink = jnp.zeros((1, LANES), F32)
        for hq in range(N_Q_HEADS):
            sink = sink_ref[hq]
            sc = sc_ref[hq] + bias_ref[hq]
            m = jnp.maximum(jnp.max(sc, axis=0, keepdims=True), sink)
            e = jnp.exp(sc - m)
            es = jnp.exp(sink - m)
            inv = 1.0 / (jnp.sum(e, axis=0, keepdims=True) + es)
            pr = e * inv
            dpr = dp_ref[hq]
            row = jnp.sum(dpr * pr, axis=0, keepdims=True)
            p_ref[hq] = pr.astype(BF16)
            ds_ref[hq] = (pr * (dpr - row)).astype(BF16)
            dsink = dsink + jnp.where(lane == hq, -jnp.sum(es * inv * row, keepdims=True), 0.0)
        dk_acc = [jnp.zeros((2 * BLK, LANES), F32), jnp.zeros((2 * BLK, LANES), F32)]
        dv_acc = [jnp.zeros((2 * BLK, LANES), F32), jnp.zeros((2 * BLK, LANES), F32)]
        for s in range(ATTN_W // LANES):
            dq_t = jnp.zeros((LANES, BLK), F32)
            for hq, hs, half, same in _heads():
                if hs != s:
                    continue
                qm, dom = operands(s, half)
                ds = ds_ref[hq]
                dk_acc[same] = dk_acc[same] + _mm(ds, qm)
                dv_acc[same] = dv_acc[same] + _mm(p_ref[hq], dom)
                in_half = (sub < HEAD_DIM) if half == 0 else (sub >= HEAD_DIM)
                dq_t = dq_t + jnp.where(in_half, _mm(kc_t if same else ks_t, ds), 0.0)
            dzq_ref[:, s * LANES:(s + 1) * LANES] = (dq_t.T * (HEAD_DIM ** -0.5)).astype(BF16)
        dk_acc = dk_acc[True] + pltpu.roll(dk_acc[False], HEAD_DIM, 1)
        dv_acc = dv_acc[True] + pltpu.roll(dv_acc[False], HEAD_DIM, 1)
        cur = pl.multiple_of(n * BLK, BLK)
        dkv_ref[pl.ds(cur, BLK), 0:KV_W] = dk_acc[BLK:, :]
        dkv_ref[pl.ds(cur, BLK), KV_W:2 * KV_W] = dv_acc[BLK:, :]

        @pl.when(n > 0)
        def _():
            prv = pl.multiple_of((n - 1) * BLK, BLK)
            dkv_ref[pl.ds(prv, BLK), 0:KV_W] += dk_acc[:BLK, :]
            dkv_ref[pl.ds(prv, BLK), KV_W:2 * KV_W] += dv_acc[:BLK, :]

        _accumulate(dsink_ref, dsink)

        for rows in _strips():
            u_ref[rows, :], du_ref[rows, :] = _gelu_and_grad(
                z_ref[rows, ATTN_W + 2 * KV_W:ATTN_W + 2 * KV_W + GM_W])
            gv, dv_ref[rows, :] = _gelu_and_grad(z_ref[rows, ATTN_W + 2 * KV_W + GM_W:D_IN])
            vn_ref[rows, :], xhat_ref[rows, :], rstd = _layernorm_strip(gv, lng_ref, lnb_ref)
            rstd_ref[rows, :] = jnp.broadcast_to(rstd, (STRIP, LANES))
        mixed = _gm_mixed(vn_ref, ws_ref, bs_ref, lo, hi)

        @pl.when(n == 0)
        def _():
            dws_ref[...] = jnp.zeros_like(dws_ref)

        dbs = jnp.zeros((BLK, LANES), F32)
        for s in range(GM_W // LANES):
            slab = slice(s * LANES, (s + 1) * LANES)
            dgm = dam_ref[:, ATTN_W + s * LANES:ATTN_W + (s + 1) * LANES]
            dzuv_ref[:, slab] = (dgm * mixed[s] * du_ref[:, slab]).astype(BF16)
            dmx = dgm * u_ref[:, slab]
            vsb = vn_ref[:, slab].astype(BF16)
            dvn = jnp.zeros((BLK, LANES), F32)
            for half in range(2):
                h = 2 * s + half
                mask = lo if half == 0 else hi
                dmm = jnp.where(mask, dmx, 0.0)
                dbs = dbs + jnp.where(lane == h, jnp.sum(dmm, axis=1, keepdims=True), 0.0)
                dmm = dmm.astype(BF16)
                _, causal = _tril_w(ws_ref, h)
                ti = lax.broadcasted_iota(jnp.int32, (BLK, BLK), 0)
                si = lax.broadcasted_iota(jnp.int32, (BLK, BLK), 1)
                wt_t = jnp.where(ti <= si, wst_ref[h], 0.0).astype(BF16)
                dvn = dvn + jnp.where(mask, _mm(wt_t, dmm), 0.0)
                dws_ref[h] += jnp.where(causal, _nt(dmm, vsb), 0.0)
            dvn_ref[:, slab] = dvn
        _accumulate(dbs_ref, dbs)
        dlnb = jnp.zeros((STRIP, GM_W), F32)
        dlng = jnp.zeros((STRIP, GM_W), F32)
        for rows in _strips():
            dvn = dvn_ref[rows, :]
            xhat = xhat_ref[rows, :]
            dlnb = dlnb + dvn
            dlng = dlng + dvn * xhat
            dxh = dvn * lng_ref[...]
            dgv = rstd_ref[rows, 0:1] * (dxh - jnp.mean(dxh, axis=-1, keepdims=True)
                                         - xhat * jnp.mean(dxh * xhat, axis=-1, keepdims=True))
            dzuv_ref[rows, GM_W:] = (dgv * dv_ref[rows, :]).astype(BF16)
        _accumulate(dlnb_ref, jnp.sum(dlnb, axis=0, keepdims=True))
        _accumulate(dlng_ref, jnp.sum(dlng, axis=0, keepdims=True))

    return pl.pallas_call(
        body, name="b2_attn_gm", grid=(nb,),
        in_specs=[_rows(BLK, ATTN_W + GM_W)] + _block_specs_z(nb) + [
            pl.BlockSpec(memory_space=pltpu.SMEM), _bias_spec(transposed=True), _whole((1, GM_W)),
            _whole((1, GM_W)), _whole((N_Q_HEADS, BLK, BLK)), _whole((N_Q_HEADS, BLK, BLK)), _whole((BLK, GM_W))],
        out_specs=[_rows(BLK, ATTN_W), _whole((t, 2 * KV_W)), _rows(BLK, 2 * GM_W), _whole((1, LANES)),
                   _whole((1, GM_W)), _whole((1, GM_W)), _whole((N_Q_HEADS, BLK, BLK)), _whole((BLK, LANES))],
        out_shape=[_sds((t, ATTN_W), BF16), _sds((t, 2 * KV_W), F32), _sds((t, 2 * GM_W), BF16),
                   _sds((1, LANES), F32), _sds((1, GM_W), F32), _sds((1, GM_W), F32),
                   _sds((N_Q_HEADS, BLK, BLK), F32), _sds((BLK, LANES), F32)],
        scratch_shapes=[pltpu.VMEM((N_Q_HEADS, 2 * BLK, BLK), F32)] * 2
        + [pltpu.VMEM((N_Q_HEADS, 2 * BLK, BLK), BF16)] * 2 + [pltpu.VMEM((BLK, GM_W), F32)] * 5
        + [pltpu.VMEM((BLK, LANES), F32), pltpu.VMEM((BLK, GM_W), F32)],
        compiler_params=_params())(dam, z, z, z, sinks, jnp.swapaxes(bias, 2, 3), ln_g, ln_b, ws,
                                   jnp.swapaxes(ws, 1, 2), bs_full)


def _b1_norm_in(dzq, dkv, dzuv, w_t, h, g, dh1):
    t = h.shape[0]
    tm = _tile(t, 512)

    def body(dzq_ref, dkv_ref, dzuv_ref, w_ref, h_ref, g_ref, dh_ref, dz_ref, dh0_ref, dg_ref):
        dz = jnp.concatenate([dzq_ref[...], dkv_ref[...].astype(BF16), dzuv_ref[...]], axis=1)
        dz_ref[...] = dz
        dx, dg = _rms_bwd(_mm(dz, w_ref[...]), h_ref[...], g_ref[...])
        dh0_ref[...] = dh_ref[...] + dx
        _accumulate(dg_ref, dg)

    return pl.pallas_call(
        body, name="b1_norm_in", grid=(t // tm,),
        in_specs=[_rows(tm, ATTN_W), _rows(tm, 2 * KV_W), _rows(tm, 2 * GM_W), _whole((D_IN, D_MODEL)),
                  _rows(tm, D_MODEL), _whole((1, D_MODEL)), _rows(tm, D_MODEL)],
        out_specs=[_rows(tm, D_IN), _rows(tm, D_MODEL), _whole((1, D_MODEL))],
        out_shape=[_sds((t, D_IN), BF16), _sds((t, D_MODEL), F32), _sds((1, D_MODEL), F32)],
        compiler_params=_params())(dzq, dkv, dzuv, w_t, h, g, dh1)


def _weight_grad(x, dy, name):
    t, k = x.shape
    n = dy.shape[1]
    tm = _tile(t, 1024)
    steps = t // tm

    def body(x_ref, dy_ref, dw_ref, acc_ref):
        i = pl.program_id(0)

        @pl.when(i == 0)
        def _():
            acc_ref[...] = _tn(x_ref[...].astype(BF16), dy_ref[...])

        @pl.when(i > 0)
        def _():
            acc_ref[...] += _tn(x_ref[...].astype(BF16), dy_ref[...])

        @pl.when(i == steps - 1)
        def _():
            dw_ref[...] = acc_ref[...].astype(BF16)

    return pl.pallas_call(
        body, name=name, grid=(steps,),
        in_specs=[_rows(tm, k), _rows(tm, n)],
        out_specs=_whole((k, n)),
        out_shape=_sds((k, n), BF16),
        scratch_shapes=[pltpu.VMEM((k, n), F32)],
        compiler_params=_params())(x, dy)


_ANY = pl.BlockSpec(memory_space=pl.ANY)
_HBM = pl.BlockSpec(memory_space=pltpu.HBM)
_SEM = pl.BlockSpec(memory_space=pltpu.SEMAPHORE)
_EFFECT = pltpu.SideEffectType.DATAFLOW_SIDE_EFFECTING


def _mesh_pos():
    return lax.axis_index("x"), lax.axis_index("y"), lax.axis_index("c")


def _other_chips(x, y):
    return [(1 - x, y), (x, 1 - y), (1 - x, 1 - y)]


def _peers(mode, x, y, c):
    if mode == "devices":
        peers = []
        for j in range(1, N_DEV):
            px = 1 - x if (j >> 2) & 1 else x
            py = 1 - y if (j >> 1) & 1 else y
            pc = 1 - c if j & 1 else c
            peers.append(((px, py, pc), 4 * px + 2 * py + pc))
        return peers, 4 * x + 2 * y + c
    return [((px, py, c), 2 * px + py) for px, py in _other_chips(x, y)], 2 * x + y


def _descriptors(mode, srcs, lands, send_sems, recv_sems, with_incoming=True):
    x, y, c = _mesh_pos()
    peers, me = _peers(mode, x, y, c)
    scatter = mode == "scatter"
    outgoing, incoming = [], []
    for i, (src, land) in enumerate(zip(srcs, lands)):
        for j, (dev, slot) in enumerate(peers):
            sem = i * len(peers) + j
            common = dict(send_sem=send_sems.at[sem], recv_sem=recv_sems.at[sem], device_id=dev,
                          device_id_type=pl.DeviceIdType.MESH)
            outgoing.append(pltpu.make_async_remote_copy(
                src_ref=src.at[slot] if scatter else src, dst_ref=land.at[me], **common))
            if with_incoming:
                incoming.append(pltpu.make_async_remote_copy(
                    src_ref=src.at[me] if scatter else src, dst_ref=land.at[slot], **common))
    return outgoing, incoming


def _n_sems(mode, k):
    return k * ((N_DEV if mode == "devices" else N_CHIPS) - 1)


def _exchange_start(srcs, lands, mode, name, after=None):
    k = len(srcs)
    arrs = [*srcs, *lands]

    def body(*refs):
        skip = 1 if after is not None else 0
        send_sems, recv_sems = refs[2 * k + skip], refs[2 * k + skip + 1]
        outgoing, _ = _descriptors(mode, refs[:k], refs[k:2 * k], send_sems, recv_sems, with_incoming=False)
        for cp in outgoing:
            cp.start()
        refs[-1][...] = jnp.zeros_like(refs[-1])

    operands = [pltpu.with_memory_space_constraint(a, pltpu.HBM) for a in arrs]
    in_specs = [_HBM] * (2 * k)
    if after is not None:
        operands.append(after)
        in_specs.append(_ANY)
    sems = pltpu.SemaphoreType.DMA((_n_sems(mode, k),))
    res = pl.pallas_call(
        body, name=name, in_specs=in_specs,
        out_shape=(sems, sems, *[pltpu.HBM(a.shape, a.dtype) for a in arrs], _sds((8, LANES), F32)),
        out_specs=(_SEM, _SEM, *[_HBM] * (2 * k), pl.BlockSpec(memory_space=pltpu.VMEM)),
        input_output_aliases={i: 2 + i for i in range(2 * k)},
        compiler_params=pltpu.CompilerParams(has_side_effects=_EFFECT))(*operands)
    return dict(mode=mode, send=res[0], recv=res[1], srcs=res[2:2 + k], lands=res[2 + k:2 + 2 * k], token=res[-1])


def _exchange_wait(flight, name, after):
    mode, k = flight["mode"], len(flight["srcs"])
    arrs = [*flight["srcs"], *flight["lands"]]

    def body(*refs):
        outgoing, incoming = _descriptors(mode, refs[:k], refs[k:2 * k], refs[2 * k], refs[2 * k + 1])
        for cp in outgoing:
            cp.wait_send()
        for cp in incoming:
            cp.wait_recv()

    res = pl.pallas_call(
        body, name=name, in_specs=[_HBM] * (2 * k) + [_SEM, _SEM, _ANY],
        out_shape=tuple(pltpu.HBM(a.shape, a.dtype) for a in arrs), out_specs=tuple([_HBM] * (2 * k)),
        input_output_aliases={i: i for i in range(2 * k)},
        compiler_params=pltpu.CompilerParams(has_side_effects=_EFFECT))(*arrs, flight["send"], flight["recv"], after)
    return res[:k], res[k:]


def _sibling_exchange(arrs, name):
    k = len(arrs)

    def body(*refs):
        ins, outs = refs[:k], refs[k:2 * k]
        send_sems, recv_sems = refs[2 * k:]
        x, y, c = _mesh_pos()
        cps = []
        for i in range(k):
            cp = pltpu.make_async_remote_copy(
                src_ref=ins[i], dst_ref=outs[i], send_sem=send_sems.at[i], recv_sem=recv_sems.at[i],
                device_id=(x, y, 1 - c), device_id_type=pl.DeviceIdType.MESH)
            cp.start()
            cps.append(cp)
        for cp in cps:
            cp.wait()

    return pl.pallas_call(
        body, name=name, in_specs=[_ANY] * k, out_specs=[_ANY] * k,
        out_shape=[_sds(a.shape, a.dtype) for a in arrs],
        scratch_shapes=[pltpu.SemaphoreType.DMA((k,)), pltpu.SemaphoreType.DMA((k,))])(*arrs)


def _adamw(w, g, m, v):
    m = ADAM_B1 * m + (1.0 - ADAM_B1) * g
    v = ADAM_B2 * v + (1.0 - ADAM_B2) * (g * g)
    m_hat = m / (1.0 - ADAM_B1 ** ADAM_STEP)
    v_hat = v / (1.0 - ADAM_B2 ** ADAM_STEP)
    delta = -ADAM_LR * (m_hat / (jnp.sqrt(v_hat) + ADAM_EPS) + ADAM_WD * w)
    return delta, m, v


def _row_tile(rows, cols, n_arrays):
    budget = VMEM_LIMIT_BYTES // 4
    padded = -(-cols // LANES) * LANES
    tr = max(8, budget // (2 * n_arrays * padded * 4))
    tr = min(rows, 1 << (tr.bit_length() - 1))
    while rows % tr:
        tr //= 2
    return tr


def _sum_chips(landing, name):
    _, r, c = landing.shape
    tr = _row_tile(r, c, 4)

    def body(l_ref, s_ref):
        acc = l_ref[0].astype(F32)
        for s in range(1, N_CHIPS):
            acc = acc + l_ref[s].astype(F32)
        s_ref[...] = acc

    return pl.pallas_call(
        body, name=name, grid=(r // tr,),
        in_specs=[pl.BlockSpec((N_CHIPS, tr, c), lambda i: (0, i, 0))],
        out_specs=_rows(tr, c), out_shape=_sds((r, c), F32),
        compiler_params=_params())(landing)


def _adamw_big(mine, sibling, w, m, v, prev, layer, name):
    _, r, c = w.shape
    tr = _row_tile(r, c, 9)
    stacked = pl.BlockSpec((None, tr, c), lambda i: (layer, i, 0))

    def body(a_ref, b_ref, w_ref, m_ref, v_ref, *rest):
        g_out, d_out, m_out, v_out = rest[-4:]
        g = a_ref[...] + b_ref[...]
        g_out[...] = g
        d_out[...], m_out[...], v_out[...] = _adamw(w_ref[...], g, m_ref[...], v_ref[...])

    prev = list(prev) if prev is not None else []
    return pl.pallas_call(
        body, name=name, grid=(r // tr,),
        in_specs=[_rows(tr, c)] * 2 + [stacked] * 3 + [_ANY] * len(prev),
        out_specs=[stacked] * 4, out_shape=[_sds(w.shape, F32)] * 4,
        input_output_aliases={5 + j: j for j in range(len(prev))},
        compiler_params=_params())(mine, sibling, w, m, v, *prev)


def _adamw_small(gathered, w, m, v):
    r = w.shape[0]

    def body(a_ref, w_ref, m_ref, v_ref, g_out, d_out, m_out, v_out):
        g = a_ref[0]
        for d in range(1, N_DEV):
            g = g + a_ref[d]
        g_out[...] = g
        d_out[...], m_out[...], v_out[...] = _adamw(w_ref[...], g, m_ref[...], v_ref[...])

    return pl.pallas_call(
        body, name="adamw_small", grid=(1,),
        in_specs=[_whole((N_DEV, r, LANES))] + [_whole((r, LANES))] * 3,
        out_specs=[_whole((r, LANES))] * 4, out_shape=[_sds((r, LANES), F32)] * 4,
        compiler_params=_params())(gathered, w, m, v)


def _shard_view(name, stacked):
    return jnp.swapaxes(stacked, 1, 2) if name in TRANSPOSED else stacked


def _row_split(name):
    return name in TRANSPOSED or BIG[name] == 1


def _assemble(name, landed):
    _, r, c = landed.shape
    if _row_split(name):
        return landed.reshape(N_CHIPS * r, c)
    return landed.transpose(1, 0, 2).reshape(r, N_CHIPS * c)


def _split(name, whole):
    r, c = whole.shape
    if _row_split(name):
        return whole.reshape(N_CHIPS, r // N_CHIPS, c)
    return whole.reshape(r, N_CHIPS, c // N_CHIPS).transpose(1, 0, 2)


def _with_own_slot(landed, own, slot):
    return lax.dynamic_update_index_in_dim(landed, own, slot, 0)


def _pack_small(params, names):
    pieces = []
    for name in names:
        flat = params[name].reshape(-1)
        pieces.append(jnp.pad(flat, (0, -flat.shape[0] % SMALL_PAD)))
    return jnp.concatenate(pieces).reshape(-1, LANES)


def _unpack_small(packed, like, names):
    flat = packed.reshape(-1)
    out, off = {}, 0
    for name in names:
        size = like[name].size
        out[name] = flat[off:off + size].reshape(like[name].shape)
        off += size + (-size % SMALL_PAD)
    return out


def kernel(x, p, ln_mix_pre, w_in, attn_sinks, gm_ln_g, gm_ln_b, gm_ws, gm_bs, g_attn_out, g_gm_out, w_out, ln_mix_post, ln_ffn_pre, w_ffn_gate, w_ffn_up, w_ffn_down, ln_ffn_post, w_ple, ln_ple_gate, w_ple_gate, loss_target, m_ln_mix_pre, m_w_in, m_attn_sinks, m_gm_ln_g, m_gm_ln_b, m_gm_ws, m_gm_bs, m_g_attn_out, m_g_gm_out, m_w_out, m_ln_mix_post, m_ln_ffn_pre, m_w_ffn_gate, m_w_ffn_up, m_w_ffn_down, m_ln_ffn_post, m_w_ple, m_ln_ple_gate, m_w_ple_gate, v_ln_mix_pre, v_w_in, v_attn_sinks, v_gm_ln_g, v_gm_ln_b, v_gm_ws, v_gm_bs, v_g_attn_out, v_g_gm_out, v_w_out, v_ln_mix_post, v_ln_ffn_pre, v_w_ffn_gate, v_w_ffn_up, v_w_ffn_down, v_ln_ffn_post, v_w_ple, v_ln_ple_gate, v_w_ple_gate):
    given = dict(locals())
    wts = {n: given[n] for n in WEIGHTS}
    mom = {n: given["m_" + n] for n in WEIGHTS}
    var = {n: given["v_" + n] for n in WEIGHTS}
    depth = w_in.shape[0]
    h = x[0]
    target = loss_target[0]
    chip = 2 * lax.axis_index("x") + lax.axis_index("y")
    device = 2 * chip + lax.axis_index("c")
    row = lambda a, i: a[i][None, :]
    bs_full = [jnp.repeat(gm_bs[i].T, HEAD_DIM, axis=1) for i in range(depth)]
    bias = _alibi_bias()
    kinds = ("grad", "delta", "m", "v")
    wview = {n: _shard_view(n, wts[n]) for n in BIG_NAMES}
    mview = {n: _shard_view(n, mom[n]) for n in BIG_NAMES}
    vview = {n: _shard_view(n, var[n]) for n in BIG_NAMES}

    zero = lambda flight: flight["token"][0:1, 0:1]

    def start_gather(i, names, after, tag):
        shards = [wview[n][i].astype(BF16) for n in names]
        lands = [lax.empty((N_CHIPS,) + s.shape, BF16) for s in shards]
        return _exchange_start(shards, lands, "gather", f"gather_weights_start_{i}{tag}", after)

    def finish_gather(flight, names, i, after, tag):
        shards, lands = _exchange_wait(flight, f"gather_weights_wait_{i}{tag}", after)
        return {n: _assemble(n, _with_own_slot(l, s, chip)) for n, s, l in zip(names, shards, lands)}

    first, after = [], None
    for k, names in enumerate(GATHER_GROUPS):
        first.append(start_gather(0, names, after, "abc"[k]))
        after = first[-1]["token"]
    full = [finish_gather(first[0], GATHER_GROUPS[0], 0, after, "a")] + [None] * (depth - 1)
    saved = []
    for i in range(depth):
        w = full[i]
        g_in = row(ln_mix_pre, i)
        if i + 1 < depth:
            flight = start_gather(i + 1, BIG_NAMES, w['w_in'], "")
            g_in = g_in + zero(flight)
        z, a = _f1_norm_in(h, g_in, w['w_in'])
        am = _f2_attn_gm(z, attn_sinks[i], bias, row(gm_ln_g, i), row(gm_ln_b, i), gm_ws[i], bs_full[i])
        if i == 0:
            w.update(finish_gather(first[1], GATHER_GROUPS[1], 0, am, "b"))
        heads, mix, h1 = _f3_mix_out(am, h, row(g_attn_out, i), row(g_gm_out, i), w['w_out'], row(ln_mix_post, i))
        f, gt, up = _f4a_ffn_in(h1, row(ln_ffn_pre, i), w['w_ffn_gate'], w['w_ffn_up'])
        if i == 0:
            w.update(finish_gather(first[2], GATHER_GROUPS[2], 0, gt, "c"))
        dn, h2 = _f4b_ffn_out(gt, up, w['w_ffn_down'], h1, row(ln_ffn_post, i))
        r, pg, pe, h3 = _f5_ple(h2, row(ln_ple_gate, i), w['w_ple_gate'], p[i, 0], w['w_ple'])
        saved.append(dict(h=h, z=z, a=a, am=am, heads=heads, mix=mix, h1=h1, f=f, gt=gt, up=up, dn=dn, h2=h2,
                          r=r, pg=pg, pe=pe))
        h = h3
        if i + 1 < depth:
            full[i + 1] = finish_gather(flight, BIG_NAMES, i + 1, h3, "")

    sq, dh = _loss_head(h, target)
    loss = lax.psum(0.5 / D_MODEL * sq[0, 0], ("x", "y", "c"))

    chain = {n: None for n in BIG_NAMES}
    small_out = [{k: {} for k in kinds} for _ in range(depth)]

    def start_scatter(i, names, dws, after, tag):
        parts = [_split(n, dws[n]) for n in names]
        lands = [lax.empty(q.shape, BF16) for q in parts]
        return _exchange_start(parts, lands, "scatter", f"scatter_grads_start_{i}{tag}", after)

    def finish_scatter(i, names, flight, after, tag):
        parts, lands = _exchange_wait(flight, f"scatter_grads_wait_{i}{tag}", after)
        lands = [_with_own_slot(l, lax.dynamic_index_in_dim(q, chip, 0, keepdims=False), chip)
                 for q, l in zip(parts, lands)]
        partial = [_sum_chips(l, "sum_chips_" + n) for n, l in zip(names, lands)]
        sibling = _sibling_exchange(partial, name="sibling_grads")
        for n, mine, sib in zip(names, partial, sibling):
            chain[n] = _adamw_big(mine, sib, wview[n], mview[n], vview[n], chain[n], i, "adamw_" + n)
        return lands[0]

    def start_small(i, names, small, after, tag):
        packed = _pack_small(small, names)
        land = lax.empty((N_DEV,) + packed.shape, F32)
        return _exchange_start([packed], [land], "devices", f"gather_small_grads_start_{i}{tag}", after)

    def finish_small(i, names, flight, after, tag):
        (packed,), (gathered,) = _exchange_wait(flight, f"gather_small_grads_wait_{i}{tag}", after)
        gathered = _with_own_slot(gathered, packed, device)
        layer = lambda d: _pack_small({n: d[n][i] for n in names}, names)
        res = _adamw_small(gathered, layer(wts), layer(mom), layer(var))
        for k, a in zip(kinds, res):
            small_out[i][k].update(_unpack_small(a, {n: wts[n][i] for n in names}, names))
        return gathered

    pending, done, behind = [], None, None
    for i in reversed(range(depth)):
        s, w = saved[i], full[i]
        last = i == 0
        dws, small = {}, {}
        gain = row(ln_ple_gate, i)
        if behind is not None:
            gain = gain + behind
        dpe, dpg, dh2, dg = _b5_ple(dh, s['h2'], s['pg'], s['pe'], gain, w['w_ple_gate'])
        small['ln_ple_gate'] = dg[0]
        dws['w_ple'] = _weight_grad(p[i, 0], dpe, "dw_ple")
        dws['w_ple_gate'] = _weight_grad(s['r'], dpg, "dw_ple_gate")

        ddn, act, dgt, dup, dg = _b4a_ffn_out(dh2, s['dn'], row(ln_ffn_post, i), w['w_ffn_down'], s['gt'], s['up'])
        small['ln_ffn_post'] = dg[0]
        dws['w_ffn_down'] = _weight_grad(act, ddn, "dw_ffn_down")
        dws['w_ffn_gate'] = _weight_grad(dgt, s['f'], "dw_ffn_gate")
        dws['w_ffn_up'] = _weight_grad(dup, s['f'], "dw_ffn_up")
        gain = row(ln_ffn_pre, i)
        if last:
            flight_a = start_scatter(i, SCATTER_GROUPS[0], dws, None, "a")
            gain = gain + zero(flight_a)
        dh1, dg = _b4b_ffn_in(dgt, dup, w['w_ffn_gate'], w['w_ffn_up'], s['h1'], gain, dh2)
        small['ln_ffn_pre'] = dg[0]

        dmix, dam, dgp, dga, dgg = _b3_mix_out(dh1, s['mix'], row(ln_mix_post, i), w['w_out'], s['am'],
                                               row(g_attn_out, i), row(g_gm_out, i))
        small['ln_mix_post'] = dgp[0]
        small['g_attn_out'] = dga[0]
        small['g_gm_out'] = dgg[0]
        dws['w_out'] = _weight_grad(s['heads'], dmix, "dw_out")
        gain = row(gm_ln_g, i)
        if last:
            flight_b = start_scatter(i, SCATTER_GROUPS[1], dws, flight_a["token"], "b")
            gain = gain + zero(flight_b)

        dzq, dkv, dzuv, dsink, dlng, dlnb, dgws, dbs = _b2_attn_gm(
            dam, s['z'], attn_sinks[i], bias, gain, row(gm_ln_b, i), gm_ws[i], bs_full[i])
        small['attn_sinks'] = dsink[0, :N_Q_HEADS]
        small['gm_ln_g'] = dlng[0]
        small['gm_ln_b'] = dlnb[0]
        small['gm_ws'] = dgws
        small['gm_bs'] = dbs[:, :N_Q_HEADS].T
        gain = row(ln_mix_pre, i)
        if last:
            flight_s = start_small(i, SMALL_EARLY, small, flight_b["token"], "a")
            gain = gain + zero(flight_s)

        dz, dh, dg = _b1_norm_in(dzq, dkv, dzuv, w['w_in'], s['h'], gain, dh1)
        small['ln_mix_pre'] = dg[0]
        for finish in pending:
            done = finish(dh)
        pending = []
        if last:
            done = finish_scatter(i, SCATTER_GROUPS[0], flight_a, dz, "a")
            done = finish_scatter(i, SCATTER_GROUPS[1], flight_b, done, "b")
        dws['w_in'] = _weight_grad(dz, s['a'], "dw_in")
        if last:
            flight_c = start_scatter(i, SCATTER_GROUPS[2], dws, done, "c")
            flight_t = start_small(i, SMALL_LATE, small, flight_c["token"], "b")
            finish_small(i, SMALL_EARLY, flight_s, flight_t["token"], "a")
            finish_scatter(i, SCATTER_GROUPS[2], flight_c, flight_t["token"], "c")
            finish_small(i, SMALL_LATE, flight_t, flight_t["token"], "b")
        else:
            flight_c = start_scatter(i, BIG_NAMES, dws, done, "")
            flight_s = start_small(i, SMALL_NAMES, small, flight_c["token"], "")
            behind = zero(flight_s)
            pending = [functools.partial(finish_scatter, i, BIG_NAMES, flight_c, tag=""),
                       functools.partial(finish_small, i, SMALL_NAMES, flight_s, tag="")]
    grad_x = dh[None]

    out = {k: {n: _shard_view(n, chain[n][j]) for n in BIG_NAMES} for j, k in enumerate(kinds)}
    for k in kinds:
        out[k].update({n: jnp.stack([small_out[i][k][n] for i in range(depth)]) for n in SMALL_NAMES})

    return (loss, grad_x, *[out["grad"][n] for n in WEIGHTS], *[out["delta"][n] for n in WEIGHTS],
            *[out["m"][n] for n in WEIGHTS], *[out["v"][n] for n in WEIGHTS])
```

```python
import functools
import math

import jax
import jax.numpy as jnp
from jax import lax
from jax.experimental import pallas as pl
from jax.experimental.pallas import tpu as pltpu

F32 = jnp.float32
BF16 = jnp.bfloat16

D_MODEL = 1024
HEAD_DIM = 64
N_Q_HEADS = 8
BLK = 128
ATTN_W = 512
KV_W = 128
GM_W = 512
D_IN = ATTN_W + 2 * KV_W + 2 * GM_W
D_FF = 2816
PLE_DIM = 256
DEPTH = 4
NORM_EPS = 1e-6
NEG_BIG = -1e30
N_CHIPS = 4
N_DEV = 8

ADAM_LR = 0.001
ADAM_B1 = 0.9
ADAM_B2 = 0.999
ADAM_EPS = 1e-08
ADAM_WD = 0.01
ADAM_STEP = 10

VMEM_LIMIT_BYTES = 56 * 1024 * 1024
LANES = 128
STRIP = 16
GELU_C0 = math.sqrt(2.0 / math.pi)
GELU_C1 = 0.044715
ALIBI_SLOPES = tuple(2.0 ** (-8.0 * (h + 1.0) / N_Q_HEADS) for h in range(N_Q_HEADS))

WEIGHTS = ['ln_mix_pre', 'w_in', 'attn_sinks', 'gm_ln_g', 'gm_ln_b', 'gm_ws', 'gm_bs', 'g_attn_out',
           'g_gm_out', 'w_out', 'ln_mix_post', 'ln_ffn_pre', 'w_ffn_gate', 'w_ffn_up', 'w_ffn_down',
           'ln_ffn_post', 'w_ple', 'ln_ple_gate', 'w_ple_gate']
BIG = {'w_in': 2, 'w_out': 1, 'w_ffn_gate': 2, 'w_ffn_up': 2, 'w_ffn_down': 1, 'w_ple': 2, 'w_ple_gate': 1}
BIG_NAMES = list(BIG)
TRANSPOSED = ('w_in', 'w_ffn_gate', 'w_ffn_up')
SMALL_NAMES = [n for n in WEIGHTS if n not in BIG]
SMALL_PAD = 1024
GATHER_GROUPS = (('w_in',), ('w_out', 'w_ffn_gate', 'w_ffn_up'), ('w_ffn_down', 'w_ple_gate', 'w_ple'))
SCATTER_GROUPS = (('w_ple', 'w_ple_gate', 'w_ffn_down', 'w_ffn_gate', 'w_ffn_up'), ('w_out',), ('w_in',))
SMALL_LATE = ('ln_mix_pre',)
SMALL_EARLY = tuple(n for n in SMALL_NAMES if n not in SMALL_LATE)


def _nt(a, b):
    return lax.dot_general(a, b, (((1,), (1,)), ((), ())), preferred_element_type=F32)


def _tn(a, b):
    return lax.dot_general(a, b, (((0,), (0,)), ((), ())), preferred_element_type=F32)


def _mm(a, b):
    return jnp.dot(a, b, preferred_element_type=F32)


def _rms(x, g):
    r = lax.rsqrt(jnp.mean(x * x, axis=-1, keepdims=True) + NORM_EPS)
    return x * r * g


def _rms_bwd(dy, x, g):
    r = lax.rsqrt(jnp.mean(x * x, axis=-1, keepdims=True) + NORM_EPS)
    xh = x * r
    dg = jnp.sum(dy * xh, axis=0, keepdims=True)
    dxh = dy * g
    dx = r * (dxh - xh * jnp.mean(dxh * xh, axis=-1, keepdims=True))
    return dx, dg


def _gelu(x):
    return 0.5 * x * (1.0 + jnp.tanh(GELU_C0 * (x + GELU_C1 * x * x * x)))


def _sigmoid(x):
    return 0.5 * jnp.tanh(0.5 * x) + 0.5


def _rows(tm, n):
    return pl.BlockSpec((tm, n), lambda i: (i, 0))


def _whole(shape):
    return pl.BlockSpec(shape, lambda i: (0,) * len(shape))


def _accumulate(ref, val):
    @pl.when(pl.program_id(0) == 0)
    def _():
        ref[...] = jnp.zeros_like(ref)

    ref[...] += val


def _params(n_axes=1):
    return pltpu.CompilerParams(dimension_semantics=("arbitrary",) * n_axes,
                                vmem_limit_bytes=VMEM_LIMIT_BYTES)


def _sds(shape, dtype):
    return jax.ShapeDtypeStruct(shape, dtype)


def _tile(t, want):
    return min(t, want)


def _f1_norm_in(h, g, w_t):
    t = h.shape[0]
    tm = _tile(t, 512)

    def body(h_ref, g_ref, w_ref, z_ref, a_ref):
        a = _rms(h_ref[...], g_ref[...]).astype(BF16)
        a_ref[...] = a
        z_ref[...] = _nt(a, w_ref[...]).astype(BF16)

    return pl.pallas_call(
        body, name="f1_norm_in", grid=(t // tm,),
        in_specs=[_rows(tm, D_MODEL), _whole((1, D_MODEL)), _whole((D_IN, D_MODEL))],
        out_specs=[_rows(tm, D_IN), _rows(tm, D_MODEL)],
        out_shape=[_sds((t, D_IN), BF16), _sds((t, D_MODEL), BF16)],
        compiler_params=_params())(h, g, w_t)


def _alibi_bias():
    ti = jnp.arange(BLK)[:, None]
    ji = jnp.arange(2 * BLK)[None, :]
    dist = ti + BLK - ji
    band = (dist >= 0) & (dist < BLK)
    bias = -jnp.asarray(ALIBI_SLOPES, F32)[:, None, None] * dist.astype(F32)[None]
    return jnp.stack([jnp.where((band & (ji >= BLK))[None], bias, NEG_BIG), jnp.where(band[None], bias, NEG_BIG)])


def _strips():
    return [slice(r * STRIP, (r + 1) * STRIP) for r in range(BLK // STRIP)]


def _bias_spec(transposed=False):
    tile = (2 * BLK, BLK) if transposed else (BLK, 2 * BLK)
    return pl.BlockSpec((None, N_Q_HEADS) + tile, lambda i: (jnp.minimum(i, 1), 0, 0, 0))


def _softmax_strip(s_ref, bias_ref, hq, rows, sink):
    sc = s_ref[hq, rows, :] + bias_ref[hq, rows, :]
    m = jnp.maximum(jnp.max(sc, axis=1, keepdims=True), sink)
    e = jnp.exp(sc - m)
    es = jnp.exp(sink - m)
    inv = 1.0 / (jnp.sum(e, axis=1, keepdims=True) + es)
    return e * inv, es * inv


def _kv_window(z_ref, kp_ref, vp_ref):
    kcat = jnp.concatenate([kp_ref[...], z_ref[:, ATTN_W:ATTN_W + KV_W]], axis=0).astype(F32)
    vcat = jnp.concatenate([vp_ref[...], z_ref[:, ATTN_W + KV_W:ATTN_W + 2 * KV_W]], axis=0).astype(F32)
    kswap = pltpu.roll(kcat, HEAD_DIM, 1)
    vswap = pltpu.roll(vcat, HEAD_DIM, 1)
    return kcat.astype(BF16), kswap.astype(BF16), vcat, vswap


def _gelu_and_grad(x):
    t = jnp.tanh(GELU_C0 * (x + GELU_C1 * x * x * x))
    return 0.5 * x * (1.0 + t), 0.5 * (1.0 + t) + 0.5 * x * (1.0 - t * t) * GELU_C0 * (1.0 + 3.0 * GELU_C1 * x * x)


def _layernorm_strip(gv, lng_ref, lnb_ref):
    xc = gv - jnp.mean(gv, axis=-1, keepdims=True)
    rstd = lax.rsqrt(jnp.mean(xc * xc, axis=-1, keepdims=True) + NORM_EPS)
    xhat = xc * rstd
    return xhat * lng_ref[...] + lnb_ref[...], xhat, rstd


def _tril_w(ws_ref, h):
    ti = lax.broadcasted_iota(jnp.int32, (BLK, BLK), 0)
    si = lax.broadcasted_iota(jnp.int32, (BLK, BLK), 1)
    causal = si <= ti
    return jnp.where(causal, ws_ref[h], 0.0).astype(BF16), causal


def _gm_mixed(vn, ws_ref, bs_ref, lo, hi):
    slabs = []
    for s in range(GM_W // LANES):
        vs = vn[:, s * LANES:(s + 1) * LANES]
        w0, _ = _tril_w(ws_ref, 2 * s)
        w1, _ = _tril_w(ws_ref, 2 * s + 1)
        mixed = (_mm(w0, jnp.where(lo, vs, 0.0).astype(BF16))
                 + _mm(w1, jnp.where(hi, vs, 0.0).astype(BF16))
                 + bs_ref[:, s * LANES:(s + 1) * LANES])
        slabs.append(mixed)
    return slabs


def _block_specs_z(nb):
    prev = lambda i: (jnp.maximum(i - 1, 0), ATTN_W // KV_W)
    prev_v = lambda i: (jnp.maximum(i - 1, 0), ATTN_W // KV_W + 1)
    return [_rows(BLK, D_IN), pl.BlockSpec((BLK, KV_W), prev), pl.BlockSpec((BLK, KV_W), prev_v)]


def _heads():
    return [(2 * s + half, s, half, s // 2 == half) for s in range(ATTN_W // LANES) for half in range(2)]


def _f2_attn_gm(z, sinks, bias, ln_g, ln_b, ws, bs_full):
    t = z.shape[0]
    nb = t // BLK

    def body(z_ref, kp_ref, vp_ref, sink_ref, bias_ref, lng_ref, lnb_ref, ws_ref, bs_ref, am_ref,
             s_ref, p_ref, u_ref, vn_ref):
        lane = lax.broadcasted_iota(jnp.int32, (1, LANES), 1)
        lo = lane < HEAD_DIM
        hi = lane >= HEAD_DIM
        kc, ks, vcat, vswap = _kv_window(z_ref, kp_ref, vp_ref)
        for hq, s, half, same in _heads():
            qs = z_ref[:, s * LANES:(s + 1) * LANES].astype(F32) * (HEAD_DIM ** -0.5)
            qm = jnp.where(lo if half == 0 else hi, qs, 0.0).astype(BF16)
            s_ref[hq] = _nt(qm, kc if same else ks)
        for hq in range(N_Q_HEADS):
            for rows in _strips():
                pr, _ = _softmax_strip(s_ref, bias_ref, hq, rows, sink_ref[hq])
                p_ref[hq, rows, :] = pr.astype(BF16)
        for s in range(ATTN_W // LANES):
            o = jnp.zeros((BLK, LANES), F32)
            for hq, hs, half, same in _heads():
                if hs == s:
                    vm = jnp.where(lo if half == 0 else hi, vcat if same else vswap, 0.0).astype(BF16)
                    o = o + _mm(p_ref[hq], vm)
            am_ref[:, s * LANES:(s + 1) * LANES] = o.astype(BF16)

        for rows in _strips():
            u_ref[rows, :] = _gelu(z_ref[rows, ATTN_W + 2 * KV_W:ATTN_W + 2 * KV_W + GM_W].astype(F32))
            vn_ref[rows, :], _, _ = _layernorm_strip(
                _gelu(z_ref[rows, ATTN_W + 2 * KV_W + GM_W:D_IN].astype(F32)), lng_ref, lnb_ref)
        mixed = _gm_mixed(vn_ref, ws_ref, bs_ref, lo, hi)
        for s in range(GM_W // LANES):
            am_ref[:, ATTN_W + s * LANES:ATTN_W + (s + 1) * LANES] = (
                u_ref[:, s * LANES:(s + 1) * LANES] * mixed[s]).astype(BF16)

    return pl.pallas_call(
        body, name="f2_attn_gm", grid=(nb,),
        in_specs=_block_specs_z(nb) + [
            pl.BlockSpec(memory_space=pltpu.SMEM), _bias_spec(), _whole((1, GM_W)),
            _whole((1, GM_W)), _whole((N_Q_HEADS, BLK, BLK)), _whole((BLK, GM_W))],
        out_specs=_rows(BLK, ATTN_W + GM_W),
        out_shape=_sds((t, ATTN_W + GM_W), BF16),
        scratch_shapes=[pltpu.VMEM((N_Q_HEADS, BLK, 2 * BLK), F32), pltpu.VMEM((N_Q_HEADS, BLK, 2 * BLK), BF16),
                        pltpu.VMEM((BLK, GM_W), F32), pltpu.VMEM((BLK, GM_W), F32)],
        compiler_params=_params())(z, z, z, sinks, bias, ln_g, ln_b, ws, bs_full)


def _f3_mix_out(am, h, ga, gg, w, gpost):
    t = h.shape[0]
    tm = _tile(t, 512)

    def body(am_ref, h_ref, ga_ref, gg_ref, w_ref, gp_ref, heads_ref, mix_ref, h1_ref):
        heads = jnp.concatenate([_rms(am_ref[:, :ATTN_W].astype(F32), ga_ref[...]),
                                 _rms(am_ref[:, ATTN_W:].astype(F32), gg_ref[...])], axis=1).astype(BF16)
        heads_ref[...] = heads
        mix = _mm(heads, w_ref[...])
        mix_ref[...] = mix.astype(BF16)
        h1_ref[...] = h_ref[...] + _rms(mix, gp_ref[...])

    return pl.pallas_call(
        body, name="f3_mix_out", grid=(t // tm,),
        in_specs=[_rows(tm, D_MODEL), _rows(tm, D_MODEL), _whole((1, ATTN_W)), _whole((1, GM_W)),
                  _whole((D_MODEL, D_MODEL)), _whole((1, D_MODEL))],
        out_specs=[_rows(tm, D_MODEL)] * 3,
        out_shape=[_sds((t, D_MODEL), BF16), _sds((t, D_MODEL), BF16), _sds((t, D_MODEL), F32)],
        compiler_params=_params())(am, h, ga, gg, w, gpost)


def _f4a_ffn_in(h1, gf, wg_t, wu_t):
    t = h1.shape[0]
    tm = _tile(t, 256)

    def body(h_ref, g_ref, wg_ref, wu_ref, f_ref, gt_ref, up_ref):
        f = _rms(h_ref[...], g_ref[...]).astype(BF16)
        f_ref[...] = f
        gt_ref[...] = _nt(f, wg_ref[...]).astype(BF16)
        up_ref[...] = _nt(f, wu_ref[...]).astype(BF16)

    return pl.pallas_call(
        body, name="f4a_ffn_in", grid=(t // tm,),
        in_specs=[_rows(tm, D_MODEL), _whole((1, D_MODEL)), _whole((D_FF, D_MODEL)), _whole((D_FF, D_MODEL))],
        out_specs=[_rows(tm, D_MODEL), _rows(tm, D_FF), _rows(tm, D_FF)],
        out_shape=[_sds((t, D_MODEL), BF16), _sds((t, D_FF), BF16), _sds((t, D_FF), BF16)],
        compiler_params=_params())(h1, gf, wg_t, wu_t)


def _f4b_ffn_out(gt, up, wd, h1, gfp):
    t = h1.shape[0]
    tm = _tile(t, 256)

    def body(gt_ref, up_ref, wd_ref, h_ref, g_ref, dn_ref, h2_ref):
        gt = gt_ref[...].astype(F32)
        act = (gt * _sigmoid(gt) * up_ref[...].astype(F32)).astype(BF16)
        dn = _mm(act, wd_ref[...])
        dn_ref[...] = dn.astype(BF16)
        h2_ref[...] = h_ref[...] + _rms(dn, g_ref[...])

    return pl.pallas_call(
        body, name="f4b_ffn_out", grid=(t // tm,),
        in_specs=[_rows(tm, D_FF), _rows(tm, D_FF), _whole((D_FF, D_MODEL)), _rows(tm, D_MODEL),
                  _whole((1, D_MODEL))],
        out_specs=[_rows(tm, D_MODEL)] * 2,
        out_shape=[_sds((t, D_MODEL), BF16), _sds((t, D_MODEL), F32)],
        compiler_params=_params())(gt, up, wd, h1, gfp)


def _f5_ple(h2, gpl, wpg, p, wple):
    t = h2.shape[0]
    tm = _tile(t, 512)

    def body(h_ref, g_ref, wpg_ref, p_ref, wple_ref, r_ref, pg_ref, pe_ref, h3_ref):
        h = h_ref[...]
        r = _rms(h, g_ref[...]).astype(BF16)
        r_ref[...] = r
        pg = _mm(r, wpg_ref[...])
        pe = _mm(p_ref[...].astype(BF16), wple_ref[...])
        pg_ref[...] = pg.astype(BF16)
        pe_ref[...] = pe.astype(BF16)
        h3_ref[...] = h + pe * _sigmoid(pg)

    return pl.pallas_call(
        body, name="f5_ple", grid=(t // tm,),
        in_specs=[_rows(tm, D_MODEL), _whole((1, D_MODEL)), _whole((D_MODEL, D_MODEL)), _rows(tm, PLE_DIM),
                  _whole((PLE_DIM, D_MODEL))],
        out_specs=[_rows(tm, D_MODEL)] * 4,
        out_shape=[_sds((t, D_MODEL), BF16)] * 3 + [_sds((t, D_MODEL), F32)],
        compiler_params=_params())(h2, gpl, wpg, p, wple)


def _loss_head(y, target):
    t = y.shape[0]
    tm = _tile(t, 512)

    def body(y_ref, t_ref, sq_ref, dy_ref):
        err = y_ref[...] - t_ref[...]
        dy_ref[...] = err * (1.0 / D_MODEL)
        _accumulate(sq_ref, jnp.sum(err * err, keepdims=True))

    return pl.pallas_call(
        body, name="loss_head", grid=(t // tm,),
        in_specs=[_rows(tm, D_MODEL)] * 2,
        out_specs=[_whole((1, LANES)), _rows(tm, D_MODEL)],
        out_shape=[_sds((1, LANES), F32), _sds((t, D_MODEL), F32)],
        compiler_params=_params())(y, target)


def _b5_ple(dh3, h2, pg, pe, gpl, wpg):
    t = h2.shape[0]
    tm = _tile(t, 512)

    def body(dh_ref, h_ref, pg_ref, pe_ref, g_ref, w_ref, dpe_ref, dpg_ref, dh2_ref, dg_ref):
        dh = dh_ref[...]
        s = _sigmoid(pg_ref[...].astype(F32))
        dpe_ref[...] = (dh * s).astype(BF16)
        dpg = (dh * pe_ref[...].astype(F32) * s * (1.0 - s)).astype(BF16)
        dpg_ref[...] = dpg
        dx, dg = _rms_bwd(_nt(dpg, w_ref[...]), h_ref[...], g_ref[...])
        dh2_ref[...] = dh + dx
        _accumulate(dg_ref, dg)

    return pl.pallas_call(
        body, name="b5_ple", grid=(t // tm,),
        in_specs=[_rows(tm, D_MODEL)] * 4 + [_whole((1, D_MODEL)), _whole((D_MODEL, D_MODEL))],
        out_specs=[_rows(tm, D_MODEL)] * 3 + [_whole((1, D_MODEL))],
        out_shape=[_sds((t, D_MODEL), BF16)] * 2 + [_sds((t, D_MODEL), F32), _sds((1, D_MODEL), F32)],
        compiler_params=_params())(dh3, h2, pg, pe, gpl, wpg)


def _b4a_ffn_out(dh2, dn, gfp, wd, gt, up):
    t = dh2.shape[0]
    tm = _tile(t, 256)

    def body(dh_ref, dn_ref, g_ref, wd_ref, gt_ref, up_ref, ddn_ref, act_ref, dgt_ref, dup_ref, dg_ref):
        ddn, dg = _rms_bwd(dh_ref[...], dn_ref[...].astype(F32), g_ref[...])
        _accumulate(dg_ref, dg)
        ddn = ddn.astype(BF16)
        ddn_ref[...] = ddn
        dact = _nt(ddn, wd_ref[...])
        gt = gt_ref[...].astype(F32)
        up = up_ref[...].astype(F32)
        sg = _sigmoid(gt)
        silu = gt * sg
        act_ref[...] = (silu * up).astype(BF16)
        dup_ref[...] = (dact * silu).astype(BF16)
        dgt_ref[...] = (dact * up * (sg * (1.0 + gt * (1.0 - sg)))).astype(BF16)

    return pl.pallas_call(
        body, name="b4a_ffn_out", grid=(t // tm,),
        in_specs=[_rows(tm, D_MODEL), _rows(tm, D_MODEL), _whole((1, D_MODEL)), _whole((D_FF, D_MODEL)),
                  _rows(tm, D_FF), _rows(tm, D_FF)],
        out_specs=[_rows(tm, D_MODEL), _rows(tm, D_FF), _rows(tm, D_FF), _rows(tm, D_FF), _whole((1, D_MODEL))],
        out_shape=[_sds((t, D_MODEL), BF16)] + [_sds((t, D_FF), BF16)] * 3 + [_sds((1, D_MODEL), F32)],
        compiler_params=_params())(dh2, dn, gfp, wd, gt, up)


def _b4b_ffn_in(dgt, dup, wg_t, wu_t, h1, gf, dh2):
    t = h1.shape[0]
    tm = _tile(t, 512)

    def body(dgt_ref, dup_ref, wg_ref, wu_ref, h_ref, g_ref, dh_ref, dh1_ref, dg_ref):
        df = _mm(dgt_ref[...], wg_ref[...]) + _mm(dup_ref[...], wu_ref[...])
        dx, dg = _rms_bwd(df, h_ref[...], g_ref[...])
        dh1_ref[...] = dh_ref[...] + dx
        _accumulate(dg_ref, dg)

    return pl.pallas_call(
        body, name="b4b_ffn_in", grid=(t // tm,),
        in_specs=[_rows(tm, D_FF), _rows(tm, D_FF), _whole((D_FF, D_MODEL)), _whole((D_FF, D_MODEL)),
                  _rows(tm, D_MODEL), _whole((1, D_MODEL)), _rows(tm, D_MODEL)],
        out_specs=[_rows(tm, D_MODEL), _whole((1, D_MODEL))],
        out_shape=[_sds((t, D_MODEL), F32), _sds((1, D_MODEL), F32)],
        compiler_params=_params())(dgt, dup, wg_t, wu_t, h1, gf, dh2)


def _b3_mix_out(dh1, mix, gpost, w, am, ga, gg):
    t = dh1.shape[0]
    tm = _tile(t, 512)

    def body(dh_ref, mix_ref, gp_ref, w_ref, am_ref, ga_ref, gg_ref, dmix_ref, dam_ref, dgp_ref, dga_ref, dgg_ref):
        dmix, dgp = _rms_bwd(dh_ref[...], mix_ref[...].astype(F32), gp_ref[...])
        _accumulate(dgp_ref, dgp)
        dmix = dmix.astype(BF16)
        dmix_ref[...] = dmix
        dheads = _nt(dmix, w_ref[...])
        dat, dga = _rms_bwd(dheads[:, :ATTN_W], am_ref[:, :ATTN_W].astype(F32), ga_ref[...])
        dgm, dgg = _rms_bwd(dheads[:, ATTN_W:], am_ref[:, ATTN_W:].astype(F32), gg_ref[...])
        dam_ref[:, :ATTN_W] = dat.astype(BF16)
        dam_ref[:, ATTN_W:] = dgm.astype(BF16)
        _accumulate(dga_ref, dga)
        _accumulate(dgg_ref, dgg)

    return pl.pallas_call(
        body, name="b3_mix_out", grid=(t // tm,),
        in_specs=[_rows(tm, D_MODEL), _rows(tm, D_MODEL), _whole((1, D_MODEL)), _whole((D_MODEL, D_MODEL)),
                  _rows(tm, D_MODEL), _whole((1, ATTN_W)), _whole((1, GM_W))],
        out_specs=[_rows(tm, D_MODEL), _rows(tm, D_MODEL), _whole((1, D_MODEL)), _whole((1, ATTN_W)),
                   _whole((1, GM_W))],
        out_shape=[_sds((t, D_MODEL), BF16), _sds((t, D_MODEL), BF16), _sds((1, D_MODEL), F32),
                   _sds((1, ATTN_W), F32), _sds((1, GM_W), F32)],
        compiler_params=_params())(dh1, mix, gpost, w, am, ga, gg)


def _b2_attn_gm(dam, z, sinks, bias, ln_g, ln_b, ws, bs_full):
    t = z.shape[0]
    nb = t // BLK

    def body(dam_ref, z_ref, kp_ref, vp_ref, sink_ref, bias_ref, lng_ref, lnb_ref, ws_ref, wst_ref, bs_ref,
             dzq_ref, dkv_ref, dzuv_ref, dsink_ref, dlng_ref, dlnb_ref, dws_ref, dbs_ref,
             sc_ref, dp_ref, p_ref, ds_ref, u_ref, du_ref, dv_ref, vn_ref, xhat_ref, rstd_ref, dvn_ref):
        n = pl.program_id(0)
        lane = lax.broadcasted_iota(jnp.int32, (1, LANES), 1)
        lo = lane < HEAD_DIM
        hi = lane >= HEAD_DIM
        sub = lax.broadcasted_iota(jnp.int32, (LANES, 1), 0)
        kc, ks, vcat, vswap = _kv_window(z_ref, kp_ref, vp_ref)
        vc = vcat.astype(BF16)
        vs_ = vswap.astype(BF16)
        kc_t = kc.T
        ks_t = ks.T

        def operands(s, half):
            mask = lo if half == 0 else hi
            qs = z_ref[:, s * LANES:(s + 1) * LANES].astype(F32) * (HEAD_DIM ** -0.5)
            qm = jnp.where(mask, qs, 0.0).astype(BF16)
            dom = jnp.where(mask, dam_ref[:, s * LANES:(s + 1) * LANES], 0.0).astype(BF16)
            return qm, dom

        for hq, s, half, same in _heads():
            qm, dom = operands(s, half)
            sc_ref[hq] = _nt(kc if same else ks, qm)
            dp_ref[hq] = _nt(vc if same else vs_, dom)
        dsink = jnp.zeros((1, LANES), F32)
        for hq in range(N_Q_HEADS):
            sink = sink_ref[hq]
            sc = sc_ref[hq] + bias_ref[hq]
            m = jnp.maximum(jnp.max(sc, axis=0, keepdims=True), sink)
            e = jnp.exp(sc - m)
            es = jnp.exp(sink - m)
            inv = 1.0 / (jnp.sum(e, axis=0, keepdims=True) + es)
            pr = e * inv
            dpr = dp_ref[hq]
            row = jnp.sum(dpr * pr, axis=0, keepdims=True)
            p_ref[hq] = pr.astype(BF16)
            ds_ref[hq] = (pr * (dpr - row)).astype(BF16)
            dsink = dsink + jnp.where(lane == hq, -jnp.sum(es * inv * row, keepdims=True), 0.0)
        dk_acc = [jnp.zeros((2 * BLK, LANES), F32), jnp.zeros((2 * BLK, LANES), F32)]
        dv_acc = [jnp.zeros((2 * BLK, LANES), F32), jnp.zeros((2 * BLK, LANES), F32)]
        for s in range(ATTN_W // LANES):
            dq_t = jnp.zeros((LANES, BLK), F32)
            for hq, hs, half, same in _heads():
                if hs != s:
                    continue
                qm, dom = operands(s, half)
                ds = ds_ref[hq]
                dk_acc[same] = dk_acc[same] + _mm(ds, qm)
                dv_acc[same] = dv_acc[same] + _mm(p_ref[hq], dom)
                in_half = (sub < HEAD_DIM) if half == 0 else (sub >= HEAD_DIM)
                dq_t = dq_t + jnp.where(in_half, _mm(kc_t if same else ks_t, ds), 0.0)
            dzq_ref[:, s * LANES:(s + 1) * LANES] = (dq_t.T * (HEAD_DIM ** -0.5)).astype(BF16)
        dk_acc = dk_acc[True] + pltpu.roll(dk_acc[False], HEAD_DIM, 1)
        dv_acc = dv_acc[True] + pltpu.roll(dv_acc[False], HEAD_DIM, 1)
        cur = pl.multiple_of(n * BLK, BLK)
        dkv_ref[pl.ds(cur, BLK), 0:KV_W] = dk_acc[BLK:, :]
        dkv_ref[pl.ds(cur, BLK), KV_W:2 * KV_W] = dv_acc[BLK:, :]

        @pl.when(n > 0)
        def _():
            prv = pl.multiple_of((n - 1) * BLK, BLK)
            dkv_ref[pl.ds(prv, BLK), 0:KV_W] += dk_acc[:BLK, :]
            dkv_ref[pl.ds(prv, BLK), KV_W:2 * KV_W] += dv_acc[:BLK, :]

        _accumulate(dsink_ref, dsink)

        for rows in _strips():
            u_ref[rows, :], du_ref[rows, :] = _gelu_and_grad(
                z_ref[rows, ATTN_W + 2 * KV_W:ATTN_W + 2 * KV_W + GM_W].astype(F32))
            gv, dv_ref[rows, :] = _gelu_and_grad(z_ref[rows, ATTN_W + 2 * KV_W + GM_W:D_IN].astype(F32))
            vn_ref[rows, :], xhat_ref[rows, :], rstd = _layernorm_strip(gv, lng_ref, lnb_ref)
            rstd_ref[rows, :] = jnp.broadcast_to(rstd, (STRIP, LANES))
        mixed = _gm_mixed(vn_ref, ws_ref, bs_ref, lo, hi)

        @pl.when(n == 0)
        def _():
            dws_ref[...] = jnp.zeros_like(dws_ref)

        dbs = jnp.zeros((BLK, LANES), F32)
        for s in range(GM_W // LANES):
            slab = slice(s * LANES, (s + 1) * LANES)
            dgm = dam_ref[:, ATTN_W + s * LANES:ATTN_W + (s + 1) * LANES]
            dzuv_ref[:, slab] = (dgm * mixed[s] * du_ref[:, slab]).astype(BF16)
            dmx = dgm * u_ref[:, slab]
            vsb = vn_ref[:, slab].astype(BF16)
            dvn = jnp.zeros((BLK, LANES), F32)
            for half in range(2):
                h = 2 * s + half
                mask = lo if half == 0 else hi
                dmm = jnp.where(mask, dmx, 0.0)
                dbs = dbs + jnp.where(lane == h, jnp.sum(dmm, axis=1, keepdims=True), 0.0)
                dmm = dmm.astype(BF16)
                _, causal = _tril_w(ws_ref, h)
                ti = lax.broadcasted_iota(jnp.int32, (BLK, BLK), 0)
                si = lax.broadcasted_iota(jnp.int32, (BLK, BLK), 1)
                wt_t = jnp.where(ti <= si, wst_ref[h], 0.0).astype(BF16)
                dvn = dvn + jnp.where(mask, _mm(wt_t, dmm), 0.0)
                dws_ref[h] += jnp.where(causal, _nt(dmm, vsb), 0.0)
            dvn_ref[:, slab] = dvn
        _accumulate(dbs_ref, dbs)
        dlnb = jnp.zeros((STRIP, GM_W), F32)
        dlng = jnp.zeros((STRIP, GM_W), F32)
        for rows in _strips():
            dvn = dvn_ref[rows, :]
            xhat = xhat_ref[rows, :]
            dlnb = dlnb + dvn
            dlng = dlng + dvn * xhat
            dxh = dvn * lng_ref[...]
            dgv = rstd_ref[rows, 0:1] * (dxh - jnp.mean(dxh, axis=-1, keepdims=True)
                                         - xhat * jnp.mean(dxh * xhat, axis=-1, keepdims=True))
            dzuv_ref[rows, GM_W:] = (dgv * dv_ref[rows, :]).astype(BF16)
        _accumulate(dlnb_ref, jnp.sum(dlnb, axis=0, keepdims=True))
        _accumulate(dlng_ref, jnp.sum(dlng, axis=0, keepdims=True))

    return pl.pallas_call(
        body, name="b2_attn_gm", grid=(nb,),
        in_specs=[_rows(BLK, ATTN_W + GM_W)] + _block_specs_z(nb) + [
            pl.BlockSpec(memory_space=pltpu.SMEM), _bias_spec(transposed=True), _whole((1, GM_W)),
            _whole((1, GM_W)), _whole((N_Q_HEADS, BLK, BLK)), _whole((N_Q_HEADS, BLK, BLK)), _whole((BLK, GM_W))],
        out_specs=[_rows(BLK, ATTN_W), _whole((t, 2 * KV_W)), _rows(BLK, 2 * GM_W), _whole((1, LANES)),
                   _whole((1, GM_W)), _whole((1, GM_W)), _whole((N_Q_HEADS, BLK, BLK)), _whole((BLK, LANES))],
        out_shape=[_sds((t, ATTN_W), BF16), _sds((t, 2 * KV_W), F32), _sds((t, 2 * GM_W), BF16),
                   _sds((1, LANES), F32), _sds((1, GM_W), F32), _sds((1, GM_W), F32),
                   _sds((N_Q_HEADS, BLK, BLK), F32), _sds((BLK, LANES), F32)],
        scratch_shapes=[pltpu.VMEM((N_Q_HEADS, 2 * BLK, BLK), F32)] * 2
        + [pltpu.VMEM((N_Q_HEADS, 2 * BLK, BLK), BF16)] * 2 + [pltpu.VMEM((BLK, GM_W), F32)] * 5
        + [pltpu.VMEM((BLK, LANES), F32), pltpu.VMEM((BLK, GM_W), F32)],
        compiler_params=_params())(dam, z, z, z, sinks, jnp.swapaxes(bias, 2, 3), ln_g, ln_b, ws,
                                   jnp.swapaxes(ws, 1, 2), bs_full)


def _b1_norm_in(dzq, dkv, dzuv, w_t, h, g, dh1):
    t = h.shape[0]
    tm = _tile(t, 512)

    def body(dzq_ref, dkv_ref, dzuv_ref, w_ref, h_ref, g_ref, dh_ref, dz_ref, dh0_ref, dg_ref):
        dz = jnp.concatenate([dzq_ref[...], dkv_ref[...].astype(BF16), dzuv_ref[...]], axis=1)
        dz_ref[...] = dz
        dx, dg = _rms_bwd(_mm(dz, w_ref[...]), h_ref[...], g_ref[...])
        dh0_ref[...] = dh_ref[...] + dx
        _accumulate(dg_ref, dg)

    return pl.pallas_call(
        body, name="b1_norm_in", grid=(t // tm,),
        in_specs=[_rows(tm, ATTN_W), _rows(tm, 2 * KV_W), _rows(tm, 2 * GM_W), _whole((D_IN, D_MODEL)),
                  _rows(tm, D_MODEL), _whole((1, D_MODEL)), _rows(tm, D_MODEL)],
        out_specs=[_rows(tm, D_IN), _rows(tm, D_MODEL), _whole((1, D_MODEL))],
        out_shape=[_sds((t, D_IN), BF16), _sds((t, D_MODEL), F32), _sds((1, D_MODEL), F32)],
        compiler_params=_params())(dzq, dkv, dzuv, w_t, h, g, dh1)


def _weight_grad(x, dy, name):
    t, k = x.shape
    n = dy.shape[1]
    tm = _tile(t, 1024)
    steps = t // tm

    def body(x_ref, dy_ref, dw_ref, acc_ref):
        i = pl.program_id(0)

        @pl.when(i == 0)
        def _():
            acc_ref[...] = _tn(x_ref[...].astype(BF16), dy_ref[...])

        @pl.when(i > 0)
        def _():
            acc_ref[...] += _tn(x_ref[...].astype(BF16), dy_ref[...])

        @pl.when(i == steps - 1)
        def _():
            dw_ref[...] = acc_ref[...].astype(BF16)

    return pl.pallas_call(
        body, name=name, grid=(steps,),
        in_specs=[_rows(tm, k), _rows(tm, n)],
        out_specs=_whole((k, n)),
        out_shape=_sds((k, n), BF16),
        scratch_shapes=[pltpu.VMEM((k, n), F32)],
        compiler_params=_params())(x, dy)


_ANY = pl.BlockSpec(memory_space=pl.ANY)
_HBM = pl.BlockSpec(memory_space=pltpu.HBM)
_SEM = pl.BlockSpec(memory_space=pltpu.SEMAPHORE)
_EFFECT = pltpu.SideEffectType.DATAFLOW_SIDE_EFFECTING


def _mesh_pos():
    return lax.axis_index("x"), lax.axis_index("y"), lax.axis_index("c")


def _other_chips(x, y):
    return [(1 - x, y), (x, 1 - y), (1 - x, 1 - y)]


def _peers(mode, x, y, c):
    if mode == "devices":
        peers = []
        for j in range(1, N_DEV):
            px = 1 - x if (j >> 2) & 1 else x
            py = 1 - y if (j >> 1) & 1 else y
            pc = 1 - c if j & 1 else c
            peers.append(((px, py, pc), 4 * px + 2 * py + pc))
        return peers, 4 * x + 2 * y + c
    return [((px, py, c), 2 * px + py) for px, py in _other_chips(x, y)], 2 * x + y


def _descriptors(mode, srcs, lands, send_sems, recv_sems, with_incoming=True):
    x, y, c = _mesh_pos()
    peers, me = _peers(mode, x, y, c)
    scatter = mode == "scatter"
    outgoing, incoming = [], []
    for i, (src, land) in enumerate(zip(srcs, lands)):
        for j, (dev, slot) in enumerate(peers):
            sem = i * len(peers) + j
            common = dict(send_sem=send_sems.at[sem], recv_sem=recv_sems.at[sem], device_id=dev,
                          device_id_type=pl.DeviceIdType.MESH)
            outgoing.append(pltpu.make_async_remote_copy(
                src_ref=src.at[slot] if scatter else src, dst_ref=land.at[me], **common))
            if with_incoming:
                incoming.append(pltpu.make_async_remote_copy(
                    src_ref=src.at[me] if scatter else src, dst_ref=land.at[slot], **common))
    return outgoing, incoming


def _n_sems(mode, k):
    return k * ((N_DEV if mode == "devices" else N_CHIPS) - 1)


def _exchange_start(srcs, lands, mode, name, after=None):
    k = len(srcs)
    arrs = [*srcs, *lands]

    def body(*refs):
        skip = 1 if after is not None else 0
        send_sems, recv_sems = refs[2 * k + skip], refs[2 * k + skip + 1]
        outgoing, _ = _descriptors(mode, refs[:k], refs[k:2 * k], send_sems, recv_sems, with_incoming=False)
        for cp in outgoing:
            cp.start()
        refs[-1][...] = jnp.zeros_like(refs[-1])

    operands = [pltpu.with_memory_space_constraint(a, pltpu.HBM) for a in arrs]
    in_specs = [_HBM] * (2 * k)
    if after is not None:
        operands.append(after)
        in_specs.append(_ANY)
    sems = pltpu.SemaphoreType.DMA((_n_sems(mode, k),))
    res = pl.pallas_call(
        body, name=name, in_specs=in_specs,
        out_shape=(sems, sems, *[pltpu.HBM(a.shape, a.dtype) for a in arrs], _sds((8, LANES), F32)),
        out_specs=(_SEM, _SEM, *[_HBM] * (2 * k), pl.BlockSpec(memory_space=pltpu.VMEM)),
        input_output_aliases={i: 2 + i for i in range(2 * k)},
        compiler_params=pltpu.CompilerParams(has_side_effects=_EFFECT))(*operands)
    return dict(mode=mode, send=res[0], recv=res[1], srcs=res[2:2 + k], lands=res[2 + k:2 + 2 * k], token=res[-1])


def _exchange_wait(flight, name, after):
    mode, k = flight["mode"], len(flight["srcs"])
    arrs = [*flight["srcs"], *flight["lands"]]

    def body(*refs):
        outgoing, incoming = _descriptors(mode, refs[:k], refs[k:2 * k], refs[2 * k], refs[2 * k + 1])
        for cp in outgoing:
            cp.wait_send()
        for cp in incoming:
            cp.wait_recv()

    res = pl.pallas_call(
        body, name=name, in_specs=[_HBM] * (2 * k) + [_SEM, _SEM, _ANY],
        out_shape=tuple(pltpu.HBM(a.shape, a.dtype) for a in arrs), out_specs=tuple([_HBM] * (2 * k)),
        input_output_aliases={i: i for i in range(2 * k)},
        compiler_params=pltpu.CompilerParams(has_side_effects=_EFFECT))(*arrs, flight["send"], flight["recv"], after)
    return res[:k], res[k:]


def _sibling_exchange(arrs, name):
    k = len(arrs)

    def body(*refs):
        ins, outs = refs[:k], refs[k:2 * k]
        send_sems, recv_sems = refs[2 * k:]
        x, y, c = _mesh_pos()
        cps = []
        for i in range(k):
            cp = pltpu.make_async_remote_copy(
                src_ref=ins[i], dst_ref=outs[i], send_sem=send_sems.at[i], recv_sem=recv_sems.at[i],
                device_id=(x, y, 1 - c), device_id_type=pl.DeviceIdType.MESH)
            cp.start()
            cps.append(cp)
        for cp in cps:
            cp.wait()

    return pl.pallas_call(
        body, name=name, in_specs=[_ANY] * k, out_specs=[_ANY] * k,
        out_shape=[_sds(a.shape, a.dtype) for a in arrs],
        scratch_shapes=[pltpu.SemaphoreType.DMA((k,)), pltpu.SemaphoreType.DMA((k,))])(*arrs)


def _adamw(w, g, m, v):
    m = ADAM_B1 * m + (1.0 - ADAM_B1) * g
    v = ADAM_B2 * v + (1.0 - ADAM_B2) * (g * g)
    m_hat = m / (1.0 - ADAM_B1 ** ADAM_STEP)
    v_hat = v / (1.0 - ADAM_B2 ** ADAM_STEP)
    delta = -ADAM_LR * (m_hat / (jnp.sqrt(v_hat) + ADAM_EPS) + ADAM_WD * w)
    return delta, m, v


def _row_tile(rows, cols, n_arrays):
    budget = VMEM_LIMIT_BYTES // 2
    padded = -(-cols // LANES) * LANES
    cap = min(rows, max(16, budget // (2 * n_arrays * padded * 4)))
    return max(tr for tr in range(16, cap + 1, 16) if rows % tr == 0)


def _sum_chips(landing, name):
    _, r, c = landing.shape
    tr = _row_tile(r, c, 4)

    def body(l_ref, s_ref):
        acc = l_ref[0].astype(F32)
        for s in range(1, N_CHIPS):
            acc = acc + l_ref[s].astype(F32)
        s_ref[...] = acc

    return pl.pallas_call(
        body, name=name, grid=(r // tr,),
        in_specs=[pl.BlockSpec((N_CHIPS, tr, c), lambda i: (0, i, 0))],
        out_specs=_rows(tr, c), out_shape=_sds((r, c), F32),
        compiler_params=_params())(landing)


def _adamw_big(mine, sibling, w, m, v, prev, layer, name):
    _, r, c = w.shape
    tr = _row_tile(r, c, 9)
    stacked = pl.BlockSpec((None, tr, c), lambda i: (layer, i, 0))

    def body(a_ref, b_ref, w_ref, m_ref, v_ref, *rest):
        g_out, d_out, m_out, v_out = rest[-4:]
        g = a_ref[...] + b_ref[...]
        g_out[...] = g
        d_out[...], m_out[...], v_out[...] = _adamw(w_ref[...], g, m_ref[...], v_ref[...])

    prev = list(prev) if prev is not None else []
    return pl.pallas_call(
        body, name=name, grid=(r // tr,),
        in_specs=[_rows(tr, c)] * 2 + [stacked] * 3 + [_ANY] * len(prev),
        out_specs=[stacked] * 4, out_shape=[_sds(w.shape, F32)] * 4,
        input_output_aliases={5 + j: j for j in range(len(prev))},
        compiler_params=_params())(mine, sibling, w, m, v, *prev)


def _adamw_small(gathered, w, m, v):
    r = w.shape[0]

    def body(a_ref, w_ref, m_ref, v_ref, g_out, d_out, m_out, v_out):
        g = a_ref[0]
        for d in range(1, N_DEV):
            g = g + a_ref[d]
        g_out[...] = g
        d_out[...], m_out[...], v_out[...] = _adamw(w_ref[...], g, m_ref[...], v_ref[...])

    return pl.pallas_call(
        body, name="adamw_small", grid=(1,),
        in_specs=[_whole((N_DEV, r, LANES))] + [_whole((r, LANES))] * 3,
        out_specs=[_whole((r, LANES))] * 4, out_shape=[_sds((r, LANES), F32)] * 4,
        compiler_params=_params())(gathered, w, m, v)


def _shard_view(name, stacked):
    return jnp.swapaxes(stacked, 1, 2) if name in TRANSPOSED else stacked


def _row_split(name):
    return name in TRANSPOSED or BIG[name] == 1


def _assemble(name, landed):
    _, r, c = landed.shape
    if _row_split(name):
        return landed.reshape(N_CHIPS * r, c)
    return landed.transpose(1, 0, 2).reshape(r, N_CHIPS * c)


def _split(name, whole):
    r, c = whole.shape
    if _row_split(name):
        return whole.reshape(N_CHIPS, r // N_CHIPS, c)
    return whole.reshape(r, N_CHIPS, c // N_CHIPS).transpose(1, 0, 2)


def _with_own_slot(landed, own, slot):
    return lax.dynamic_update_index_in_dim(landed, own, slot, 0)


def _pack_small(params, names):
    pieces = []
    for name in names:
        flat = params[name].reshape(-1)
        pieces.append(jnp.pad(flat, (0, -flat.shape[0] % SMALL_PAD)))
    return jnp.concatenate(pieces).reshape(-1, LANES)


def _unpack_small(packed, like, names):
    flat = packed.reshape(-1)
    out, off = {}, 0
    for name in names:
        size = like[name].size
        out[name] = flat[off:off + size].reshape(like[name].shape)
        off += size + (-size % SMALL_PAD)
    return out


def kernel(x, p, ln_mix_pre, w_in, attn_sinks, gm_ln_g, gm_ln_b, gm_ws, gm_bs, g_attn_out, g_gm_out, w_out, ln_mix_post, ln_ffn_pre, w_ffn_gate, w_ffn_up, w_ffn_down, ln_ffn_post, w_ple, ln_ple_gate, w_ple_gate, loss_target, m_ln_mix_pre, m_w_in, m_attn_sinks, m_gm_ln_g, m_gm_ln_b, m_gm_ws, m_gm_bs, m_g_attn_out, m_g_gm_out, m_w_out, m_ln_mix_post, m_ln_ffn_pre, m_w_ffn_gate, m_w_ffn_up, m_w_ffn_down, m_ln_ffn_post, m_w_ple, m_ln_ple_gate, m_w_ple_gate, v_ln_mix_pre, v_w_in, v_attn_sinks, v_gm_ln_g, v_gm_ln_b, v_gm_ws, v_gm_bs, v_g_attn_out, v_g_gm_out, v_w_out, v_ln_mix_post, v_ln_ffn_pre, v_w_ffn_gate, v_w_ffn_up, v_w_ffn_down, v_ln_ffn_post, v_w_ple, v_ln_ple_gate, v_w_ple_gate):
    given = dict(locals())
    wts = {n: given[n] for n in WEIGHTS}
    mom = {n: given["m_" + n] for n in WEIGHTS}
    var = {n: given["v_" + n] for n in WEIGHTS}
    depth = w_in.shape[0]
    h = x[0]
    target = loss_target[0]
    chip = 2 * lax.axis_index("x") + lax.axis_index("y")
    device = 2 * chip + lax.axis_index("c")
    row = lambda a, i: a[i][None, :]
    bs_full = [jnp.repeat(gm_bs[i].T, HEAD_DIM, axis=1) for i in range(depth)]
    bias = _alibi_bias()
    kinds = ("grad", "delta", "m", "v")
    wview = {n: _shard_view(n, wts[n]) for n in BIG_NAMES}
    mview = {n: _shard_view(n, mom[n]) for n in BIG_NAMES}
    vview = {n: _shard_view(n, var[n]) for n in BIG_NAMES}

    zero = lambda flight: flight["token"][0:1, 0:1]

    def start_gather(i, names, after, tag):
        shards = [wview[n][i].astype(BF16) for n in names]
        lands = [lax.empty((N_CHIPS,) + s.shape, BF16) for s in shards]
        return _exchange_start(shards, lands, "gather", f"gather_weights_start_{i}{tag}", after)

    def finish_gather(flight, names, i, after, tag):
        shards, lands = _exchange_wait(flight, f"gather_weights_wait_{i}{tag}", after)
        return {n: _assemble(n, _with_own_slot(l, s, chip)) for n, s, l in zip(names, shards, lands)}

    first, after = [], None
    for k, names in enumerate(GATHER_GROUPS):
        first.append(start_gather(0, names, after, "abc"[k]))
        after = first[-1]["token"]
    full = [finish_gather(first[0], GATHER_GROUPS[0], 0, after, "a")] + [None] * (depth - 1)
    saved = []
    for i in range(depth):
        w = full[i]
        g_in = row(ln_mix_pre, i)
        if i + 1 < depth:
            flight = start_gather(i + 1, BIG_NAMES, w['w_in'], "")
            g_in = g_in + zero(flight)
        z, a = _f1_norm_in(h, g_in, w['w_in'])
        am = _f2_attn_gm(z, attn_sinks[i], bias, row(gm_ln_g, i), row(gm_ln_b, i), gm_ws[i], bs_full[i])
        if i == 0:
            w.update(finish_gather(first[1], GATHER_GROUPS[1], 0, am, "b"))
        heads, mix, h1 = _f3_mix_out(am, h, row(g_attn_out, i), row(g_gm_out, i), w['w_out'], row(ln_mix_post, i))
        f, gt, up = _f4a_ffn_in(h1, row(ln_ffn_pre, i), w['w_ffn_gate'], w['w_ffn_up'])
        if i == 0:
            w.update(finish_gather(first[2], GATHER_GROUPS[2], 0, gt, "c"))
        dn, h2 = _f4b_ffn_out(gt, up, w['w_ffn_down'], h1, row(ln_ffn_post, i))
        r, pg, pe, h3 = _f5_ple(h2, row(ln_ple_gate, i), w['w_ple_gate'], p[i, 0], w['w_ple'])
        saved.append(dict(h=h, z=z, a=a, am=am, heads=heads, mix=mix, h1=h1, f=f, gt=gt, up=up, dn=dn, h2=h2,
                          r=r, pg=pg, pe=pe))
        h = h3
        if i + 1 < depth:
            full[i + 1] = finish_gather(flight, BIG_NAMES, i + 1, h3, "")

    sq, dh = _loss_head(h, target)
    loss = lax.psum(0.5 / D_MODEL * sq[0, 0], ("x", "y", "c"))

    chain = {n: None for n in BIG_NAMES}
    small_out = [{k: {} for k in kinds} for _ in range(depth)]

    def start_scatter(i, names, dws, after, tag):
        parts = [_split(n, dws[n]) for n in names]
        lands = [lax.empty(q.shape, BF16) for q in parts]
        return _exchange_start(parts, lands, "scatter", f"scatter_grads_start_{i}{tag}", after)

    def finish_scatter(i, names, flight, after, tag):
        parts, lands = _exchange_wait(flight, f"scatter_grads_wait_{i}{tag}", after)
        lands = [_with_own_slot(l, lax.dynamic_index_in_dim(q, chip, 0, keepdims=False), chip)
                 for q, l in zip(parts, lands)]
        partial = [_sum_chips(l, "sum_chips_" + n) for n, l in zip(names, lands)]
        sibling = _sibling_exchange(partial, name="sibling_grads")
        for n, mine, sib in zip(names, partial, sibling):
            chain[n] = _adamw_big(mine, sib, wview[n], mview[n], vview[n], chain[n], i, "adamw_" + n)
        return lands[0]

    def start_small(i, names, small, after, tag):
        packed = _pack_small(small, names)
        land = lax.empty((N_DEV,) + packed.shape, F32)
        return _exchange_start([packed], [land], "devices", f"gather_small_grads_start_{i}{tag}", after)

    def finish_small(i, names, flight, after, tag):
        (packed,), (gathered,) = _exchange_wait(flight, f"gather_small_grads_wait_{i}{tag}", after)
        gathered = _with_own_slot(gathered, packed, device)
        layer = lambda d: _pack_small({n: d[n][i] for n in names}, names)
        res = _adamw_small(gathered, layer(wts), layer(mom), layer(var))
        for k, a in zip(kinds, res):
            small_out[i][k].update(_unpack_small(a, {n: wts[n][i] for n in names}, names))
        return gathered

    pending, done, behind = [], None, None
    for i in reversed(range(depth)):
        s, w = saved[i], full[i]
        last = i == 0
        dws, small = {}, {}
        gain = row(ln_ple_gate, i)
        if behind is not None:
            gain = gain + behind
        dpe, dpg, dh2, dg = _b5_ple(dh, s['h2'], s['pg'], s['pe'], gain, w['w_ple_gate'])
        small['ln_ple_gate'] = dg[0]
        dws['w_ple'] = _weight_grad(p[i, 0], dpe, "dw_ple")
        dws['w_ple_gate'] = _weight_grad(s['r'], dpg, "dw_ple_gate")

        ddn, act, dgt, dup, dg = _b4a_ffn_out(dh2, s['dn'], row(ln_ffn_post, i), w['w_ffn_down'], s['gt'], s['up'])
        small['ln_ffn_post'] = dg[0]
        dws['w_ffn_down'] = _weight_grad(act, ddn, "dw_ffn_down")
        dws['w_ffn_gate'] = _weight_grad(dgt, s['f'], "dw_ffn_gate")
        dws['w_ffn_up'] = _weight_grad(dup, s['f'], "dw_ffn_up")
        gain = row(ln_ffn_pre, i)
        if last:
            flight_a = start_scatter(i, SCATTER_GROUPS[0], dws, None, "a")
            gain = gain + zero(flight_a)
        dh1, dg = _b4b_ffn_in(dgt, dup, w['w_ffn_gate'], w['w_ffn_up'], s['h1'], gain, dh2)
        small['ln_ffn_pre'] = dg[0]

        dmix, dam, dgp, dga, dgg = _b3_mix_out(dh1, s['mix'], row(ln_mix_post, i), w['w_out'], s['am'],
                                               row(g_attn_out, i), row(g_gm_out, i))
        small['ln_mix_post'] = dgp[0]
        small['g_attn_out'] = dga[0]
        small['g_gm_out'] = dgg[0]
        dws['w_out'] = _weight_grad(s['heads'], dmix, "dw_out")
        gain = row(gm_ln_g, i)
        if last:
            flight_b = start_scatter(i, SCATTER_GROUPS[1], dws, flight_a["token"], "b")
            gain = gain + zero(flight_b)

        dzq, dkv, dzuv, dsink, dlng, dlnb, dgws, dbs = _b2_attn_gm(
            dam, s['z'], attn_sinks[i], bias, gain, row(gm_ln_b, i), gm_ws[i], bs_full[i])
        small['attn_sinks'] = dsink[0, :N_Q_HEADS]
        small['gm_ln_g'] = dlng[0]
        small['gm_ln_b'] = dlnb[0]
        small['gm_ws'] = dgws
        small['gm_bs'] = dbs[:, :N_Q_HEADS].T
        gain = row(ln_mix_pre, i)
        if last:
            flight_s = start_small(i, SMALL_EARLY, small, flight_b["token"], "a")
            gain = gain + zero(flight_s)

        dz, dh, dg = _b1_norm_in(dzq, dkv, dzuv, w['w_in'], s['h'], gain, dh1)
        small['ln_mix_pre'] = dg[0]
        for finish in pending:
            done = finish(dh)
        pending = []
        if last:
            done = finish_scatter(i, SCATTER_GROUPS[0], flight_a, dz, "a")
            done = finish_scatter(i, SCATTER_GROUPS[1], flight_b, done, "b")
        dws['w_in'] = _weight_grad(dz, s['a'], "dw_in")
        if last:
            flight_c = start_scatter(i, SCATTER_GROUPS[2], dws, done, "c")
            flight_t = start_small(i, SMALL_LATE, small, flight_c["token"], "b")
            finish_small(i, SMALL_EARLY, flight_s, flight_t["token"], "a")
            finish_scatter(i, SCATTER_GROUPS[2], flight_c, flight_t["token"], "c")
            finish_small(i, SMALL_LATE, flight_t, flight_t["token"], "b")
        else:
            flight_c = start_scatter(i, BIG_NAMES, dws, done, "")
            flight_s = start_small(i, SMALL_NAMES, small, flight_c["token"], "")
            behind = zero(flight_s)
            pending = [functools.partial(finish_scatter, i, BIG_NAMES, flight_c, tag=""),
                       functools.partial(finish_small, i, SMALL_NAMES, flight_s, tag="")]
    grad_x = dh[None]

    out = {k: {n: _shard_view(n, chain[n][j]) for n in BIG_NAMES} for j, k in enumerate(kinds)}
    for k in kinds:
        out[k].update({n: jnp.stack([small_out[i][k][n] for i in range(depth)]) for n in SMALL_NAMES})

    return (loss, grad_x, *[out["grad"][n] for n in WEIGHTS], *[out["delta"][n] for n in WEIGHTS],
            *[out["m"][n] for n in WEIGHTS], *[out["v"][n] for n in WEIGHTS])
```

```python
import functools
import math

import jax
import jax.numpy as jnp
from jax import lax
from jax.experimental import pallas as pl
from jax.experimental.pallas import tpu as pltpu

F32 = jnp.float32
BF16 = jnp.bfloat16

D_MODEL = 1024
HEAD_DIM = 64
N_Q_HEADS = 8
BLK = 128
ATTN_W = 512
KV_W = 128
GM_W = 512
D_IN = ATTN_W + 2 * KV_W + 2 * GM_W
D_FF = 2816
PLE_DIM = 256
DEPTH = 4
NORM_EPS = 1e-6
NEG_BIG = -1e30
N_CHIPS = 4
N_DEV = 8

ADAM_LR = 0.001
ADAM_B1 = 0.9
ADAM_B2 = 0.999
ADAM_EPS = 1e-08
ADAM_WD = 0.01
ADAM_STEP = 10

VMEM_LIMIT_BYTES = 56 * 1024 * 1024
LANES = 128
STRIP = 16
GELU_C0 = math.sqrt(2.0 / math.pi)
GELU_C1 = 0.044715
ALIBI_SLOPES = tuple(2.0 ** (-8.0 * (h + 1.0) / N_Q_HEADS) for h in range(N_Q_HEADS))

WEIGHTS = ['ln_mix_pre', 'w_in', 'attn_sinks', 'gm_ln_g', 'gm_ln_b', 'gm_ws', 'gm_bs', 'g_attn_out',
           'g_gm_out', 'w_out', 'ln_mix_post', 'ln_ffn_pre', 'w_ffn_gate', 'w_ffn_up', 'w_ffn_down',
           'ln_ffn_post', 'w_ple', 'ln_ple_gate', 'w_ple_gate']
BIG = {'w_in': 2, 'w_out': 1, 'w_ffn_gate': 2, 'w_ffn_up': 2, 'w_ffn_down': 1, 'w_ple': 2, 'w_ple_gate': 1}
BIG_NAMES = list(BIG)
TRANSPOSED = ('w_in', 'w_ffn_gate', 'w_ffn_up')
SMALL_NAMES = [n for n in WEIGHTS if n not in BIG]
SMALL_PAD = 1024
GATHER_GROUPS = (('w_in',), ('w_out', 'w_ffn_gate', 'w_ffn_up'), ('w_ffn_down', 'w_ple_gate', 'w_ple'))
SCATTER_GROUPS = (('w_ple', 'w_ple_gate', 'w_ffn_down', 'w_ffn_gate', 'w_ffn_up'), ('w_out',), ('w_in',))
SMALL_LATE = ('ln_mix_pre',)
SMALL_EARLY = tuple(n for n in SMALL_NAMES if n not in SMALL_LATE)


def _nt(a, b):
    return lax.dot_general(a, b, (((1,), (1,)), ((), ())), preferred_element_type=F32)


def _tn(a, b):
    return lax.dot_general(a, b, (((0,), (0,)), ((), ())), preferred_element_type=F32)


def _mm(a, b):
    return jnp.dot(a, b, preferred_element_type=F32)


def _rms(x, g):
    r = lax.rsqrt(jnp.mean(x * x, axis=-1, keepdims=True) + NORM_EPS)
    return x * r * g


def _rms_bwd(dy, x, g):
    r = lax.rsqrt(jnp.mean(x * x, axis=-1, keepdims=True) + NORM_EPS)
    xh = x * r
    dg = jnp.sum(dy * xh, axis=0, keepdims=True)
    dxh = dy * g
    dx = r * (dxh - xh * jnp.mean(dxh * xh, axis=-1, keepdims=True))
    return dx, dg


def _gelu(x):
    return 0.5 * x * (1.0 + jnp.tanh(GELU_C0 * (x + GELU_C1 * x * x * x)))


def _sigmoid(x):
    return 0.5 * jnp.tanh(0.5 * x) + 0.5


def _rows(tm, n):
    return pl.BlockSpec((tm, n), lambda i: (i, 0))


def _whole(shape):
    return pl.BlockSpec(shape, lambda i: (0,) * len(shape))


def _accumulate(ref, val):
    @pl.when(pl.program_id(0) == 0)
    def _():
        ref[...] = jnp.zeros_like(ref)

    ref[...] += val


def _params(n_axes=1):
    return pltpu.CompilerParams(dimension_semantics=("arbitrary",) * n_axes,
                                vmem_limit_bytes=VMEM_LIMIT_BYTES)


def _sds(shape, dtype):
    return jax.ShapeDtypeStruct(shape, dtype)


def _tile(t, want):
    return min(t, want)


def _f1_norm_in(h, g, w_t):
    t = h.shape[0]
    tm = _tile(t, 512)

    def body(h_ref, g_ref, w_ref, z_ref, a_ref):
        a = _rms(h_ref[...], g_ref[...]).astype(BF16)
        a_ref[...] = a
        z_ref[...] = _nt(a, w_ref[...]).astype(BF16)

    return pl.pallas_call(
        body, name="f1_norm_in", grid=(t // tm,),
        in_specs=[_rows(tm, D_MODEL), _whole((1, D_MODEL)), _whole((D_IN, D_MODEL))],
        out_specs=[_rows(tm, D_IN), _rows(tm, D_MODEL)],
        out_shape=[_sds((t, D_IN), BF16), _sds((t, D_MODEL), BF16)],
        compiler_params=_params())(h, g, w_t)


def _alibi_bias():
    ti = jnp.arange(BLK)[:, None]
    ji = jnp.arange(2 * BLK)[None, :]
    dist = ti + BLK - ji
    band = (dist >= 0) & (dist < BLK)
    bias = -jnp.asarray(ALIBI_SLOPES, F32)[:, None, None] * dist.astype(F32)[None]
    return jnp.stack([jnp.where((band & (ji >= BLK))[None], bias, NEG_BIG), jnp.where(band[None], bias, NEG_BIG)])


def _strips():
    return [slice(r * STRIP, (r + 1) * STRIP) for r in range(BLK // STRIP)]


def _bias_spec(transposed=False):
    tile = (2 * BLK, BLK) if transposed else (BLK, 2 * BLK)
    return pl.BlockSpec((None, N_Q_HEADS) + tile, lambda i: (jnp.minimum(i, 1), 0, 0, 0))


def _softmax_strip(s_ref, bias_ref, hq, rows, sink):
    sc = s_ref[hq, rows, :] + bias_ref[hq, rows, :]
    m = jnp.maximum(jnp.max(sc, axis=1, keepdims=True), sink)
    e = jnp.exp(sc - m)
    es = jnp.exp(sink - m)
    inv = 1.0 / (jnp.sum(e, axis=1, keepdims=True) + es)
    return e * inv, es * inv


def _kv_window(z_ref, kp_ref, vp_ref):
    kcat = jnp.concatenate([kp_ref[...], z_ref[:, ATTN_W:ATTN_W + KV_W]], axis=0).astype(F32)
    vcat = jnp.concatenate([vp_ref[...], z_ref[:, ATTN_W + KV_W:ATTN_W + 2 * KV_W]], axis=0).astype(F32)
    kswap = pltpu.roll(kcat, HEAD_DIM, 1)
    vswap = pltpu.roll(vcat, HEAD_DIM, 1)
    return kcat.astype(BF16), kswap.astype(BF16), vcat, vswap


def _gelu_and_grad(x):
    t = jnp.tanh(GELU_C0 * (x + GELU_C1 * x * x * x))
    return 0.5 * x * (1.0 + t), 0.5 * (1.0 + t) + 0.5 * x * (1.0 - t * t) * GELU_C0 * (1.0 + 3.0 * GELU_C1 * x * x)


def _layernorm_strip(gv, lng_ref, lnb_ref):
    xc = gv - jnp.mean(gv, axis=-1, keepdims=True)
    rstd = lax.rsqrt(jnp.mean(xc * xc, axis=-1, keepdims=True) + NORM_EPS)
    xhat = xc * rstd
    return xhat * lng_ref[...] + lnb_ref[...], xhat, rstd


def _tril_w(ws_ref, h):
    ti = lax.broadcasted_iota(jnp.int32, (BLK, BLK), 0)
    si = lax.broadcasted_iota(jnp.int32, (BLK, BLK), 1)
    causal = si <= ti
    return jnp.where(causal, ws_ref[h], 0.0).astype(BF16), causal


def _gm_mixed(vn, ws_ref, bs_ref, lo, hi):
    slabs = []
    for s in range(GM_W // LANES):
        vs = vn[:, s * LANES:(s + 1) * LANES]
        w0, _ = _tril_w(ws_ref, 2 * s)
        w1, _ = _tril_w(ws_ref, 2 * s + 1)
        mixed = (_mm(w0, jnp.where(lo, vs, 0.0).astype(BF16))
                 + _mm(w1, jnp.where(hi, vs, 0.0).astype(BF16))
                 + bs_ref[:, s * LANES:(s + 1) * LANES])
        slabs.append(mixed)
    return slabs


def _block_specs_z(nb):
    prev = lambda i: (jnp.maximum(i - 1, 0), ATTN_W // KV_W)
    prev_v = lambda i: (jnp.maximum(i - 1, 0), ATTN_W // KV_W + 1)
    return [_rows(BLK, D_IN), pl.BlockSpec((BLK, KV_W), prev), pl.BlockSpec((BLK, KV_W), prev_v)]


def _heads():
    return [(2 * s + half, s, half, s // 2 == half) for s in range(ATTN_W // LANES) for half in range(2)]


def _f2_attn_gm(z, sinks, bias, ln_g, ln_b, ws, bs_full):
    t = z.shape[0]
    nb = t // BLK

    def body(z_ref, kp_ref, vp_ref, sink_ref, bias_ref, lng_ref, lnb_ref, ws_ref, bs_ref, am_ref,
             s_ref, p_ref, u_ref, vn_ref):
        lane = lax.broadcasted_iota(jnp.int32, (1, LANES), 1)
        lo = lane < HEAD_DIM
        hi = lane >= HEAD_DIM
        kc, ks, vcat, vswap = _kv_window(z_ref, kp_ref, vp_ref)
        for hq, s, half, same in _heads():
            qs = z_ref[:, s * LANES:(s + 1) * LANES].astype(F32) * (HEAD_DIM ** -0.5)
            qm = jnp.where(lo if half == 0 else hi, qs, 0.0).astype(BF16)
            s_ref[hq] = _nt(qm, kc if same else ks)
        for hq in range(N_Q_HEADS):
            for rows in _strips():
                pr, _ = _softmax_strip(s_ref, bias_ref, hq, rows, sink_ref[hq])
                p_ref[hq, rows, :] = pr.astype(BF16)
        for s in range(ATTN_W // LANES):
            o = jnp.zeros((BLK, LANES), F32)
            for hq, hs, half, same in _heads():
                if hs == s:
                    vm = jnp.where(lo if half == 0 else hi, vcat if same else vswap, 0.0).astype(BF16)
                    o = o + _mm(p_ref[hq], vm)
            am_ref[:, s * LANES:(s + 1) * LANES] = o.astype(BF16)

        for rows in _strips():
            u_ref[rows, :] = _gelu(z_ref[rows, ATTN_W + 2 * KV_W:ATTN_W + 2 * KV_W + GM_W].astype(F32))
            vn_ref[rows, :], _, _ = _layernorm_strip(
                _gelu(z_ref[rows, ATTN_W + 2 * KV_W + GM_W:D_IN].astype(F32)), lng_ref, lnb_ref)
        mixed = _gm_mixed(vn_ref, ws_ref, bs_ref, lo, hi)
        for s in range(GM_W // LANES):
            am_ref[:, ATTN_W + s * LANES:ATTN_W + (s + 1) * LANES] = (
                u_ref[:, s * LANES:(s + 1) * LANES] * mixed[s]).astype(BF16)

    return pl.pallas_call(
        body, name="f2_attn_gm", grid=(nb,),
        in_specs=_block_specs_z(nb) + [
            pl.BlockSpec(memory_space=pltpu.SMEM), _bias_spec(), _whole((1, GM_W)),
            _whole((1, GM_W)), _whole((N_Q_HEADS, BLK, BLK)), _whole((BLK, GM_W))],
        out_specs=_rows(BLK, ATTN_W + GM_W),
        out_shape=_sds((t, ATTN_W + GM_W), BF16),
        scratch_shapes=[pltpu.VMEM((N_Q_HEADS, BLK, 2 * BLK), F32), pltpu.VMEM((N_Q_HEADS, BLK, 2 * BLK), BF16),
                        pltpu.VMEM((BLK, GM_W), F32), pltpu.VMEM((BLK, GM_W), F32)],
        compiler_params=_params())(z, z, z, sinks, bias, ln_g, ln_b, ws, bs_full)


def _f3_mix_out(am, h, ga, gg, w, gpost):
    t = h.shape[0]
    tm = _tile(t, 512)

    def body(am_ref, h_ref, ga_ref, gg_ref, w_ref, gp_ref, heads_ref, mix_ref, h1_ref):
        heads = jnp.concatenate([_rms(am_ref[:, :ATTN_W].astype(F32), ga_ref[...]),
                                 _rms(am_ref[:, ATTN_W:].astype(F32), gg_ref[...])], axis=1).astype(BF16)
        heads_ref[...] = heads
        mix = _mm(heads, w_ref[...])
        mix_ref[...] = mix.astype(BF16)
        h1_ref[...] = h_ref[...] + _rms(mix, gp_ref[...])

    return pl.pallas_call(
        body, name="f3_mix_out", grid=(t // tm,),
        in_specs=[_rows(tm, D_MODEL), _rows(tm, D_MODEL), _whole((1, ATTN_W)), _whole((1, GM_W)),
                  _whole((D_MODEL, D_MODEL)), _whole((1, D_MODEL))],
        out_specs=[_rows(tm, D_MODEL)] * 3,
        out_shape=[_sds((t, D_MODEL), BF16), _sds((t, D_MODEL), BF16), _sds((t, D_MODEL), F32)],
        compiler_params=_params())(am, h, ga, gg, w, gpost)


def _f4a_ffn_in(h1, gf, wg_t, wu_t):
    t = h1.shape[0]
    tm = _tile(t, 256)

    def body(h_ref, g_ref, wg_ref, wu_ref, f_ref, gt_ref, up_ref):
        f = _rms(h_ref[...], g_ref[...]).astype(BF16)
        f_ref[...] = f
        gt_ref[...] = _nt(f, wg_ref[...]).astype(BF16)
        up_ref[...] = _nt(f, wu_ref[...]).astype(BF16)

    return pl.pallas_call(
        body, name="f4a_ffn_in", grid=(t // tm,),
        in_specs=[_rows(tm, D_MODEL), _whole((1, D_MODEL)), _whole((D_FF, D_MODEL)), _whole((D_FF, D_MODEL))],
        out_specs=[_rows(tm, D_MODEL), _rows(tm, D_FF), _rows(tm, D_FF)],
        out_shape=[_sds((t, D_MODEL), BF16), _sds((t, D_FF), BF16), _sds((t, D_FF), BF16)],
        compiler_params=_params())(h1, gf, wg_t, wu_t)


def _f4b_ffn_out(gt, up, wd, h1, gfp):
    t = h1.shape[0]
    tm = _tile(t, 256)

    def body(gt_ref, up_ref, wd_ref, h_ref, g_ref, dn_ref, h2_ref):
        gt = gt_ref[...].astype(F32)
        act = (gt * _sigmoid(gt) * up_ref[...].astype(F32)).astype(BF16)
        dn = _mm(act, wd_ref[...])
        dn_ref[...] = dn.astype(BF16)
        h2_ref[...] = h_ref[...] + _rms(dn, g_ref[...])

    return pl.pallas_call(
        body, name="f4b_ffn_out", grid=(t // tm,),
        in_specs=[_rows(tm, D_FF), _rows(tm, D_FF), _whole((D_FF, D_MODEL)), _rows(tm, D_MODEL),
                  _whole((1, D_MODEL))],
        out_specs=[_rows(tm, D_MODEL)] * 2,
        out_shape=[_sds((t, D_MODEL), BF16), _sds((t, D_MODEL), F32)],
        compiler_params=_params())(gt, up, wd, h1, gfp)


def _f5_ple(h2, gpl, wpg, p, wple):
    t = h2.shape[0]
    tm = _tile(t, 512)

    def body(h_ref, g_ref, wpg_ref, p_ref, wple_ref, r_ref, pg_ref, pe_ref, h3_ref):
        h = h_ref[...]
        r = _rms(h, g_ref[...]).astype(BF16)
        r_ref[...] = r
        pg = _mm(r, wpg_ref[...])
        pe = _mm(p_ref[...].astype(BF16), wple_ref[...])
        pg_ref[...] = pg.astype(BF16)
        pe_ref[...] = pe.astype(BF16)
        h3_ref[...] = h + pe * _sigmoid(pg)

    return pl.pallas_call(
        body, name="f5_ple", grid=(t // tm,),
        in_specs=[_rows(tm, D_MODEL), _whole((1, D_MODEL)), _whole((D_MODEL, D_MODEL)), _rows(tm, PLE_DIM),
                  _whole((PLE_DIM, D_MODEL))],
        out_specs=[_rows(tm, D_MODEL)] * 4,
        out_shape=[_sds((t, D_MODEL), BF16)] * 3 + [_sds((t, D_MODEL), F32)],
        compiler_params=_params())(h2, gpl, wpg, p, wple)


def _loss_head(y, target):
    t = y.shape[0]
    tm = _tile(t, 512)

    def body(y_ref, t_ref, sq_ref, dy_ref):
        err = y_ref[...] - t_ref[...]
        dy_ref[...] = err * (1.0 / D_MODEL)
        _accumulate(sq_ref, jnp.sum(err * err, keepdims=True))

    return pl.pallas_call(
        body, name="loss_head", grid=(t // tm,),
        in_specs=[_rows(tm, D_MODEL)] * 2,
        out_specs=[_whole((1, LANES)), _rows(tm, D_MODEL)],
        out_shape=[_sds((1, LANES), F32), _sds((t, D_MODEL), F32)],
        compiler_params=_params())(y, target)


def _b5_ple(dh3, h2, pg, pe, gpl, wpg):
    t = h2.shape[0]
    tm = _tile(t, 512)

    def body(dh_ref, h_ref, pg_ref, pe_ref, g_ref, w_ref, dpe_ref, dpg_ref, dh2_ref, dg_ref):
        dh = dh_ref[...]
        s = _sigmoid(pg_ref[...].astype(F32))
        dpe_ref[...] = (dh * s).astype(BF16)
        dpg = (dh * pe_ref[...].astype(F32) * s * (1.0 - s)).astype(BF16)
        dpg_ref[...] = dpg
        dx, dg = _rms_bwd(_nt(dpg, w_ref[...]), h_ref[...], g_ref[...])
        dh2_ref[...] = dh + dx
        _accumulate(dg_ref, dg)

    return pl.pallas_call(
        body, name="b5_ple", grid=(t // tm,),
        in_specs=[_rows(tm, D_MODEL)] * 4 + [_whole((1, D_MODEL)), _whole((D_MODEL, D_MODEL))],
        out_specs=[_rows(tm, D_MODEL)] * 3 + [_whole((1, D_MODEL))],
        out_shape=[_sds((t, D_MODEL), BF16)] * 2 + [_sds((t, D_MODEL), F32), _sds((1, D_MODEL), F32)],
        compiler_params=_params())(dh3, h2, pg, pe, gpl, wpg)


def _b4a_ffn_out(dh2, dn, gfp, wd, gt, up):
    t = dh2.shape[0]
    tm = _tile(t, 256)

    def body(dh_ref, dn_ref, g_ref, wd_ref, gt_ref, up_ref, ddn_ref, act_ref, dgt_ref, dup_ref, dg_ref):
        ddn, dg = _rms_bwd(dh_ref[...], dn_ref[...].astype(F32), g_ref[...])
        _accumulate(dg_ref, dg)
        ddn = ddn.astype(BF16)
        ddn_ref[...] = ddn
        dact = _nt(ddn, wd_ref[...])
        gt = gt_ref[...].astype(F32)
        up = up_ref[...].astype(F32)
        sg = _sigmoid(gt)
        silu = gt * sg
        act_ref[...] = (silu * up).astype(BF16)
        dup_ref[...] = (dact * silu).astype(BF16)
        dgt_ref[...] = (dact * up * (sg * (1.0 + gt * (1.0 - sg)))).astype(BF16)

    return pl.pallas_call(
        body, name="b4a_ffn_out", grid=(t // tm,),
        in_specs=[_rows(tm, D_MODEL), _rows(tm, D_MODEL), _whole((1, D_MODEL)), _whole((D_FF, D_MODEL)),
                  _rows(tm, D_FF), _rows(tm, D_FF)],
        out_specs=[_rows(tm, D_MODEL), _rows(tm, D_FF), _rows(tm, D_FF), _rows(tm, D_FF), _whole((1, D_MODEL))],
        out_shape=[_sds((t, D_MODEL), BF16)] + [_sds((t, D_FF), BF16)] * 3 + [_sds((1, D_MODEL), F32)],
        compiler_params=_params())(dh2, dn, gfp, wd, gt, up)


def _b4b_ffn_in(dgt, dup, wg_t, wu_t, h1, gf, dh2):
    t = h1.shape[0]
    tm = _tile(t, 512)

    def body(dgt_ref, dup_ref, wg_ref, wu_ref, h_ref, g_ref, dh_ref, dh1_ref, dg_ref):
        df = _mm(dgt_ref[...], wg_ref[...]) + _mm(dup_ref[...], wu_ref[...])
        dx, dg = _rms_bwd(df, h_ref[...], g_ref[...])
        dh1_ref[...] = dh_ref[...] + dx
        _accumulate(dg_ref, dg)

    return pl.pallas_call(
        body, name="b4b_ffn_in", grid=(t // tm,),
        in_specs=[_rows(tm, D_FF), _rows(tm, D_FF), _whole((D_FF, D_MODEL)), _whole((D_FF, D_MODEL)),
                  _rows(tm, D_MODEL), _whole((1, D_MODEL)), _rows(tm, D_MODEL)],
        out_specs=[_rows(tm, D_MODEL), _whole((1, D_MODEL))],
        out_shape=[_sds((t, D_MODEL), F32), _sds((1, D_MODEL), F32)],
        compiler_params=_params())(dgt, dup, wg_t, wu_t, h1, gf, dh2)


def _b3_mix_out(dh1, mix, gpost, w, am, ga, gg):
    t = dh1.shape[0]
    tm = _tile(t, 512)

    def body(dh_ref, mix_ref, gp_ref, w_ref, am_ref, ga_ref, gg_ref, dmix_ref, dam_ref, dgp_ref, dga_ref, dgg_ref):
        dmix, dgp = _rms_bwd(dh_ref[...], mix_ref[...].astype(F32), gp_ref[...])
        _accumulate(dgp_ref, dgp)
        dmix = dmix.astype(BF16)
        dmix_ref[...] = dmix
        dheads = _nt(dmix, w_ref[...])
        dat, dga = _rms_bwd(dheads[:, :ATTN_W], am_ref[:, :ATTN_W].astype(F32), ga_ref[...])
        dgm, dgg = _rms_bwd(dheads[:, ATTN_W:], am_ref[:, ATTN_W:].astype(F32), gg_ref[...])
        dam_ref[:, :ATTN_W] = dat.astype(BF16)
        dam_ref[:, ATTN_W:] = dgm.astype(BF16)
        _accumulate(dga_ref, dga)
        _accumulate(dgg_ref, dgg)

    return pl.pallas_call(
        body, name="b3_mix_out", grid=(t // tm,),
        in_specs=[_rows(tm, D_MODEL), _rows(tm, D_MODEL), _whole((1, D_MODEL)), _whole((D_MODEL, D_MODEL)),
                  _rows(tm, D_MODEL), _whole((1, ATTN_W)), _whole((1, GM_W))],
        out_specs=[_rows(tm, D_MODEL), _rows(tm, D_MODEL), _whole((1, D_MODEL)), _whole((1, ATTN_W)),
                   _whole((1, GM_W))],
        out_shape=[_sds((t, D_MODEL), BF16), _sds((t, D_MODEL), BF16), _sds((1, D_MODEL), F32),
                   _sds((1, ATTN_W), F32), _sds((1, GM_W), F32)],
        compiler_params=_params())(dh1, mix, gpost, w, am, ga, gg)


def _b2_attn_gm(dam, z, sinks, bias, ln_g, ln_b, ws, bs_full):
    t = z.shape[0]
    nb = t // BLK

    def body(dam_ref, z_ref, kp_ref, vp_ref, sink_ref, bias_ref, lng_ref, lnb_ref, ws_ref, wst_ref, bs_ref,
             dzq_ref, dkv_ref, dzuv_ref, dsink_ref, dlng_ref, dlnb_ref, dws_ref, dbs_ref,
             sc_ref, dp_ref, p_ref, ds_ref, qcat_ref, docat_ref, dqt_ref,
             u_ref, du_ref, dv_ref, vn_ref, xhat_ref, rstd_ref, dvn_ref):
        n = pl.program_id(0)
        lane = lax.broadcasted_iota(jnp.int32, (1, LANES), 1)
        lo = lane < HEAD_DIM
        hi = lane >= HEAD_DIM
        sub = lax.broadcasted_iota(jnp.int32, (LANES, 1), 0)
        kc, ks, vcat, vswap = _kv_window(z_ref, kp_ref, vp_ref)
        vc = vcat.astype(BF16)
        vs_ = vswap.astype(BF16)
        kc_t = kc.T
        ks_t = ks.T

        order = [h for h, _, _, same in _heads() if same] + [h for h, _, _, same in _heads() if not same]
        place = {h: i for i, h in enumerate(order)}
        half_width = (N_Q_HEADS // 2) * BLK
        groups = ((slice(0, half_width), kc, vc, kc_t), (slice(half_width, 2 * half_width), ks, vs_, ks_t))
        for hq, s, half, _ in _heads():
            mask = lo if half == 0 else hi
            rows = slice(place[hq] * BLK, (place[hq] + 1) * BLK)
            qs = z_ref[:, s * LANES:(s + 1) * LANES].astype(F32) * (HEAD_DIM ** -0.5)
            qcat_ref[rows, :] = jnp.where(mask, qs, 0.0).astype(BF16)
            docat_ref[rows, :] = jnp.where(mask, dam_ref[:, s * LANES:(s + 1) * LANES], 0.0).astype(BF16)
        for cols, keys, values, _ in groups:
            sc_ref[:, cols] = _nt(keys, qcat_ref[cols, :])
            dp_ref[:, cols] = _nt(values, docat_ref[cols, :])
        dsink = jnp.zeros((1, LANES), F32)
        for hq in range(N_Q_HEADS):
            cols = slice(place[hq] * BLK, (place[hq] + 1) * BLK)
            sink = sink_ref[hq]
            sc = sc_ref[:, cols] + bias_ref[hq]
            m = jnp.maximum(jnp.max(sc, axis=0, keepdims=True), sink)
            e = jnp.exp(sc - m)
            es = jnp.exp(sink - m)
            inv = 1.0 / (jnp.sum(e, axis=0, keepdims=True) + es)
            pr = e * inv
            dpr = dp_ref[:, cols]
            row = jnp.sum(dpr * pr, axis=0, keepdims=True)
            p_ref[:, cols] = pr.astype(BF16)
            ds_ref[:, cols] = (pr * (dpr - row)).astype(BF16)
            dsink = dsink + jnp.where(lane == hq, -jnp.sum(es * inv * row, keepdims=True), 0.0)
        dk_parts, dv_parts = [], []
        for cols, _, _, keys_t in groups:
            dk_parts.append(_mm(ds_ref[:, cols], qcat_ref[cols, :]))
            dv_parts.append(_mm(p_ref[:, cols], docat_ref[cols, :]))
            dqt_ref[:, cols] = _mm(keys_t, ds_ref[:, cols])
        dk_acc = dk_parts[0] + pltpu.roll(dk_parts[1], HEAD_DIM, 1)
        dv_acc = dv_parts[0] + pltpu.roll(dv_parts[1], HEAD_DIM, 1)
        for s in range(ATTN_W // LANES):
            first, second = (slice(place[h] * BLK, (place[h] + 1) * BLK) for h in (2 * s, 2 * s + 1))
            dq_t = jnp.where(sub < HEAD_DIM, dqt_ref[:, first], dqt_ref[:, second])
            dzq_ref[:, s * LANES:(s + 1) * LANES] = (dq_t.T * (HEAD_DIM ** -0.5)).astype(BF16)
        cur = pl.multiple_of(n * BLK, BLK)
        dkv_ref[pl.ds(cur, BLK), 0:KV_W] = dk_acc[BLK:, :]
        dkv_ref[pl.ds(cur, BLK), KV_W:2 * KV_W] = dv_acc[BLK:, :]

        @pl.when(n > 0)
        def _():
            prv = pl.multiple_of((n - 1) * BLK, BLK)
            dkv_ref[pl.ds(prv, BLK), 0:KV_W] += dk_acc[:BLK, :]
            dkv_ref[pl.ds(prv, BLK), KV_W:2 * KV_W] += dv_acc[:BLK, :]

        _accumulate(dsink_ref, dsink)

        for rows in _strips():
            u_ref[rows, :], du_ref[rows, :] = _gelu_and_grad(
                z_ref[rows, ATTN_W + 2 * KV_W:ATTN_W + 2 * KV_W + GM_W].astype(F32))
            gv, dv_ref[rows, :] = _gelu_and_grad(z_ref[rows, ATTN_W + 2 * KV_W + GM_W:D_IN].astype(F32))
            vn_ref[rows, :], xhat_ref[rows, :], rstd = _layernorm_strip(gv, lng_ref, lnb_ref)
            rstd_ref[rows, :] = jnp.broadcast_to(rstd, (STRIP, LANES))
        mixed = _gm_mixed(vn_ref, ws_ref, bs_ref, lo, hi)

        @pl.when(n == 0)
        def _():
            dws_ref[...] = jnp.zeros_like(dws_ref)

        dbs = jnp.zeros((BLK, LANES), F32)
        for s in range(GM_W // LANES):
            slab = slice(s * LANES, (s + 1) * LANES)
            dgm = dam_ref[:, ATTN_W + s * LANES:ATTN_W + (s + 1) * LANES]
            dzuv_ref[:, slab] = (dgm * mixed[s] * du_ref[:, slab]).astype(BF16)
            dmx = dgm * u_ref[:, slab]
            vsb = vn_ref[:, slab].astype(BF16)
            dvn = jnp.zeros((BLK, LANES), F32)
            for half in range(2):
                h = 2 * s + half
                mask = lo if half == 0 else hi
                dmm = jnp.where(mask, dmx, 0.0)
                dbs = dbs + jnp.where(lane == h, jnp.sum(dmm, axis=1, keepdims=True), 0.0)
                dmm = dmm.astype(BF16)
                _, causal = _tril_w(ws_ref, h)
                ti = lax.broadcasted_iota(jnp.int32, (BLK, BLK), 0)
                si = lax.broadcasted_iota(jnp.int32, (BLK, BLK), 1)
                wt_t = jnp.where(ti <= si, wst_ref[h], 0.0).astype(BF16)
                dvn = dvn + jnp.where(mask, _mm(wt_t, dmm), 0.0)
                dws_ref[h] += jnp.where(causal, _nt(dmm, vsb), 0.0)
            dvn_ref[:, slab] = dvn
        _accumulate(dbs_ref, dbs)
        dlnb = jnp.zeros((STRIP, GM_W), F32)
        dlng = jnp.zeros((STRIP, GM_W), F32)
        for rows in _strips():
            dvn = dvn_ref[rows, :]
            xhat = xhat_ref[rows, :]
            dlnb = dlnb + dvn
            dlng = dlng + dvn * xhat
            dxh = dvn * lng_ref[...]
            dgv = rstd_ref[rows, 0:1] * (dxh - jnp.mean(dxh, axis=-1, keepdims=True)
                                         - xhat * jnp.mean(dxh * xhat, axis=-1, keepdims=True))
            dzuv_ref[rows, GM_W:] = (dgv * dv_ref[rows, :]).astype(BF16)
        _accumulate(dlnb_ref, jnp.sum(dlnb, axis=0, keepdims=True))
        _accumulate(dlng_ref, jnp.sum(dlng, axis=0, keepdims=True))

    return pl.pallas_call(
        body, name="b2_attn_gm", grid=(nb,),
        in_specs=[_rows(BLK, ATTN_W + GM_W)] + _block_specs_z(nb) + [
            pl.BlockSpec(memory_space=pltpu.SMEM), _bias_spec(transposed=True), _whole((1, GM_W)),
            _whole((1, GM_W)), _whole((N_Q_HEADS, BLK, BLK)), _whole((N_Q_HEADS, BLK, BLK)), _whole((BLK, GM_W))],
        out_specs=[_rows(BLK, ATTN_W), _whole((t, 2 * KV_W)), _rows(BLK, 2 * GM_W), _whole((1, LANES)),
                   _whole((1, GM_W)), _whole((1, GM_W)), _whole((N_Q_HEADS, BLK, BLK)), _whole((BLK, LANES))],
        out_shape=[_sds((t, ATTN_W), BF16), _sds((t, 2 * KV_W), F32), _sds((t, 2 * GM_W), BF16),
                   _sds((1, LANES), F32), _sds((1, GM_W), F32), _sds((1, GM_W), F32),
                   _sds((N_Q_HEADS, BLK, BLK), F32), _sds((BLK, LANES), F32)],
        scratch_shapes=[pltpu.VMEM((2 * BLK, N_Q_HEADS * BLK), F32)] * 2
        + [pltpu.VMEM((2 * BLK, N_Q_HEADS * BLK), BF16)] * 2 + [pltpu.VMEM((N_Q_HEADS * BLK, LANES), BF16)] * 2
        + [pltpu.VMEM((LANES, N_Q_HEADS * BLK), F32)] + [pltpu.VMEM((BLK, GM_W), F32)] * 5
        + [pltpu.VMEM((BLK, LANES), F32), pltpu.VMEM((BLK, GM_W), F32)],
        compiler_params=_params())(dam, z, z, z, sinks, jnp.swapaxes(bias, 2, 3), ln_g, ln_b, ws,
                                   jnp.swapaxes(ws, 1, 2), bs_full)


def _b1_norm_in(dzq, dkv, dzuv, w_t, h, g, dh1):
    t = h.shape[0]
    tm = _tile(t, 512)

    def body(dzq_ref, dkv_ref, dzuv_ref, w_ref, h_ref, g_ref, dh_ref, dz_ref, dh0_ref, dg_ref):
        dz = jnp.concatenate([dzq_ref[...], dkv_ref[...].astype(BF16), dzuv_ref[...]], axis=1)
        dz_ref[...] = dz
        dx, dg = _rms_bwd(_mm(dz, w_ref[...]), h_ref[...], g_ref[...])
        dh0_ref[...] = dh_ref[...] + dx
        _accumulate(dg_ref, dg)

    return pl.pallas_call(
        body, name="b1_norm_in", grid=(t // tm,),
        in_specs=[_rows(tm, ATTN_W), _rows(tm, 2 * KV_W), _rows(tm, 2 * GM_W), _whole((D_IN, D_MODEL)),
                  _rows(tm, D_MODEL), _whole((1, D_MODEL)), _rows(tm, D_MODEL)],
        out_specs=[_rows(tm, D_IN), _rows(tm, D_MODEL), _whole((1, D_MODEL))],
        out_shape=[_sds((t, D_IN), BF16), _sds((t, D_MODEL), F32), _sds((1, D_MODEL), F32)],
        compiler_params=_params())(dzq, dkv, dzuv, w_t, h, g, dh1)


def _weight_grad(x, dy, name):
    t, k = x.shape
    n = dy.shape[1]
    tm = _tile(t, 1024)
    steps = t // tm

    def body(x_ref, dy_ref, dw_ref, acc_ref):
        i = pl.program_id(0)

        @pl.when(i == 0)
        def _():
            acc_ref[...] = _tn(x_ref[...].astype(BF16), dy_ref[...])

        @pl.when(i > 0)
        def _():
            acc_ref[...] += _tn(x_ref[...].astype(BF16), dy_ref[...])

        @pl.when(i == steps - 1)
        def _():
            dw_ref[...] = acc_ref[...].astype(BF16)

    return pl.pallas_call(
        body, name=name, grid=(steps,),
        in_specs=[_rows(tm, k), _rows(tm, n)],
        out_specs=_whole((k, n)),
        out_shape=_sds((k, n), BF16),
        scratch_shapes=[pltpu.VMEM((k, n), F32)],
        compiler_params=_params())(x, dy)


_ANY = pl.BlockSpec(memory_space=pl.ANY)
_HBM = pl.BlockSpec(memory_space=pltpu.HBM)
_SEM = pl.BlockSpec(memory_space=pltpu.SEMAPHORE)
_EFFECT = pltpu.SideEffectType.DATAFLOW_SIDE_EFFECTING


def _mesh_pos():
    return lax.axis_index("x"), lax.axis_index("y"), lax.axis_index("c")


def _other_chips(x, y):
    return [(1 - x, y), (x, 1 - y), (1 - x, 1 - y)]


def _peers(mode, x, y, c):
    if mode == "devices":
        peers = []
        for j in range(1, N_DEV):
            px = 1 - x if (j >> 2) & 1 else x
            py = 1 - y if (j >> 1) & 1 else y
            pc = 1 - c if j & 1 else c
            peers.append(((px, py, pc), 4 * px + 2 * py + pc))
        return peers, 4 * x + 2 * y + c
    if mode == "sibling":
        return [((x, y, 1 - c), None)], None
    return [((px, py, c), 2 * px + py) for px, py in _other_chips(x, y)], 2 * x + y


def _descriptors(mode, srcs, lands, send_sems, recv_sems, with_incoming=True):
    x, y, c = _mesh_pos()
    peers, me = _peers(mode, x, y, c)
    scatter = mode == "scatter"
    outgoing, incoming = [], []
    for i, (src, land) in enumerate(zip(srcs, lands)):
        for j, (dev, slot) in enumerate(peers):
            sem = i * len(peers) + j
            common = dict(send_sem=send_sems.at[sem], recv_sem=recv_sems.at[sem], device_id=dev,
                          device_id_type=pl.DeviceIdType.MESH)
            whole = mode == "sibling"
            outgoing.append(pltpu.make_async_remote_copy(
                src_ref=src.at[slot] if scatter else src, dst_ref=land if whole else land.at[me], **common))
            if with_incoming:
                incoming.append(pltpu.make_async_remote_copy(
                    src_ref=src.at[me] if scatter else src, dst_ref=land if whole else land.at[slot], **common))
    return outgoing, incoming


def _n_sems(mode, k):
    return k * {"devices": N_DEV - 1, "sibling": 1}.get(mode, N_CHIPS - 1)


def _exchange_start(srcs, lands, mode, name, after=None):
    k = len(srcs)
    arrs = [*srcs, *lands]

    def body(*refs):
        skip = 1 if after is not None else 0
        send_sems, recv_sems = refs[2 * k + skip], refs[2 * k + skip + 1]
        outgoing, _ = _descriptors(mode, refs[:k], refs[k:2 * k], send_sems, recv_sems, with_incoming=False)
        for cp in outgoing:
            cp.start()
        refs[-1][...] = jnp.zeros_like(refs[-1])

    operands = [pltpu.with_memory_space_constraint(a, pltpu.HBM) for a in arrs]
    in_specs = [_HBM] * (2 * k)
    if after is not None:
        operands.append(after)
        in_specs.append(_ANY)
    sems = pltpu.SemaphoreType.DMA((_n_sems(mode, k),))
    res = pl.pallas_call(
        body, name=name, in_specs=in_specs,
        out_shape=(sems, sems, *[pltpu.HBM(a.shape, a.dtype) for a in arrs], _sds((8, LANES), F32)),
        out_specs=(_SEM, _SEM, *[_HBM] * (2 * k), pl.BlockSpec(memory_space=pltpu.VMEM)),
        input_output_aliases={i: 2 + i for i in range(2 * k)},
        compiler_params=pltpu.CompilerParams(has_side_effects=_EFFECT))(*operands)
    return dict(mode=mode, send=res[0], recv=res[1], srcs=res[2:2 + k], lands=res[2 + k:2 + 2 * k], token=res[-1])


def _exchange_wait(flight, name, after):
    mode, k = flight["mode"], len(flight["srcs"])
    arrs = [*flight["srcs"], *flight["lands"]]

    def body(*refs):
        outgoing, incoming = _descriptors(mode, refs[:k], refs[k:2 * k], refs[2 * k], refs[2 * k + 1])
        for cp in outgoing:
            cp.wait_send()
        for cp in incoming:
            cp.wait_recv()

    res = pl.pallas_call(
        body, name=name, in_specs=[_HBM] * (2 * k) + [_SEM, _SEM, _ANY],
        out_shape=tuple(pltpu.HBM(a.shape, a.dtype) for a in arrs), out_specs=tuple([_HBM] * (2 * k)),
        input_output_aliases={i: i for i in range(2 * k)},
        compiler_params=pltpu.CompilerParams(has_side_effects=_EFFECT))(*arrs, flight["send"], flight["recv"], after)
    return res[:k], res[k:]


def _adamw(w, g, m, v):
    m = ADAM_B1 * m + (1.0 - ADAM_B1) * g
    v = ADAM_B2 * v + (1.0 - ADAM_B2) * (g * g)
    m_hat = m / (1.0 - ADAM_B1 ** ADAM_STEP)
    v_hat = v / (1.0 - ADAM_B2 ** ADAM_STEP)
    delta = -ADAM_LR * (m_hat / (jnp.sqrt(v_hat) + ADAM_EPS) + ADAM_WD * w)
    return delta, m, v


def _row_tile(rows, cols, n_arrays):
    budget = VMEM_LIMIT_BYTES // 2
    padded = -(-cols // LANES) * LANES
    cap = min(rows, max(16, budget // (2 * n_arrays * padded * 4)))
    return max(tr for tr in range(16, cap + 1, 16) if rows % tr == 0)


def _sum_chips(landing, name):
    _, r, c = landing.shape
    tr = _row_tile(r, c, 4)

    def body(l_ref, s_ref):
        acc = l_ref[0].astype(F32)
        for s in range(1, N_CHIPS):
            acc = acc + l_ref[s].astype(F32)
        s_ref[...] = acc

    return pl.pallas_call(
        body, name=name, grid=(r // tr,),
        in_specs=[pl.BlockSpec((N_CHIPS, tr, c), lambda i: (0, i, 0))],
        out_specs=_rows(tr, c), out_shape=_sds((r, c), F32),
        compiler_params=_params())(landing)


def _adamw_big(mine, sibling, w, m, v, prev, layer, name):
    _, r, c = w.shape
    tr = _row_tile(r, c, 9)
    stacked = pl.BlockSpec((None, tr, c), lambda i: (layer, i, 0))

    def body(a_ref, b_ref, w_ref, m_ref, v_ref, *rest):
        g_out, d_out, m_out, v_out = rest[-4:]
        g = a_ref[...] + b_ref[...]
        g_out[...] = g
        d_out[...], m_out[...], v_out[...] = _adamw(w_ref[...], g, m_ref[...], v_ref[...])

    prev = list(prev) if prev is not None else []
    return pl.pallas_call(
        body, name=name, grid=(r // tr,),
        in_specs=[_rows(tr, c)] * 2 + [stacked] * 3 + [_ANY] * len(prev),
        out_specs=[stacked] * 4, out_shape=[_sds(w.shape, F32)] * 4,
        input_output_aliases={5 + j: j for j in range(len(prev))},
        compiler_params=_params())(mine, sibling, w, m, v, *prev)


def _adamw_small(gathered, w, m, v):
    r = w.shape[0]

    def body(a_ref, w_ref, m_ref, v_ref, g_out, d_out, m_out, v_out):
        g = a_ref[0]
        for d in range(1, N_DEV):
            g = g + a_ref[d]
        g_out[...] = g
        d_out[...], m_out[...], v_out[...] = _adamw(w_ref[...], g, m_ref[...], v_ref[...])

    return pl.pallas_call(
        body, name="adamw_small", grid=(1,),
        in_specs=[_whole((N_DEV, r, LANES))] + [_whole((r, LANES))] * 3,
        out_specs=[_whole((r, LANES))] * 4, out_shape=[_sds((r, LANES), F32)] * 4,
        compiler_params=_params())(gathered, w, m, v)


def _shard_view(name, stacked):
    return jnp.swapaxes(stacked, 1, 2) if name in TRANSPOSED else stacked


def _row_split(name):
    return name in TRANSPOSED or BIG[name] == 1


def _assemble(name, landed):
    _, r, c = landed.shape
    if _row_split(name):
        return landed.reshape(N_CHIPS * r, c)
    return landed.transpose(1, 0, 2).reshape(r, N_CHIPS * c)


def _split(name, whole):
    r, c = whole.shape
    if _row_split(name):
        return whole.reshape(N_CHIPS, r // N_CHIPS, c)
    return whole.reshape(r, N_CHIPS, c // N_CHIPS).transpose(1, 0, 2)


def _with_own_slot(landed, own, slot):
    return lax.dynamic_update_index_in_dim(landed, own, slot, 0)


def _pack_small(params, names):
    pieces = []
    for name in names:
        flat = params[name].reshape(-1)
        pieces.append(jnp.pad(flat, (0, -flat.shape[0] % SMALL_PAD)))
    return jnp.concatenate(pieces).reshape(-1, LANES)


def _unpack_small(packed, like, names):
    flat = packed.reshape(-1)
    out, off = {}, 0
    for name in names:
        size = like[name].size
        out[name] = flat[off:off + size].reshape(like[name].shape)
        off += size + (-size % SMALL_PAD)
    return out


def kernel(x, p, ln_mix_pre, w_in, attn_sinks, gm_ln_g, gm_ln_b, gm_ws, gm_bs, g_attn_out, g_gm_out, w_out, ln_mix_post, ln_ffn_pre, w_ffn_gate, w_ffn_up, w_ffn_down, ln_ffn_post, w_ple, ln_ple_gate, w_ple_gate, loss_target, m_ln_mix_pre, m_w_in, m_attn_sinks, m_gm_ln_g, m_gm_ln_b, m_gm_ws, m_gm_bs, m_g_attn_out, m_g_gm_out, m_w_out, m_ln_mix_post, m_ln_ffn_pre, m_w_ffn_gate, m_w_ffn_up, m_w_ffn_down, m_ln_ffn_post, m_w_ple, m_ln_ple_gate, m_w_ple_gate, v_ln_mix_pre, v_w_in, v_attn_sinks, v_gm_ln_g, v_gm_ln_b, v_gm_ws, v_gm_bs, v_g_attn_out, v_g_gm_out, v_w_out, v_ln_mix_post, v_ln_ffn_pre, v_w_ffn_gate, v_w_ffn_up, v_w_ffn_down, v_ln_ffn_post, v_w_ple, v_ln_ple_gate, v_w_ple_gate):
    given = dict(locals())
    wts = {n: given[n] for n in WEIGHTS}
    mom = {n: given["m_" + n] for n in WEIGHTS}
    var = {n: given["v_" + n] for n in WEIGHTS}
    depth = w_in.shape[0]
    h = x[0]
    target = loss_target[0]
    chip = 2 * lax.axis_index("x") + lax.axis_index("y")
    device = 2 * chip + lax.axis_index("c")
    row = lambda a, i: a[i][None, :]
    bs_full = [jnp.repeat(gm_bs[i].T, HEAD_DIM, axis=1) for i in range(depth)]
    bias = _alibi_bias()
    kinds = ("grad", "delta", "m", "v")
    wview = {n: _shard_view(n, wts[n]) for n in BIG_NAMES}
    mview = {n: _shard_view(n, mom[n]) for n in BIG_NAMES}
    vview = {n: _shard_view(n, var[n]) for n in BIG_NAMES}

    zero = lambda flight: flight["token"][0:1, 0:1]

    def start_gather(i, names, after, tag):
        shards = [wview[n][i].astype(BF16) for n in names]
        lands = [lax.empty((N_CHIPS,) + s.shape, BF16) for s in shards]
        return _exchange_start(shards, lands, "gather", f"gather_weights_start_{i}{tag}", after)

    def finish_gather(flight, names, i, after, tag):
        shards, lands = _exchange_wait(flight, f"gather_weights_wait_{i}{tag}", after)
        return {n: _assemble(n, _with_own_slot(l, s, chip)) for n, s, l in zip(names, shards, lands)}

    first, after = [], None
    for k, names in enumerate(GATHER_GROUPS):
        first.append(start_gather(0, names, after, "abc"[k]))
        after = first[-1]["token"]
    full = [finish_gather(first[0], GATHER_GROUPS[0], 0, after, "a")] + [None] * (depth - 1)
    saved = []
    for i in range(depth):
        w = full[i]
        g_in = row(ln_mix_pre, i)
        if i + 1 < depth:
            flight = start_gather(i + 1, BIG_NAMES, w['w_in'], "")
            g_in = g_in + zero(flight)
        z, a = _f1_norm_in(h, g_in, w['w_in'])
        am = _f2_attn_gm(z, attn_sinks[i], bias, row(gm_ln_g, i), row(gm_ln_b, i), gm_ws[i], bs_full[i])
        if i == 0:
            w.update(finish_gather(first[1], GATHER_GROUPS[1], 0, am, "b"))
        heads, mix, h1 = _f3_mix_out(am, h, row(g_attn_out, i), row(g_gm_out, i), w['w_out'], row(ln_mix_post, i))
        f, gt, up = _f4a_ffn_in(h1, row(ln_ffn_pre, i), w['w_ffn_gate'], w['w_ffn_up'])
        if i == 0:
            w.update(finish_gather(first[2], GATHER_GROUPS[2], 0, gt, "c"))
        dn, h2 = _f4b_ffn_out(gt, up, w['w_ffn_down'], h1, row(ln_ffn_post, i))
        r, pg, pe, h3 = _f5_ple(h2, row(ln_ple_gate, i), w['w_ple_gate'], p[i, 0], w['w_ple'])
        saved.append(dict(h=h, z=z, a=a, am=am, heads=heads, mix=mix, h1=h1, f=f, gt=gt, up=up, dn=dn, h2=h2,
                          r=r, pg=pg, pe=pe))
        h = h3
        if i + 1 < depth:
            full[i + 1] = finish_gather(flight, BIG_NAMES, i + 1, h3, "")

    sq, dh = _loss_head(h, target)
    loss = lax.psum(0.5 / D_MODEL * sq[0, 0], ("x", "y", "c"))

    chain = {n: None for n in BIG_NAMES}
    owed = []
    small_out = [{k: {} for k in kinds} for _ in range(depth)]

    def start_scatter(i, names, dws, after, tag):
        parts = [_split(n, dws[n]) for n in names]
        lands = [lax.empty(q.shape, BF16) for q in parts]
        return _exchange_start(parts, lands, "scatter", f"scatter_grads_start_{i}{tag}", after)

    def finish_scatter(i, names, flight, after, tag):
        parts, lands = _exchange_wait(flight, f"scatter_grads_wait_{i}{tag}", after)
        lands = [_with_own_slot(l, lax.dynamic_index_in_dim(q, chip, 0, keepdims=False), chip)
                 for q, l in zip(parts, lands)]
        partial = [_sum_chips(l, "sum_chips_" + n) for n, l in zip(names, lands)]
        flight = _exchange_start(partial, [lax.empty(q.shape, F32) for q in partial], "sibling",
                                 f"sibling_grads_start_{i}{tag}", after)

        def update(later):
            mine, theirs = _exchange_wait(flight, f"sibling_grads_wait_{i}{tag}", later)
            for n, a, b in zip(names, mine, theirs):
                chain[n] = _adamw_big(a, b, wview[n], mview[n], vview[n], chain[n], i, "adamw_" + n)

        owed.append(update)
        return flight["token"]

    def settle(later):
        while owed:
            owed.pop(0)(later)

    def start_small(i, names, small, after, tag):
        packed = _pack_small(small, names)
        land = lax.empty((N_DEV,) + packed.shape, F32)
        return _exchange_start([packed], [land], "devices", f"gather_small_grads_start_{i}{tag}", after)

    def finish_small(i, names, flight, after, tag):
        (packed,), (gathered,) = _exchange_wait(flight, f"gather_small_grads_wait_{i}{tag}", after)
        gathered = _with_own_slot(gathered, packed, device)
        layer = lambda d: _pack_small({n: d[n][i] for n in names}, names)
        res = _adamw_small(gathered, layer(wts), layer(mom), layer(var))
        for k, a in zip(kinds, res):
            small_out[i][k].update(_unpack_small(a, {n: wts[n][i] for n in names}, names))
        return gathered

    pending, done, behind = [], None, None
    for i in reversed(range(depth)):
        s, w = saved[i], full[i]
        last = i == 0
        dws, small = {}, {}
        gain = row(ln_ple_gate, i)
        if behind is not None:
            gain = gain + behind
        dpe, dpg, dh2, dg = _b5_ple(dh, s['h2'], s['pg'], s['pe'], gain, w['w_ple_gate'])
        small['ln_ple_gate'] = dg[0]
        dws['w_ple'] = _weight_grad(p[i, 0], dpe, "dw_ple")
        dws['w_ple_gate'] = _weight_grad(s['r'], dpg, "dw_ple_gate")

        ddn, act, dgt, dup, dg = _b4a_ffn_out(dh2, s['dn'], row(ln_ffn_post, i), w['w_ffn_down'], s['gt'], s['up'])
        small['ln_ffn_post'] = dg[0]
        dws['w_ffn_down'] = _weight_grad(act, ddn, "dw_ffn_down")
        dws['w_ffn_gate'] = _weight_grad(dgt, s['f'], "dw_ffn_gate")
        dws['w_ffn_up'] = _weight_grad(dup, s['f'], "dw_ffn_up")
        gain = row(ln_ffn_pre, i)
        if last:
            flight_a = start_scatter(i, SCATTER_GROUPS[0], dws, None, "a")
            gain = gain + zero(flight_a)
        dh1, dg = _b4b_ffn_in(dgt, dup, w['w_ffn_gate'], w['w_ffn_up'], s['h1'], gain, dh2)
        small['ln_ffn_pre'] = dg[0]

        dmix, dam, dgp, dga, dgg = _b3_mix_out(dh1, s['mix'], row(ln_mix_post, i), w['w_out'], s['am'],
                                               row(g_attn_out, i), row(g_gm_out, i))
        small['ln_mix_post'] = dgp[0]
        small['g_attn_out'] = dga[0]
        small['g_gm_out'] = dgg[0]
        dws['w_out'] = _weight_grad(s['heads'], dmix, "dw_out")
        gain = row(gm_ln_g, i)
        if last:
            flight_b = start_scatter(i, SCATTER_GROUPS[1], dws, flight_a["token"], "b")
            gain = gain + zero(flight_b)

        dzq, dkv, dzuv, dsink, dlng, dlnb, dgws, dbs = _b2_attn_gm(
            dam, s['z'], attn_sinks[i], bias, gain, row(gm_ln_b, i), gm_ws[i], bs_full[i])
        small['attn_sinks'] = dsink[0, :N_Q_HEADS]
        small['gm_ln_g'] = dlng[0]
        small['gm_ln_b'] = dlnb[0]
        small['gm_ws'] = dgws
        small['gm_bs'] = dbs[:, :N_Q_HEADS].T
        gain = row(ln_mix_pre, i)
        if last:
            flight_s = start_small(i, SMALL_EARLY, small, flight_b["token"], "a")
            gain = gain + zero(flight_s)

        dz, dh, dg = _b1_norm_in(dzq, dkv, dzuv, w['w_in'], s['h'], gain, dh1)
        small['ln_mix_pre'] = dg[0]
        settle(dh)
        for finish in pending:
            done = finish(dh)
        pending = []
        if last:
            done = finish_scatter(i, SCATTER_GROUPS[0], flight_a, dz, "a")
            done = finish_scatter(i, SCATTER_GROUPS[1], flight_b, done, "b")
        dws['w_in'] = _weight_grad(dz, s['a'], "dw_in")
        if last:
            flight_c = start_scatter(i, SCATTER_GROUPS[2], dws, done, "c")
            flight_t = start_small(i, SMALL_LATE, small, flight_c["token"], "b")
            finish_small(i, SMALL_EARLY, flight_s, flight_t["token"], "a")
            done = finish_scatter(i, SCATTER_GROUPS[2], flight_c, flight_t["token"], "c")
            finish_small(i, SMALL_LATE, flight_t, flight_t["token"], "b")
            settle(done)
        else:
            flight_c = start_scatter(i, BIG_NAMES, dws, done, "")
            flight_s = start_small(i, SMALL_NAMES, small, flight_c["token"], "")
            behind = zero(flight_s)
            pending = [functools.partial(finish_small, i, SMALL_NAMES, flight_s, tag=""),
                       functools.partial(finish_scatter, i, BIG_NAMES, flight_c, tag="")]
    grad_x = dh[None]

    out = {k: {n: _shard_view(n, chain[n][j]) for n in BIG_NAMES} for j, k in enumerate(kinds)}
    for k in kinds:
        out[k].update({n: jnp.stack([small_out[i][k][n] for i in range(depth)]) for n in SMALL_NAMES})

    return (loss, grad_x, *[out["grad"][n] for n in WEIGHTS], *[out["delta"][n] for n in WEIGHTS],
            *[out["m"][n] for n in WEIGHTS], *[out["v"][n] for n in WEIGHTS])
```

```python
import functools
import math

import jax
import jax.numpy as jnp
from jax import lax
from jax.experimental import pallas as pl
from jax.experimental.pallas import tpu as pltpu

F32 = jnp.float32
BF16 = jnp.bfloat16

D_MODEL = 1024
HEAD_DIM = 64
N_Q_HEADS = 8
BLK = 128
ATTN_W = 512
KV_W = 128
GM_W = 512
D_IN = ATTN_W + 2 * KV_W + 2 * GM_W
D_FF = 2816
PLE_DIM = 256
DEPTH = 4
NORM_EPS = 1e-6
NEG_BIG = -1e30
N_CHIPS = 4
N_DEV = 8

ADAM_LR = 0.001
ADAM_B1 = 0.9
ADAM_B2 = 0.999
ADAM_EPS = 1e-08
ADAM_WD = 0.01
ADAM_STEP = 10

VMEM_LIMIT_BYTES = 56 * 1024 * 1024
LANES = 128
STRIP = 16
GELU_C0 = math.sqrt(2.0 / math.pi)
GELU_C1 = 0.044715
ALIBI_SLOPES = tuple(2.0 ** (-8.0 * (h + 1.0) / N_Q_HEADS) for h in range(N_Q_HEADS))

WEIGHTS = ['ln_mix_pre', 'w_in', 'attn_sinks', 'gm_ln_g', 'gm_ln_b', 'gm_ws', 'gm_bs', 'g_attn_out',
           'g_gm_out', 'w_out', 'ln_mix_post', 'ln_ffn_pre', 'w_ffn_gate', 'w_ffn_up', 'w_ffn_down',
           'ln_ffn_post', 'w_ple', 'ln_ple_gate', 'w_ple_gate']
BIG = {'w_in': 2, 'w_out': 1, 'w_ffn_gate': 2, 'w_ffn_up': 2, 'w_ffn_down': 1, 'w_ple': 2, 'w_ple_gate': 1}
BIG_NAMES = list(BIG)
TRANSPOSED = ('w_in', 'w_ffn_gate', 'w_ffn_up')
SMALL_NAMES = [n for n in WEIGHTS if n not in BIG]
SMALL_PAD = 1024
GATHER_GROUPS = (('w_in',), ('w_out', 'w_ffn_gate', 'w_ffn_up'), ('w_ffn_down', 'w_ple_gate', 'w_ple'))
SCATTER_GROUPS = (('w_ple', 'w_ple_gate', 'w_ffn_down', 'w_ffn_gate', 'w_ffn_up'), ('w_out',), ('w_in',))
SMALL_LATE = ('ln_mix_pre',)
SMALL_EARLY = tuple(n for n in SMALL_NAMES if n not in SMALL_LATE)


def _nt(a, b):
    return lax.dot_general(a, b, (((1,), (1,)), ((), ())), preferred_element_type=F32)


def _tn(a, b):
    return lax.dot_general(a, b, (((0,), (0,)), ((), ())), preferred_element_type=F32)


def _mm(a, b):
    return jnp.dot(a, b, preferred_element_type=F32)


def _rms(x, g):
    r = lax.rsqrt(jnp.mean(x * x, axis=-1, keepdims=True) + NORM_EPS)
    return x * r * g


def _rms_bwd(dy, x, g):
    r = lax.rsqrt(jnp.mean(x * x, axis=-1, keepdims=True) + NORM_EPS)
    xh = x * r
    dg = jnp.sum(dy * xh, axis=0, keepdims=True)
    dxh = dy * g
    dx = r * (dxh - xh * jnp.mean(dxh * xh, axis=-1, keepdims=True))
    return dx, dg


def _gelu(x):
    return 0.5 * x * (1.0 + jnp.tanh(GELU_C0 * (x + GELU_C1 * x * x * x)))


def _sigmoid(x):
    return 0.5 * jnp.tanh(0.5 * x) + 0.5


def _rows(tm, n):
    return pl.BlockSpec((tm, n), lambda i: (i, 0))


def _whole(shape):
    return pl.BlockSpec(shape, lambda i: (0,) * len(shape))


def _accumulate(ref, val):
    @pl.when(pl.program_id(0) == 0)
    def _():
        ref[...] = jnp.zeros_like(ref)

    ref[...] += val


def _params(n_axes=1):
    return pltpu.CompilerParams(dimension_semantics=("arbitrary",) * n_axes,
                                vmem_limit_bytes=VMEM_LIMIT_BYTES)


def _sds(shape, dtype):
    return jax.ShapeDtypeStruct(shape, dtype)


def _tile(t, want):
    return min(t, want)


def _f1_norm_in(h, g, w_t):
    t = h.shape[0]
    tm = _tile(t, 512)

    def body(h_ref, g_ref, w_ref, z_ref, a_ref):
        a = _rms(h_ref[...], g_ref[...]).astype(BF16)
        a_ref[...] = a
        z_ref[...] = _nt(a, w_ref[...]).astype(BF16)

    return pl.pallas_call(
        body, name="f1_norm_in", grid=(t // tm,),
        in_specs=[_rows(tm, D_MODEL), _whole((1, D_MODEL)), _whole((D_IN, D_MODEL))],
        out_specs=[_rows(tm, D_IN), _rows(tm, D_MODEL)],
        out_shape=[_sds((t, D_IN), BF16), _sds((t, D_MODEL), BF16)],
        compiler_params=_params())(h, g, w_t)


def _alibi_bias():
    ti = jnp.arange(BLK)[:, None]
    ji = jnp.arange(2 * BLK)[None, :]
    dist = ti + BLK - ji
    band = (dist >= 0) & (dist < BLK)
    bias = -jnp.asarray(ALIBI_SLOPES, F32)[:, None, None] * dist.astype(F32)[None]
    return jnp.stack([jnp.where((band & (ji >= BLK))[None], bias, NEG_BIG), jnp.where(band[None], bias, NEG_BIG)])


def _strips():
    return [slice(r * STRIP, (r + 1) * STRIP) for r in range(BLK // STRIP)]


def _bias_spec(transposed=False):
    tile = (2 * BLK, BLK) if transposed else (BLK, 2 * BLK)
    return pl.BlockSpec((None, N_Q_HEADS) + tile, lambda i: (jnp.minimum(i, 1), 0, 0, 0))


def _softmax_strip(s_ref, bias_ref, hq, rows, sink):
    sc = s_ref[hq, rows, :] + bias_ref[hq, rows, :]
    m = jnp.maximum(jnp.max(sc, axis=1, keepdims=True), sink)
    e = jnp.exp(sc - m)
    es = jnp.exp(sink - m)
    inv = 1.0 / (jnp.sum(e, axis=1, keepdims=True) + es)
    return e * inv, es * inv


def _kv_window(z_ref, kp_ref, vp_ref):
    kcat = jnp.concatenate([kp_ref[...], z_ref[:, ATTN_W:ATTN_W + KV_W]], axis=0).astype(F32)
    vcat = jnp.concatenate([vp_ref[...], z_ref[:, ATTN_W + KV_W:ATTN_W + 2 * KV_W]], axis=0).astype(F32)
    kswap = pltpu.roll(kcat, HEAD_DIM, 1)
    vswap = pltpu.roll(vcat, HEAD_DIM, 1)
    return kcat.astype(BF16), kswap.astype(BF16), vcat, vswap


def _gelu_and_grad(x):
    t = jnp.tanh(GELU_C0 * (x + GELU_C1 * x * x * x))
    return 0.5 * x * (1.0 + t), 0.5 * (1.0 + t) + 0.5 * x * (1.0 - t * t) * GELU_C0 * (1.0 + 3.0 * GELU_C1 * x * x)


def _layernorm_strip(gv, lng_ref, lnb_ref):
    xc = gv - jnp.mean(gv, axis=-1, keepdims=True)
    rstd = lax.rsqrt(jnp.mean(xc * xc, axis=-1, keepdims=True) + NORM_EPS)
    xhat = xc * rstd
    return xhat * lng_ref[...] + lnb_ref[...], xhat, rstd


def _tril_w(ws_ref, h):
    ti = lax.broadcasted_iota(jnp.int32, (BLK, BLK), 0)
    si = lax.broadcasted_iota(jnp.int32, (BLK, BLK), 1)
    causal = si <= ti
    return jnp.where(causal, ws_ref[h], 0.0).astype(BF16), causal


def _gm_mixed(vn, ws_ref, bs_ref, lo, hi):
    slabs = []
    for s in range(GM_W // LANES):
        vs = vn[:, s * LANES:(s + 1) * LANES]
        w0, _ = _tril_w(ws_ref, 2 * s)
        w1, _ = _tril_w(ws_ref, 2 * s + 1)
        mixed = (_mm(w0, jnp.where(lo, vs, 0.0).astype(BF16))
                 + _mm(w1, jnp.where(hi, vs, 0.0).astype(BF16))
                 + bs_ref[:, s * LANES:(s + 1) * LANES])
        slabs.append(mixed)
    return slabs


def _block_specs_z(nb):
    prev = lambda i: (jnp.maximum(i - 1, 0), ATTN_W // KV_W)
    prev_v = lambda i: (jnp.maximum(i - 1, 0), ATTN_W // KV_W + 1)
    return [_rows(BLK, D_IN), pl.BlockSpec((BLK, KV_W), prev), pl.BlockSpec((BLK, KV_W), prev_v)]


def _heads():
    return [(2 * s + half, s, half, s // 2 == half) for s in range(ATTN_W // LANES) for half in range(2)]


def _f2_attn_gm(z, sinks, bias, ln_g, ln_b, ws, bs_full):
    t = z.shape[0]
    nb = t // BLK

    def body(z_ref, kp_ref, vp_ref, sink_ref, bias_ref, lng_ref, lnb_ref, ws_ref, bs_ref, am_ref,
             s_ref, p_ref, u_ref, vn_ref):
        lane = lax.broadcasted_iota(jnp.int32, (1, LANES), 1)
        lo = lane < HEAD_DIM
        hi = lane >= HEAD_DIM
        kc, ks, vcat, vswap = _kv_window(z_ref, kp_ref, vp_ref)
        for hq, s, half, same in _heads():
            qs = z_ref[:, s * LANES:(s + 1) * LANES].astype(F32) * (HEAD_DIM ** -0.5)
            qm = jnp.where(lo if half == 0 else hi, qs, 0.0).astype(BF16)
            s_ref[hq] = _nt(qm, kc if same else ks)
        for hq in range(N_Q_HEADS):
            for rows in _strips():
                pr, _ = _softmax_strip(s_ref, bias_ref, hq, rows, sink_ref[hq])
                p_ref[hq, rows, :] = pr.astype(BF16)
        for s in range(ATTN_W // LANES):
            o = jnp.zeros((BLK, LANES), F32)
            for hq, hs, half, same in _heads():
                if hs == s:
                    vm = jnp.where(lo if half == 0 else hi, vcat if same else vswap, 0.0).astype(BF16)
                    o = o + _mm(p_ref[hq], vm)
            am_ref[:, s * LANES:(s + 1) * LANES] = o.astype(BF16)

        for rows in _strips():
            u_ref[rows, :] = _gelu(z_ref[rows, ATTN_W + 2 * KV_W:ATTN_W + 2 * KV_W + GM_W].astype(F32))
            vn_ref[rows, :], _, _ = _layernorm_strip(
                _gelu(z_ref[rows, ATTN_W + 2 * KV_W + GM_W:D_IN].astype(F32)), lng_ref, lnb_ref)
        mixed = _gm_mixed(vn_ref, ws_ref, bs_ref, lo, hi)
        for s in range(GM_W // LANES):
            am_ref[:, ATTN_W + s * LANES:ATTN_W + (s + 1) * LANES] = (
                u_ref[:, s * LANES:(s + 1) * LANES] * mixed[s]).astype(BF16)

    return pl.pallas_call(
        body, name="f2_attn_gm", grid=(nb,),
        in_specs=_block_specs_z(nb) + [
            pl.BlockSpec(memory_space=pltpu.SMEM), _bias_spec(), _whole((1, GM_W)),
            _whole((1, GM_W)), _whole((N_Q_HEADS, BLK, BLK)), _whole((BLK, GM_W))],
        out_specs=_rows(BLK, ATTN_W + GM_W),
        out_shape=_sds((t, ATTN_W + GM_W), BF16),
        scratch_shapes=[pltpu.VMEM((N_Q_HEADS, BLK, 2 * BLK), F32), pltpu.VMEM((N_Q_HEADS, BLK, 2 * BLK), BF16),
                        pltpu.VMEM((BLK, GM_W), F32), pltpu.VMEM((BLK, GM_W), F32)],
        compiler_params=_params())(z, z, z, sinks, bias, ln_g, ln_b, ws, bs_full)


def _f3_mix_out(am, h, ga, gg, w, gpost):
    t = h.shape[0]
    tm = _tile(t, 512)

    def body(am_ref, h_ref, ga_ref, gg_ref, w_ref, gp_ref, heads_ref, mix_ref, h1_ref):
        heads = jnp.concatenate([_rms(am_ref[:, :ATTN_W].astype(F32), ga_ref[...]),
                                 _rms(am_ref[:, ATTN_W:].astype(F32), gg_ref[...])], axis=1).astype(BF16)
        heads_ref[...] = heads
        mix = _mm(heads, w_ref[...])
        mix_ref[...] = mix.astype(BF16)
        h1_ref[...] = h_ref[...] + _rms(mix, gp_ref[...])

    return pl.pallas_call(
        body, name="f3_mix_out", grid=(t // tm,),
        in_specs=[_rows(tm, D_MODEL), _rows(tm, D_MODEL), _whole((1, ATTN_W)), _whole((1, GM_W)),
                  _whole((D_MODEL, D_MODEL)), _whole((1, D_MODEL))],
        out_specs=[_rows(tm, D_MODEL)] * 3,
        out_shape=[_sds((t, D_MODEL), BF16), _sds((t, D_MODEL), BF16), _sds((t, D_MODEL), F32)],
        compiler_params=_params())(am, h, ga, gg, w, gpost)


def _f4a_ffn_in(h1, gf, wg_t, wu_t):
    t = h1.shape[0]
    tm = _tile(t, 256)

    def body(h_ref, g_ref, wg_ref, wu_ref, f_ref, gt_ref, up_ref):
        f = _rms(h_ref[...], g_ref[...]).astype(BF16)
        f_ref[...] = f
        gt_ref[...] = _nt(f, wg_ref[...]).astype(BF16)
        up_ref[...] = _nt(f, wu_ref[...]).astype(BF16)

    return pl.pallas_call(
        body, name="f4a_ffn_in", grid=(t // tm,),
        in_specs=[_rows(tm, D_MODEL), _whole((1, D_MODEL)), _whole((D_FF, D_MODEL)), _whole((D_FF, D_MODEL))],
        out_specs=[_rows(tm, D_MODEL), _rows(tm, D_FF), _rows(tm, D_FF)],
        out_shape=[_sds((t, D_MODEL), BF16), _sds((t, D_FF), BF16), _sds((t, D_FF), BF16)],
        compiler_params=_params())(h1, gf, wg_t, wu_t)


def _f4b_ffn_out(gt, up, wd, h1, gfp):
    t = h1.shape[0]
    tm = _tile(t, 256)

    def body(gt_ref, up_ref, wd_ref, h_ref, g_ref, dn_ref, h2_ref):
        gt = gt_ref[...].astype(F32)
        act = (gt * _sigmoid(gt) * up_ref[...].astype(F32)).astype(BF16)
        dn = _mm(act, wd_ref[...])
        dn_ref[...] = dn.astype(BF16)
        h2_ref[...] = h_ref[...] + _rms(dn, g_ref[...])

    return pl.pallas_call(
        body, name="f4b_ffn_out", grid=(t // tm,),
        in_specs=[_rows(tm, D_FF), _rows(tm, D_FF), _whole((D_FF, D_MODEL)), _rows(tm, D_MODEL),
                  _whole((1, D_MODEL))],
        out_specs=[_rows(tm, D_MODEL)] * 2,
        out_shape=[_sds((t, D_MODEL), BF16), _sds((t, D_MODEL), F32)],
        compiler_params=_params())(gt, up, wd, h1, gfp)


def _f5_ple(h2, gpl, wpg, p, wple):
    t = h2.shape[0]
    tm = _tile(t, 512)

    def body(h_ref, g_ref, wpg_ref, p_ref, wple_ref, r_ref, pg_ref, pe_ref, h3_ref):
        h = h_ref[...]
        r = _rms(h, g_ref[...]).astype(BF16)
        r_ref[...] = r
        pg = _mm(r, wpg_ref[...])
        pe = _mm(p_ref[...].astype(BF16), wple_ref[...])
        pg_ref[...] = pg.astype(BF16)
        pe_ref[...] = pe.astype(BF16)
        h3_ref[...] = h + pe * _sigmoid(pg)

    return pl.pallas_call(
        body, name="f5_ple", grid=(t // tm,),
        in_specs=[_rows(tm, D_MODEL), _whole((1, D_MODEL)), _whole((D_MODEL, D_MODEL)), _rows(tm, PLE_DIM),
                  _whole((PLE_DIM, D_MODEL))],
        out_specs=[_rows(tm, D_MODEL)] * 4,
        out_shape=[_sds((t, D_MODEL), BF16)] * 3 + [_sds((t, D_MODEL), F32)],
        compiler_params=_params())(h2, gpl, wpg, p, wple)


def _loss_head(y, target):
    t = y.shape[0]
    tm = _tile(t, 512)

    def body(y_ref, t_ref, sq_ref, dy_ref):
        err = y_ref[...] - t_ref[...]
        dy_ref[...] = err * (1.0 / D_MODEL)
        _accumulate(sq_ref, jnp.sum(err * err, keepdims=True))

    return pl.pallas_call(
        body, name="loss_head", grid=(t // tm,),
        in_specs=[_rows(tm, D_MODEL)] * 2,
        out_specs=[_whole((1, LANES)), _rows(tm, D_MODEL)],
        out_shape=[_sds((1, LANES), F32), _sds((t, D_MODEL), F32)],
        compiler_params=_params())(y, target)


def _b5_ple(dh3, h2, pg, pe, gpl, wpg):
    t = h2.shape[0]
    tm = _tile(t, 512)

    def body(dh_ref, h_ref, pg_ref, pe_ref, g_ref, w_ref, dpe_ref, dpg_ref, dh2_ref, dg_ref):
        dh = dh_ref[...]
        s = _sigmoid(pg_ref[...].astype(F32))
        dpe_ref[...] = (dh * s).astype(BF16)
        dpg = (dh * pe_ref[...].astype(F32) * s * (1.0 - s)).astype(BF16)
        dpg_ref[...] = dpg
        dx, dg = _rms_bwd(_nt(dpg, w_ref[...]), h_ref[...], g_ref[...])
        dh2_ref[...] = dh + dx
        _accumulate(dg_ref, dg)

    return pl.pallas_call(
        body, name="b5_ple", grid=(t // tm,),
        in_specs=[_rows(tm, D_MODEL)] * 4 + [_whole((1, D_MODEL)), _whole((D_MODEL, D_MODEL))],
        out_specs=[_rows(tm, D_MODEL)] * 3 + [_whole((1, D_MODEL))],
        out_shape=[_sds((t, D_MODEL), BF16)] * 2 + [_sds((t, D_MODEL), F32), _sds((1, D_MODEL), F32)],
        compiler_params=_params())(dh3, h2, pg, pe, gpl, wpg)


def _b4a_ffn_out(dh2, dn, gfp, wd, gt, up):
    t = dh2.shape[0]
    tm = _tile(t, 256)

    def body(dh_ref, dn_ref, g_ref, wd_ref, gt_ref, up_ref, ddn_ref, act_ref, dgt_ref, dup_ref, dg_ref):
        ddn, dg = _rms_bwd(dh_ref[...], dn_ref[...].astype(F32), g_ref[...])
        _accumulate(dg_ref, dg)
        ddn = ddn.astype(BF16)
        ddn_ref[...] = ddn
        dact = _nt(ddn, wd_ref[...])
        gt = gt_ref[...].astype(F32)
        up = up_ref[...].astype(F32)
        sg = _sigmoid(gt)
        silu = gt * sg
        act_ref[...] = (silu * up).astype(BF16)
        dup_ref[...] = (dact * silu).astype(BF16)
        dgt_ref[...] = (dact * up * (sg * (1.0 + gt * (1.0 - sg)))).astype(BF16)

    return pl.pallas_call(
        body, name="b4a_ffn_out", grid=(t // tm,),
        in_specs=[_rows(tm, D_MODEL), _rows(tm, D_MODEL), _whole((1, D_MODEL)), _whole((D_FF, D_MODEL)),
                  _rows(tm, D_FF), _rows(tm, D_FF)],
        out_specs=[_rows(tm, D_MODEL), _rows(tm, D_FF), _rows(tm, D_FF), _rows(tm, D_FF), _whole((1, D_MODEL))],
        out_shape=[_sds((t, D_MODEL), BF16)] + [_sds((t, D_FF), BF16)] * 3 + [_sds((1, D_MODEL), F32)],
        compiler_params=_params())(dh2, dn, gfp, wd, gt, up)


def _b4b_ffn_in(dgt, dup, wg_t, wu_t, h1, gf, dh2):
    t = h1.shape[0]
    tm = _tile(t, 512)

    def body(dgt_ref, dup_ref, wg_ref, wu_ref, h_ref, g_ref, dh_ref, dh1_ref, dg_ref):
        df = _mm(dgt_ref[...], wg_ref[...]) + _mm(dup_ref[...], wu_ref[...])
        dx, dg = _rms_bwd(df, h_ref[...], g_ref[...])
        dh1_ref[...] = dh_ref[...] + dx
        _accumulate(dg_ref, dg)

    return pl.pallas_call(
        body, name="b4b_ffn_in", grid=(t // tm,),
        in_specs=[_rows(tm, D_FF), _rows(tm, D_FF), _whole((D_FF, D_MODEL)), _whole((D_FF, D_MODEL)),
                  _rows(tm, D_MODEL), _whole((1, D_MODEL)), _rows(tm, D_MODEL)],
        out_specs=[_rows(tm, D_MODEL), _whole((1, D_MODEL))],
        out_shape=[_sds((t, D_MODEL), F32), _sds((1, D_MODEL), F32)],
        compiler_params=_params())(dgt, dup, wg_t, wu_t, h1, gf, dh2)


def _b3_mix_out(dh1, mix, gpost, w, am, ga, gg):
    t = dh1.shape[0]
    tm = _tile(t, 512)

    def body(dh_ref, mix_ref, gp_ref, w_ref, am_ref, ga_ref, gg_ref, dmix_ref, dam_ref, dgp_ref, dga_ref, dgg_ref):
        dmix, dgp = _rms_bwd(dh_ref[...], mix_ref[...].astype(F32), gp_ref[...])
        _accumulate(dgp_ref, dgp)
        dmix = dmix.astype(BF16)
        dmix_ref[...] = dmix
        dheads = _nt(dmix, w_ref[...])
        dat, dga = _rms_bwd(dheads[:, :ATTN_W], am_ref[:, :ATTN_W].astype(F32), ga_ref[...])
        dgm, dgg = _rms_bwd(dheads[:, ATTN_W:], am_ref[:, ATTN_W:].astype(F32), gg_ref[...])
        dam_ref[:, :ATTN_W] = dat.astype(BF16)
        dam_ref[:, ATTN_W:] = dgm.astype(BF16)
        _accumulate(dga_ref, dga)
        _accumulate(dgg_ref, dgg)

    return pl.pallas_call(
        body, name="b3_mix_out", grid=(t // tm,),
        in_specs=[_rows(tm, D_MODEL), _rows(tm, D_MODEL), _whole((1, D_MODEL)), _whole((D_MODEL, D_MODEL)),
                  _rows(tm, D_MODEL), _whole((1, ATTN_W)), _whole((1, GM_W))],
        out_specs=[_rows(tm, D_MODEL), _rows(tm, D_MODEL), _whole((1, D_MODEL)), _whole((1, ATTN_W)),
                   _whole((1, GM_W))],
        out_shape=[_sds((t, D_MODEL), BF16), _sds((t, D_MODEL), BF16), _sds((1, D_MODEL), F32),
                   _sds((1, ATTN_W), F32), _sds((1, GM_W), F32)],
        compiler_params=_params())(dh1, mix, gpost, w, am, ga, gg)


def _b2_attn_gm(dam, z, sinks, bias, ln_g, ln_b, ws, bs_full):
    t = z.shape[0]
    nb = t // BLK

    def body(dam_ref, z_ref, kp_ref, vp_ref, sink_ref, bias_ref, lng_ref, lnb_ref, ws_ref, wst_ref, bs_ref,
             dzq_ref, dkv_ref, dzuv_ref, dsink_ref, dlng_ref, dlnb_ref, dws_ref, dbs_ref,
             sc_ref, dp_ref, p_ref, ds_ref, u_ref, du_ref, dv_ref, vn_ref, xhat_ref, rstd_ref, dvn_ref):
        n = pl.program_id(0)
        lane = lax.broadcasted_iota(jnp.int32, (1, LANES), 1)
        lo = lane < HEAD_DIM
        hi = lane >= HEAD_DIM
        sub = lax.broadcasted_iota(jnp.int32, (LANES, 1), 0)
        kc, ks, vcat, vswap = _kv_window(z_ref, kp_ref, vp_ref)
        vc = vcat.astype(BF16)
        vs_ = vswap.astype(BF16)
        kc_t = kc.T
        ks_t = ks.T

        def operands(s, half):
            mask = lo if half == 0 else hi
            qs = z_ref[:, s * LANES:(s + 1) * LANES].astype(F32) * (HEAD_DIM ** -0.5)
            qm = jnp.where(mask, qs, 0.0).astype(BF16)
            dom = jnp.where(mask, dam_ref[:, s * LANES:(s + 1) * LANES], 0.0).astype(BF16)
            return qm, dom

        for hq, s, half, same in _heads():
            qm, dom = operands(s, half)
            sc_ref[hq] = _nt(kc if same else ks, qm)
            dp_ref[hq] = _nt(vc if same else vs_, dom)
        dsink = jnp.zeros((1, LANES), F32)
        for hq in range(N_Q_HEADS):
            sink = sink_ref[hq]
            sc = sc_ref[hq] + bias_ref[hq]
            m = jnp.maximum(jnp.max(sc, axis=0, keepdims=True), sink)
            e = jnp.exp(sc - m)
            es = jnp.exp(sink - m)
            inv = 1.0 / (jnp.sum(e, axis=0, keepdims=True) + es)
            pr = e * inv
            dpr = dp_ref[hq]
            row = jnp.sum(dpr * pr, axis=0, keepdims=True)
            p_ref[hq] = pr.astype(BF16)
            ds_ref[hq] = (pr * (dpr - row)).astype(BF16)
            dsink = dsink + jnp.where(lane == hq, -jnp.sum(es * inv * row, keepdims=True), 0.0)
        dk_acc = [jnp.zeros((2 * BLK, LANES), F32), jnp.zeros((2 * BLK, LANES), F32)]
        dv_acc = [jnp.zeros((2 * BLK, LANES), F32), jnp.zeros((2 * BLK, LANES), F32)]
        for s in range(ATTN_W // LANES):
            dq_t = jnp.zeros((LANES, BLK), F32)
            for hq, hs, half, same in _heads():
                if hs != s:
                    continue
                qm, dom = operands(s, half)
                ds = ds_ref[hq]
                dk_acc[same] = dk_acc[same] + _mm(ds, qm)
                dv_acc[same] = dv_acc[same] + _mm(p_ref[hq], dom)
                in_half = (sub < HEAD_DIM) if half == 0 else (sub >= HEAD_DIM)
                dq_t = dq_t + jnp.where(in_half, _mm(kc_t if same else ks_t, ds), 0.0)
            dzq_ref[:, s * LANES:(s + 1) * LANES] = (dq_t.T * (HEAD_DIM ** -0.5)).astype(BF16)
        dk_acc = dk_acc[True] + pltpu.roll(dk_acc[False], HEAD_DIM, 1)
        dv_acc = dv_acc[True] + pltpu.roll(dv_acc[False], HEAD_DIM, 1)
        cur = pl.multiple_of(n * BLK, BLK)
        dkv_ref[pl.ds(cur, BLK), 0:KV_W] = dk_acc[BLK:, :]
        dkv_ref[pl.ds(cur, BLK), KV_W:2 * KV_W] = dv_acc[BLK:, :]

        @pl.when(n > 0)
        def _():
            prv = pl.multiple_of((n - 1) * BLK, BLK)
            dkv_ref[pl.ds(prv, BLK), 0:KV_W] += dk_acc[:BLK, :]
            dkv_ref[pl.ds(prv, BLK), KV_W:2 * KV_W] += dv_acc[:BLK, :]

        _accumulate(dsink_ref, dsink)

        for rows in _strips():
            u_ref[rows, :], du_ref[rows, :] = _gelu_and_grad(
                z_ref[rows, ATTN_W + 2 * KV_W:ATTN_W + 2 * KV_W + GM_W].astype(F32))
            gv, dv_ref[rows, :] = _gelu_and_grad(z_ref[rows, ATTN_W + 2 * KV_W + GM_W:D_IN].astype(F32))
            vn_ref[rows, :], xhat_ref[rows, :], rstd = _layernorm_strip(gv, lng_ref, lnb_ref)
            rstd_ref[rows, :] = jnp.broadcast_to(rstd, (STRIP, LANES))
        mixed = _gm_mixed(vn_ref, ws_ref, bs_ref, lo, hi)

        @pl.when(n == 0)
        def _():
            dws_ref[...] = jnp.zeros_like(dws_ref)

        dbs = jnp.zeros((BLK, LANES), F32)
        for s in range(GM_W // LANES):
            slab = slice(s * LANES, (s + 1) * LANES)
            dgm = dam_ref[:, ATTN_W + s * LANES:ATTN_W + (s + 1) * LANES]
            dzuv_ref[:, slab] = (dgm * mixed[s] * du_ref[:, slab]).astype(BF16)
            dmx = dgm * u_ref[:, slab]
            vsb = vn_ref[:, slab].astype(BF16)
            dvn = jnp.zeros((BLK, LANES), F32)
            for half in range(2):
                h = 2 * s + half
                mask = lo if half == 0 else hi
                dmm = jnp.where(mask, dmx, 0.0)
                dbs = dbs + jnp.where(lane == h, jnp.sum(dmm, axis=1, keepdims=True), 0.0)
                dmm = dmm.astype(BF16)
                _, causal = _tril_w(ws_ref, h)
                ti = lax.broadcasted_iota(jnp.int32, (BLK, BLK), 0)
                si = lax.broadcasted_iota(jnp.int32, (BLK, BLK), 1)
                wt_t = jnp.where(ti <= si, wst_ref[h], 0.0).astype(BF16)
                dvn = dvn + jnp.where(mask, _mm(wt_t, dmm), 0.0)
                dws_ref[h] += jnp.where(causal, _nt(dmm, vsb), 0.0)
            dvn_ref[:, slab] = dvn
        _accumulate(dbs_ref, dbs)
        dlnb = jnp.zeros((STRIP, GM_W), F32)
        dlng = jnp.zeros((STRIP, GM_W), F32)
        for rows in _strips():
            dvn = dvn_ref[rows, :]
            xhat = xhat_ref[rows, :]
            dlnb = dlnb + dvn
            dlng = dlng + dvn * xhat
            dxh = dvn * lng_ref[...]
            dgv = rstd_ref[rows, 0:1] * (dxh - jnp.mean(dxh, axis=-1, keepdims=True)
                                         - xhat * jnp.mean(dxh * xhat, axis=-1, keepdims=True))
            dzuv_ref[rows, GM_W:] = (dgv * dv_ref[rows, :]).astype(BF16)
        _accumulate(dlnb_ref, jnp.sum(dlnb, axis=0, keepdims=True))
        _accumulate(dlng_ref, jnp.sum(dlng, axis=0, keepdims=True))

    return pl.pallas_call(
        body, name="b2_attn_gm", grid=(nb,),
        in_specs=[_rows(BLK, ATTN_W + GM_W)] + _block_specs_z(nb) + [
            pl.BlockSpec(memory_space=pltpu.SMEM), _bias_spec(transposed=True), _whole((1, GM_W)),
            _whole((1, GM_W)), _whole((N_Q_HEADS, BLK, BLK)), _whole((N_Q_HEADS, BLK, BLK)), _whole((BLK, GM_W))],
        out_specs=[_rows(BLK, ATTN_W), _whole((t, 2 * KV_W)), _rows(BLK, 2 * GM_W), _whole((1, LANES)),
                   _whole((1, GM_W)), _whole((1, GM_W)), _whole((N_Q_HEADS, BLK, BLK)), _whole((BLK, LANES))],
        out_shape=[_sds((t, ATTN_W), BF16), _sds((t, 2 * KV_W), F32), _sds((t, 2 * GM_W), BF16),
                   _sds((1, LANES), F32), _sds((1, GM_W), F32), _sds((1, GM_W), F32),
                   _sds((N_Q_HEADS, BLK, BLK), F32), _sds((BLK, LANES), F32)],
        scratch_shapes=[pltpu.VMEM((N_Q_HEADS, 2 * BLK, BLK), F32)] * 2
        + [pltpu.VMEM((N_Q_HEADS, 2 * BLK, BLK), BF16)] * 2 + [pltpu.VMEM((BLK, GM_W), F32)] * 5
        + [pltpu.VMEM((BLK, LANES), F32), pltpu.VMEM((BLK, GM_W), F32)],
        compiler_params=_params())(dam, z, z, z, sinks, jnp.swapaxes(bias, 2, 3), ln_g, ln_b, ws,
                                   jnp.swapaxes(ws, 1, 2), bs_full)


def _b1_norm_in(dzq, dkv, dzuv, w_t, h, g, dh1):
    t = h.shape[0]
    tm = _tile(t, 512)

    def body(dzq_ref, dkv_ref, dzuv_ref, w_ref, h_ref, g_ref, dh_ref, dz_ref, dh0_ref, dg_ref):
        dz = jnp.concatenate([dzq_ref[...], dkv_ref[...].astype(BF16), dzuv_ref[...]], axis=1)
        dz_ref[...] = dz
        dx, dg = _rms_bwd(_mm(dz, w_ref[...]), h_ref[...], g_ref[...])
        dh0_ref[...] = dh_ref[...] + dx
        _accumulate(dg_ref, dg)

    return pl.pallas_call(
        body, name="b1_norm_in", grid=(t // tm,),
        in_specs=[_rows(tm, ATTN_W), _rows(tm, 2 * KV_W), _rows(tm, 2 * GM_W), _whole((D_IN, D_MODEL)),
                  _rows(tm, D_MODEL), _whole((1, D_MODEL)), _rows(tm, D_MODEL)],
        out_specs=[_rows(tm, D_IN), _rows(tm, D_MODEL), _whole((1, D_MODEL))],
        out_shape=[_sds((t, D_IN), BF16), _sds((t, D_MODEL), F32), _sds((1, D_MODEL), F32)],
        compiler_params=_params())(dzq, dkv, dzuv, w_t, h, g, dh1)


def _weight_grad(x, dy, name):
    t, k = x.shape
    n = dy.shape[1]
    tm = _tile(t, 1024)
    steps = t // tm

    def body(x_ref, dy_ref, dw_ref, acc_ref):
        i = pl.program_id(0)

        @pl.when(i == 0)
        def _():
            acc_ref[...] = _tn(x_ref[...].astype(BF16), dy_ref[...])

        @pl.when(i > 0)
        def _():
            acc_ref[...] += _tn(x_ref[...].astype(BF16), dy_ref[...])

        @pl.when(i == steps - 1)
        def _():
            dw_ref[...] = acc_ref[...].astype(BF16)

    return pl.pallas_call(
        body, name=name, grid=(steps,),
        in_specs=[_rows(tm, k), _rows(tm, n)],
        out_specs=_whole((k, n)),
        out_shape=_sds((k, n), BF16),
        scratch_shapes=[pltpu.VMEM((k, n), F32)],
        compiler_params=_params())(x, dy)


_ANY = pl.BlockSpec(memory_space=pl.ANY)
_HBM = pl.BlockSpec(memory_space=pltpu.HBM)
_SEM = pl.BlockSpec(memory_space=pltpu.SEMAPHORE)
_EFFECT = pltpu.SideEffectType.DATAFLOW_SIDE_EFFECTING


def _mesh_pos():
    return lax.axis_index("x"), lax.axis_index("y"), lax.axis_index("c")


def _other_chips(x, y):
    return [(1 - x, y), (x, 1 - y), (1 - x, 1 - y)]


def _peers(mode, x, y, c):
    if mode == "devices":
        peers = []
        for j in range(1, N_DEV):
            px = 1 - x if (j >> 2) & 1 else x
            py = 1 - y if (j >> 1) & 1 else y
            pc = 1 - c if j & 1 else c
            peers.append(((px, py, pc), 4 * px + 2 * py + pc))
        return peers, 4 * x + 2 * y + c
    if mode == "sibling":
        return [((x, y, 1 - c), None)], None
    return [((px, py, c), 2 * px + py) for px, py in _other_chips(x, y)], 2 * x + y


def _descriptors(mode, srcs, lands, send_sems, recv_sems, with_incoming=True):
    x, y, c = _mesh_pos()
    peers, me = _peers(mode, x, y, c)
    scatter = mode == "scatter"
    outgoing, incoming = [], []
    for i, (src, land) in enumerate(zip(srcs, lands)):
        for j, (dev, slot) in enumerate(peers):
            sem = i * len(peers) + j
            common = dict(send_sem=send_sems.at[sem], recv_sem=recv_sems.at[sem], device_id=dev,
                          device_id_type=pl.DeviceIdType.MESH)
            whole = mode == "sibling"
            outgoing.append(pltpu.make_async_remote_copy(
                src_ref=src.at[slot] if scatter else src, dst_ref=land if whole else land.at[me], **common))
            if with_incoming:
                incoming.append(pltpu.make_async_remote_copy(
                    src_ref=src.at[me] if scatter else src, dst_ref=land if whole else land.at[slot], **common))
    return outgoing, incoming


def _n_sems(mode, k):
    return k * {"devices": N_DEV - 1, "sibling": 1}.get(mode, N_CHIPS - 1)


def _exchange_start(srcs, lands, mode, name, after=None):
    k = len(srcs)
    arrs = [*srcs, *lands]

    def body(*refs):
        skip = 1 if after is not None else 0
        send_sems, recv_sems = refs[2 * k + skip], refs[2 * k + skip + 1]
        outgoing, _ = _descriptors(mode, refs[:k], refs[k:2 * k], send_sems, recv_sems, with_incoming=False)
        for cp in outgoing:
            cp.start()
        refs[-1][...] = jnp.zeros_like(refs[-1])

    operands = [pltpu.with_memory_space_constraint(a, pltpu.HBM) for a in arrs]
    in_specs = [_HBM] * (2 * k)
    if after is not None:
        operands.append(after)
        in_specs.append(_ANY)
    sems = pltpu.SemaphoreType.DMA((_n_sems(mode, k),))
    res = pl.pallas_call(
        body, name=name, in_specs=in_specs,
        out_shape=(sems, sems, *[pltpu.HBM(a.shape, a.dtype) for a in arrs], _sds((8, LANES), F32)),
        out_specs=(_SEM, _SEM, *[_HBM] * (2 * k), pl.BlockSpec(memory_space=pltpu.VMEM)),
        input_output_aliases={i: 2 + i for i in range(2 * k)},
        compiler_params=pltpu.CompilerParams(has_side_effects=_EFFECT))(*operands)
    return dict(mode=mode, send=res[0], recv=res[1], srcs=res[2:2 + k], lands=res[2 + k:2 + 2 * k], token=res[-1])


def _exchange_wait(flight, name, after):
    mode, k = flight["mode"], len(flight["srcs"])
    arrs = [*flight["srcs"], *flight["lands"]]

    def body(*refs):
        outgoing, incoming = _descriptors(mode, refs[:k], refs[k:2 * k], refs[2 * k], refs[2 * k + 1])
        for cp in outgoing:
            cp.wait_send()
        for cp in incoming:
            cp.wait_recv()

    res = pl.pallas_call(
        body, name=name, in_specs=[_HBM] * (2 * k) + [_SEM, _SEM, _ANY],
        out_shape=tuple(pltpu.HBM(a.shape, a.dtype) for a in arrs), out_specs=tuple([_HBM] * (2 * k)),
        input_output_aliases={i: i for i in range(2 * k)},
        compiler_params=pltpu.CompilerParams(has_side_effects=_EFFECT))(*arrs, flight["send"], flight["recv"], after)
    return res[:k], res[k:]


def _adamw(w, g, m, v):
    m = ADAM_B1 * m + (1.0 - ADAM_B1) * g
    v = ADAM_B2 * v + (1.0 - ADAM_B2) * (g * g)
    m_hat = m / (1.0 - ADAM_B1 ** ADAM_STEP)
    v_hat = v / (1.0 - ADAM_B2 ** ADAM_STEP)
    delta = -ADAM_LR * (m_hat / (jnp.sqrt(v_hat) + ADAM_EPS) + ADAM_WD * w)
    return delta, m, v


def _row_tile(rows, cols, n_arrays):
    budget = VMEM_LIMIT_BYTES // 2
    padded = -(-cols // LANES) * LANES
    cap = min(rows, max(16, budget // (2 * n_arrays * padded * 4)))
    return max(tr for tr in range(16, cap + 1, 16) if rows % tr == 0)


def _sum_chips(landing, name):
    _, r, c = landing.shape
    tr = _row_tile(r, c, 4)

    def body(l_ref, s_ref):
        acc = l_ref[0].astype(F32)
        for s in range(1, N_CHIPS):
            acc = acc + l_ref[s].astype(F32)
        s_ref[...] = acc

    return pl.pallas_call(
        body, name=name, grid=(r // tr,),
        in_specs=[pl.BlockSpec((N_CHIPS, tr, c), lambda i: (0, i, 0))],
        out_specs=_rows(tr, c), out_shape=_sds((r, c), F32),
        compiler_params=_params())(landing)


def _adamw_big(mine, sibling, w, m, v, prev, layer, name):
    _, r, c = w.shape
    tr = _row_tile(r, c, 9)
    stacked = pl.BlockSpec((None, tr, c), lambda i: (layer, i, 0))

    def body(a_ref, b_ref, w_ref, m_ref, v_ref, *rest):
        g_out, d_out, m_out, v_out = rest[-4:]
        g = a_ref[...] + b_ref[...]
        g_out[...] = g
        d_out[...], m_out[...], v_out[...] = _adamw(w_ref[...], g, m_ref[...], v_ref[...])

    prev = list(prev) if prev is not None else []
    return pl.pallas_call(
        body, name=name, grid=(r // tr,),
        in_specs=[_rows(tr, c)] * 2 + [stacked] * 3 + [_ANY] * len(prev),
        out_specs=[stacked] * 4, out_shape=[_sds(w.shape, F32)] * 4,
        input_output_aliases={5 + j: j for j in range(len(prev))},
        compiler_params=_params())(mine, sibling, w, m, v, *prev)


def _adamw_small(gathered, w, m, v):
    r = w.shape[0]

    def body(a_ref, w_ref, m_ref, v_ref, g_out, d_out, m_out, v_out):
        g = a_ref[0]
        for d in range(1, N_DEV):
            g = g + a_ref[d]
        g_out[...] = g
        d_out[...], m_out[...], v_out[...] = _adamw(w_ref[...], g, m_ref[...], v_ref[...])

    return pl.pallas_call(
        body, name="adamw_small", grid=(1,),
        in_specs=[_whole((N_DEV, r, LANES))] + [_whole((r, LANES))] * 3,
        out_specs=[_whole((r, LANES))] * 4, out_shape=[_sds((r, LANES), F32)] * 4,
        compiler_params=_params())(gathered, w, m, v)


def _shard_view(name, stacked):
    return jnp.swapaxes(stacked, 1, 2) if name in TRANSPOSED else stacked


def _row_split(name):
    return name in TRANSPOSED or BIG[name] == 1


def _assemble(name, landed):
    _, r, c = landed.shape
    if _row_split(name):
        return landed.reshape(N_CHIPS * r, c)
    return landed.transpose(1, 0, 2).reshape(r, N_CHIPS * c)


def _split(name, whole):
    r, c = whole.shape
    if _row_split(name):
        return whole.reshape(N_CHIPS, r // N_CHIPS, c)
    return whole.reshape(r, N_CHIPS, c // N_CHIPS).transpose(1, 0, 2)


def _with_own_slot(landed, own, slot):
    return lax.dynamic_update_index_in_dim(landed, own, slot, 0)


def _pack_small(params, names):
    pieces = []
    for name in names:
        flat = params[name].reshape(-1)
        pieces.append(jnp.pad(flat, (0, -flat.shape[0] % SMALL_PAD)))
    return jnp.concatenate(pieces).reshape(-1, LANES)


def _unpack_small(packed, like, names):
    flat = packed.reshape(-1)
    out, off = {}, 0
    for name in names:
        size = like[name].size
        out[name] = flat[off:off + size].reshape(like[name].shape)
        off += size + (-size % SMALL_PAD)
    return out


def kernel(x, p, ln_mix_pre, w_in, attn_sinks, gm_ln_g, gm_ln_b, gm_ws, gm_bs, g_attn_out, g_gm_out, w_out, ln_mix_post, ln_ffn_pre, w_ffn_gate, w_ffn_up, w_ffn_down, ln_ffn_post, w_ple, ln_ple_gate, w_ple_gate, loss_target, m_ln_mix_pre, m_w_in, m_attn_sinks, m_gm_ln_g, m_gm_ln_b, m_gm_ws, m_gm_bs, m_g_attn_out, m_g_gm_out, m_w_out, m_ln_mix_post, m_ln_ffn_pre, m_w_ffn_gate, m_w_ffn_up, m_w_ffn_down, m_ln_ffn_post, m_w_ple, m_ln_ple_gate, m_w_ple_gate, v_ln_mix_pre, v_w_in, v_attn_sinks, v_gm_ln_g, v_gm_ln_b, v_gm_ws, v_gm_bs, v_g_attn_out, v_g_gm_out, v_w_out, v_ln_mix_post, v_ln_ffn_pre, v_w_ffn_gate, v_w_ffn_up, v_w_ffn_down, v_ln_ffn_post, v_w_ple, v_ln_ple_gate, v_w_ple_gate):
    given = dict(locals())
    wts = {n: given[n] for n in WEIGHTS}
    mom = {n: given["m_" + n] for n in WEIGHTS}
    var = {n: given["v_" + n] for n in WEIGHTS}
    depth = w_in.shape[0]
    h = x[0]
    target = loss_target[0]
    chip = 2 * lax.axis_index("x") + lax.axis_index("y")
    device = 2 * chip + lax.axis_index("c")
    row = lambda a, i: a[i][None, :]
    bs_full = [jnp.repeat(gm_bs[i].T, HEAD_DIM, axis=1) for i in range(depth)]
    bias = _alibi_bias()
    kinds = ("grad", "delta", "m", "v")
    wview = {n: _shard_view(n, wts[n]) for n in BIG_NAMES}
    mview = {n: _shard_view(n, mom[n]) for n in BIG_NAMES}
    vview = {n: _shard_view(n, var[n]) for n in BIG_NAMES}

    zero = lambda flight: flight["token"][0:1, 0:1]

    def start_gather(i, names, after, tag):
        shards = [wview[n][i].astype(BF16) for n in names]
        lands = [lax.empty((N_CHIPS,) + s.shape, BF16) for s in shards]
        return _exchange_start(shards, lands, "gather", f"gather_weights_start_{i}{tag}", after)

    def finish_gather(flight, names, i, after, tag):
        shards, lands = _exchange_wait(flight, f"gather_weights_wait_{i}{tag}", after)
        return {n: _assemble(n, _with_own_slot(l, s, chip)) for n, s, l in zip(names, shards, lands)}

    first, after = [], None
    for k, names in enumerate(GATHER_GROUPS):
        first.append(start_gather(0, names, after, "abc"[k]))
        after = first[-1]["token"]
    full = [finish_gather(first[0], GATHER_GROUPS[0], 0, after, "a")] + [None] * (depth - 1)
    saved = []
    for i in range(depth):
        w = full[i]
        g_in = row(ln_mix_pre, i)
        if i + 1 < depth:
            flight = start_gather(i + 1, BIG_NAMES, w['w_in'], "")
            g_in = g_in + zero(flight)
        z, a = _f1_norm_in(h, g_in, w['w_in'])
        am = _f2_attn_gm(z, attn_sinks[i], bias, row(gm_ln_g, i), row(gm_ln_b, i), gm_ws[i], bs_full[i])
        if i == 0:
            w.update(finish_gather(first[1], GATHER_GROUPS[1], 0, am, "b"))
        heads, mix, h1 = _f3_mix_out(am, h, row(g_attn_out, i), row(g_gm_out, i), w['w_out'], row(ln_mix_post, i))
        f, gt, up = _f4a_ffn_in(h1, row(ln_ffn_pre, i), w['w_ffn_gate'], w['w_ffn_up'])
        if i == 0:
            w.update(finish_gather(first[2], GATHER_GROUPS[2], 0, gt, "c"))
        dn, h2 = _f4b_ffn_out(gt, up, w['w_ffn_down'], h1, row(ln_ffn_post, i))
        r, pg, pe, h3 = _f5_ple(h2, row(ln_ple_gate, i), w['w_ple_gate'], p[i, 0], w['w_ple'])
        saved.append(dict(h=h, z=z, a=a, am=am, heads=heads, mix=mix, h1=h1, f=f, gt=gt, up=up, dn=dn, h2=h2,
                          r=r, pg=pg, pe=pe))
        h = h3
        if i + 1 < depth:
            full[i + 1] = finish_gather(flight, BIG_NAMES, i + 1, h3, "")

    sq, dh = _loss_head(h, target)
    loss = lax.psum(0.5 / D_MODEL * sq[0, 0], ("x", "y", "c"))

    chain = {n: None for n in BIG_NAMES}
    owed = []
    small_out = [{k: {} for k in kinds} for _ in range(depth)]

    def start_scatter(i, names, dws, after, tag):
        parts = [_split(n, dws[n]) for n in names]
        lands = [lax.empty(q.shape, BF16) for q in parts]
        return _exchange_start(parts, lands, "scatter", f"scatter_grads_start_{i}{tag}", after)

    def finish_scatter(i, names, flight, after, tag):
        parts, lands = _exchange_wait(flight, f"scatter_grads_wait_{i}{tag}", after)
        lands = [_with_own_slot(l, lax.dynamic_index_in_dim(q, chip, 0, keepdims=False), chip)
                 for q, l in zip(parts, lands)]
        partial = [_sum_chips(l, "sum_chips_" + n) for n, l in zip(names, lands)]
        flight = _exchange_start(partial, [lax.empty(q.shape, F32) for q in partial], "sibling",
                                 f"sibling_grads_start_{i}{tag}", after)

        def update(later):
            mine, theirs = _exchange_wait(flight, f"sibling_grads_wait_{i}{tag}", later)
            for n, a, b in zip(names, mine, theirs):
                chain[n] = _adamw_big(a, b, wview[n], mview[n], vview[n], chain[n], i, "adamw_" + n)

        owed.append(update)
        return flight["token"]

    def settle(later):
        while owed:
            owed.pop(0)(later)

    def start_small(i, names, small, after, tag):
        packed = _pack_small(small, names)
        land = lax.empty((N_DEV,) + packed.shape, F32)
        return _exchange_start([packed], [land], "devices", f"gather_small_grads_start_{i}{tag}", after)

    def finish_small(i, names, flight, after, tag):
        (packed,), (gathered,) = _exchange_wait(flight, f"gather_small_grads_wait_{i}{tag}", after)
        gathered = _with_own_slot(gathered, packed, device)
        layer = lambda d: _pack_small({n: d[n][i] for n in names}, names)
        res = _adamw_small(gathered, layer(wts), layer(mom), layer(var))
        for k, a in zip(kinds, res):
            small_out[i][k].update(_unpack_small(a, {n: wts[n][i] for n in names}, names))
        return gathered

    pending, done, behind = [], None, None
    for i in reversed(range(depth)):
        s, w = saved[i], full[i]
        last = i == 0
        dws, small = {}, {}
        gain = row(ln_ple_gate, i)
        if behind is not None:
            gain = gain + behind
        dpe, dpg, dh2, dg = _b5_ple(dh, s['h2'], s['pg'], s['pe'], gain, w['w_ple_gate'])
        small['ln_ple_gate'] = dg[0]
        dws['w_ple'] = _weight_grad(p[i, 0], dpe, "dw_ple")
        dws['w_ple_gate'] = _weight_grad(s['r'], dpg, "dw_ple_gate")

        ddn, act, dgt, dup, dg = _b4a_ffn_out(dh2, s['dn'], row(ln_ffn_post, i), w['w_ffn_down'], s['gt'], s['up'])
        small['ln_ffn_post'] = dg[0]
        dws['w_ffn_down'] = _weight_grad(act, ddn, "dw_ffn_down")
        dws['w_ffn_gate'] = _weight_grad(dgt, s['f'], "dw_ffn_gate")
        dws['w_ffn_up'] = _weight_grad(dup, s['f'], "dw_ffn_up")
        gain = row(ln_ffn_pre, i)
        if last:
            flight_a = start_scatter(i, SCATTER_GROUPS[0], dws, None, "a")
            gain = gain + zero(flight_a)
        dh1, dg = _b4b_ffn_in(dgt, dup, w['w_ffn_gate'], w['w_ffn_up'], s['h1'], gain, dh2)
        small['ln_ffn_pre'] = dg[0]

        dmix, dam, dgp, dga, dgg = _b3_mix_out(dh1, s['mix'], row(ln_mix_post, i), w['w_out'], s['am'],
                                               row(g_attn_out, i), row(g_gm_out, i))
        small['ln_mix_post'] = dgp[0]
        small['g_attn_out'] = dga[0]
        small['g_gm_out'] = dgg[0]
        dws['w_out'] = _weight_grad(s['heads'], dmix, "dw_out")
        gain = row(gm_ln_g, i)
        if last:
            flight_b = start_scatter(i, SCATTER_GROUPS[1], dws, flight_a["token"], "b")
            gain = gain + zero(flight_b)

        dzq, dkv, dzuv, dsink, dlng, dlnb, dgws, dbs = _b2_attn_gm(
            dam, s['z'], attn_sinks[i], bias, gain, row(gm_ln_b, i), gm_ws[i], bs_full[i])
        small['attn_sinks'] = dsink[0, :N_Q_HEADS]
        small['gm_ln_g'] = dlng[0]
        small['gm_ln_b'] = dlnb[0]
        small['gm_ws'] = dgws
        small['gm_bs'] = dbs[:, :N_Q_HEADS].T
        gain = row(ln_mix_pre, i)
        if last:
            flight_s = start_small(i, SMALL_EARLY, small, flight_b["token"], "a")
            gain = gain + zero(flight_s)

        dz, dh, dg = _b1_norm_in(dzq, dkv, dzuv, w['w_in'], s['h'], gain, dh1)
        small['ln_mix_pre'] = dg[0]
        settle(dh)
        for finish in pending:
            done = finish(dh)
        pending = []
        if last:
            done = finish_scatter(i, SCATTER_GROUPS[0], flight_a, dz, "a")
            done = finish_scatter(i, SCATTER_GROUPS[1], flight_b, done, "b")
        dws['w_in'] = _weight_grad(dz, s['a'], "dw_in")
        if last:
            flight_c = start_scatter(i, SCATTER_GROUPS[2], dws, done, "c")
            flight_t = start_small(i, SMALL_LATE, small, flight_c["token"], "b")
            finish_small(i, SMALL_EARLY, flight_s, flight_t["token"], "a")
            done = finish_scatter(i, SCATTER_GROUPS[2], flight_c, flight_t["token"], "c")
            finish_small(i, SMALL_LATE, flight_t, flight_t["token"], "b")
            settle(done)
        else:
            flight_c = start_scatter(i, BIG_NAMES, dws, done, "")
            flight_s = start_small(i, SMALL_NAMES, small, flight_c["token"], "")
            behind = zero(flight_s)
            pending = [functools.partial(finish_small, i, SMALL_NAMES, flight_s, tag=""),
                       functools.partial(finish_scatter, i, BIG_NAMES, flight_c, tag="")]
    grad_x = dh[None]

    out = {k: {n: _shard_view(n, chain[n][j]) for n in BIG_NAMES} for j, k in enumerate(kinds)}
    for k in kinds:
        out[k].update({n: jnp.stack([small_out[i][k][n] for i in range(depth)]) for n in SMALL_NAMES})

    return (loss, grad_x, *[out["grad"][n] for n in WEIGHTS], *[out["delta"][n] for n in WEIGHTS],
            *[out["m"][n] for n in WEIGHTS], *[out["v"][n] for n in WEIGHTS])
```

```python
import functools
import math

import jax
import jax.numpy as jnp
from jax import lax
from jax.experimental import pallas as pl
from jax.experimental.pallas import tpu as pltpu

F32 = jnp.float32
BF16 = jnp.bfloat16

D_MODEL = 1024
HEAD_DIM = 64
N_Q_HEADS = 8
BLK = 128
ATTN_W = 512
KV_W = 128
GM_W = 512
D_IN = ATTN_W + 2 * KV_W + 2 * GM_W
D_FF = 2816
PLE_DIM = 256
DEPTH = 4
NORM_EPS = 1e-6
NEG_BIG = -1e30
N_CHIPS = 4
N_DEV = 8

ADAM_LR = 0.001
ADAM_B1 = 0.9
ADAM_B2 = 0.999
ADAM_EPS = 1e-08
ADAM_WD = 0.01
ADAM_STEP = 10

VMEM_LIMIT_BYTES = 56 * 1024 * 1024
LANES = 128
STRIP = 16
GELU_C0 = math.sqrt(2.0 / math.pi)
GELU_C1 = 0.044715
ALIBI_SLOPES = tuple(2.0 ** (-8.0 * (h + 1.0) / N_Q_HEADS) for h in range(N_Q_HEADS))

WEIGHTS = ['ln_mix_pre', 'w_in', 'attn_sinks', 'gm_ln_g', 'gm_ln_b', 'gm_ws', 'gm_bs', 'g_attn_out',
           'g_gm_out', 'w_out', 'ln_mix_post', 'ln_ffn_pre', 'w_ffn_gate', 'w_ffn_up', 'w_ffn_down',
           'ln_ffn_post', 'w_ple', 'ln_ple_gate', 'w_ple_gate']
BIG = {'w_in': 2, 'w_out': 1, 'w_ffn_gate': 2, 'w_ffn_up': 2, 'w_ffn_down': 1, 'w_ple': 2, 'w_ple_gate': 1}
BIG_NAMES = list(BIG)
TRANSPOSED = ('w_in', 'w_ffn_gate', 'w_ffn_up')
SMALL_NAMES = [n for n in WEIGHTS if n not in BIG]
SMALL_PAD = 1024
GATHER_GROUPS = (('w_in',), ('w_out', 'w_ffn_gate', 'w_ffn_up'), ('w_ffn_down', 'w_ple_gate', 'w_ple'))
SCATTER_GROUPS = (('w_ple', 'w_ple_gate', 'w_ffn_down', 'w_ffn_gate', 'w_ffn_up'), ('w_out',), ('w_in',))
SMALL_LATE = ('ln_mix_pre',)
SMALL_EARLY = tuple(n for n in SMALL_NAMES if n not in SMALL_LATE)


def _nt(a, b):
    return lax.dot_general(a, b, (((1,), (1,)), ((), ())), preferred_element_type=F32)


def _tn(a, b):
    return lax.dot_general(a, b, (((0,), (0,)), ((), ())), preferred_element_type=F32)


def _mm(a, b):
    return jnp.dot(a, b, preferred_element_type=F32)


def _rms(x, g):
    r = lax.rsqrt(jnp.mean(x * x, axis=-1, keepdims=True) + NORM_EPS)
    return x * r * g


def _rms_bwd(dy, x, g):
    r = lax.rsqrt(jnp.mean(x * x, axis=-1, keepdims=True) + NORM_EPS)
    xh = x * r
    dg = jnp.sum(dy * xh, axis=0, keepdims=True)
    dxh = dy * g
    dx = r * (dxh - xh * jnp.mean(dxh * xh, axis=-1, keepdims=True))
    return dx, dg


def _gelu(x):
    return 0.5 * x * (1.0 + jnp.tanh(GELU_C0 * (x + GELU_C1 * x * x * x)))


def _sigmoid(x):
    return 0.5 * jnp.tanh(0.5 * x) + 0.5


def _rows(tm, n):
    return pl.BlockSpec((tm, n), lambda i: (i, 0))


def _whole(shape):
    return pl.BlockSpec(shape, lambda i: (0,) * len(shape))


def _accumulate(ref, val):
    @pl.when(pl.program_id(0) == 0)
    def _():
        ref[...] = jnp.zeros_like(ref)

    ref[...] += val


def _params(n_axes=1):
    return pltpu.CompilerParams(dimension_semantics=("arbitrary",) * n_axes,
                                vmem_limit_bytes=VMEM_LIMIT_BYTES)


def _sds(shape, dtype):
    return jax.ShapeDtypeStruct(shape, dtype)


def _tile(t, want):
    return min(t, want)


def _f1_norm_in(h, g, w_t):
    t = h.shape[0]
    tm = _tile(t, 512)

    def body(h_ref, g_ref, w_ref, z_ref, a_ref):
        a = _rms(h_ref[...], g_ref[...]).astype(BF16)
        a_ref[...] = a
        z_ref[...] = _nt(a, w_ref[...]).astype(BF16)

    return pl.pallas_call(
        body, name="f1_norm_in", grid=(t // tm,),
        in_specs=[_rows(tm, D_MODEL), _whole((1, D_MODEL)), _whole((D_IN, D_MODEL))],
        out_specs=[_rows(tm, D_IN), _rows(tm, D_MODEL)],
        out_shape=[_sds((t, D_IN), BF16), _sds((t, D_MODEL), BF16)],
        compiler_params=_params())(h, g, w_t)


def _alibi_bias():
    ti = jnp.arange(BLK)[:, None]
    ji = jnp.arange(2 * BLK)[None, :]
    dist = ti + BLK - ji
    band = (dist >= 0) & (dist < BLK)
    bias = -jnp.asarray(ALIBI_SLOPES, F32)[:, None, None] * dist.astype(F32)[None]
    return jnp.stack([jnp.where((band & (ji >= BLK))[None], bias, NEG_BIG), jnp.where(band[None], bias, NEG_BIG)])


def _strips():
    return [slice(r * STRIP, (r + 1) * STRIP) for r in range(BLK // STRIP)]


def _bias_spec(transposed=False):
    tile = (2 * BLK, BLK) if transposed else (BLK, 2 * BLK)
    return pl.BlockSpec((None, N_Q_HEADS) + tile, lambda i: (jnp.minimum(i, 1), 0, 0, 0))


def _softmax_strip(s_ref, bias_ref, hq, rows, sink):
    sc = s_ref[hq, rows, :] + bias_ref[hq, rows, :]
    m = jnp.maximum(jnp.max(sc, axis=1, keepdims=True), sink)
    e = jnp.exp(sc - m)
    es = jnp.exp(sink - m)
    inv = 1.0 / (jnp.sum(e, axis=1, keepdims=True) + es)
    return e * inv, es * inv


def _kv_window(z_ref, kp_ref, vp_ref):
    kcat = jnp.concatenate([kp_ref[...], z_ref[:, ATTN_W:ATTN_W + KV_W]], axis=0).astype(F32)
    vcat = jnp.concatenate([vp_ref[...], z_ref[:, ATTN_W + KV_W:ATTN_W + 2 * KV_W]], axis=0).astype(F32)
    kswap = pltpu.roll(kcat, HEAD_DIM, 1)
    vswap = pltpu.roll(vcat, HEAD_DIM, 1)
    return kcat.astype(BF16), kswap.astype(BF16), vcat, vswap


def _gelu_and_grad(x):
    t = jnp.tanh(GELU_C0 * (x + GELU_C1 * x * x * x))
    return 0.5 * x * (1.0 + t), 0.5 * (1.0 + t) + 0.5 * x * (1.0 - t * t) * GELU_C0 * (1.0 + 3.0 * GELU_C1 * x * x)


def _layernorm_strip(gv, lng_ref, lnb_ref):
    xc = gv - jnp.mean(gv, axis=-1, keepdims=True)
    rstd = lax.rsqrt(jnp.mean(xc * xc, axis=-1, keepdims=True) + NORM_EPS)
    xhat = xc * rstd
    return xhat * lng_ref[...] + lnb_ref[...], xhat, rstd


def _tril_w(ws_ref, h):
    ti = lax.broadcasted_iota(jnp.int32, (BLK, BLK), 0)
    si = lax.broadcasted_iota(jnp.int32, (BLK, BLK), 1)
    causal = si <= ti
    return jnp.where(causal, ws_ref[h], 0.0).astype(BF16), causal


def _gm_mixed(vn, ws_ref, bs_ref, lo, hi):
    slabs = []
    for s in range(GM_W // LANES):
        vs = vn[:, s * LANES:(s + 1) * LANES]
        w0, _ = _tril_w(ws_ref, 2 * s)
        w1, _ = _tril_w(ws_ref, 2 * s + 1)
        mixed = (_mm(w0, jnp.where(lo, vs, 0.0).astype(BF16))
                 + _mm(w1, jnp.where(hi, vs, 0.0).astype(BF16))
                 + bs_ref[:, s * LANES:(s + 1) * LANES])
        slabs.append(mixed)
    return slabs


def _block_specs_z(nb):
    prev = lambda i: (jnp.maximum(i - 1, 0), ATTN_W // KV_W)
    prev_v = lambda i: (jnp.maximum(i - 1, 0), ATTN_W // KV_W + 1)
    return [_rows(BLK, D_IN), pl.BlockSpec((BLK, KV_W), prev), pl.BlockSpec((BLK, KV_W), prev_v)]


def _heads():
    return [(2 * s + half, s, half, s // 2 == half) for s in range(ATTN_W // LANES) for half in range(2)]


def _f2_attn_gm(z, sinks, bias, ln_g, ln_b, ws, bs_full):
    t = z.shape[0]
    nb = t // BLK

    def body(z_ref, kp_ref, vp_ref, sink_ref, bias_ref, lng_ref, lnb_ref, ws_ref, bs_ref, am_ref,
             s_ref, p_ref, u_ref, vn_ref):
        lane = lax.broadcasted_iota(jnp.int32, (1, LANES), 1)
        lo = lane < HEAD_DIM
        hi = lane >= HEAD_DIM
        kc, ks, vcat, vswap = _kv_window(z_ref, kp_ref, vp_ref)
        for hq, s, half, same in _heads():
            qs = z_ref[:, s * LANES:(s + 1) * LANES].astype(F32) * (HEAD_DIM ** -0.5)
            qm = jnp.where(lo if half == 0 else hi, qs, 0.0).astype(BF16)
            s_ref[hq] = _nt(qm, kc if same else ks)
        for hq in range(N_Q_HEADS):
            for rows in _strips():
                pr, _ = _softmax_strip(s_ref, bias_ref, hq, rows, sink_ref[hq])
                p_ref[hq, rows, :] = pr.astype(BF16)
        for s in range(ATTN_W // LANES):
            o = jnp.zeros((BLK, LANES), F32)
            for hq, hs, half, same in _heads():
                if hs == s:
                    vm = jnp.where(lo if half == 0 else hi, vcat if same else vswap, 0.0).astype(BF16)
                    o = o + _mm(p_ref[hq], vm)
            am_ref[:, s * LANES:(s + 1) * LANES] = o.astype(BF16)

        for rows in _strips():
            u_ref[rows, :] = _gelu(z_ref[rows, ATTN_W + 2 * KV_W:ATTN_W + 2 * KV_W + GM_W].astype(F32))
            vn_ref[rows, :], _, _ = _layernorm_strip(
                _gelu(z_ref[rows, ATTN_W + 2 * KV_W + GM_W:D_IN].astype(F32)), lng_ref, lnb_ref)
        mixed = _gm_mixed(vn_ref, ws_ref, bs_ref, lo, hi)
        for s in range(GM_W // LANES):
            am_ref[:, ATTN_W + s * LANES:ATTN_W + (s + 1) * LANES] = (
                u_ref[:, s * LANES:(s + 1) * LANES] * mixed[s]).astype(BF16)

    return pl.pallas_call(
        body, name="f2_attn_gm", grid=(nb,),
        in_specs=_block_specs_z(nb) + [
            pl.BlockSpec(memory_space=pltpu.SMEM), _bias_spec(), _whole((1, GM_W)),
            _whole((1, GM_W)), _whole((N_Q_HEADS, BLK, BLK)), _whole((BLK, GM_W))],
        out_specs=_rows(BLK, ATTN_W + GM_W),
        out_shape=_sds((t, ATTN_W + GM_W), BF16),
        scratch_shapes=[pltpu.VMEM((N_Q_HEADS, BLK, 2 * BLK), F32), pltpu.VMEM((N_Q_HEADS, BLK, 2 * BLK), BF16),
                        pltpu.VMEM((BLK, GM_W), F32), pltpu.VMEM((BLK, GM_W), F32)],
        compiler_params=_params())(z, z, z, sinks, bias, ln_g, ln_b, ws, bs_full)


def _f3_mix_out(am, h, ga, gg, w, gpost):
    t = h.shape[0]
    tm = _tile(t, 512)

    def body(am_ref, h_ref, ga_ref, gg_ref, w_ref, gp_ref, heads_ref, mix_ref, h1_ref):
        heads = jnp.concatenate([_rms(am_ref[:, :ATTN_W].astype(F32), ga_ref[...]),
                                 _rms(am_ref[:, ATTN_W:].astype(F32), gg_ref[...])], axis=1).astype(BF16)
        heads_ref[...] = heads
        mix = _mm(heads, w_ref[...])
        mix_ref[...] = mix.astype(BF16)
        h1_ref[...] = h_ref[...] + _rms(mix, gp_ref[...])

    return pl.pallas_call(
        body, name="f3_mix_out", grid=(t // tm,),
        in_specs=[_rows(tm, D_MODEL), _rows(tm, D_MODEL), _whole((1, ATTN_W)), _whole((1, GM_W)),
                  _whole((D_MODEL, D_MODEL)), _whole((1, D_MODEL))],
        out_specs=[_rows(tm, D_MODEL)] * 3,
        out_shape=[_sds((t, D_MODEL), BF16), _sds((t, D_MODEL), BF16), _sds((t, D_MODEL), F32)],
        compiler_params=_params())(am, h, ga, gg, w, gpost)


def _f4a_ffn_in(h1, gf, wg_t, wu_t):
    t = h1.shape[0]
    tm = _tile(t, 256)

    def body(h_ref, g_ref, wg_ref, wu_ref, f_ref, gt_ref, up_ref):
        f = _rms(h_ref[...], g_ref[...]).astype(BF16)
        f_ref[...] = f
        gt_ref[...] = _nt(f, wg_ref[...]).astype(BF16)
        up_ref[...] = _nt(f, wu_ref[...]).astype(BF16)

    return pl.pallas_call(
        body, name="f4a_ffn_in", grid=(t // tm,),
        in_specs=[_rows(tm, D_MODEL), _whole((1, D_MODEL)), _whole((D_FF, D_MODEL)), _whole((D_FF, D_MODEL))],
        out_specs=[_rows(tm, D_MODEL), _rows(tm, D_FF), _rows(tm, D_FF)],
        out_shape=[_sds((t, D_MODEL), BF16), _sds((t, D_FF), BF16), _sds((t, D_FF), BF16)],
        compiler_params=_params())(h1, gf, wg_t, wu_t)


def _f4b_ffn_out(gt, up, wd, h1, gfp):
    t = h1.shape[0]
    tm = _tile(t, 256)

    def body(gt_ref, up_ref, wd_ref, h_ref, g_ref, dn_ref, h2_ref):
        gt = gt_ref[...].astype(F32)
        act = (gt * _sigmoid(gt) * up_ref[...].astype(F32)).astype(BF16)
        dn = _mm(act, wd_ref[...])
        dn_ref[...] = dn.astype(BF16)
        h2_ref[...] = h_ref[...] + _rms(dn, g_ref[...])

    return pl.pallas_call(
        body, name="f4b_ffn_out", grid=(t // tm,),
        in_specs=[_rows(tm, D_FF), _rows(tm, D_FF), _whole((D_FF, D_MODEL)), _rows(tm, D_MODEL),
                  _whole((1, D_MODEL))],
        out_specs=[_rows(tm, D_MODEL)] * 2,
        out_shape=[_sds((t, D_MODEL), BF16), _sds((t, D_MODEL), F32)],
        compiler_params=_params())(gt, up, wd, h1, gfp)


def _f5_ple(h2, gpl, wpg, p, wple):
    t = h2.shape[0]
    tm = _tile(t, 512)

    def body(h_ref, g_ref, wpg_ref, p_ref, wple_ref, r_ref, pg_ref, pe_ref, h3_ref):
        h = h_ref[...]
        r = _rms(h, g_ref[...]).astype(BF16)
        r_ref[...] = r
        pg = _mm(r, wpg_ref[...])
        pe = _mm(p_ref[...].astype(BF16), wple_ref[...])
        pg_ref[...] = pg.astype(BF16)
        pe_ref[...] = pe.astype(BF16)
        h3_ref[...] = h + pe * _sigmoid(pg)

    return pl.pallas_call(
        body, name="f5_ple", grid=(t // tm,),
        in_specs=[_rows(tm, D_MODEL), _whole((1, D_MODEL)), _whole((D_MODEL, D_MODEL)), _rows(tm, PLE_DIM),
                  _whole((PLE_DIM, D_MODEL))],
        out_specs=[_rows(tm, D_MODEL)] * 4,
        out_shape=[_sds((t, D_MODEL), BF16)] * 3 + [_sds((t, D_MODEL), F32)],
        compiler_params=_params())(h2, gpl, wpg, p, wple)


def _loss_head(y, target):
    t = y.shape[0]
    tm = _tile(t, 512)

    def body(y_ref, t_ref, sq_ref, dy_ref):
        err = y_ref[...] - t_ref[...]
        dy_ref[...] = err * (1.0 / D_MODEL)
        _accumulate(sq_ref, jnp.sum(err * err, keepdims=True))

    return pl.pallas_call(
        body, name="loss_head", grid=(t // tm,),
        in_specs=[_rows(tm, D_MODEL)] * 2,
        out_specs=[_whole((1, LANES)), _rows(tm, D_MODEL)],
        out_shape=[_sds((1, LANES), F32), _sds((t, D_MODEL), F32)],
        compiler_params=_params())(y, target)


def _b5_ple(dh3, h2, pg, pe, gpl, wpg):
    t = h2.shape[0]
    tm = _tile(t, 512)

    def body(dh_ref, h_ref, pg_ref, pe_ref, g_ref, w_ref, dpe_ref, dpg_ref, dh2_ref, dg_ref):
        dh = dh_ref[...]
        s = _sigmoid(pg_ref[...].astype(F32))
        dpe_ref[...] = (dh * s).astype(BF16)
        dpg = (dh * pe_ref[...].astype(F32) * s * (1.0 - s)).astype(BF16)
        dpg_ref[...] = dpg
        dx, dg = _rms_bwd(_nt(dpg, w_ref[...]), h_ref[...], g_ref[...])
        dh2_ref[...] = dh + dx
        _accumulate(dg_ref, dg)

    return pl.pallas_call(
        body, name="b5_ple", grid=(t // tm,),
        in_specs=[_rows(tm, D_MODEL)] * 4 + [_whole((1, D_MODEL)), _whole((D_MODEL, D_MODEL))],
        out_specs=[_rows(tm, D_MODEL)] * 3 + [_whole((1, D_MODEL))],
        out_shape=[_sds((t, D_MODEL), BF16)] * 2 + [_sds((t, D_MODEL), F32), _sds((1, D_MODEL), F32)],
        compiler_params=_params())(dh3, h2, pg, pe, gpl, wpg)


def _b4a_ffn_out(dh2, dn, gfp, wd, gt, up):
    t = dh2.shape[0]
    tm = _tile(t, 256)

    def body(dh_ref, dn_ref, g_ref, wd_ref, gt_ref, up_ref, ddn_ref, act_ref, dgt_ref, dup_ref, dg_ref):
        ddn, dg = _rms_bwd(dh_ref[...], dn_ref[...].astype(F32), g_ref[...])
        _accumulate(dg_ref, dg)
        ddn = ddn.astype(BF16)
        ddn_ref[...] = ddn
        dact = _nt(ddn, wd_ref[...])
        gt = gt_ref[...].astype(F32)
        up = up_ref[...].astype(F32)
        sg = _sigmoid(gt)
        silu = gt * sg
        act_ref[...] = (silu * up).astype(BF16)
        dup_ref[...] = (dact * silu).astype(BF16)
        dgt_ref[...] = (dact * up * (sg * (1.0 + gt * (1.0 - sg)))).astype(BF16)

    return pl.pallas_call(
        body, name="b4a_ffn_out", grid=(t // tm,),
        in_specs=[_rows(tm, D_MODEL), _rows(tm, D_MODEL), _whole((1, D_MODEL)), _whole((D_FF, D_MODEL)),
                  _rows(tm, D_FF), _rows(tm, D_FF)],
        out_specs=[_rows(tm, D_MODEL), _rows(tm, D_FF), _rows(tm, D_FF), _rows(tm, D_FF), _whole((1, D_MODEL))],
        out_shape=[_sds((t, D_MODEL), BF16)] + [_sds((t, D_FF), BF16)] * 3 + [_sds((1, D_MODEL), F32)],
        compiler_params=_params())(dh2, dn, gfp, wd, gt, up)


def _b4b_ffn_in(dgt, dup, wg_t, wu_t, h1, gf, dh2):
    t = h1.shape[0]
    tm = _tile(t, 512)

    def body(dgt_ref, dup_ref, wg_ref, wu_ref, h_ref, g_ref, dh_ref, dh1_ref, dg_ref):
        df = _mm(dgt_ref[...], wg_ref[...]) + _mm(dup_ref[...], wu_ref[...])
        dx, dg = _rms_bwd(df, h_ref[...], g_ref[...])
        dh1_ref[...] = dh_ref[...] + dx
        _accumulate(dg_ref, dg)

    return pl.pallas_call(
        body, name="b4b_ffn_in", grid=(t // tm,),
        in_specs=[_rows(tm, D_FF), _rows(tm, D_FF), _whole((D_FF, D_MODEL)), _whole((D_FF, D_MODEL)),
                  _rows(tm, D_MODEL), _whole((1, D_MODEL)), _rows(tm, D_MODEL)],
        out_specs=[_rows(tm, D_MODEL), _whole((1, D_MODEL))],
        out_shape=[_sds((t, D_MODEL), F32), _sds((1, D_MODEL), F32)],
        compiler_params=_params())(dgt, dup, wg_t, wu_t, h1, gf, dh2)


def _b3_mix_out(dh1, mix, gpost, w, am, ga, gg):
    t = dh1.shape[0]
    tm = _tile(t, 512)

    def body(dh_ref, mix_ref, gp_ref, w_ref, am_ref, ga_ref, gg_ref, dmix_ref, dam_ref, dgp_ref, dga_ref, dgg_ref):
        dmix, dgp = _rms_bwd(dh_ref[...], mix_ref[...].astype(F32), gp_ref[...])
        _accumulate(dgp_ref, dgp)
        dmix = dmix.astype(BF16)
        dmix_ref[...] = dmix
        dheads = _nt(dmix, w_ref[...])
        dat, dga = _rms_bwd(dheads[:, :ATTN_W], am_ref[:, :ATTN_W].astype(F32), ga_ref[...])
        dgm, dgg = _rms_bwd(dheads[:, ATTN_W:], am_ref[:, ATTN_W:].astype(F32), gg_ref[...])
        dam_ref[:, :ATTN_W] = dat.astype(BF16)
        dam_ref[:, ATTN_W:] = dgm.astype(BF16)
        _accumulate(dga_ref, dga)
        _accumulate(dgg_ref, dgg)

    return pl.pallas_call(
        body, name="b3_mix_out", grid=(t // tm,),
        in_specs=[_rows(tm, D_MODEL), _rows(tm, D_MODEL), _whole((1, D_MODEL)), _whole((D_MODEL, D_MODEL)),
                  _rows(tm, D_MODEL), _whole((1, ATTN_W)), _whole((1, GM_W))],
        out_specs=[_rows(tm, D_MODEL), _rows(tm, D_MODEL), _whole((1, D_MODEL)), _whole((1, ATTN_W)),
                   _whole((1, GM_W))],
        out_shape=[_sds((t, D_MODEL), BF16), _sds((t, D_MODEL), BF16), _sds((1, D_MODEL), F32),
                   _sds((1, ATTN_W), F32), _sds((1, GM_W), F32)],
        compiler_params=_params())(dh1, mix, gpost, w, am, ga, gg)


def _b2_attn_gm(dam, z, sinks, bias, ln_g, ln_b, ws, bs_full):
    t = z.shape[0]
    nb = t // BLK

    def body(dam_ref, z_ref, kp_ref, vp_ref, sink_ref, bias_ref, lng_ref, lnb_ref, ws_ref, wst_ref, bs_ref,
             dzq_ref, dkv_ref, dzuv_ref, dsink_ref, dlng_ref, dlnb_ref, dws_ref, dbs_ref,
             sc_ref, dp_ref, p_ref, ds_ref, u_ref, du_ref, dv_ref, vn_ref, xhat_ref, rstd_ref, dvn_ref):
        n = pl.program_id(0)
        lane = lax.broadcasted_iota(jnp.int32, (1, LANES), 1)
        lo = lane < HEAD_DIM
        hi = lane >= HEAD_DIM
        sub = lax.broadcasted_iota(jnp.int32, (LANES, 1), 0)
        kc, ks, vcat, vswap = _kv_window(z_ref, kp_ref, vp_ref)
        vc = vcat.astype(BF16)
        vs_ = vswap.astype(BF16)
        kc_t = kc.T
        ks_t = ks.T

        def operands(s, half):
            mask = lo if half == 0 else hi
            qs = z_ref[:, s * LANES:(s + 1) * LANES].astype(F32) * (HEAD_DIM ** -0.5)
            qm = jnp.where(mask, qs, 0.0).astype(BF16)
            dom = jnp.where(mask, dam_ref[:, s * LANES:(s + 1) * LANES], 0.0).astype(BF16)
            return qm, dom

        for hq, s, half, same in _heads():
            qm, dom = operands(s, half)
            sc_ref[hq] = _nt(kc if same else ks, qm)
            dp_ref[hq] = _nt(vc if same else vs_, dom)
        dsink = jnp.zeros((1, LANES), F32)
        for hq in range(N_Q_HEADS):
            sink = sink_ref[hq]
            sc = sc_ref[hq] + bias_ref[hq]
            m = jnp.maximum(jnp.max(sc, axis=0, keepdims=True), sink)
            e = jnp.exp(sc - m)
            es = jnp.exp(sink - m)
            inv = 1.0 / (jnp.sum(e, axis=0, keepdims=True) + es)
            pr = e * inv
            dpr = dp_ref[hq]
            row = jnp.sum(dpr * pr, axis=0, keepdims=True)
            p_ref[hq] = pr.astype(BF16)
            ds_ref[hq] = (pr * (dpr - row)).astype(BF16)
            dsink = dsink + jnp.where(lane == hq, -jnp.sum(es * inv * row, keepdims=True), 0.0)
        dk_acc = [jnp.zeros((2 * BLK, LANES), F32), jnp.zeros((2 * BLK, LANES), F32)]
        dv_acc = [jnp.zeros((2 * BLK, LANES), F32), jnp.zeros((2 * BLK, LANES), F32)]
        for s in range(ATTN_W // LANES):
            dq_t = jnp.zeros((LANES, BLK), F32)
            for hq, hs, half, same in _heads():
                if hs != s:
                    continue
                qm, dom = operands(s, half)
                ds = ds_ref[hq]
                dk_acc[same] = dk_acc[same] + _mm(ds, qm)
                dv_acc[same] = dv_acc[same] + _mm(p_ref[hq], dom)
                in_half = (sub < HEAD_DIM) if half == 0 else (sub >= HEAD_DIM)
                dq_t = dq_t + jnp.where(in_half, _mm(kc_t if same else ks_t, ds), 0.0)
            dzq_ref[:, s * LANES:(s + 1) * LANES] = (dq_t.T * (HEAD_DIM ** -0.5)).astype(BF16)
        dk_acc = dk_acc[True] + pltpu.roll(dk_acc[False], HEAD_DIM, 1)
        dv_acc = dv_acc[True] + pltpu.roll(dv_acc[False], HEAD_DIM, 1)
        cur = pl.multiple_of(n * BLK, BLK)
        dkv_ref[pl.ds(cur, BLK), 0:KV_W] = dk_acc[BLK:, :]
        dkv_ref[pl.ds(cur, BLK), KV_W:2 * KV_W] = dv_acc[BLK:, :]

        @pl.when(n > 0)
        def _():
            prv = pl.multiple_of((n - 1) * BLK, BLK)
            dkv_ref[pl.ds(prv, BLK), 0:KV_W] += dk_acc[:BLK, :]
            dkv_ref[pl.ds(prv, BLK), KV_W:2 * KV_W] += dv_acc[:BLK, :]

        _accumulate(dsink_ref, dsink)

        for rows in _strips():
            u_ref[rows, :], du_ref[rows, :] = _gelu_and_grad(
                z_ref[rows, ATTN_W + 2 * KV_W:ATTN_W + 2 * KV_W + GM_W].astype(F32))
            gv, dv_ref[rows, :] = _gelu_and_grad(z_ref[rows, ATTN_W + 2 * KV_W + GM_W:D_IN].astype(F32))
            vn_ref[rows, :], xhat_ref[rows, :], rstd = _layernorm_strip(gv, lng_ref, lnb_ref)
            rstd_ref[rows, :] = jnp.broadcast_to(rstd, (STRIP, LANES))
        mixed = _gm_mixed(vn_ref, ws_ref, bs_ref, lo, hi)

        @pl.when(n == 0)
        def _():
            dws_ref[...] = jnp.zeros_like(dws_ref)

        dbs = jnp.zeros((BLK, LANES), F32)
        for s in range(GM_W // LANES):
            slab = slice(s * LANES, (s + 1) * LANES)
            dgm = dam_ref[:, ATTN_W + s * LANES:ATTN_W + (s + 1) * LANES]
            dzuv_ref[:, slab] = (dgm * mixed[s] * du_ref[:, slab]).astype(BF16)
            dmx = dgm * u_ref[:, slab]
            vsb = vn_ref[:, slab].astype(BF16)
            dvn = jnp.zeros((BLK, LANES), F32)
            for half in range(2):
                h = 2 * s + half
                mask = lo if half == 0 else hi
                dmm = jnp.where(mask, dmx, 0.0)
                dbs = dbs + jnp.where(lane == h, jnp.sum(dmm, axis=1, keepdims=True), 0.0)
                dmm = dmm.astype(BF16)
                _, causal = _tril_w(ws_ref, h)
                ti = lax.broadcasted_iota(jnp.int32, (BLK, BLK), 0)
                si = lax.broadcasted_iota(jnp.int32, (BLK, BLK), 1)
                wt_t = jnp.where(ti <= si, wst_ref[h], 0.0).astype(BF16)
                dvn = dvn + jnp.where(mask, _mm(wt_t, dmm), 0.0)
                dws_ref[h] += jnp.where(causal, _nt(dmm, vsb), 0.0)
            dvn_ref[:, slab] = dvn
        _accumulate(dbs_ref, dbs)
        dlnb = jnp.zeros((STRIP, GM_W), F32)
        dlng = jnp.zeros((STRIP, GM_W), F32)
        for rows in _strips():
            dvn = dvn_ref[rows, :]
            xhat = xhat_ref[rows, :]
            dlnb = dlnb + dvn
            dlng = dlng + dvn * xhat
            dxh = dvn * lng_ref[...]
            dgv = rstd_ref[rows, 0:1] * (dxh - jnp.mean(dxh, axis=-1, keepdims=True)
                                         - xhat * jnp.mean(dxh * xhat, axis=-1, keepdims=True))
            dzuv_ref[rows, GM_W:] = (dgv * dv_ref[rows, :]).astype(BF16)
        _accumulate(dlnb_ref, jnp.sum(dlnb, axis=0, keepdims=True))
        _accumulate(dlng_ref, jnp.sum(dlng, axis=0, keepdims=True))

    return pl.pallas_call(
        body, name="b2_attn_gm", grid=(nb,),
        in_specs=[_rows(BLK, ATTN_W + GM_W)] + _block_specs_z(nb) + [
            pl.BlockSpec(memory_space=pltpu.SMEM), _bias_spec(transposed=True), _whole((1, GM_W)),
            _whole((1, GM_W)), _whole((N_Q_HEADS, BLK, BLK)), _whole((N_Q_HEADS, BLK, BLK)), _whole((BLK, GM_W))],
        out_specs=[_rows(BLK, ATTN_W), _whole((t, 2 * KV_W)), _rows(BLK, 2 * GM_W), _whole((1, LANES)),
                   _whole((1, GM_W)), _whole((1, GM_W)), _whole((N_Q_HEADS, BLK, BLK)), _whole((BLK, LANES))],
        out_shape=[_sds((t, ATTN_W), BF16), _sds((t, 2 * KV_W), F32), _sds((t, 2 * GM_W), BF16),
                   _sds((1, LANES), F32), _sds((1, GM_W), F32), _sds((1, GM_W), F32),
                   _sds((N_Q_HEADS, BLK, BLK), F32), _sds((BLK, LANES), F32)],
        scratch_shapes=[pltpu.VMEM((N_Q_HEADS, 2 * BLK, BLK), F32)] * 2
        + [pltpu.VMEM((N_Q_HEADS, 2 * BLK, BLK), BF16)] * 2 + [pltpu.VMEM((BLK, GM_W), F32)] * 5
        + [pltpu.VMEM((BLK, LANES), F32), pltpu.VMEM((BLK, GM_W), F32)],
        compiler_params=_params())(dam, z, z, z, sinks, jnp.swapaxes(bias, 2, 3), ln_g, ln_b, ws,
                                   jnp.swapaxes(ws, 1, 2), bs_full)


def _b1_norm_in(dzq, dkv, dzuv, w_t, h, g, dh1):
    t = h.shape[0]
    tm = _tile(t, 512)

    def body(dzq_ref, dkv_ref, dzuv_ref, w_ref, h_ref, g_ref, dh_ref, dh0_ref, dg_ref):
        dz = jnp.concatenate([dzq_ref[...], dkv_ref[...].astype(BF16), dzuv_ref[...]], axis=1)
        dx, dg = _rms_bwd(_mm(dz, w_ref[...]), h_ref[...], g_ref[...])
        dh0_ref[...] = dh_ref[...] + dx
        _accumulate(dg_ref, dg)

    return pl.pallas_call(
        body, name="b1_norm_in", grid=(t // tm,),
        in_specs=[_rows(tm, ATTN_W), _rows(tm, 2 * KV_W), _rows(tm, 2 * GM_W), _whole((D_IN, D_MODEL)),
                  _rows(tm, D_MODEL), _whole((1, D_MODEL)), _rows(tm, D_MODEL)],
        out_specs=[_rows(tm, D_MODEL), _whole((1, D_MODEL))],
        out_shape=[_sds((t, D_MODEL), F32), _sds((1, D_MODEL), F32)],
        compiler_params=_params())(dzq, dkv, dzuv, w_t, h, g, dh1)


def _weight_grad(x, dy, name):
    xs = tuple(x) if isinstance(x, (tuple, list)) else (x,)
    t = xs[0].shape[0]
    k = sum(a.shape[1] for a in xs)
    n = dy.shape[1]
    tm = _tile(t, 1024)
    steps = t // tm

    def body(*refs):
        x_refs, dy_ref, dw_ref, acc_ref = refs[:len(xs)], refs[-3], refs[-2], refs[-1]
        i = pl.program_id(0)

        def product():
            cols = [r[...].astype(BF16) for r in x_refs]
            return _tn(cols[0] if len(cols) == 1 else jnp.concatenate(cols, axis=1), dy_ref[...])

        @pl.when(i == 0)
        def _():
            acc_ref[...] = product()

        @pl.when(i > 0)
        def _():
            acc_ref[...] += product()

        @pl.when(i == steps - 1)
        def _():
            dw_ref[...] = acc_ref[...].astype(BF16)

    return pl.pallas_call(
        body, name=name, grid=(steps,),
        in_specs=[_rows(tm, a.shape[1]) for a in xs] + [_rows(tm, n)],
        out_specs=_whole((k, n)),
        out_shape=_sds((k, n), BF16),
        scratch_shapes=[pltpu.VMEM((k, n), F32)],
        compiler_params=_params())(*xs, dy)


_ANY = pl.BlockSpec(memory_space=pl.ANY)
_HBM = pl.BlockSpec(memory_space=pltpu.HBM)
_SEM = pl.BlockSpec(memory_space=pltpu.SEMAPHORE)
_EFFECT = pltpu.SideEffectType.DATAFLOW_SIDE_EFFECTING


def _mesh_pos():
    return lax.axis_index("x"), lax.axis_index("y"), lax.axis_index("c")


def _other_chips(x, y):
    return [(1 - x, y), (x, 1 - y), (1 - x, 1 - y)]


def _peers(mode, x, y, c):
    if mode == "devices":
        peers = []
        for j in range(1, N_DEV):
            px = 1 - x if (j >> 2) & 1 else x
            py = 1 - y if (j >> 1) & 1 else y
            pc = 1 - c if j & 1 else c
            peers.append(((px, py, pc), 4 * px + 2 * py + pc))
        return peers, 4 * x + 2 * y + c
    if mode == "sibling":
        return [((x, y, 1 - c), None)], None
    return [((px, py, c), 2 * px + py) for px, py in _other_chips(x, y)], 2 * x + y


def _descriptors(mode, srcs, lands, send_sems, recv_sems, with_incoming=True):
    x, y, c = _mesh_pos()
    peers, me = _peers(mode, x, y, c)
    scatter = mode == "scatter"
    outgoing, incoming = [], []
    for i, (src, land) in enumerate(zip(srcs, lands)):
        for j, (dev, slot) in enumerate(peers):
            sem = i * len(peers) + j
            common = dict(send_sem=send_sems.at[sem], recv_sem=recv_sems.at[sem], device_id=dev,
                          device_id_type=pl.DeviceIdType.MESH)
            whole = mode == "sibling"
            outgoing.append(pltpu.make_async_remote_copy(
                src_ref=src.at[slot] if scatter else src, dst_ref=land if whole else land.at[me], **common))
            if with_incoming:
                incoming.append(pltpu.make_async_remote_copy(
                    src_ref=src.at[me] if scatter else src, dst_ref=land if whole else land.at[slot], **common))
    return outgoing, incoming


def _n_sems(mode, k):
    return k * {"devices": N_DEV - 1, "sibling": 1}.get(mode, N_CHIPS - 1)


def _exchange_start(srcs, lands, mode, name, after=None):
    k = len(srcs)
    arrs = [*srcs, *lands]

    def body(*refs):
        skip = 1 if after is not None else 0
        send_sems, recv_sems = refs[2 * k + skip], refs[2 * k + skip + 1]
        outgoing, _ = _descriptors(mode, refs[:k], refs[k:2 * k], send_sems, recv_sems, with_incoming=False)
        for cp in outgoing:
            cp.start()
        refs[-1][...] = jnp.zeros_like(refs[-1])

    operands = [pltpu.with_memory_space_constraint(a, pltpu.HBM) for a in arrs]
    in_specs = [_HBM] * (2 * k)
    if after is not None:
        operands.append(after)
        in_specs.append(_ANY)
    sems = pltpu.SemaphoreType.DMA((_n_sems(mode, k),))
    res = pl.pallas_call(
        body, name=name, in_specs=in_specs,
        out_shape=(sems, sems, *[pltpu.HBM(a.shape, a.dtype) for a in arrs], _sds((8, LANES), F32)),
        out_specs=(_SEM, _SEM, *[_HBM] * (2 * k), pl.BlockSpec(memory_space=pltpu.VMEM)),
        input_output_aliases={i: 2 + i for i in range(2 * k)},
        compiler_params=pltpu.CompilerParams(has_side_effects=_EFFECT))(*operands)
    return dict(mode=mode, send=res[0], recv=res[1], srcs=res[2:2 + k], lands=res[2 + k:2 + 2 * k], token=res[-1])


def _exchange_wait(flight, name, after):
    mode, k = flight["mode"], len(flight["srcs"])
    arrs = [*flight["srcs"], *flight["lands"]]

    def body(*refs):
        outgoing, incoming = _descriptors(mode, refs[:k], refs[k:2 * k], refs[2 * k], refs[2 * k + 1])
        for cp in outgoing:
            cp.wait_send()
        for cp in incoming:
            cp.wait_recv()

    res = pl.pallas_call(
        body, name=name, in_specs=[_HBM] * (2 * k) + [_SEM, _SEM, _ANY],
        out_shape=tuple(pltpu.HBM(a.shape, a.dtype) for a in arrs), out_specs=tuple([_HBM] * (2 * k)),
        input_output_aliases={i: i for i in range(2 * k)},
        compiler_params=pltpu.CompilerParams(has_side_effects=_EFFECT))(*arrs, flight["send"], flight["recv"], after)
    return res[:k], res[k:]


def _adamw(w, g, m, v):
    m = ADAM_B1 * m + (1.0 - ADAM_B1) * g
    v = ADAM_B2 * v + (1.0 - ADAM_B2) * (g * g)
    m_hat = m / (1.0 - ADAM_B1 ** ADAM_STEP)
    v_hat = v / (1.0 - ADAM_B2 ** ADAM_STEP)
    delta = -ADAM_LR * (m_hat / (jnp.sqrt(v_hat) + ADAM_EPS) + ADAM_WD * w)
    return delta, m, v


def _row_tile(rows, cols, n_arrays):
    budget = VMEM_LIMIT_BYTES // 2
    padded = -(-cols // LANES) * LANES
    cap = min(rows, max(16, budget // (2 * n_arrays * padded * 4)))
    return max(tr for tr in range(16, cap + 1, 16) if rows % tr == 0)


def _sum_chips(landing, name):
    _, r, c = landing.shape
    tr = _row_tile(r, c, 4)

    def body(l_ref, s_ref):
        acc = l_ref[0].astype(F32)
        for s in range(1, N_CHIPS):
            acc = acc + l_ref[s].astype(F32)
        s_ref[...] = acc

    return pl.pallas_call(
        body, name=name, grid=(r // tr,),
        in_specs=[pl.BlockSpec((N_CHIPS, tr, c), lambda i: (0, i, 0))],
        out_specs=_rows(tr, c), out_shape=_sds((r, c), F32),
        compiler_params=_params())(landing)


def _adamw_big(mine, sibling, w, m, v, prev, layer, name):
    _, r, c = w.shape
    tr = _row_tile(r, c, 9)
    stacked = pl.BlockSpec((None, tr, c), lambda i: (layer, i, 0))

    def body(a_ref, b_ref, w_ref, m_ref, v_ref, *rest):
        g_out, d_out, m_out, v_out = rest[-4:]
        g = a_ref[...] + b_ref[...]
        g_out[...] = g
        d_out[...], m_out[...], v_out[...] = _adamw(w_ref[...], g, m_ref[...], v_ref[...])

    prev = list(prev) if prev is not None else []
    return pl.pallas_call(
        body, name=name, grid=(r // tr,),
        in_specs=[_rows(tr, c)] * 2 + [stacked] * 3 + [_ANY] * len(prev),
        out_specs=[stacked] * 4, out_shape=[_sds(w.shape, F32)] * 4,
        input_output_aliases={5 + j: j for j in range(len(prev))},
        compiler_params=_params())(mine, sibling, w, m, v, *prev)


def _adamw_small(gathered, w, m, v):
    r = w.shape[0]

    def body(a_ref, w_ref, m_ref, v_ref, g_out, d_out, m_out, v_out):
        g = a_ref[0]
        for d in range(1, N_DEV):
            g = g + a_ref[d]
        g_out[...] = g
        d_out[...], m_out[...], v_out[...] = _adamw(w_ref[...], g, m_ref[...], v_ref[...])

    return pl.pallas_call(
        body, name="adamw_small", grid=(1,),
        in_specs=[_whole((N_DEV, r, LANES))] + [_whole((r, LANES))] * 3,
        out_specs=[_whole((r, LANES))] * 4, out_shape=[_sds((r, LANES), F32)] * 4,
        compiler_params=_params())(gathered, w, m, v)


def _shard_view(name, stacked):
    return jnp.swapaxes(stacked, 1, 2) if name in TRANSPOSED else stacked


def _row_split(name):
    return name in TRANSPOSED or BIG[name] == 1


def _assemble(name, landed):
    _, r, c = landed.shape
    if _row_split(name):
        return landed.reshape(N_CHIPS * r, c)
    return landed.transpose(1, 0, 2).reshape(r, N_CHIPS * c)


def _split(name, whole):
    r, c = whole.shape
    if _row_split(name):
        return whole.reshape(N_CHIPS, r // N_CHIPS, c)
    return whole.reshape(r, N_CHIPS, c // N_CHIPS).transpose(1, 0, 2)


def _with_own_slot(landed, own, slot):
    return lax.dynamic_update_index_in_dim(landed, own, slot, 0)


def _pack_small(params, names):
    pieces = []
    for name in names:
        flat = params[name].reshape(-1)
        pieces.append(jnp.pad(flat, (0, -flat.shape[0] % SMALL_PAD)))
    return jnp.concatenate(pieces).reshape(-1, LANES)


def _unpack_small(packed, like, names):
    flat = packed.reshape(-1)
    out, off = {}, 0
    for name in names:
        size = like[name].size
        out[name] = flat[off:off + size].reshape(like[name].shape)
        off += size + (-size % SMALL_PAD)
    return out


def kernel(x, p, ln_mix_pre, w_in, attn_sinks, gm_ln_g, gm_ln_b, gm_ws, gm_bs, g_attn_out, g_gm_out, w_out, ln_mix_post, ln_ffn_pre, w_ffn_gate, w_ffn_up, w_ffn_down, ln_ffn_post, w_ple, ln_ple_gate, w_ple_gate, loss_target, m_ln_mix_pre, m_w_in, m_attn_sinks, m_gm_ln_g, m_gm_ln_b, m_gm_ws, m_gm_bs, m_g_attn_out, m_g_gm_out, m_w_out, m_ln_mix_post, m_ln_ffn_pre, m_w_ffn_gate, m_w_ffn_up, m_w_ffn_down, m_ln_ffn_post, m_w_ple, m_ln_ple_gate, m_w_ple_gate, v_ln_mix_pre, v_w_in, v_attn_sinks, v_gm_ln_g, v_gm_ln_b, v_gm_ws, v_gm_bs, v_g_attn_out, v_g_gm_out, v_w_out, v_ln_mix_post, v_ln_ffn_pre, v_w_ffn_gate, v_w_ffn_up, v_w_ffn_down, v_ln_ffn_post, v_w_ple, v_ln_ple_gate, v_w_ple_gate):
    given = dict(locals())
    wts = {n: given[n] for n in WEIGHTS}
    mom = {n: given["m_" + n] for n in WEIGHTS}
    var = {n: given["v_" + n] for n in WEIGHTS}
    depth = w_in.shape[0]
    h = x[0]
    target = loss_target[0]
    chip = 2 * lax.axis_index("x") + lax.axis_index("y")
    device = 2 * chip + lax.axis_index("c")
    row = lambda a, i: a[i][None, :]
    bs_full = [jnp.repeat(gm_bs[i].T, HEAD_DIM, axis=1) for i in range(depth)]
    bias = _alibi_bias()
    kinds = ("grad", "delta", "m", "v")
    wview = {n: _shard_view(n, wts[n]) for n in BIG_NAMES}
    mview = {n: _shard_view(n, mom[n]) for n in BIG_NAMES}
    vview = {n: _shard_view(n, var[n]) for n in BIG_NAMES}

    zero = lambda flight: flight["token"][0:1, 0:1]

    def start_gather(i, names, after, tag):
        shards = [wview[n][i].astype(BF16) for n in names]
        lands = [lax.empty((N_CHIPS,) + s.shape, BF16) for s in shards]
        return _exchange_start(shards, lands, "gather", f"gather_weights_start_{i}{tag}", after)

    def finish_gather(flight, names, i, after, tag):
        shards, lands = _exchange_wait(flight, f"gather_weights_wait_{i}{tag}", after)
        return {n: _assemble(n, _with_own_slot(l, s, chip)) for n, s, l in zip(names, shards, lands)}

    first, after = [], None
    for k, names in enumerate(GATHER_GROUPS):
        first.append(start_gather(0, names, after, "abc"[k]))
        after = first[-1]["token"]
    full = [finish_gather(first[0], GATHER_GROUPS[0], 0, after, "a")] + [None] * (depth - 1)
    saved = []
    for i in range(depth):
        w = full[i]
        g_in = row(ln_mix_pre, i)
        if i + 1 < depth:
            flight = start_gather(i + 1, BIG_NAMES, w['w_in'], "")
            g_in = g_in + zero(flight)
        z, a = _f1_norm_in(h, g_in, w['w_in'])
        am = _f2_attn_gm(z, attn_sinks[i], bias, row(gm_ln_g, i), row(gm_ln_b, i), gm_ws[i], bs_full[i])
        if i == 0:
            w.update(finish_gather(first[1], GATHER_GROUPS[1], 0, am, "b"))
        heads, mix, h1 = _f3_mix_out(am, h, row(g_attn_out, i), row(g_gm_out, i), w['w_out'], row(ln_mix_post, i))
        f, gt, up = _f4a_ffn_in(h1, row(ln_ffn_pre, i), w['w_ffn_gate'], w['w_ffn_up'])
        if i == 0:
            w.update(finish_gather(first[2], GATHER_GROUPS[2], 0, gt, "c"))
        dn, h2 = _f4b_ffn_out(gt, up, w['w_ffn_down'], h1, row(ln_ffn_post, i))
        r, pg, pe, h3 = _f5_ple(h2, row(ln_ple_gate, i), w['w_ple_gate'], p[i, 0], w['w_ple'])
        saved.append(dict(h=h, z=z, a=a, am=am, heads=heads, mix=mix, h1=h1, f=f, gt=gt, up=up, dn=dn, h2=h2,
                          r=r, pg=pg, pe=pe))
        h = h3
        if i + 1 < depth:
            full[i + 1] = finish_gather(flight, BIG_NAMES, i + 1, h3, "")

    sq, dh = _loss_head(h, target)
    loss = lax.psum(0.5 / D_MODEL * sq[0, 0], ("x", "y", "c"))

    chain = {n: None for n in BIG_NAMES}
    owed = []
    small_out = [{k: {} for k in kinds} for _ in range(depth)]

    def start_scatter(i, names, dws, after, tag):
        parts = [_split(n, dws[n]) for n in names]
        lands = [lax.empty(q.shape, BF16) for q in parts]
        return _exchange_start(parts, lands, "scatter", f"scatter_grads_start_{i}{tag}", after)

    def finish_scatter(i, names, flight, after, tag):
        parts, lands = _exchange_wait(flight, f"scatter_grads_wait_{i}{tag}", after)
        lands = [_with_own_slot(l, lax.dynamic_index_in_dim(q, chip, 0, keepdims=False), chip)
                 for q, l in zip(parts, lands)]
        partial = [_sum_chips(l, "sum_chips_" + n) for n, l in zip(names, lands)]
        flight = _exchange_start(partial, [lax.empty(q.shape, F32) for q in partial], "sibling",
                                 f"sibling_grads_start_{i}{tag}", after)

        def update(later):
            mine, theirs = _exchange_wait(flight, f"sibling_grads_wait_{i}{tag}", later)
            for n, a, b in zip(names, mine, theirs):
                chain[n] = _adamw_big(a, b, wview[n], mview[n], vview[n], chain[n], i, "adamw_" + n)

        owed.append(update)
        return flight["token"]

    def settle(later):
        while owed:
            owed.pop(0)(later)

    def start_small(i, names, small, after, tag):
        packed = _pack_small(small, names)
        land = lax.empty((N_DEV,) + packed.shape, F32)
        return _exchange_start([packed], [land], "devices", f"gather_small_grads_start_{i}{tag}", after)

    def finish_small(i, names, flight, after, tag):
        (packed,), (gathered,) = _exchange_wait(flight, f"gather_small_grads_wait_{i}{tag}", after)
        gathered = _with_own_slot(gathered, packed, device)
        layer = lambda d: _pack_small({n: d[n][i] for n in names}, names)
        res = _adamw_small(gathered, layer(wts), layer(mom), layer(var))
        for k, a in zip(kinds, res):
            small_out[i][k].update(_unpack_small(a, {n: wts[n][i] for n in names}, names))
        return gathered

    pending, done, behind = [], None, None
    for i in reversed(range(depth)):
        s, w = saved[i], full[i]
        last = i == 0
        dws, small = {}, {}
        gain = row(ln_ple_gate, i)
        if behind is not None:
            gain = gain + behind
        dpe, dpg, dh2, dg = _b5_ple(dh, s['h2'], s['pg'], s['pe'], gain, w['w_ple_gate'])
        small['ln_ple_gate'] = dg[0]
        dws['w_ple'] = _weight_grad(p[i, 0], dpe, "dw_ple")
        dws['w_ple_gate'] = _weight_grad(s['r'], dpg, "dw_ple_gate")

        ddn, act, dgt, dup, dg = _b4a_ffn_out(dh2, s['dn'], row(ln_ffn_post, i), w['w_ffn_down'], s['gt'], s['up'])
        small['ln_ffn_post'] = dg[0]
        dws['w_ffn_down'] = _weight_grad(act, ddn, "dw_ffn_down")
        dws['w_ffn_gate'] = _weight_grad(dgt, s['f'], "dw_ffn_gate")
        dws['w_ffn_up'] = _weight_grad(dup, s['f'], "dw_ffn_up")
        gain = row(ln_ffn_pre, i)
        if last:
            flight_a = start_scatter(i, SCATTER_GROUPS[0], dws, None, "a")
            gain = gain + zero(flight_a)
        dh1, dg = _b4b_ffn_in(dgt, dup, w['w_ffn_gate'], w['w_ffn_up'], s['h1'], gain, dh2)
        small['ln_ffn_pre'] = dg[0]

        dmix, dam, dgp, dga, dgg = _b3_mix_out(dh1, s['mix'], row(ln_mix_post, i), w['w_out'], s['am'],
                                               row(g_attn_out, i), row(g_gm_out, i))
        small['ln_mix_post'] = dgp[0]
        small['g_attn_out'] = dga[0]
        small['g_gm_out'] = dgg[0]
        dws['w_out'] = _weight_grad(s['heads'], dmix, "dw_out")
        gain = row(gm_ln_g, i)
        if last:
            flight_b = start_scatter(i, SCATTER_GROUPS[1], dws, flight_a["token"], "b")
            gain = gain + zero(flight_b)

        dzq, dkv, dzuv, dsink, dlng, dlnb, dgws, dbs = _b2_attn_gm(
            dam, s['z'], attn_sinks[i], bias, gain, row(gm_ln_b, i), gm_ws[i], bs_full[i])
        small['attn_sinks'] = dsink[0, :N_Q_HEADS]
        small['gm_ln_g'] = dlng[0]
        small['gm_ln_b'] = dlnb[0]
        small['gm_ws'] = dgws
        small['gm_bs'] = dbs[:, :N_Q_HEADS].T
        gain = row(ln_mix_pre, i)
        if last:
            flight_s = start_small(i, SMALL_EARLY, small, flight_b["token"], "a")
            gain = gain + zero(flight_s)

        dh, dg = _b1_norm_in(dzq, dkv, dzuv, w['w_in'], s['h'], gain, dh1)
        small['ln_mix_pre'] = dg[0]
        settle(dh)
        for finish in pending:
            done = finish(dh)
        pending = []
        if last:
            done = finish_scatter(i, SCATTER_GROUPS[0], flight_a, dh, "a")
            done = finish_scatter(i, SCATTER_GROUPS[1], flight_b, done, "b")
        dws['w_in'] = _weight_grad((dzq, dkv, dzuv), s['a'], "dw_in")
        if last:
            flight_c = start_scatter(i, SCATTER_GROUPS[2], dws, done, "c")
            flight_t = start_small(i, SMALL_LATE, small, flight_c["token"], "b")
            finish_small(i, SMALL_EARLY, flight_s, flight_t["token"], "a")
            done = finish_scatter(i, SCATTER_GROUPS[2], flight_c, flight_t["token"], "c")
            finish_small(i, SMALL_LATE, flight_t, flight_t["token"], "b")
            settle(done)
        else:
            flight_c = start_scatter(i, BIG_NAMES, dws, done, "")
            flight_s = start_small(i, SMALL_NAMES, small, flight_c["token"], "")
            behind = zero(flight_s)
            pending = [functools.partial(finish_small, i, SMALL_NAMES, flight_s, tag=""),
                       functools.partial(finish_scatter, i, BIG_NAMES, flight_c, tag="")]
    grad_x = dh[None]

    out = {k: {n: _shard_view(n, chain[n][j]) for n in BIG_NAMES} for j, k in enumerate(kinds)}
    for k in kinds:
        out[k].update({n: jnp.stack([small_out[i][k][n] for i in range(depth)]) for n in SMALL_NAMES})

    return (loss, grad_x, *[out["grad"][n] for n in WEIGHTS], *[out["delta"][n] for n in WEIGHTS],
            *[out["m"][n] for n in WEIGHTS], *[out["v"][n] for n in WEIGHTS])
```

```python
import functools
import math

import jax
import jax.numpy as jnp
from jax import lax
from jax.experimental import pallas as pl
from jax.experimental.pallas import tpu as pltpu

F32 = jnp.float32
BF16 = jnp.bfloat16

D_MODEL = 1024
HEAD_DIM = 64
N_Q_HEADS = 8
BLK = 128
ATTN_W = 512
KV_W = 128
GM_W = 512
D_IN = ATTN_W + 2 * KV_W + 2 * GM_W
D_FF = 2816
PLE_DIM = 256
DEPTH = 4
NORM_EPS = 1e-6
NEG_BIG = -1e30
N_CHIPS = 4
N_DEV = 8

ADAM_LR = 0.001
ADAM_B1 = 0.9
ADAM_B2 = 0.999
ADAM_EPS = 1e-08
ADAM_WD = 0.01
ADAM_STEP = 10

VMEM_LIMIT_BYTES = 56 * 1024 * 1024
LANES = 128
STRIP = 16
GELU_C0 = math.sqrt(2.0 / math.pi)
GELU_C1 = 0.044715
ALIBI_SLOPES = tuple(2.0 ** (-8.0 * (h + 1.0) / N_Q_HEADS) for h in range(N_Q_HEADS))

WEIGHTS = ['ln_mix_pre', 'w_in', 'attn_sinks', 'gm_ln_g', 'gm_ln_b', 'gm_ws', 'gm_bs', 'g_attn_out',
           'g_gm_out', 'w_out', 'ln_mix_post', 'ln_ffn_pre', 'w_ffn_gate', 'w_ffn_up', 'w_ffn_down',
           'ln_ffn_post', 'w_ple', 'ln_ple_gate', 'w_ple_gate']
BIG = {'w_in': 2, 'w_out': 1, 'w_ffn_gate': 2, 'w_ffn_up': 2, 'w_ffn_down': 1, 'w_ple': 2, 'w_ple_gate': 1}
BIG_NAMES = list(BIG)
TRANSPOSED = ('w_in', 'w_ffn_gate', 'w_ffn_up')
SMALL_NAMES = [n for n in WEIGHTS if n not in BIG]
SMALL_PAD = 1024
GATHER_GROUPS = (('w_in',), ('w_out', 'w_ffn_gate', 'w_ffn_up'), ('w_ffn_down', 'w_ple_gate', 'w_ple'))
SCATTER_GROUPS = (('w_ple', 'w_ple_gate', 'w_ffn_down', 'w_ffn_gate', 'w_ffn_up'), ('w_out',), ('w_in',))
SMALL_LATE = ('ln_mix_pre',)
SMALL_EARLY = tuple(n for n in SMALL_NAMES if n not in SMALL_LATE)


def _nt(a, b):
    return lax.dot_general(a, b, (((1,), (1,)), ((), ())), preferred_element_type=F32)


def _tn(a, b):
    return lax.dot_general(a, b, (((0,), (0,)), ((), ())), preferred_element_type=F32)


def _mm(a, b):
    return jnp.dot(a, b, preferred_element_type=F32)


def _rms(x, g):
    r = lax.rsqrt(jnp.mean(x * x, axis=-1, keepdims=True) + NORM_EPS)
    return x * r * g


def _rms_bwd(dy, x, g):
    r = lax.rsqrt(jnp.mean(x * x, axis=-1, keepdims=True) + NORM_EPS)
    xh = x * r
    dg = jnp.sum(dy * xh, axis=0, keepdims=True)
    dxh = dy * g
    dx = r * (dxh - xh * jnp.mean(dxh * xh, axis=-1, keepdims=True))
    return dx, dg


def _gelu(x):
    return 0.5 * x * (1.0 + jnp.tanh(GELU_C0 * (x + GELU_C1 * x * x * x)))


def _sigmoid(x):
    return 0.5 * jnp.tanh(0.5 * x) + 0.5


def _rows(tm, n):
    return pl.BlockSpec((tm, n), lambda i: (i, 0))


def _whole(shape):
    return pl.BlockSpec(shape, lambda i: (0,) * len(shape))


def _accumulate(ref, val):
    @pl.when(pl.program_id(0) == 0)
    def _():
        ref[...] = jnp.zeros_like(ref)

    ref[...] += val


def _params(n_axes=1):
    return pltpu.CompilerParams(dimension_semantics=("arbitrary",) * n_axes,
                                vmem_limit_bytes=VMEM_LIMIT_BYTES)


def _sds(shape, dtype):
    return jax.ShapeDtypeStruct(shape, dtype)


def _tile(t, want):
    return min(t, want)


def _f1_norm_in(h, g, w_t):
    t = h.shape[0]
    tm = _tile(t, 512)

    def body(h_ref, g_ref, w_ref, z_ref, a_ref):
        a = _rms(h_ref[...], g_ref[...]).astype(BF16)
        a_ref[...] = a
        z_ref[...] = _nt(a, w_ref[...]).astype(BF16)

    return pl.pallas_call(
        body, name="f1_norm_in", grid=(t // tm,),
        in_specs=[_rows(tm, D_MODEL), _whole((1, D_MODEL)), _whole((D_IN, D_MODEL))],
        out_specs=[_rows(tm, D_IN), _rows(tm, D_MODEL)],
        out_shape=[_sds((t, D_IN), BF16), _sds((t, D_MODEL), BF16)],
        compiler_params=_params())(h, g, w_t)


def _alibi_bias():
    ti = jnp.arange(BLK)[:, None]
    ji = jnp.arange(2 * BLK)[None, :]
    dist = ti + BLK - ji
    band = (dist >= 0) & (dist < BLK)
    bias = -jnp.asarray(ALIBI_SLOPES, F32)[:, None, None] * dist.astype(F32)[None]
    return jnp.stack([jnp.where((band & (ji >= BLK))[None], bias, NEG_BIG), jnp.where(band[None], bias, NEG_BIG)])


def _strips():
    return [slice(r * STRIP, (r + 1) * STRIP) for r in range(BLK // STRIP)]


def _bias_spec(transposed=False):
    tile = (2 * BLK, BLK) if transposed else (BLK, 2 * BLK)
    return pl.BlockSpec((None, N_Q_HEADS) + tile, lambda i: (jnp.minimum(i, 1), 0, 0, 0))


def _softmax_strip(s_ref, bias_ref, hq, rows, sink):
    sc = s_ref[hq, rows, :] + bias_ref[hq, rows, :]
    m = jnp.maximum(jnp.max(sc, axis=1, keepdims=True), sink)
    e = jnp.exp(sc - m)
    es = jnp.exp(sink - m)
    inv = 1.0 / (jnp.sum(e, axis=1, keepdims=True) + es)
    return e * inv, es * inv


def _kv_window(z_ref, kp_ref, vp_ref):
    kcat = jnp.concatenate([kp_ref[...], z_ref[:, ATTN_W:ATTN_W + KV_W]], axis=0).astype(F32)
    vcat = jnp.concatenate([vp_ref[...], z_ref[:, ATTN_W + KV_W:ATTN_W + 2 * KV_W]], axis=0).astype(F32)
    kswap = pltpu.roll(kcat, HEAD_DIM, 1)
    vswap = pltpu.roll(vcat, HEAD_DIM, 1)
    return kcat.astype(BF16), kswap.astype(BF16), vcat, vswap


def _gelu_and_grad(x):
    t = jnp.tanh(GELU_C0 * (x + GELU_C1 * x * x * x))
    return 0.5 * x * (1.0 + t), 0.5 * (1.0 + t) + 0.5 * x * (1.0 - t * t) * GELU_C0 * (1.0 + 3.0 * GELU_C1 * x * x)


def _layernorm_strip(gv, lng_ref, lnb_ref):
    xc = gv - jnp.mean(gv, axis=-1, keepdims=True)
    rstd = lax.rsqrt(jnp.mean(xc * xc, axis=-1, keepdims=True) + NORM_EPS)
    xhat = xc * rstd
    return xhat * lng_ref[...] + lnb_ref[...], xhat, rstd


def _tril_w(ws_ref, h):
    ti = lax.broadcasted_iota(jnp.int32, (BLK, BLK), 0)
    si = lax.broadcasted_iota(jnp.int32, (BLK, BLK), 1)
    causal = si <= ti
    return jnp.where(causal, ws_ref[h], 0.0).astype(BF16), causal


def _gm_mixed(vn, ws_ref, bs_ref, lo, hi):
    slabs = []
    for s in range(GM_W // LANES):
        vs = vn[:, s * LANES:(s + 1) * LANES]
        w0, _ = _tril_w(ws_ref, 2 * s)
        w1, _ = _tril_w(ws_ref, 2 * s + 1)
        mixed = (_mm(w0, jnp.where(lo, vs, 0.0).astype(BF16))
                 + _mm(w1, jnp.where(hi, vs, 0.0).astype(BF16))
                 + bs_ref[:, s * LANES:(s + 1) * LANES])
        slabs.append(mixed)
    return slabs


def _block_specs_z(nb):
    prev = lambda i: (jnp.maximum(i - 1, 0), ATTN_W // KV_W)
    prev_v = lambda i: (jnp.maximum(i - 1, 0), ATTN_W // KV_W + 1)
    return [_rows(BLK, D_IN), pl.BlockSpec((BLK, KV_W), prev), pl.BlockSpec((BLK, KV_W), prev_v)]


def _heads():
    return [(2 * s + half, s, half, s // 2 == half) for s in range(ATTN_W // LANES) for half in range(2)]


def _f2_attn_gm(z, sinks, bias, ln_g, ln_b, ws, bs_full):
    t = z.shape[0]
    nb = t // BLK

    def body(z_ref, kp_ref, vp_ref, sink_ref, bias_ref, lng_ref, lnb_ref, ws_ref, bs_ref, am_ref,
             s_ref, p_ref, u_ref, vn_ref):
        lane = lax.broadcasted_iota(jnp.int32, (1, LANES), 1)
        lo = lane < HEAD_DIM
        hi = lane >= HEAD_DIM
        kc, ks, vcat, vswap = _kv_window(z_ref, kp_ref, vp_ref)
        for hq, s, half, same in _heads():
            qs = z_ref[:, s * LANES:(s + 1) * LANES].astype(F32) * (HEAD_DIM ** -0.5)
            qm = jnp.where(lo if half == 0 else hi, qs, 0.0).astype(BF16)
            s_ref[hq] = _nt(qm, kc if same else ks)
        for hq in range(N_Q_HEADS):
            for rows in _strips():
                pr, _ = _softmax_strip(s_ref, bias_ref, hq, rows, sink_ref[hq])
                p_ref[hq, rows, :] = pr.astype(BF16)
        for s in range(ATTN_W // LANES):
            o = jnp.zeros((BLK, LANES), F32)
            for hq, hs, half, same in _heads():
                if hs == s:
                    vm = jnp.where(lo if half == 0 else hi, vcat if same else vswap, 0.0).astype(BF16)
                    o = o + _mm(p_ref[hq], vm)
            am_ref[:, s * LANES:(s + 1) * LANES] = o.astype(BF16)

        for rows in _strips():
            u_ref[rows, :] = _gelu(z_ref[rows, ATTN_W + 2 * KV_W:ATTN_W + 2 * KV_W + GM_W].astype(F32))
            vn_ref[rows, :], _, _ = _layernorm_strip(
                _gelu(z_ref[rows, ATTN_W + 2 * KV_W + GM_W:D_IN].astype(F32)), lng_ref, lnb_ref)
        mixed = _gm_mixed(vn_ref, ws_ref, bs_ref, lo, hi)
        for s in range(GM_W // LANES):
            am_ref[:, ATTN_W + s * LANES:ATTN_W + (s + 1) * LANES] = (
                u_ref[:, s * LANES:(s + 1) * LANES] * mixed[s]).astype(BF16)

    return pl.pallas_call(
        body, name="f2_attn_gm", grid=(nb,),
        in_specs=_block_specs_z(nb) + [
            pl.BlockSpec(memory_space=pltpu.SMEM), _bias_spec(), _whole((1, GM_W)),
            _whole((1, GM_W)), _whole((N_Q_HEADS, BLK, BLK)), _whole((BLK, GM_W))],
        out_specs=_rows(BLK, ATTN_W + GM_W),
        out_shape=_sds((t, ATTN_W + GM_W), BF16),
        scratch_shapes=[pltpu.VMEM((N_Q_HEADS, BLK, 2 * BLK), F32), pltpu.VMEM((N_Q_HEADS, BLK, 2 * BLK), BF16),
                        pltpu.VMEM((BLK, GM_W), F32), pltpu.VMEM((BLK, GM_W), F32)],
        compiler_params=_params())(z, z, z, sinks, bias, ln_g, ln_b, ws, bs_full)


def _f3_mix_out(am, h, ga, gg, w, gpost):
    t = h.shape[0]
    tm = _tile(t, 512)

    def body(am_ref, h_ref, ga_ref, gg_ref, w_ref, gp_ref, heads_ref, mix_ref, h1_ref):
        heads = jnp.concatenate([_rms(am_ref[:, :ATTN_W].astype(F32), ga_ref[...]),
                                 _rms(am_ref[:, ATTN_W:].astype(F32), gg_ref[...])], axis=1).astype(BF16)
        heads_ref[...] = heads
        mix = _mm(heads, w_ref[...])
        mix_ref[...] = mix.astype(BF16)
        h1_ref[...] = h_ref[...] + _rms(mix, gp_ref[...])

    return pl.pallas_call(
        body, name="f3_mix_out", grid=(t // tm,),
        in_specs=[_rows(tm, D_MODEL), _rows(tm, D_MODEL), _whole((1, ATTN_W)), _whole((1, GM_W)),
                  _whole((D_MODEL, D_MODEL)), _whole((1, D_MODEL))],
        out_specs=[_rows(tm, D_MODEL)] * 3,
        out_shape=[_sds((t, D_MODEL), BF16), _sds((t, D_MODEL), BF16), _sds((t, D_MODEL), F32)],
        compiler_params=_params())(am, h, ga, gg, w, gpost)


def _f4a_ffn_in(h1, gf, wg_t, wu_t):
    t = h1.shape[0]
    tm = _tile(t, 256)

    def body(h_ref, g_ref, wg_ref, wu_ref, f_ref, gt_ref, up_ref):
        f = _rms(h_ref[...], g_ref[...]).astype(BF16)
        f_ref[...] = f
        gt_ref[...] = _nt(f, wg_ref[...]).astype(BF16)
        up_ref[...] = _nt(f, wu_ref[...]).astype(BF16)

    return pl.pallas_call(
        body, name="f4a_ffn_in", grid=(t // tm,),
        in_specs=[_rows(tm, D_MODEL), _whole((1, D_MODEL)), _whole((D_FF, D_MODEL)), _whole((D_FF, D_MODEL))],
        out_specs=[_rows(tm, D_MODEL), _rows(tm, D_FF), _rows(tm, D_FF)],
        out_shape=[_sds((t, D_MODEL), BF16), _sds((t, D_FF), BF16), _sds((t, D_FF), BF16)],
        compiler_params=_params())(h1, gf, wg_t, wu_t)


def _f4b_ffn_out(gt, up, wd, h1, gfp):
    t = h1.shape[0]
    tm = _tile(t, 256)

    def body(gt_ref, up_ref, wd_ref, h_ref, g_ref, dn_ref, h2_ref):
        gt = gt_ref[...].astype(F32)
        act = (gt * _sigmoid(gt) * up_ref[...].astype(F32)).astype(BF16)
        dn = _mm(act, wd_ref[...])
        dn_ref[...] = dn.astype(BF16)
        h2_ref[...] = h_ref[...] + _rms(dn, g_ref[...])

    return pl.pallas_call(
        body, name="f4b_ffn_out", grid=(t // tm,),
        in_specs=[_rows(tm, D_FF), _rows(tm, D_FF), _whole((D_FF, D_MODEL)), _rows(tm, D_MODEL),
                  _whole((1, D_MODEL))],
        out_specs=[_rows(tm, D_MODEL)] * 2,
        out_shape=[_sds((t, D_MODEL), BF16), _sds((t, D_MODEL), F32)],
        compiler_params=_params())(gt, up, wd, h1, gfp)


def _f5_ple(h2, gpl, wpg, p, wple, target=None):
    t = h2.shape[0]
    tm = _tile(t, 512)
    with_loss = target is not None

    def body(h_ref, g_ref, wpg_ref, p_ref, wple_ref, *rest):
        r_ref, pg_ref, pe_ref, last_ref = rest[-5:-1] if with_loss else rest[-4:]
        h = h_ref[...]
        r = _rms(h, g_ref[...]).astype(BF16)
        r_ref[...] = r
        pg = _mm(r, wpg_ref[...])
        pe = _mm(p_ref[...].astype(BF16), wple_ref[...])
        pg_ref[...] = pg.astype(BF16)
        pe_ref[...] = pe.astype(BF16)
        h3 = h + pe * _sigmoid(pg)
        if with_loss:
            err = h3 - rest[0][...]
            last_ref[...] = err * (1.0 / D_MODEL)
            _accumulate(rest[-1], jnp.sum(err * err, keepdims=True))
        else:
            last_ref[...] = h3

    return pl.pallas_call(
        body, name="f5_ple_loss" if with_loss else "f5_ple", grid=(t // tm,),
        in_specs=[_rows(tm, D_MODEL), _whole((1, D_MODEL)), _whole((D_MODEL, D_MODEL)), _rows(tm, PLE_DIM),
                  _whole((PLE_DIM, D_MODEL))] + [_rows(tm, D_MODEL)] * with_loss,
        out_specs=[_rows(tm, D_MODEL)] * 4 + [_whole((1, LANES))] * with_loss,
        out_shape=[_sds((t, D_MODEL), BF16)] * 3 + [_sds((t, D_MODEL), F32)] + [_sds((1, LANES), F32)] * with_loss,
        compiler_params=_params())(h2, gpl, wpg, p, wple, *([target] if with_loss else []))


def _b5_ple(dh3, h2, pg, pe, gpl, wpg):
    t = h2.shape[0]
    tm = _tile(t, 512)

    def body(dh_ref, h_ref, pg_ref, pe_ref, g_ref, w_ref, dpe_ref, dpg_ref, dh2_ref, dg_ref):
        dh = dh_ref[...]
        s = _sigmoid(pg_ref[...].astype(F32))
        dpe_ref[...] = (dh * s).astype(BF16)
        dpg = (dh * pe_ref[...].astype(F32) * s * (1.0 - s)).astype(BF16)
        dpg_ref[...] = dpg
        dx, dg = _rms_bwd(_nt(dpg, w_ref[...]), h_ref[...], g_ref[...])
        dh2_ref[...] = dh + dx
        _accumulate(dg_ref, dg)

    return pl.pallas_call(
        body, name="b5_ple", grid=(t // tm,),
        in_specs=[_rows(tm, D_MODEL)] * 4 + [_whole((1, D_MODEL)), _whole((D_MODEL, D_MODEL))],
        out_specs=[_rows(tm, D_MODEL)] * 3 + [_whole((1, D_MODEL))],
        out_shape=[_sds((t, D_MODEL), BF16)] * 2 + [_sds((t, D_MODEL), F32), _sds((1, D_MODEL), F32)],
        compiler_params=_params())(dh3, h2, pg, pe, gpl, wpg)


def _b4a_ffn_out(dh2, dn, gfp, wd, gt, up):
    t = dh2.shape[0]
    tm = _tile(t, 256)

    def body(dh_ref, dn_ref, g_ref, wd_ref, gt_ref, up_ref, ddn_ref, act_ref, dgt_ref, dup_ref, dg_ref):
        ddn, dg = _rms_bwd(dh_ref[...], dn_ref[...].astype(F32), g_ref[...])
        _accumulate(dg_ref, dg)
        ddn = ddn.astype(BF16)
        ddn_ref[...] = ddn
        dact = _nt(ddn, wd_ref[...])
        gt = gt_ref[...].astype(F32)
        up = up_ref[...].astype(F32)
        sg = _sigmoid(gt)
        silu = gt * sg
        act_ref[...] = (silu * up).astype(BF16)
        dup_ref[...] = (dact * silu).astype(BF16)
        dgt_ref[...] = (dact * up * (sg * (1.0 + gt * (1.0 - sg)))).astype(BF16)

    return pl.pallas_call(
        body, name="b4a_ffn_out", grid=(t // tm,),
        in_specs=[_rows(tm, D_MODEL), _rows(tm, D_MODEL), _whole((1, D_MODEL)), _whole((D_FF, D_MODEL)),
                  _rows(tm, D_FF), _rows(tm, D_FF)],
        out_specs=[_rows(tm, D_MODEL), _rows(tm, D_FF), _rows(tm, D_FF), _rows(tm, D_FF), _whole((1, D_MODEL))],
        out_shape=[_sds((t, D_MODEL), BF16)] + [_sds((t, D_FF), BF16)] * 3 + [_sds((1, D_MODEL), F32)],
        compiler_params=_params())(dh2, dn, gfp, wd, gt, up)


def _b4b_ffn_in(dgt, dup, wg_t, wu_t, h1, gf, dh2):
    t = h1.shape[0]
    tm = _tile(t, 512)

    def body(dgt_ref, dup_ref, wg_ref, wu_ref, h_ref, g_ref, dh_ref, dh1_ref, dg_ref):
        df = _mm(dgt_ref[...], wg_ref[...]) + _mm(dup_ref[...], wu_ref[...])
        dx, dg = _rms_bwd(df, h_ref[...], g_ref[...])
        dh1_ref[...] = dh_ref[...] + dx
        _accumulate(dg_ref, dg)

    return pl.pallas_call(
        body, name="b4b_ffn_in", grid=(t // tm,),
        in_specs=[_rows(tm, D_FF), _rows(tm, D_FF), _whole((D_FF, D_MODEL)), _whole((D_FF, D_MODEL)),
                  _rows(tm, D_MODEL), _whole((1, D_MODEL)), _rows(tm, D_MODEL)],
        out_specs=[_rows(tm, D_MODEL), _whole((1, D_MODEL))],
        out_shape=[_sds((t, D_MODEL), F32), _sds((1, D_MODEL), F32)],
        compiler_params=_params())(dgt, dup, wg_t, wu_t, h1, gf, dh2)


def _b3_mix_out(dh1, mix, gpost, w, am, ga, gg):
    t = dh1.shape[0]
    tm = _tile(t, 512)

    def body(dh_ref, mix_ref, gp_ref, w_ref, am_ref, ga_ref, gg_ref, dmix_ref, dam_ref, dgp_ref, dga_ref, dgg_ref):
        dmix, dgp = _rms_bwd(dh_ref[...], mix_ref[...].astype(F32), gp_ref[...])
        _accumulate(dgp_ref, dgp)
        dmix = dmix.astype(BF16)
        dmix_ref[...] = dmix
        dheads = _nt(dmix, w_ref[...])
        dat, dga = _rms_bwd(dheads[:, :ATTN_W], am_ref[:, :ATTN_W].astype(F32), ga_ref[...])
        dgm, dgg = _rms_bwd(dheads[:, ATTN_W:], am_ref[:, ATTN_W:].astype(F32), gg_ref[...])
        dam_ref[:, :ATTN_W] = dat.astype(BF16)
        dam_ref[:, ATTN_W:] = dgm.astype(BF16)
        _accumulate(dga_ref, dga)
        _accumulate(dgg_ref, dgg)

    return pl.pallas_call(
        body, name="b3_mix_out", grid=(t // tm,),
        in_specs=[_rows(tm, D_MODEL), _rows(tm, D_MODEL), _whole((1, D_MODEL)), _whole((D_MODEL, D_MODEL)),
                  _rows(tm, D_MODEL), _whole((1, ATTN_W)), _whole((1, GM_W))],
        out_specs=[_rows(tm, D_MODEL), _rows(tm, D_MODEL), _whole((1, D_MODEL)), _whole((1, ATTN_W)),
                   _whole((1, GM_W))],
        out_shape=[_sds((t, D_MODEL), BF16), _sds((t, D_MODEL), BF16), _sds((1, D_MODEL), F32),
                   _sds((1, ATTN_W), F32), _sds((1, GM_W), F32)],
        compiler_params=_params())(dh1, mix, gpost, w, am, ga, gg)


def _b2_attn_gm(dam, z, sinks, bias, ln_g, ln_b, ws, bs_full):
    t = z.shape[0]
    nb = t // BLK

    def body(dam_ref, z_ref, kp_ref, vp_ref, sink_ref, bias_ref, lng_ref, lnb_ref, ws_ref, wst_ref, bs_ref,
             dzq_ref, dkv_ref, dzuv_ref, dsink_ref, dlng_ref, dlnb_ref, dws_ref, dbs_ref,
             sc_ref, dp_ref, p_ref, ds_ref, u_ref, du_ref, dv_ref, vn_ref, xhat_ref, rstd_ref, dvn_ref):
        n = pl.program_id(0)
        lane = lax.broadcasted_iota(jnp.int32, (1, LANES), 1)
        lo = lane < HEAD_DIM
        hi = lane >= HEAD_DIM
        sub = lax.broadcasted_iota(jnp.int32, (LANES, 1), 0)
        kc, ks, vcat, vswap = _kv_window(z_ref, kp_ref, vp_ref)
        vc = vcat.astype(BF16)
        vs_ = vswap.astype(BF16)
        kc_t = kc.T
        ks_t = ks.T

        def operands(s, half):
            mask = lo if half == 0 else hi
            qs = z_ref[:, s * LANES:(s + 1) * LANES].astype(F32) * (HEAD_DIM ** -0.5)
            qm = jnp.where(mask, qs, 0.0).astype(BF16)
            dom = jnp.where(mask, dam_ref[:, s * LANES:(s + 1) * LANES], 0.0).astype(BF16)
            return qm, dom

        for hq, s, half, same in _heads():
            qm, dom = operands(s, half)
            sc_ref[hq] = _nt(kc if same else ks, qm)
            dp_ref[hq] = _nt(vc if same else vs_, dom)
        dsink = jnp.zeros((1, LANES), F32)
        for hq in range(N_Q_HEADS):
            sink = sink_ref[hq]
            sc = sc_ref[hq] + bias_ref[hq]
            m = jnp.maximum(jnp.max(sc, axis=0, keepdims=True), sink)
            e = jnp.exp(sc - m)
            es = jnp.exp(sink - m)
            inv = 1.0 / (jnp.sum(e, axis=0, keepdims=True) + es)
            pr = e * inv
            dpr = dp_ref[hq]
            row = jnp.sum(dpr * pr, axis=0, keepdims=True)
            p_ref[hq] = pr.astype(BF16)
            ds_ref[hq] = (pr * (dpr - row)).astype(BF16)
            dsink = dsink + jnp.where(lane == hq, -jnp.sum(es * inv * row, keepdims=True), 0.0)
        dk_acc = [jnp.zeros((2 * BLK, LANES), F32), jnp.zeros((2 * BLK, LANES), F32)]
        dv_acc = [jnp.zeros((2 * BLK, LANES), F32), jnp.zeros((2 * BLK, LANES), F32)]
        for s in range(ATTN_W // LANES):
            dq_t = jnp.zeros((LANES, BLK), F32)
            for hq, hs, half, same in _heads():
                if hs != s:
                    continue
                qm, dom = operands(s, half)
                ds = ds_ref[hq]
                dk_acc[same] = dk_acc[same] + _mm(ds, qm)
                dv_acc[same] = dv_acc[same] + _mm(p_ref[hq], dom)
                in_half = (sub < HEAD_DIM) if half == 0 else (sub >= HEAD_DIM)
                dq_t = dq_t + jnp.where(in_half, _mm(kc_t if same else ks_t, ds), 0.0)
            dzq_ref[:, s * LANES:(s + 1) * LANES] = (dq_t.T * (HEAD_DIM ** -0.5)).astype(BF16)
        dk_acc = dk_acc[True] + pltpu.roll(dk_acc[False], HEAD_DIM, 1)
        dv_acc = dv_acc[True] + pltpu.roll(dv_acc[False], HEAD_DIM, 1)
        cur = pl.multiple_of(n * BLK, BLK)
        dkv_ref[pl.ds(cur, BLK), 0:KV_W] = dk_acc[BLK:, :]
        dkv_ref[pl.ds(cur, BLK), KV_W:2 * KV_W] = dv_acc[BLK:, :]

        @pl.when(n > 0)
        def _():
            prv = pl.multiple_of((n - 1) * BLK, BLK)
            dkv_ref[pl.ds(prv, BLK), 0:KV_W] += dk_acc[:BLK, :]
            dkv_ref[pl.ds(prv, BLK), KV_W:2 * KV_W] += dv_acc[:BLK, :]

        _accumulate(dsink_ref, dsink)

        for rows in _strips():
            u_ref[rows, :], du_ref[rows, :] = _gelu_and_grad(
                z_ref[rows, ATTN_W + 2 * KV_W:ATTN_W + 2 * KV_W + GM_W].astype(F32))
            gv, dv_ref[rows, :] = _gelu_and_grad(z_ref[rows, ATTN_W + 2 * KV_W + GM_W:D_IN].astype(F32))
            vn_ref[rows, :], xhat_ref[rows, :], rstd = _layernorm_strip(gv, lng_ref, lnb_ref)
            rstd_ref[rows, :] = jnp.broadcast_to(rstd, (STRIP, LANES))
        mixed = _gm_mixed(vn_ref, ws_ref, bs_ref, lo, hi)

        @pl.when(n == 0)
        def _():
            dws_ref[...] = jnp.zeros_like(dws_ref)

        dbs = jnp.zeros((BLK, LANES), F32)
        for s in range(GM_W // LANES):
            slab = slice(s * LANES, (s + 1) * LANES)
            dgm = dam_ref[:, ATTN_W + s * LANES:ATTN_W + (s + 1) * LANES]
            dzuv_ref[:, slab] = (dgm * mixed[s] * du_ref[:, slab]).astype(BF16)
            dmx = dgm * u_ref[:, slab]
            vsb = vn_ref[:, slab].astype(BF16)
            dvn = jnp.zeros((BLK, LANES), F32)
            for half in range(2):
                h = 2 * s + half
                mask = lo if half == 0 else hi
                dmm = jnp.where(mask, dmx, 0.0)
                dbs = dbs + jnp.where(lane == h, jnp.sum(dmm, axis=1, keepdims=True), 0.0)
                dmm = dmm.astype(BF16)
                _, causal = _tril_w(ws_ref, h)
                ti = lax.broadcasted_iota(jnp.int32, (BLK, BLK), 0)
                si = lax.broadcasted_iota(jnp.int32, (BLK, BLK), 1)
                wt_t = jnp.where(ti <= si, wst_ref[h], 0.0).astype(BF16)
                dvn = dvn + jnp.where(mask, _mm(wt_t, dmm), 0.0)
                dws_ref[h] += jnp.where(causal, _nt(dmm, vsb), 0.0)
            dvn_ref[:, slab] = dvn
        _accumulate(dbs_ref, dbs)
        dlnb = jnp.zeros((STRIP, GM_W), F32)
        dlng = jnp.zeros((STRIP, GM_W), F32)
        for rows in _strips():
            dvn = dvn_ref[rows, :]
            xhat = xhat_ref[rows, :]
            dlnb = dlnb + dvn
            dlng = dlng + dvn * xhat
            dxh = dvn * lng_ref[...]
            dgv = rstd_ref[rows, 0:1] * (dxh - jnp.mean(dxh, axis=-1, keepdims=True)
                                         - xhat * jnp.mean(dxh * xhat, axis=-1, keepdims=True))
            dzuv_ref[rows, GM_W:] = (dgv * dv_ref[rows, :]).astype(BF16)
        _accumulate(dlnb_ref, jnp.sum(dlnb, axis=0, keepdims=True))
        _accumulate(dlng_ref, jnp.sum(dlng, axis=0, keepdims=True))

    return pl.pallas_call(
        body, name="b2_attn_gm", grid=(nb,),
        in_specs=[_rows(BLK, ATTN_W + GM_W)] + _block_specs_z(nb) + [
            pl.BlockSpec(memory_space=pltpu.SMEM), _bias_spec(transposed=True), _whole((1, GM_W)),
            _whole((1, GM_W)), _whole((N_Q_HEADS, BLK, BLK)), _whole((N_Q_HEADS, BLK, BLK)), _whole((BLK, GM_W))],
        out_specs=[_rows(BLK, ATTN_W), _whole((t, 2 * KV_W)), _rows(BLK, 2 * GM_W), _whole((1, LANES)),
                   _whole((1, GM_W)), _whole((1, GM_W)), _whole((N_Q_HEADS, BLK, BLK)), _whole((BLK, LANES))],
        out_shape=[_sds((t, ATTN_W), BF16), _sds((t, 2 * KV_W), F32), _sds((t, 2 * GM_W), BF16),
                   _sds((1, LANES), F32), _sds((1, GM_W), F32), _sds((1, GM_W), F32),
                   _sds((N_Q_HEADS, BLK, BLK), F32), _sds((BLK, LANES), F32)],
        scratch_shapes=[pltpu.VMEM((N_Q_HEADS, 2 * BLK, BLK), F32)] * 2
        + [pltpu.VMEM((N_Q_HEADS, 2 * BLK, BLK), BF16)] * 2 + [pltpu.VMEM((BLK, GM_W), F32)] * 5
        + [pltpu.VMEM((BLK, LANES), F32), pltpu.VMEM((BLK, GM_W), F32)],
        compiler_params=_params())(dam, z, z, z, sinks, jnp.swapaxes(bias, 2, 3), ln_g, ln_b, ws,
                                   jnp.swapaxes(ws, 1, 2), bs_full)


def _b1_norm_in(dzq, dkv, dzuv, w_t, h, g, dh1):
    t = h.shape[0]
    tm = _tile(t, 512)

    def body(dzq_ref, dkv_ref, dzuv_ref, w_ref, h_ref, g_ref, dh_ref, dh0_ref, dg_ref):
        dz = jnp.concatenate([dzq_ref[...], dkv_ref[...].astype(BF16), dzuv_ref[...]], axis=1)
        dx, dg = _rms_bwd(_mm(dz, w_ref[...]), h_ref[...], g_ref[...])
        dh0_ref[...] = dh_ref[...] + dx
        _accumulate(dg_ref, dg)

    return pl.pallas_call(
        body, name="b1_norm_in", grid=(t // tm,),
        in_specs=[_rows(tm, ATTN_W), _rows(tm, 2 * KV_W), _rows(tm, 2 * GM_W), _whole((D_IN, D_MODEL)),
                  _rows(tm, D_MODEL), _whole((1, D_MODEL)), _rows(tm, D_MODEL)],
        out_specs=[_rows(tm, D_MODEL), _whole((1, D_MODEL))],
        out_shape=[_sds((t, D_MODEL), F32), _sds((1, D_MODEL), F32)],
        compiler_params=_params())(dzq, dkv, dzuv, w_t, h, g, dh1)


def _weight_grad(x, dy, name):
    xs = tuple(x) if isinstance(x, (tuple, list)) else (x,)
    t = xs[0].shape[0]
    k = sum(a.shape[1] for a in xs)
    n = dy.shape[1]
    tm = _tile(t, 1024)
    steps = t // tm

    def body(*refs):
        x_refs, dy_ref, dw_ref, acc_ref = refs[:len(xs)], refs[-3], refs[-2], refs[-1]
        i = pl.program_id(0)

        def product():
            cols = [r[...].astype(BF16) for r in x_refs]
            return _tn(cols[0] if len(cols) == 1 else jnp.concatenate(cols, axis=1), dy_ref[...])

        @pl.when(i == 0)
        def _():
            acc_ref[...] = product()

        @pl.when(i > 0)
        def _():
            acc_ref[...] += product()

        @pl.when(i == steps - 1)
        def _():
            dw_ref[...] = acc_ref[...].astype(BF16)

    return pl.pallas_call(
        body, name=name, grid=(steps,),
        in_specs=[_rows(tm, a.shape[1]) for a in xs] + [_rows(tm, n)],
        out_specs=_whole((k, n)),
        out_shape=_sds((k, n), BF16),
        scratch_shapes=[pltpu.VMEM((k, n), F32)],
        compiler_params=_params())(*xs, dy)


_ANY = pl.BlockSpec(memory_space=pl.ANY)
_HBM = pl.BlockSpec(memory_space=pltpu.HBM)
_SEM = pl.BlockSpec(memory_space=pltpu.SEMAPHORE)
_EFFECT = pltpu.SideEffectType.DATAFLOW_SIDE_EFFECTING


def _mesh_pos():
    return lax.axis_index("x"), lax.axis_index("y"), lax.axis_index("c")


def _other_chips(x, y):
    return [(1 - x, y), (x, 1 - y), (1 - x, 1 - y)]


def _peers(mode, x, y, c):
    if mode == "devices":
        peers = []
        for j in range(1, N_DEV):
            px = 1 - x if (j >> 2) & 1 else x
            py = 1 - y if (j >> 1) & 1 else y
            pc = 1 - c if j & 1 else c
            peers.append(((px, py, pc), 4 * px + 2 * py + pc))
        return peers, 4 * x + 2 * y + c
    if mode == "sibling":
        return [((x, y, 1 - c), None)], None
    return [((px, py, c), 2 * px + py) for px, py in _other_chips(x, y)], 2 * x + y


def _descriptors(mode, srcs, lands, send_sems, recv_sems, with_incoming=True):
    x, y, c = _mesh_pos()
    peers, me = _peers(mode, x, y, c)
    scatter = mode == "scatter"
    outgoing, incoming = [], []
    for i, (src, land) in enumerate(zip(srcs, lands)):
        for j, (dev, slot) in enumerate(peers):
            sem = i * len(peers) + j
            common = dict(send_sem=send_sems.at[sem], recv_sem=recv_sems.at[sem], device_id=dev,
                          device_id_type=pl.DeviceIdType.MESH)
            whole = mode == "sibling"
            outgoing.append(pltpu.make_async_remote_copy(
                src_ref=src.at[slot] if scatter else src, dst_ref=land if whole else land.at[me], **common))
            if with_incoming:
                incoming.append(pltpu.make_async_remote_copy(
                    src_ref=src.at[me] if scatter else src, dst_ref=land if whole else land.at[slot], **common))
    return outgoing, incoming


def _n_sems(mode, k):
    return k * {"devices": N_DEV - 1, "sibling": 1}.get(mode, N_CHIPS - 1)


def _exchange_start(srcs, lands, mode, name, after=None):
    k = len(srcs)
    arrs = [*srcs, *lands]

    def body(*refs):
        skip = 1 if after is not None else 0
        send_sems, recv_sems = refs[2 * k + skip], refs[2 * k + skip + 1]
        outgoing, _ = _descriptors(mode, refs[:k], refs[k:2 * k], send_sems, recv_sems, with_incoming=False)
        for cp in outgoing:
            cp.start()
        refs[-1][...] = jnp.zeros_like(refs[-1])

    operands = [pltpu.with_memory_space_constraint(a, pltpu.HBM) for a in arrs]
    in_specs = [_HBM] * (2 * k)
    if after is not None:
        operands.append(after)
        in_specs.append(_ANY)
    sems = pltpu.SemaphoreType.DMA((_n_sems(mode, k),))
    res = pl.pallas_call(
        body, name=name, in_specs=in_specs,
        out_shape=(sems, sems, *[pltpu.HBM(a.shape, a.dtype) for a in arrs], _sds((8, LANES), F32)),
        out_specs=(_SEM, _SEM, *[_HBM] * (2 * k), pl.BlockSpec(memory_space=pltpu.VMEM)),
        input_output_aliases={i: 2 + i for i in range(2 * k)},
        compiler_params=pltpu.CompilerParams(has_side_effects=_EFFECT))(*operands)
    return dict(mode=mode, send=res[0], recv=res[1], srcs=res[2:2 + k], lands=res[2 + k:2 + 2 * k], token=res[-1])


def _exchange_wait(flight, name, after):
    mode, k = flight["mode"], len(flight["srcs"])
    arrs = [*flight["srcs"], *flight["lands"]]

    def body(*refs):
        outgoing, incoming = _descriptors(mode, refs[:k], refs[k:2 * k], refs[2 * k], refs[2 * k + 1])
        for cp in outgoing:
            cp.wait_send()
        for cp in incoming:
            cp.wait_recv()

    res = pl.pallas_call(
        body, name=name, in_specs=[_HBM] * (2 * k) + [_SEM, _SEM, _ANY],
        out_shape=tuple(pltpu.HBM(a.shape, a.dtype) for a in arrs), out_specs=tuple([_HBM] * (2 * k)),
        input_output_aliases={i: i for i in range(2 * k)},
        compiler_params=pltpu.CompilerParams(has_side_effects=_EFFECT))(*arrs, flight["send"], flight["recv"], after)
    return res[:k], res[k:]


def _adamw(w, g, m, v):
    m = ADAM_B1 * m + (1.0 - ADAM_B1) * g
    v = ADAM_B2 * v + (1.0 - ADAM_B2) * (g * g)
    m_hat = m / (1.0 - ADAM_B1 ** ADAM_STEP)
    v_hat = v / (1.0 - ADAM_B2 ** ADAM_STEP)
    delta = -ADAM_LR * (m_hat / (jnp.sqrt(v_hat) + ADAM_EPS) + ADAM_WD * w)
    return delta, m, v


def _row_tile(rows, cols, n_arrays):
    budget = VMEM_LIMIT_BYTES // 2
    padded = -(-cols // LANES) * LANES
    cap = min(rows, max(16, budget // (2 * n_arrays * padded * 4)))
    return max(tr for tr in range(16, cap + 1, 16) if rows % tr == 0)


def _sum_chips(landing, name):
    _, r, c = landing.shape
    tr = _row_tile(r, c, 4)

    def body(l_ref, s_ref):
        acc = l_ref[0].astype(F32)
        for s in range(1, N_CHIPS):
            acc = acc + l_ref[s].astype(F32)
        s_ref[...] = acc

    return pl.pallas_call(
        body, name=name, grid=(r // tr,),
        in_specs=[pl.BlockSpec((N_CHIPS, tr, c), lambda i: (0, i, 0))],
        out_specs=_rows(tr, c), out_shape=_sds((r, c), F32),
        compiler_params=_params())(landing)


def _adamw_big(mine, sibling, w, m, v, prev, layer, name):
    _, r, c = w.shape
    tr = _row_tile(r, c, 9)
    stacked = pl.BlockSpec((None, tr, c), lambda i: (layer, i, 0))

    def body(a_ref, b_ref, w_ref, m_ref, v_ref, *rest):
        g_out, d_out, m_out, v_out = rest[-4:]
        g = a_ref[...] + b_ref[...]
        g_out[...] = g
        d_out[...], m_out[...], v_out[...] = _adamw(w_ref[...], g, m_ref[...], v_ref[...])

    prev = list(prev) if prev is not None else []
    return pl.pallas_call(
        body, name=name, grid=(r // tr,),
        in_specs=[_rows(tr, c)] * 2 + [stacked] * 3 + [_ANY] * len(prev),
        out_specs=[stacked] * 4, out_shape=[_sds(w.shape, F32)] * 4,
        input_output_aliases={5 + j: j for j in range(len(prev))},
        compiler_params=_params())(mine, sibling, w, m, v, *prev)


def _adamw_small(gathered, w, m, v):
    r = w.shape[0]

    def body(a_ref, w_ref, m_ref, v_ref, g_out, d_out, m_out, v_out):
        g = a_ref[0]
        for d in range(1, N_DEV):
            g = g + a_ref[d]
        g_out[...] = g
        d_out[...], m_out[...], v_out[...] = _adamw(w_ref[...], g, m_ref[...], v_ref[...])

    return pl.pallas_call(
        body, name="adamw_small", grid=(1,),
        in_specs=[_whole((N_DEV, r, LANES))] + [_whole((r, LANES))] * 3,
        out_specs=[_whole((r, LANES))] * 4, out_shape=[_sds((r, LANES), F32)] * 4,
        compiler_params=_params())(gathered, w, m, v)


def _shard_view(name, stacked):
    return jnp.swapaxes(stacked, 1, 2) if name in TRANSPOSED else stacked


def _row_split(name):
    return name in TRANSPOSED or BIG[name] == 1


def _assemble(name, landed):
    _, r, c = landed.shape
    if _row_split(name):
        return landed.reshape(N_CHIPS * r, c)
    return landed.transpose(1, 0, 2).reshape(r, N_CHIPS * c)


def _split(name, whole):
    r, c = whole.shape
    if _row_split(name):
        return whole.reshape(N_CHIPS, r // N_CHIPS, c)
    return whole.reshape(r, N_CHIPS, c // N_CHIPS).transpose(1, 0, 2)


def _with_own_slot(landed, own, slot):
    return lax.dynamic_update_index_in_dim(landed, own, slot, 0)


def _pack_small(params, names):
    pieces = []
    for name in names:
        flat = params[name].reshape(-1)
        pieces.append(jnp.pad(flat, (0, -flat.shape[0] % SMALL_PAD)))
    return jnp.concatenate(pieces).reshape(-1, LANES)


def _unpack_small(packed, like, names):
    flat = packed.reshape(-1)
    out, off = {}, 0
    for name in names:
        size = like[name].size
        out[name] = flat[off:off + size].reshape(like[name].shape)
        off += size + (-size % SMALL_PAD)
    return out


def kernel(x, p, ln_mix_pre, w_in, attn_sinks, gm_ln_g, gm_ln_b, gm_ws, gm_bs, g_attn_out, g_gm_out, w_out, ln_mix_post, ln_ffn_pre, w_ffn_gate, w_ffn_up, w_ffn_down, ln_ffn_post, w_ple, ln_ple_gate, w_ple_gate, loss_target, m_ln_mix_pre, m_w_in, m_attn_sinks, m_gm_ln_g, m_gm_ln_b, m_gm_ws, m_gm_bs, m_g_attn_out, m_g_gm_out, m_w_out, m_ln_mix_post, m_ln_ffn_pre, m_w_ffn_gate, m_w_ffn_up, m_w_ffn_down, m_ln_ffn_post, m_w_ple, m_ln_ple_gate, m_w_ple_gate, v_ln_mix_pre, v_w_in, v_attn_sinks, v_gm_ln_g, v_gm_ln_b, v_gm_ws, v_gm_bs, v_g_attn_out, v_g_gm_out, v_w_out, v_ln_mix_post, v_ln_ffn_pre, v_w_ffn_gate, v_w_ffn_up, v_w_ffn_down, v_ln_ffn_post, v_w_ple, v_ln_ple_gate, v_w_ple_gate):
    given = dict(locals())
    wts = {n: given[n] for n in WEIGHTS}
    mom = {n: given["m_" + n] for n in WEIGHTS}
    var = {n: given["v_" + n] for n in WEIGHTS}
    depth = w_in.shape[0]
    h = x[0]
    target = loss_target[0]
    chip = 2 * lax.axis_index("x") + lax.axis_index("y")
    device = 2 * chip + lax.axis_index("c")
    row = lambda a, i: a[i][None, :]
    bs_full = [jnp.repeat(gm_bs[i].T, HEAD_DIM, axis=1) for i in range(depth)]
    bias = _alibi_bias()
    kinds = ("grad", "delta", "m", "v")
    wview = {n: _shard_view(n, wts[n]) for n in BIG_NAMES}
    mview = {n: _shard_view(n, mom[n]) for n in BIG_NAMES}
    vview = {n: _shard_view(n, var[n]) for n in BIG_NAMES}

    zero = lambda flight: flight["token"][0:1, 0:1]

    def start_gather(i, names, after, tag):
        shards = [wview[n][i].astype(BF16) for n in names]
        lands = [lax.empty((N_CHIPS,) + s.shape, BF16) for s in shards]
        return _exchange_start(shards, lands, "gather", f"gather_weights_start_{i}{tag}", after)

    def finish_gather(flight, names, i, after, tag):
        shards, lands = _exchange_wait(flight, f"gather_weights_wait_{i}{tag}", after)
        return {n: _assemble(n, _with_own_slot(l, s, chip)) for n, s, l in zip(names, shards, lands)}

    first, after = [], None
    for k, names in enumerate(GATHER_GROUPS):
        first.append(start_gather(0, names, after, "abc"[k]))
        after = first[-1]["token"]
    full = [finish_gather(first[0], GATHER_GROUPS[0], 0, after, "a")] + [None] * (depth - 1)
    saved = []
    for i in range(depth):
        w = full[i]
        g_in = row(ln_mix_pre, i)
        if i + 1 < depth:
            flight = start_gather(i + 1, BIG_NAMES, w['w_in'], "")
            g_in = g_in + zero(flight)
        z, a = _f1_norm_in(h, g_in, w['w_in'])
        am = _f2_attn_gm(z, attn_sinks[i], bias, row(gm_ln_g, i), row(gm_ln_b, i), gm_ws[i], bs_full[i])
        if i == 0:
            w.update(finish_gather(first[1], GATHER_GROUPS[1], 0, am, "b"))
        heads, mix, h1 = _f3_mix_out(am, h, row(g_attn_out, i), row(g_gm_out, i), w['w_out'], row(ln_mix_post, i))
        f, gt, up = _f4a_ffn_in(h1, row(ln_ffn_pre, i), w['w_ffn_gate'], w['w_ffn_up'])
        if i == 0:
            w.update(finish_gather(first[2], GATHER_GROUPS[2], 0, gt, "c"))
        dn, h2 = _f4b_ffn_out(gt, up, w['w_ffn_down'], h1, row(ln_ffn_post, i))
        ple = _f5_ple(h2, row(ln_ple_gate, i), w['w_ple_gate'], p[i, 0], w['w_ple'],
                      target if i + 1 == depth else None)
        r, pg, pe = ple[:3]
        saved.append(dict(h=h, z=z, a=a, am=am, heads=heads, mix=mix, h1=h1, f=f, gt=gt, up=up, dn=dn, h2=h2,
                          r=r, pg=pg, pe=pe))
        if i + 1 < depth:
            h = ple[3]
            full[i + 1] = finish_gather(flight, BIG_NAMES, i + 1, h, "")

    dh, sq = ple[3], ple[4]
    loss = lax.psum(0.5 / D_MODEL * sq[0, 0], ("x", "y", "c"))

    chain = {n: None for n in BIG_NAMES}
    owed = []
    small_out = [{k: {} for k in kinds} for _ in range(depth)]

    def start_scatter(i, names, dws, after, tag):
        parts = [_split(n, dws[n]) for n in names]
        lands = [lax.empty(q.shape, BF16) for q in parts]
        return _exchange_start(parts, lands, "scatter", f"scatter_grads_start_{i}{tag}", after)

    def finish_scatter(i, names, flight, after, tag):
        parts, lands = _exchange_wait(flight, f"scatter_grads_wait_{i}{tag}", after)
        lands = [_with_own_slot(l, lax.dynamic_index_in_dim(q, chip, 0, keepdims=False), chip)
                 for q, l in zip(parts, lands)]
        partial = [_sum_chips(l, "sum_chips_" + n) for n, l in zip(names, lands)]
        flight = _exchange_start(partial, [lax.empty(q.shape, F32) for q in partial], "sibling",
                                 f"sibling_grads_start_{i}{tag}", after)

        def update(later):
            mine, theirs = _exchange_wait(flight, f"sibling_grads_wait_{i}{tag}", later)
            for n, a, b in zip(names, mine, theirs):
                chain[n] = _adamw_big(a, b, wview[n], mview[n], vview[n], chain[n], i, "adamw_" + n)

        owed.append(update)
        return flight["token"]

    def settle(later):
        while owed:
            owed.pop(0)(later)

    def start_small(i, names, small, after, tag):
        packed = _pack_small(small, names)
        land = lax.empty((N_DEV,) + packed.shape, F32)
        return _exchange_start([packed], [land], "devices", f"gather_small_grads_start_{i}{tag}", after)

    def finish_small(i, names, flight, after, tag):
        (packed,), (gathered,) = _exchange_wait(flight, f"gather_small_grads_wait_{i}{tag}", after)
        gathered = _with_own_slot(gathered, packed, device)
        layer = lambda d: _pack_small({n: d[n][i] for n in names}, names)
        res = _adamw_small(gathered, layer(wts), layer(mom), layer(var))
        for k, a in zip(kinds, res):
            small_out[i][k].update(_unpack_small(a, {n: wts[n][i] for n in names}, names))
        return gathered

    pending, done, behind = [], None, None
    for i in reversed(range(depth)):
        s, w = saved[i], full[i]
        last = i == 0
        dws, small = {}, {}
        gain = row(ln_ple_gate, i)
        if behind is not None:
            gain = gain + behind
        dpe, dpg, dh2, dg = _b5_ple(dh, s['h2'], s['pg'], s['pe'], gain, w['w_ple_gate'])
        small['ln_ple_gate'] = dg[0]
        dws['w_ple'] = _weight_grad(p[i, 0], dpe, "dw_ple")
        dws['w_ple_gate'] = _weight_grad(s['r'], dpg, "dw_ple_gate")

        ddn, act, dgt, dup, dg = _b4a_ffn_out(dh2, s['dn'], row(ln_ffn_post, i), w['w_ffn_down'], s['gt'], s['up'])
        small['ln_ffn_post'] = dg[0]
        dws['w_ffn_down'] = _weight_grad(act, ddn, "dw_ffn_down")
        dws['w_ffn_gate'] = _weight_grad(dgt, s['f'], "dw_ffn_gate")
        dws['w_ffn_up'] = _weight_grad(dup, s['f'], "dw_ffn_up")
        gain = row(ln_ffn_pre, i)
        if last:
            flight_a = start_scatter(i, SCATTER_GROUPS[0], dws, None, "a")
            gain = gain + zero(flight_a)
        dh1, dg = _b4b_ffn_in(dgt, dup, w['w_ffn_gate'], w['w_ffn_up'], s['h1'], gain, dh2)
        small['ln_ffn_pre'] = dg[0]

        dmix, dam, dgp, dga, dgg = _b3_mix_out(dh1, s['mix'], row(ln_mix_post, i), w['w_out'], s['am'],
                                               row(g_attn_out, i), row(g_gm_out, i))
        small['ln_mix_post'] = dgp[0]
        small['g_attn_out'] = dga[0]
        small['g_gm_out'] = dgg[0]
        dws['w_out'] = _weight_grad(s['heads'], dmix, "dw_out")
        gain = row(gm_ln_g, i)
        if last:
            flight_b = start_scatter(i, SCATTER_GROUPS[1], dws, flight_a["token"], "b")
            gain = gain + zero(flight_b)

        dzq, dkv, dzuv, dsink, dlng, dlnb, dgws, dbs = _b2_attn_gm(
            dam, s['z'], attn_sinks[i], bias, gain, row(gm_ln_b, i), gm_ws[i], bs_full[i])
        small['attn_sinks'] = dsink[0, :N_Q_HEADS]
        small['gm_ln_g'] = dlng[0]
        small['gm_ln_b'] = dlnb[0]
        small['gm_ws'] = dgws
        small['gm_bs'] = dbs[:, :N_Q_HEADS].T
        gain = row(ln_mix_pre, i)
        if last:
            flight_s = start_small(i, SMALL_EARLY, small, flight_b["token"], "a")
            gain = gain + zero(flight_s)

        dh, dg = _b1_norm_in(dzq, dkv, dzuv, w['w_in'], s['h'], gain, dh1)
        small['ln_mix_pre'] = dg[0]
        settle(dh)
        for finish in pending:
            done = finish(dh)
        pending = []
        if last:
            done = finish_scatter(i, SCATTER_GROUPS[0], flight_a, dh, "a")
            done = finish_scatter(i, SCATTER_GROUPS[1], flight_b, done, "b")
        dws['w_in'] = _weight_grad((dzq, dkv, dzuv), s['a'], "dw_in")
        if last:
            flight_c = start_scatter(i, SCATTER_GROUPS[2], dws, done, "c")
            flight_t = start_small(i, SMALL_LATE, small, flight_c["token"], "b")
            finish_small(i, SMALL_EARLY, flight_s, flight_t["token"], "a")
            done = finish_scatter(i, SCATTER_GROUPS[2], flight_c, flight_t["token"], "c")
            finish_small(i, SMALL_LATE, flight_t, flight_t["token"], "b")
            settle(done)
        else:
            flight_c = start_scatter(i, BIG_NAMES, dws, done, "")
            flight_s = start_small(i, SMALL_NAMES, small, flight_c["token"], "")
            behind = zero(flight_s)
            pending = [functools.partial(finish_small, i, SMALL_NAMES, flight_s, tag=""),
                       functools.partial(finish_scatter, i, BIG_NAMES, flight_c, tag="")]
    grad_x = dh[None]

    out = {k: {n: _shard_view(n, chain[n][j]) for n in BIG_NAMES} for j, k in enumerate(kinds)}
    for k in kinds:
        out[k].update({n: jnp.stack([small_out[i][k][n] for i in range(depth)]) for n in SMALL_NAMES})

    return (loss, grad_x, *[out["grad"][n] for n in WEIGHTS], *[out["delta"][n] for n in WEIGHTS],
            *[out["m"][n] for n in WEIGHTS], *[out["v"][n] for n in WEIGHTS])
```

```python
import functools
import math

import jax
import jax.numpy as jnp
from jax import lax
from jax.experimental import pallas as pl
from jax.experimental.pallas import tpu as pltpu

F32 = jnp.float32
BF16 = jnp.bfloat16

D_MODEL = 1024
HEAD_DIM = 64
N_Q_HEADS = 8
BLK = 128
ATTN_W = 512
KV_W = 128
GM_W = 512
D_IN = ATTN_W + 2 * KV_W + 2 * GM_W
D_FF = 2816
PLE_DIM = 256
DEPTH = 4
NORM_EPS = 1e-6
NEG_BIG = -1e30
N_CHIPS = 4
N_DEV = 8

ADAM_LR = 0.001
ADAM_B1 = 0.9
ADAM_B2 = 0.999
ADAM_EPS = 1e-08
ADAM_WD = 0.01
ADAM_STEP = 10

VMEM_LIMIT_BYTES = 56 * 1024 * 1024
LANES = 128
STRIP = 16
GELU_C0 = math.sqrt(2.0 / math.pi)
GELU_C1 = 0.044715
ALIBI_SLOPES = tuple(2.0 ** (-8.0 * (h + 1.0) / N_Q_HEADS) for h in range(N_Q_HEADS))

WEIGHTS = ['ln_mix_pre', 'w_in', 'attn_sinks', 'gm_ln_g', 'gm_ln_b', 'gm_ws', 'gm_bs', 'g_attn_out',
           'g_gm_out', 'w_out', 'ln_mix_post', 'ln_ffn_pre', 'w_ffn_gate', 'w_ffn_up', 'w_ffn_down',
           'ln_ffn_post', 'w_ple', 'ln_ple_gate', 'w_ple_gate']
BIG = {'w_in': 2, 'w_out': 1, 'w_ffn_gate': 2, 'w_ffn_up': 2, 'w_ffn_down': 1, 'w_ple': 2, 'w_ple_gate': 1}
BIG_NAMES = list(BIG)
TRANSPOSED = ('w_in', 'w_ffn_gate', 'w_ffn_up')
SMALL_NAMES = [n for n in WEIGHTS if n not in BIG]
SMALL_PAD = 1024
GATHER_GROUPS = (('w_in',), ('w_out', 'w_ffn_gate', 'w_ffn_up'), ('w_ffn_down', 'w_ple_gate', 'w_ple'))
SCATTER_GROUPS = (('w_ple', 'w_ple_gate', 'w_ffn_down', 'w_ffn_gate', 'w_ffn_up'), ('w_out',), ('w_in',))
SMALL_LATE = ('ln_mix_pre',)
SMALL_EARLY = tuple(n for n in SMALL_NAMES if n not in SMALL_LATE)


def _nt(a, b):
    return lax.dot_general(a, b, (((1,), (1,)), ((), ())), preferred_element_type=F32)


def _tn(a, b):
    return lax.dot_general(a, b, (((0,), (0,)), ((), ())), preferred_element_type=F32)


def _mm(a, b):
    return jnp.dot(a, b, preferred_element_type=F32)


def _rms(x, g):
    r = lax.rsqrt(jnp.mean(x * x, axis=-1, keepdims=True) + NORM_EPS)
    return x * r * g


def _rms_bwd(dy, x, g):
    r = lax.rsqrt(jnp.mean(x * x, axis=-1, keepdims=True) + NORM_EPS)
    xh = x * r
    dg = jnp.sum(dy * xh, axis=0, keepdims=True)
    dxh = dy * g
    dx = r * (dxh - xh * jnp.mean(dxh * xh, axis=-1, keepdims=True))
    return dx, dg


def _gelu(x):
    return 0.5 * x * (1.0 + jnp.tanh(GELU_C0 * (x + GELU_C1 * x * x * x)))


def _sigmoid(x):
    return 0.5 * jnp.tanh(0.5 * x) + 0.5


def _rows(tm, n):
    return pl.BlockSpec((tm, n), lambda i: (i, 0))


def _layer_rows(tm, n, layer):
    return pl.BlockSpec((None, None, tm, n), lambda i: (layer, 0, i, 0))


def _whole(shape):
    return pl.BlockSpec(shape, lambda i: (0,) * len(shape))


def _accumulate(ref, val):
    @pl.when(pl.program_id(0) == 0)
    def _():
        ref[...] = jnp.zeros_like(ref)

    ref[...] += val


def _params(n_axes=1):
    return pltpu.CompilerParams(dimension_semantics=("arbitrary",) * n_axes,
                                vmem_limit_bytes=VMEM_LIMIT_BYTES)


def _sds(shape, dtype):
    return jax.ShapeDtypeStruct(shape, dtype)


def _tile(t, want):
    return min(t, want)


def _f1_norm_in(h, g, w_t):
    t = h.shape[0]
    tm = _tile(t, 512)

    def body(h_ref, g_ref, w_ref, z_ref, a_ref):
        a = _rms(h_ref[...], g_ref[...]).astype(BF16)
        a_ref[...] = a
        z_ref[...] = _nt(a, w_ref[...]).astype(BF16)

    return pl.pallas_call(
        body, name="f1_norm_in", grid=(t // tm,),
        in_specs=[_rows(tm, D_MODEL), _whole((1, D_MODEL)), _whole((D_IN, D_MODEL))],
        out_specs=[_rows(tm, D_IN), _rows(tm, D_MODEL)],
        out_shape=[_sds((t, D_IN), BF16), _sds((t, D_MODEL), BF16)],
        compiler_params=_params())(h, g, w_t)


def _alibi_bias():
    ti = jnp.arange(BLK)[:, None]
    ji = jnp.arange(2 * BLK)[None, :]
    dist = ti + BLK - ji
    band = (dist >= 0) & (dist < BLK)
    bias = -jnp.asarray(ALIBI_SLOPES, F32)[:, None, None] * dist.astype(F32)[None]
    return jnp.stack([jnp.where((band & (ji >= BLK))[None], bias, NEG_BIG), jnp.where(band[None], bias, NEG_BIG)])


def _strips():
    return [slice(r * STRIP, (r + 1) * STRIP) for r in range(BLK // STRIP)]


def _bias_spec(transposed=False):
    tile = (2 * BLK, BLK) if transposed else (BLK, 2 * BLK)
    return pl.BlockSpec((None, N_Q_HEADS) + tile, lambda i: (jnp.minimum(i, 1), 0, 0, 0))


def _softmax_strip(s_ref, bias_ref, hq, rows, sink):
    sc = s_ref[hq, rows, :] + bias_ref[hq, rows, :]
    m = jnp.maximum(jnp.max(sc, axis=1, keepdims=True), sink)
    e = jnp.exp(sc - m)
    es = jnp.exp(sink - m)
    inv = 1.0 / (jnp.sum(e, axis=1, keepdims=True) + es)
    return e * inv, es * inv


def _kv_window(z_ref, kp_ref, vp_ref):
    kcat = jnp.concatenate([kp_ref[...], z_ref[:, ATTN_W:ATTN_W + KV_W]], axis=0).astype(F32)
    vcat = jnp.concatenate([vp_ref[...], z_ref[:, ATTN_W + KV_W:ATTN_W + 2 * KV_W]], axis=0).astype(F32)
    kswap = pltpu.roll(kcat, HEAD_DIM, 1)
    vswap = pltpu.roll(vcat, HEAD_DIM, 1)
    return kcat.astype(BF16), kswap.astype(BF16), vcat, vswap


def _gelu_and_grad(x):
    t = jnp.tanh(GELU_C0 * (x + GELU_C1 * x * x * x))
    return 0.5 * x * (1.0 + t), 0.5 * (1.0 + t) + 0.5 * x * (1.0 - t * t) * GELU_C0 * (1.0 + 3.0 * GELU_C1 * x * x)


def _layernorm_strip(gv, lng_ref, lnb_ref):
    xc = gv - jnp.mean(gv, axis=-1, keepdims=True)
    rstd = lax.rsqrt(jnp.mean(xc * xc, axis=-1, keepdims=True) + NORM_EPS)
    xhat = xc * rstd
    return xhat * lng_ref[...] + lnb_ref[...], xhat, rstd


def _tril_w(ws_ref, h):
    ti = lax.broadcasted_iota(jnp.int32, (BLK, BLK), 0)
    si = lax.broadcasted_iota(jnp.int32, (BLK, BLK), 1)
    causal = si <= ti
    return jnp.where(causal, ws_ref[h], 0.0).astype(BF16), causal


def _gm_mixed(vn, ws_ref, bs_ref, lo, hi):
    slabs = []
    for s in range(GM_W // LANES):
        vs = vn[:, s * LANES:(s + 1) * LANES]
        w0, _ = _tril_w(ws_ref, 2 * s)
        w1, _ = _tril_w(ws_ref, 2 * s + 1)
        mixed = (_mm(w0, jnp.where(lo, vs, 0.0).astype(BF16))
                 + _mm(w1, jnp.where(hi, vs, 0.0).astype(BF16))
                 + bs_ref[:, s * LANES:(s + 1) * LANES])
        slabs.append(mixed)
    return slabs


def _block_specs_z(nb):
    prev = lambda i: (jnp.maximum(i - 1, 0), ATTN_W // KV_W)
    prev_v = lambda i: (jnp.maximum(i - 1, 0), ATTN_W // KV_W + 1)
    return [_rows(BLK, D_IN), pl.BlockSpec((BLK, KV_W), prev), pl.BlockSpec((BLK, KV_W), prev_v)]


def _heads():
    return [(2 * s + half, s, half, s // 2 == half) for s in range(ATTN_W // LANES) for half in range(2)]


def _f2_attn_gm(z, sinks, bias, ln_g, ln_b, ws, bs_full):
    t = z.shape[0]
    nb = t // BLK

    def body(z_ref, kp_ref, vp_ref, sink_ref, bias_ref, lng_ref, lnb_ref, ws_ref, bs_ref, am_ref,
             s_ref, p_ref, u_ref, vn_ref):
        lane = lax.broadcasted_iota(jnp.int32, (1, LANES), 1)
        lo = lane < HEAD_DIM
        hi = lane >= HEAD_DIM
        kc, ks, vcat, vswap = _kv_window(z_ref, kp_ref, vp_ref)
        for hq, s, half, same in _heads():
            qs = z_ref[:, s * LANES:(s + 1) * LANES].astype(F32) * (HEAD_DIM ** -0.5)
            qm = jnp.where(lo if half == 0 else hi, qs, 0.0).astype(BF16)
            s_ref[hq] = _nt(qm, kc if same else ks)
        for hq in range(N_Q_HEADS):
            for rows in _strips():
                pr, _ = _softmax_strip(s_ref, bias_ref, hq, rows, sink_ref[hq])
                p_ref[hq, rows, :] = pr.astype(BF16)
        for s in range(ATTN_W // LANES):
            o = jnp.zeros((BLK, LANES), F32)
            for hq, hs, half, same in _heads():
                if hs == s:
                    vm = jnp.where(lo if half == 0 else hi, vcat if same else vswap, 0.0).astype(BF16)
                    o = o + _mm(p_ref[hq], vm)
            am_ref[:, s * LANES:(s + 1) * LANES] = o.astype(BF16)

        for rows in _strips():
            u_ref[rows, :] = _gelu(z_ref[rows, ATTN_W + 2 * KV_W:ATTN_W + 2 * KV_W + GM_W].astype(F32))
            vn_ref[rows, :], _, _ = _layernorm_strip(
                _gelu(z_ref[rows, ATTN_W + 2 * KV_W + GM_W:D_IN].astype(F32)), lng_ref, lnb_ref)
        mixed = _gm_mixed(vn_ref, ws_ref, bs_ref, lo, hi)
        for s in range(GM_W // LANES):
            am_ref[:, ATTN_W + s * LANES:ATTN_W + (s + 1) * LANES] = (
                u_ref[:, s * LANES:(s + 1) * LANES] * mixed[s]).astype(BF16)

    return pl.pallas_call(
        body, name="f2_attn_gm", grid=(nb,),
        in_specs=_block_specs_z(nb) + [
            pl.BlockSpec(memory_space=pltpu.SMEM), _bias_spec(), _whole((1, GM_W)),
            _whole((1, GM_W)), _whole((N_Q_HEADS, BLK, BLK)), _whole((BLK, GM_W))],
        out_specs=_rows(BLK, ATTN_W + GM_W),
        out_shape=_sds((t, ATTN_W + GM_W), BF16),
        scratch_shapes=[pltpu.VMEM((N_Q_HEADS, BLK, 2 * BLK), F32), pltpu.VMEM((N_Q_HEADS, BLK, 2 * BLK), BF16),
                        pltpu.VMEM((BLK, GM_W), F32), pltpu.VMEM((BLK, GM_W), F32)],
        compiler_params=_params())(z, z, z, sinks, bias, ln_g, ln_b, ws, bs_full)


def _f3_mix_out(am, h, ga, gg, w, gpost):
    t = h.shape[0]
    tm = _tile(t, 512)

    def body(am_ref, h_ref, ga_ref, gg_ref, w_ref, gp_ref, heads_ref, mix_ref, h1_ref):
        heads = jnp.concatenate([_rms(am_ref[:, :ATTN_W].astype(F32), ga_ref[...]),
                                 _rms(am_ref[:, ATTN_W:].astype(F32), gg_ref[...])], axis=1).astype(BF16)
        heads_ref[...] = heads
        mix = _mm(heads, w_ref[...])
        mix_ref[...] = mix.astype(BF16)
        h1_ref[...] = h_ref[...] + _rms(mix, gp_ref[...])

    return pl.pallas_call(
        body, name="f3_mix_out", grid=(t // tm,),
        in_specs=[_rows(tm, D_MODEL), _rows(tm, D_MODEL), _whole((1, ATTN_W)), _whole((1, GM_W)),
                  _whole((D_MODEL, D_MODEL)), _whole((1, D_MODEL))],
        out_specs=[_rows(tm, D_MODEL)] * 3,
        out_shape=[_sds((t, D_MODEL), BF16), _sds((t, D_MODEL), BF16), _sds((t, D_MODEL), F32)],
        compiler_params=_params())(am, h, ga, gg, w, gpost)


def _f4a_ffn_in(h1, gf, wg_t, wu_t):
    t = h1.shape[0]
    tm = _tile(t, 256)

    def body(h_ref, g_ref, wg_ref, wu_ref, f_ref, gt_ref, up_ref):
        f = _rms(h_ref[...], g_ref[...]).astype(BF16)
        f_ref[...] = f
        gt_ref[...] = _nt(f, wg_ref[...]).astype(BF16)
        up_ref[...] = _nt(f, wu_ref[...]).astype(BF16)

    return pl.pallas_call(
        body, name="f4a_ffn_in", grid=(t // tm,),
        in_specs=[_rows(tm, D_MODEL), _whole((1, D_MODEL)), _whole((D_FF, D_MODEL)), _whole((D_FF, D_MODEL))],
        out_specs=[_rows(tm, D_MODEL), _rows(tm, D_FF), _rows(tm, D_FF)],
        out_shape=[_sds((t, D_MODEL), BF16), _sds((t, D_FF), BF16), _sds((t, D_FF), BF16)],
        compiler_params=_params())(h1, gf, wg_t, wu_t)


def _f4b_ffn_out(gt, up, wd, h1, gfp):
    t = h1.shape[0]
    tm = _tile(t, 256)

    def body(gt_ref, up_ref, wd_ref, h_ref, g_ref, dn_ref, h2_ref):
        gt = gt_ref[...].astype(F32)
        act = (gt * _sigmoid(gt) * up_ref[...].astype(F32)).astype(BF16)
        dn = _mm(act, wd_ref[...])
        dn_ref[...] = dn.astype(BF16)
        h2_ref[...] = h_ref[...] + _rms(dn, g_ref[...])

    return pl.pallas_call(
        body, name="f4b_ffn_out", grid=(t // tm,),
        in_specs=[_rows(tm, D_FF), _rows(tm, D_FF), _whole((D_FF, D_MODEL)), _rows(tm, D_MODEL),
                  _whole((1, D_MODEL))],
        out_specs=[_rows(tm, D_MODEL)] * 2,
        out_shape=[_sds((t, D_MODEL), BF16), _sds((t, D_MODEL), F32)],
        compiler_params=_params())(gt, up, wd, h1, gfp)


def _f5_ple(h2, gpl, wpg, p, layer, wple, target=None):
    t = h2.shape[0]
    tm = _tile(t, 512)
    with_loss = target is not None

    def body(h_ref, g_ref, wpg_ref, p_ref, wple_ref, *rest):
        r_ref, pg_ref, pe_ref, last_ref = rest[-5:-1] if with_loss else rest[-4:]
        h = h_ref[...]
        r = _rms(h, g_ref[...]).astype(BF16)
        r_ref[...] = r
        pg = _mm(r, wpg_ref[...])
        pe = _mm(p_ref[...].astype(BF16), wple_ref[...])
        pg_ref[...] = pg.astype(BF16)
        pe_ref[...] = pe.astype(BF16)
        h3 = h + pe * _sigmoid(pg)
        if with_loss:
            err = h3 - rest[0][...]
            last_ref[...] = err * (1.0 / D_MODEL)
            _accumulate(rest[-1], jnp.sum(err * err, keepdims=True))
        else:
            last_ref[...] = h3

    return pl.pallas_call(
        body, name="f5_ple_loss" if with_loss else "f5_ple", grid=(t // tm,),
        in_specs=[_rows(tm, D_MODEL), _whole((1, D_MODEL)), _whole((D_MODEL, D_MODEL)),
                  _layer_rows(tm, PLE_DIM, layer), _whole((PLE_DIM, D_MODEL))] + [_rows(tm, D_MODEL)] * with_loss,
        out_specs=[_rows(tm, D_MODEL)] * 4 + [_whole((1, LANES))] * with_loss,
        out_shape=[_sds((t, D_MODEL), BF16)] * 3 + [_sds((t, D_MODEL), F32)] + [_sds((1, LANES), F32)] * with_loss,
        compiler_params=_params())(h2, gpl, wpg, p, wple, *([target] if with_loss else []))


def _b5_ple(dh3, h2, pg, pe, gpl, wpg):
    t = h2.shape[0]
    tm = _tile(t, 512)

    def body(dh_ref, h_ref, pg_ref, pe_ref, g_ref, w_ref, dpe_ref, dpg_ref, dh2_ref, dg_ref):
        dh = dh_ref[...]
        s = _sigmoid(pg_ref[...].astype(F32))
        dpe_ref[...] = (dh * s).astype(BF16)
        dpg = (dh * pe_ref[...].astype(F32) * s * (1.0 - s)).astype(BF16)
        dpg_ref[...] = dpg
        dx, dg = _rms_bwd(_nt(dpg, w_ref[...]), h_ref[...], g_ref[...])
        dh2_ref[...] = dh + dx
        _accumulate(dg_ref, dg)

    return pl.pallas_call(
        body, name="b5_ple", grid=(t // tm,),
        in_specs=[_rows(tm, D_MODEL)] * 4 + [_whole((1, D_MODEL)), _whole((D_MODEL, D_MODEL))],
        out_specs=[_rows(tm, D_MODEL)] * 3 + [_whole((1, D_MODEL))],
        out_shape=[_sds((t, D_MODEL), BF16)] * 2 + [_sds((t, D_MODEL), F32), _sds((1, D_MODEL), F32)],
        compiler_params=_params())(dh3, h2, pg, pe, gpl, wpg)


def _b4a_ffn_out(dh2, dn, gfp, wd, gt, up):
    t = dh2.shape[0]
    tm = _tile(t, 256)

    def body(dh_ref, dn_ref, g_ref, wd_ref, gt_ref, up_ref, ddn_ref, act_ref, dgt_ref, dup_ref, dg_ref):
        ddn, dg = _rms_bwd(dh_ref[...], dn_ref[...].astype(F32), g_ref[...])
        _accumulate(dg_ref, dg)
        ddn = ddn.astype(BF16)
        ddn_ref[...] = ddn
        dact = _nt(ddn, wd_ref[...])
        gt = gt_ref[...].astype(F32)
        up = up_ref[...].astype(F32)
        sg = _sigmoid(gt)
        silu = gt * sg
        act_ref[...] = (silu * up).astype(BF16)
        dup_ref[...] = (dact * silu).astype(BF16)
        dgt_ref[...] = (dact * up * (sg * (1.0 + gt * (1.0 - sg)))).astype(BF16)

    return pl.pallas_call(
        body, name="b4a_ffn_out", grid=(t // tm,),
        in_specs=[_rows(tm, D_MODEL), _rows(tm, D_MODEL), _whole((1, D_MODEL)), _whole((D_FF, D_MODEL)),
                  _rows(tm, D_FF), _rows(tm, D_FF)],
        out_specs=[_rows(tm, D_MODEL), _rows(tm, D_FF), _rows(tm, D_FF), _rows(tm, D_FF), _whole((1, D_MODEL))],
        out_shape=[_sds((t, D_MODEL), BF16)] + [_sds((t, D_FF), BF16)] * 3 + [_sds((1, D_MODEL), F32)],
        compiler_params=_params())(dh2, dn, gfp, wd, gt, up)


def _b4b_ffn_in(dgt, dup, wg_t, wu_t, h1, gf, dh2):
    t = h1.shape[0]
    tm = _tile(t, 512)

    def body(dgt_ref, dup_ref, wg_ref, wu_ref, h_ref, g_ref, dh_ref, dh1_ref, dg_ref):
        df = _mm(dgt_ref[...], wg_ref[...]) + _mm(dup_ref[...], wu_ref[...])
        dx, dg = _rms_bwd(df, h_ref[...], g_ref[...])
        dh1_ref[...] = dh_ref[...] + dx
        _accumulate(dg_ref, dg)

    return pl.pallas_call(
        body, name="b4b_ffn_in", grid=(t // tm,),
        in_specs=[_rows(tm, D_FF), _rows(tm, D_FF), _whole((D_FF, D_MODEL)), _whole((D_FF, D_MODEL)),
                  _rows(tm, D_MODEL), _whole((1, D_MODEL)), _rows(tm, D_MODEL)],
        out_specs=[_rows(tm, D_MODEL), _whole((1, D_MODEL))],
        out_shape=[_sds((t, D_MODEL), F32), _sds((1, D_MODEL), F32)],
        compiler_params=_params())(dgt, dup, wg_t, wu_t, h1, gf, dh2)


def _b3_mix_out(dh1, mix, gpost, w, am, ga, gg):
    t = dh1.shape[0]
    tm = _tile(t, 512)

    def body(dh_ref, mix_ref, gp_ref, w_ref, am_ref, ga_ref, gg_ref, dmix_ref, dam_ref, dgp_ref, dga_ref, dgg_ref):
        dmix, dgp = _rms_bwd(dh_ref[...], mix_ref[...].astype(F32), gp_ref[...])
        _accumulate(dgp_ref, dgp)
        dmix = dmix.astype(BF16)
        dmix_ref[...] = dmix
        dheads = _nt(dmix, w_ref[...])
        dat, dga = _rms_bwd(dheads[:, :ATTN_W], am_ref[:, :ATTN_W].astype(F32), ga_ref[...])
        dgm, dgg = _rms_bwd(dheads[:, ATTN_W:], am_ref[:, ATTN_W:].astype(F32), gg_ref[...])
        dam_ref[:, :ATTN_W] = dat.astype(BF16)
        dam_ref[:, ATTN_W:] = dgm.astype(BF16)
        _accumulate(dga_ref, dga)
        _accumulate(dgg_ref, dgg)

    return pl.pallas_call(
        body, name="b3_mix_out", grid=(t // tm,),
        in_specs=[_rows(tm, D_MODEL), _rows(tm, D_MODEL), _whole((1, D_MODEL)), _whole((D_MODEL, D_MODEL)),
                  _rows(tm, D_MODEL), _whole((1, ATTN_W)), _whole((1, GM_W))],
        out_specs=[_rows(tm, D_MODEL), _rows(tm, D_MODEL), _whole((1, D_MODEL)), _whole((1, ATTN_W)),
                   _whole((1, GM_W))],
        out_shape=[_sds((t, D_MODEL), BF16), _sds((t, D_MODEL), BF16), _sds((1, D_MODEL), F32),
                   _sds((1, ATTN_W), F32), _sds((1, GM_W), F32)],
        compiler_params=_params())(dh1, mix, gpost, w, am, ga, gg)


def _b2_attn_gm(dam, z, sinks, bias, ln_g, ln_b, ws, bs_full):
    t = z.shape[0]
    nb = t // BLK

    def body(dam_ref, z_ref, kp_ref, vp_ref, sink_ref, bias_ref, lng_ref, lnb_ref, ws_ref, wst_ref, bs_ref,
             dzq_ref, dkv_ref, dzuv_ref, dsink_ref, dlng_ref, dlnb_ref, dws_ref, dbs_ref,
             sc_ref, dp_ref, p_ref, ds_ref, u_ref, du_ref, dv_ref, vn_ref, xhat_ref, rstd_ref, dvn_ref):
        n = pl.program_id(0)
        lane = lax.broadcasted_iota(jnp.int32, (1, LANES), 1)
        lo = lane < HEAD_DIM
        hi = lane >= HEAD_DIM
        sub = lax.broadcasted_iota(jnp.int32, (LANES, 1), 0)
        kc, ks, vcat, vswap = _kv_window(z_ref, kp_ref, vp_ref)
        vc = vcat.astype(BF16)
        vs_ = vswap.astype(BF16)
        kc_t = kc.T
        ks_t = ks.T

        def operands(s, half):
            mask = lo if half == 0 else hi
            qs = z_ref[:, s * LANES:(s + 1) * LANES].astype(F32) * (HEAD_DIM ** -0.5)
            qm = jnp.where(mask, qs, 0.0).astype(BF16)
            dom = jnp.where(mask, dam_ref[:, s * LANES:(s + 1) * LANES], 0.0).astype(BF16)
            return qm, dom

        for hq, s, half, same in _heads():
            qm, dom = operands(s, half)
            sc_ref[hq] = _nt(kc if same else ks, qm)
            dp_ref[hq] = _nt(vc if same else vs_, dom)
        dsink = jnp.zeros((1, LANES), F32)
        for hq in range(N_Q_HEADS):
            sink = sink_ref[hq]
            sc = sc_ref[hq] + bias_ref[hq]
            m = jnp.maximum(jnp.max(sc, axis=0, keepdims=True), sink)
            e = jnp.exp(sc - m)
            es = jnp.exp(sink - m)
            inv = 1.0 / (jnp.sum(e, axis=0, keepdims=True) + es)
            pr = e * inv
            dpr = dp_ref[hq]
            row = jnp.sum(dpr * pr, axis=0, keepdims=True)
            p_ref[hq] = pr.astype(BF16)
            ds_ref[hq] = (pr * (dpr - row)).astype(BF16)
            dsink = dsink + jnp.where(lane == hq, -jnp.sum(es * inv * row, keepdims=True), 0.0)
        dk_acc = [jnp.zeros((2 * BLK, LANES), F32), jnp.zeros((2 * BLK, LANES), F32)]
        dv_acc = [jnp.zeros((2 * BLK, LANES), F32), jnp.zeros((2 * BLK, LANES), F32)]
        for s in range(ATTN_W // LANES):
            dq_t = jnp.zeros((LANES, BLK), F32)
            for hq, hs, half, same in _heads():
                if hs != s:
                    continue
                qm, dom = operands(s, half)
                ds = ds_ref[hq]
                dk_acc[same] = dk_acc[same] + _mm(ds, qm)
                dv_acc[same] = dv_acc[same] + _mm(p_ref[hq], dom)
                in_half = (sub < HEAD_DIM) if half == 0 else (sub >= HEAD_DIM)
                dq_t = dq_t + jnp.where(in_half, _mm(kc_t if same else ks_t, ds), 0.0)
            dzq_ref[:, s * LANES:(s + 1) * LANES] = (dq_t.T * (HEAD_DIM ** -0.5)).astype(BF16)
        dk_acc = dk_acc[True] + pltpu.roll(dk_acc[False], HEAD_DIM, 1)
        dv_acc = dv_acc[True] + pltpu.roll(dv_acc[False], HEAD_DIM, 1)
        cur = pl.multiple_of(n * BLK, BLK)
        dkv_ref[pl.ds(cur, BLK), 0:KV_W] = dk_acc[BLK:, :]
        dkv_ref[pl.ds(cur, BLK), KV_W:2 * KV_W] = dv_acc[BLK:, :]

        @pl.when(n > 0)
        def _():
            prv = pl.multiple_of((n - 1) * BLK, BLK)
            dkv_ref[pl.ds(prv, BLK), 0:KV_W] += dk_acc[:BLK, :]
            dkv_ref[pl.ds(prv, BLK), KV_W:2 * KV_W] += dv_acc[:BLK, :]

        _accumulate(dsink_ref, dsink)

        for rows in _strips():
            u_ref[rows, :], du_ref[rows, :] = _gelu_and_grad(
                z_ref[rows, ATTN_W + 2 * KV_W:ATTN_W + 2 * KV_W + GM_W].astype(F32))
            gv, dv_ref[rows, :] = _gelu_and_grad(z_ref[rows, ATTN_W + 2 * KV_W + GM_W:D_IN].astype(F32))
            vn_ref[rows, :], xhat_ref[rows, :], rstd = _layernorm_strip(gv, lng_ref, lnb_ref)
            rstd_ref[rows, :] = jnp.broadcast_to(rstd, (STRIP, LANES))
        mixed = _gm_mixed(vn_ref, ws_ref, bs_ref, lo, hi)

        @pl.when(n == 0)
        def _():
            dws_ref[...] = jnp.zeros_like(dws_ref)

        dbs = jnp.zeros((BLK, LANES), F32)
        for s in range(GM_W // LANES):
            slab = slice(s * LANES, (s + 1) * LANES)
            dgm = dam_ref[:, ATTN_W + s * LANES:ATTN_W + (s + 1) * LANES]
            dzuv_ref[:, slab] = (dgm * mixed[s] * du_ref[:, slab]).astype(BF16)
            dmx = dgm * u_ref[:, slab]
            vsb = vn_ref[:, slab].astype(BF16)
            dvn = jnp.zeros((BLK, LANES), F32)
            for half in range(2):
                h = 2 * s + half
                mask = lo if half == 0 else hi
                dmm = jnp.where(mask, dmx, 0.0)
                dbs = dbs + jnp.where(lane == h, jnp.sum(dmm, axis=1, keepdims=True), 0.0)
                dmm = dmm.astype(BF16)
                _, causal = _tril_w(ws_ref, h)
                ti = lax.broadcasted_iota(jnp.int32, (BLK, BLK), 0)
                si = lax.broadcasted_iota(jnp.int32, (BLK, BLK), 1)
                wt_t = jnp.where(ti <= si, wst_ref[h], 0.0).astype(BF16)
                dvn = dvn + jnp.where(mask, _mm(wt_t, dmm), 0.0)
                dws_ref[h] += jnp.where(causal, _nt(dmm, vsb), 0.0)
            dvn_ref[:, slab] = dvn
        _accumulate(dbs_ref, dbs)
        dlnb = jnp.zeros((STRIP, GM_W), F32)
        dlng = jnp.zeros((STRIP, GM_W), F32)
        for rows in _strips():
            dvn = dvn_ref[rows, :]
            xhat = xhat_ref[rows, :]
            dlnb = dlnb + dvn
            dlng = dlng + dvn * xhat
            dxh = dvn * lng_ref[...]
            dgv = rstd_ref[rows, 0:1] * (dxh - jnp.mean(dxh, axis=-1, keepdims=True)
                                         - xhat * jnp.mean(dxh * xhat, axis=-1, keepdims=True))
            dzuv_ref[rows, GM_W:] = (dgv * dv_ref[rows, :]).astype(BF16)
        _accumulate(dlnb_ref, jnp.sum(dlnb, axis=0, keepdims=True))
        _accumulate(dlng_ref, jnp.sum(dlng, axis=0, keepdims=True))

    return pl.pallas_call(
        body, name="b2_attn_gm", grid=(nb,),
        in_specs=[_rows(BLK, ATTN_W + GM_W)] + _block_specs_z(nb) + [
            pl.BlockSpec(memory_space=pltpu.SMEM), _bias_spec(transposed=True), _whole((1, GM_W)),
            _whole((1, GM_W)), _whole((N_Q_HEADS, BLK, BLK)), _whole((N_Q_HEADS, BLK, BLK)), _whole((BLK, GM_W))],
        out_specs=[_rows(BLK, ATTN_W), _whole((t, 2 * KV_W)), _rows(BLK, 2 * GM_W), _whole((1, LANES)),
                   _whole((1, GM_W)), _whole((1, GM_W)), _whole((N_Q_HEADS, BLK, BLK)), _whole((BLK, LANES))],
        out_shape=[_sds((t, ATTN_W), BF16), _sds((t, 2 * KV_W), F32), _sds((t, 2 * GM_W), BF16),
                   _sds((1, LANES), F32), _sds((1, GM_W), F32), _sds((1, GM_W), F32),
                   _sds((N_Q_HEADS, BLK, BLK), F32), _sds((BLK, LANES), F32)],
        scratch_shapes=[pltpu.VMEM((N_Q_HEADS, 2 * BLK, BLK), F32)] * 2
        + [pltpu.VMEM((N_Q_HEADS, 2 * BLK, BLK), BF16)] * 2 + [pltpu.VMEM((BLK, GM_W), F32)] * 5
        + [pltpu.VMEM((BLK, LANES), F32), pltpu.VMEM((BLK, GM_W), F32)],
        compiler_params=_params())(dam, z, z, z, sinks, jnp.swapaxes(bias, 2, 3), ln_g, ln_b, ws,
                                   jnp.swapaxes(ws, 1, 2), bs_full)


def _b1_norm_in(dzq, dkv, dzuv, w_t, h, g, dh1):
    t = h.shape[0]
    tm = _tile(t, 512)

    def body(dzq_ref, dkv_ref, dzuv_ref, w_ref, h_ref, g_ref, dh_ref, dh0_ref, dg_ref):
        dz = jnp.concatenate([dzq_ref[...], dkv_ref[...].astype(BF16), dzuv_ref[...]], axis=1)
        dx, dg = _rms_bwd(_mm(dz, w_ref[...]), h_ref[...], g_ref[...])
        dh0_ref[...] = dh_ref[...] + dx
        _accumulate(dg_ref, dg)

    return pl.pallas_call(
        body, name="b1_norm_in", grid=(t // tm,),
        in_specs=[_rows(tm, ATTN_W), _rows(tm, 2 * KV_W), _rows(tm, 2 * GM_W), _whole((D_IN, D_MODEL)),
                  _rows(tm, D_MODEL), _whole((1, D_MODEL)), _rows(tm, D_MODEL)],
        out_specs=[_rows(tm, D_MODEL), _whole((1, D_MODEL))],
        out_shape=[_sds((t, D_MODEL), F32), _sds((1, D_MODEL), F32)],
        compiler_params=_params())(dzq, dkv, dzuv, w_t, h, g, dh1)


def _weight_grad(x, dy, name, layer=None):
    xs = tuple(x) if isinstance(x, (tuple, list)) else (x,)
    t = xs[0].shape[-2]
    k = sum(a.shape[-1] for a in xs)
    n = dy.shape[1]
    tm = _tile(t, 1024)
    steps = t // tm
    x_specs = ([_rows(tm, a.shape[1]) for a in xs] if layer is None else [_layer_rows(tm, k, layer)])

    def body(*refs):
        x_refs, dy_ref, dw_ref, acc_ref = refs[:len(xs)], refs[-3], refs[-2], refs[-1]
        i = pl.program_id(0)

        def product():
            cols = [r[...].astype(BF16) for r in x_refs]
            return _tn(cols[0] if len(cols) == 1 else jnp.concatenate(cols, axis=1), dy_ref[...])

        @pl.when(i == 0)
        def _():
            acc_ref[...] = product()

        @pl.when(i > 0)
        def _():
            acc_ref[...] += product()

        @pl.when(i == steps - 1)
        def _():
            dw_ref[...] = acc_ref[...].astype(BF16)

    return pl.pallas_call(
        body, name=name, grid=(steps,),
        in_specs=x_specs + [_rows(tm, n)],
        out_specs=_whole((k, n)),
        out_shape=_sds((k, n), BF16),
        scratch_shapes=[pltpu.VMEM((k, n), F32)],
        compiler_params=_params())(*xs, dy)


_ANY = pl.BlockSpec(memory_space=pl.ANY)
_HBM = pl.BlockSpec(memory_space=pltpu.HBM)
_SEM = pl.BlockSpec(memory_space=pltpu.SEMAPHORE)
_EFFECT = pltpu.SideEffectType.DATAFLOW_SIDE_EFFECTING


def _mesh_pos():
    return lax.axis_index("x"), lax.axis_index("y"), lax.axis_index("c")


def _other_chips(x, y):
    return [(1 - x, y), (x, 1 - y), (1 - x, 1 - y)]


def _peers(mode, x, y, c):
    if mode == "devices":
        peers = []
        for j in range(1, N_DEV):
            px = 1 - x if (j >> 2) & 1 else x
            py = 1 - y if (j >> 1) & 1 else y
            pc = 1 - c if j & 1 else c
            peers.append(((px, py, pc), 4 * px + 2 * py + pc))
        return peers, 4 * x + 2 * y + c
    if mode == "sibling":
        return [((x, y, 1 - c), None)], None
    return [((px, py, c), 2 * px + py) for px, py in _other_chips(x, y)], 2 * x + y


def _descriptors(mode, srcs, lands, send_sems, recv_sems, with_incoming=True):
    x, y, c = _mesh_pos()
    peers, me = _peers(mode, x, y, c)
    scatter = mode == "scatter"
    outgoing, incoming = [], []
    for i, (src, land) in enumerate(zip(srcs, lands)):
        for j, (dev, slot) in enumerate(peers):
            sem = i * len(peers) + j
            common = dict(send_sem=send_sems.at[sem], recv_sem=recv_sems.at[sem], device_id=dev,
                          device_id_type=pl.DeviceIdType.MESH)
            whole = mode == "sibling"
            outgoing.append(pltpu.make_async_remote_copy(
                src_ref=src.at[slot] if scatter else src, dst_ref=land if whole else land.at[me], **common))
            if with_incoming:
                incoming.append(pltpu.make_async_remote_copy(
                    src_ref=src.at[me] if scatter else src, dst_ref=land if whole else land.at[slot], **common))
    return outgoing, incoming


def _n_sems(mode, k):
    return k * {"devices": N_DEV - 1, "sibling": 1}.get(mode, N_CHIPS - 1)


def _exchange_start(srcs, lands, mode, name, after=None):
    k = len(srcs)
    arrs = [*srcs, *lands]

    def body(*refs):
        skip = 1 if after is not None else 0
        send_sems, recv_sems = refs[2 * k + skip], refs[2 * k + skip + 1]
        outgoing, _ = _descriptors(mode, refs[:k], refs[k:2 * k], send_sems, recv_sems, with_incoming=False)
        for cp in outgoing:
            cp.start()
        refs[-1][...] = jnp.zeros_like(refs[-1])

    operands = [pltpu.with_memory_space_constraint(a, pltpu.HBM) for a in arrs]
    in_specs = [_HBM] * (2 * k)
    if after is not None:
        operands.append(after)
        in_specs.append(_ANY)
    sems = pltpu.SemaphoreType.DMA((_n_sems(mode, k),))
    res = pl.pallas_call(
        body, name=name, in_specs=in_specs,
        out_shape=(sems, sems, *[pltpu.HBM(a.shape, a.dtype) for a in arrs], _sds((8, LANES), F32)),
        out_specs=(_SEM, _SEM, *[_HBM] * (2 * k), pl.BlockSpec(memory_space=pltpu.VMEM)),
        input_output_aliases={i: 2 + i for i in range(2 * k)},
        compiler_params=pltpu.CompilerParams(has_side_effects=_EFFECT))(*operands)
    return dict(mode=mode, send=res[0], recv=res[1], srcs=res[2:2 + k], lands=res[2 + k:2 + 2 * k], token=res[-1])


def _exchange_wait(flight, name, after):
    mode, k = flight["mode"], len(flight["srcs"])
    arrs = [*flight["srcs"], *flight["lands"]]

    def body(*refs):
        outgoing, incoming = _descriptors(mode, refs[:k], refs[k:2 * k], refs[2 * k], refs[2 * k + 1])
        for cp in outgoing:
            cp.wait_send()
        for cp in incoming:
            cp.wait_recv()

    res = pl.pallas_call(
        body, name=name, in_specs=[_HBM] * (2 * k) + [_SEM, _SEM, _ANY],
        out_shape=tuple(pltpu.HBM(a.shape, a.dtype) for a in arrs), out_specs=tuple([_HBM] * (2 * k)),
        input_output_aliases={i: i for i in range(2 * k)},
        compiler_params=pltpu.CompilerParams(has_side_effects=_EFFECT))(*arrs, flight["send"], flight["recv"], after)
    return res[:k], res[k:]


def _adamw(w, g, m, v):
    m = ADAM_B1 * m + (1.0 - ADAM_B1) * g
    v = ADAM_B2 * v + (1.0 - ADAM_B2) * (g * g)
    m_hat = m / (1.0 - ADAM_B1 ** ADAM_STEP)
    v_hat = v / (1.0 - ADAM_B2 ** ADAM_STEP)
    delta = -ADAM_LR * (m_hat / (jnp.sqrt(v_hat) + ADAM_EPS) + ADAM_WD * w)
    return delta, m, v


def _row_tile(rows, cols, n_arrays):
    budget = VMEM_LIMIT_BYTES // 2
    padded = -(-cols // LANES) * LANES
    cap = min(rows, max(16, budget // (2 * n_arrays * padded * 4)))
    return max(tr for tr in range(16, cap + 1, 16) if rows % tr == 0)


def _sum_chips(landing, name):
    _, r, c = landing.shape
    tr = _row_tile(r, c, 4)

    def body(l_ref, s_ref):
        acc = l_ref[0].astype(F32)
        for s in range(1, N_CHIPS):
            acc = acc + l_ref[s].astype(F32)
        s_ref[...] = acc

    return pl.pallas_call(
        body, name=name, grid=(r // tr,),
        in_specs=[pl.BlockSpec((N_CHIPS, tr, c), lambda i: (0, i, 0))],
        out_specs=_rows(tr, c), out_shape=_sds((r, c), F32),
        compiler_params=_params())(landing)


def _adamw_big(mine, sibling, w, m, v, prev, layer, name):
    _, r, c = w.shape
    tr = _row_tile(r, c, 9)
    stacked = pl.BlockSpec((None, tr, c), lambda i: (layer, i, 0))

    def body(a_ref, b_ref, w_ref, m_ref, v_ref, *rest):
        g_out, d_out, m_out, v_out = rest[-4:]
        g = a_ref[...] + b_ref[...]
        g_out[...] = g
        d_out[...], m_out[...], v_out[...] = _adamw(w_ref[...], g, m_ref[...], v_ref[...])

    prev = list(prev) if prev is not None else []
    return pl.pallas_call(
        body, name=name, grid=(r // tr,),
        in_specs=[_rows(tr, c)] * 2 + [stacked] * 3 + [_ANY] * len(prev),
        out_specs=[stacked] * 4, out_shape=[_sds(w.shape, F32)] * 4,
        input_output_aliases={5 + j: j for j in range(len(prev))},
        compiler_params=_params())(mine, sibling, w, m, v, *prev)


def _adamw_small(gathered, w, m, v):
    r = w.shape[0]

    def body(a_ref, w_ref, m_ref, v_ref, g_out, d_out, m_out, v_out):
        g = a_ref[0]
        for d in range(1, N_DEV):
            g = g + a_ref[d]
        g_out[...] = g
        d_out[...], m_out[...], v_out[...] = _adamw(w_ref[...], g, m_ref[...], v_ref[...])

    return pl.pallas_call(
        body, name="adamw_small", grid=(1,),
        in_specs=[_whole((N_DEV, r, LANES))] + [_whole((r, LANES))] * 3,
        out_specs=[_whole((r, LANES))] * 4, out_shape=[_sds((r, LANES), F32)] * 4,
        compiler_params=_params())(gathered, w, m, v)


def _shard_view(name, stacked):
    return jnp.swapaxes(stacked, 1, 2) if name in TRANSPOSED else stacked


def _row_split(name):
    return name in TRANSPOSED or BIG[name] == 1


def _assemble(name, landed):
    _, r, c = landed.shape
    if _row_split(name):
        return landed.reshape(N_CHIPS * r, c)
    return landed.transpose(1, 0, 2).reshape(r, N_CHIPS * c)


def _split(name, whole):
    r, c = whole.shape
    if _row_split(name):
        return whole.reshape(N_CHIPS, r // N_CHIPS, c)
    return whole.reshape(r, N_CHIPS, c // N_CHIPS).transpose(1, 0, 2)


def _with_own_slot(landed, own, slot):
    return lax.dynamic_update_index_in_dim(landed, own, slot, 0)


def _pack_small(params, names):
    pieces = []
    for name in names:
        flat = params[name].reshape(-1)
        pieces.append(jnp.pad(flat, (0, -flat.shape[0] % SMALL_PAD)))
    return jnp.concatenate(pieces).reshape(-1, LANES)


def _unpack_small(packed, like, names):
    flat = packed.reshape(-1)
    out, off = {}, 0
    for name in names:
        size = like[name].size
        out[name] = flat[off:off + size].reshape(like[name].shape)
        off += size + (-size % SMALL_PAD)
    return out


def kernel(x, p, ln_mix_pre, w_in, attn_sinks, gm_ln_g, gm_ln_b, gm_ws, gm_bs, g_attn_out, g_gm_out, w_out, ln_mix_post, ln_ffn_pre, w_ffn_gate, w_ffn_up, w_ffn_down, ln_ffn_post, w_ple, ln_ple_gate, w_ple_gate, loss_target, m_ln_mix_pre, m_w_in, m_attn_sinks, m_gm_ln_g, m_gm_ln_b, m_gm_ws, m_gm_bs, m_g_attn_out, m_g_gm_out, m_w_out, m_ln_mix_post, m_ln_ffn_pre, m_w_ffn_gate, m_w_ffn_up, m_w_ffn_down, m_ln_ffn_post, m_w_ple, m_ln_ple_gate, m_w_ple_gate, v_ln_mix_pre, v_w_in, v_attn_sinks, v_gm_ln_g, v_gm_ln_b, v_gm_ws, v_gm_bs, v_g_attn_out, v_g_gm_out, v_w_out, v_ln_mix_post, v_ln_ffn_pre, v_w_ffn_gate, v_w_ffn_up, v_w_ffn_down, v_ln_ffn_post, v_w_ple, v_ln_ple_gate, v_w_ple_gate):
    given = dict(locals())
    wts = {n: given[n] for n in WEIGHTS}
    mom = {n: given["m_" + n] for n in WEIGHTS}
    var = {n: given["v_" + n] for n in WEIGHTS}
    depth = w_in.shape[0]
    h = x[0]
    target = loss_target[0]
    chip = 2 * lax.axis_index("x") + lax.axis_index("y")
    device = 2 * chip + lax.axis_index("c")
    row = lambda a, i: a[i][None, :]
    bs_full = [jnp.repeat(gm_bs[i].T, HEAD_DIM, axis=1) for i in range(depth)]
    bias = _alibi_bias()
    kinds = ("grad", "delta", "m", "v")
    wview = {n: _shard_view(n, wts[n]) for n in BIG_NAMES}
    mview = {n: _shard_view(n, mom[n]) for n in BIG_NAMES}
    vview = {n: _shard_view(n, var[n]) for n in BIG_NAMES}

    zero = lambda flight: flight["token"][0:1, 0:1]

    def start_gather(i, names, after, tag):
        shards = [wview[n][i].astype(BF16) for n in names]
        lands = [lax.empty((N_CHIPS,) + s.shape, BF16) for s in shards]
        return _exchange_start(shards, lands, "gather", f"gather_weights_start_{i}{tag}", after)

    def finish_gather(flight, names, i, after, tag):
        shards, lands = _exchange_wait(flight, f"gather_weights_wait_{i}{tag}", after)
        return {n: _assemble(n, _with_own_slot(l, s, chip)) for n, s, l in zip(names, shards, lands)}

    first, after = [], None
    for k, names in enumerate(GATHER_GROUPS):
        first.append(start_gather(0, names, after, "abc"[k]))
        after = first[-1]["token"]
    full = [finish_gather(first[0], GATHER_GROUPS[0], 0, after, "a")] + [None] * (depth - 1)
    saved = []
    for i in range(depth):
        w = full[i]
        g_in = row(ln_mix_pre, i)
        if i + 1 < depth:
            flight = start_gather(i + 1, BIG_NAMES, w['w_in'], "")
            g_in = g_in + zero(flight)
        z, a = _f1_norm_in(h, g_in, w['w_in'])
        am = _f2_attn_gm(z, attn_sinks[i], bias, row(gm_ln_g, i), row(gm_ln_b, i), gm_ws[i], bs_full[i])
        if i == 0:
            w.update(finish_gather(first[1], GATHER_GROUPS[1], 0, am, "b"))
        heads, mix, h1 = _f3_mix_out(am, h, row(g_attn_out, i), row(g_gm_out, i), w['w_out'], row(ln_mix_post, i))
        f, gt, up = _f4a_ffn_in(h1, row(ln_ffn_pre, i), w['w_ffn_gate'], w['w_ffn_up'])
        if i == 0:
            w.update(finish_gather(first[2], GATHER_GROUPS[2], 0, gt, "c"))
        dn, h2 = _f4b_ffn_out(gt, up, w['w_ffn_down'], h1, row(ln_ffn_post, i))
        ple = _f5_ple(h2, row(ln_ple_gate, i), w['w_ple_gate'], p, i, w['w_ple'],
                      target if i + 1 == depth else None)
        r, pg, pe = ple[:3]
        saved.append(dict(h=h, z=z, a=a, am=am, heads=heads, mix=mix, h1=h1, f=f, gt=gt, up=up, dn=dn, h2=h2,
                          r=r, pg=pg, pe=pe))
        if i + 1 < depth:
            h = ple[3]
            full[i + 1] = finish_gather(flight, BIG_NAMES, i + 1, h, "")

    dh, sq = ple[3], ple[4]
    loss = lax.psum(0.5 / D_MODEL * sq[0, 0], ("x", "y", "c"))

    chain = {n: None for n in BIG_NAMES}
    owed = []
    small_out = [{k: {} for k in kinds} for _ in range(depth)]

    def start_scatter(i, names, dws, after, tag):
        parts = [_split(n, dws[n]) for n in names]
        lands = [lax.empty(q.shape, BF16) for q in parts]
        return _exchange_start(parts, lands, "scatter", f"scatter_grads_start_{i}{tag}", after)

    def finish_scatter(i, names, flight, after, tag):
        parts, lands = _exchange_wait(flight, f"scatter_grads_wait_{i}{tag}", after)
        lands = [_with_own_slot(l, lax.dynamic_index_in_dim(q, chip, 0, keepdims=False), chip)
                 for q, l in zip(parts, lands)]
        partial = [_sum_chips(l, "sum_chips_" + n) for n, l in zip(names, lands)]
        flight = _exchange_start(partial, [lax.empty(q.shape, F32) for q in partial], "sibling",
                                 f"sibling_grads_start_{i}{tag}", after)

        def update(later):
            mine, theirs = _exchange_wait(flight, f"sibling_grads_wait_{i}{tag}", later)
            for n, a, b in zip(names, mine, theirs):
                chain[n] = _adamw_big(a, b, wview[n], mview[n], vview[n], chain[n], i, "adamw_" + n)

        owed.append(update)
        return flight["token"]

    def settle(later):
        while owed:
            owed.pop(0)(later)

    def start_small(i, names, small, after, tag):
        packed = _pack_small(small, names)
        land = lax.empty((N_DEV,) + packed.shape, F32)
        return _exchange_start([packed], [land], "devices", f"gather_small_grads_start_{i}{tag}", after)

    def finish_small(i, names, flight, after, tag):
        (packed,), (gathered,) = _exchange_wait(flight, f"gather_small_grads_wait_{i}{tag}", after)
        gathered = _with_own_slot(gathered, packed, device)
        layer = lambda d: _pack_small({n: d[n][i] for n in names}, names)
        res = _adamw_small(gathered, layer(wts), layer(mom), layer(var))
        for k, a in zip(kinds, res):
            small_out[i][k].update(_unpack_small(a, {n: wts[n][i] for n in names}, names))
        return gathered

    pending, done, behind = [], None, None
    for i in reversed(range(depth)):
        s, w = saved[i], full[i]
        last = i == 0
        dws, small = {}, {}
        gain = row(ln_ple_gate, i)
        if behind is not None:
            gain = gain + behind
        dpe, dpg, dh2, dg = _b5_ple(dh, s['h2'], s['pg'], s['pe'], gain, w['w_ple_gate'])
        small['ln_ple_gate'] = dg[0]
        dws['w_ple'] = _weight_grad(p, dpe, "dw_ple", layer=i)
        dws['w_ple_gate'] = _weight_grad(s['r'], dpg, "dw_ple_gate")

        ddn, act, dgt, dup, dg = _b4a_ffn_out(dh2, s['dn'], row(ln_ffn_post, i), w['w_ffn_down'], s['gt'], s['up'])
        small['ln_ffn_post'] = dg[0]
        dws['w_ffn_down'] = _weight_grad(act, ddn, "dw_ffn_down")
        dws['w_ffn_gate'] = _weight_grad(dgt, s['f'], "dw_ffn_gate")
        dws['w_ffn_up'] = _weight_grad(dup, s['f'], "dw_ffn_up")
        gain = row(ln_ffn_pre, i)
        if last:
            flight_a = start_scatter(i, SCATTER_GROUPS[0], dws, None, "a")
            gain = gain + zero(flight_a)
        dh1, dg = _b4b_ffn_in(dgt, dup, w['w_ffn_gate'], w['w_ffn_up'], s['h1'], gain, dh2)
        small['ln_ffn_pre'] = dg[0]

        dmix, dam, dgp, dga, dgg = _b3_mix_out(dh1, s['mix'], row(ln_mix_post, i), w['w_out'], s['am'],
                                               row(g_attn_out, i), row(g_gm_out, i))
        small['ln_mix_post'] = dgp[0]
        small['g_attn_out'] = dga[0]
        small['g_gm_out'] = dgg[0]
        dws['w_out'] = _weight_grad(s['heads'], dmix, "dw_out")
        gain = row(gm_ln_g, i)
        if last:
            flight_b = start_scatter(i, SCATTER_GROUPS[1], dws, flight_a["token"], "b")
            gain = gain + zero(flight_b)

        dzq, dkv, dzuv, dsink, dlng, dlnb, dgws, dbs = _b2_attn_gm(
            dam, s['z'], attn_sinks[i], bias, gain, row(gm_ln_b, i), gm_ws[i], bs_full[i])
        small['attn_sinks'] = dsink[0, :N_Q_HEADS]
        small['gm_ln_g'] = dlng[0]
        small['gm_ln_b'] = dlnb[0]
        small['gm_ws'] = dgws
        small['gm_bs'] = dbs[:, :N_Q_HEADS].T
        gain = row(ln_mix_pre, i)
        if last:
            flight_s = start_small(i, SMALL_EARLY, small, flight_b["token"], "a")
            gain = gain + zero(flight_s)

        dh, dg = _b1_norm_in(dzq, dkv, dzuv, w['w_in'], s['h'], gain, dh1)
        small['ln_mix_pre'] = dg[0]
        settle(dh)
        for finish in pending:
            done = finish(dh)
        pending = []
        if last:
            done = finish_scatter(i, SCATTER_GROUPS[0], flight_a, dh, "a")
            done = finish_scatter(i, SCATTER_GROUPS[1], flight_b, done, "b")
        dws['w_in'] = _weight_grad((dzq, dkv, dzuv), s['a'], "dw_in")
        if last:
            flight_c = start_scatter(i, SCATTER_GROUPS[2], dws, done, "c")
            flight_t = start_small(i, SMALL_LATE, small, flight_c["token"], "b")
            finish_small(i, SMALL_EARLY, flight_s, flight_t["token"], "a")
            done = finish_scatter(i, SCATTER_GROUPS[2], flight_c, flight_t["token"], "c")
            finish_small(i, SMALL_LATE, flight_t, flight_t["token"], "b")
            settle(done)
        else:
            flight_c = start_scatter(i, BIG_NAMES, dws, done, "")
            flight_s = start_small(i, SMALL_NAMES, small, flight_c["token"], "")
            behind = zero(flight_s)
            pending = [functools.partial(finish_small, i, SMALL_NAMES, flight_s, tag=""),
                       functools.partial(finish_scatter, i, BIG_NAMES, flight_c, tag="")]
    grad_x = dh[None]

    out = {k: {n: _shard_view(n, chain[n][j]) for n in BIG_NAMES} for j, k in enumerate(kinds)}
    for k in kinds:
        out[k].update({n: jnp.stack([small_out[i][k][n] for i in range(depth)]) for n in SMALL_NAMES})

    return (loss, grad_x, *[out["grad"][n] for n in WEIGHTS], *[out["delta"][n] for n in WEIGHTS],
            *[out["m"][n] for n in WEIGHTS], *[out["v"][n] for n in WEIGHTS])
```

```python
import functools
import math

import jax
import jax.numpy as jnp
from jax import lax
from jax.experimental import pallas as pl
from jax.experimental.pallas import tpu as pltpu

F32 = jnp.float32
BF16 = jnp.bfloat16

D_MODEL = 1024
HEAD_DIM = 64
N_Q_HEADS = 8
BLK = 128
ATTN_W = 512
KV_W = 128
GM_W = 512
D_IN = ATTN_W + 2 * KV_W + 2 * GM_W
D_FF = 2816
PLE_DIM = 256
DEPTH = 4
NORM_EPS = 1e-6
NEG_BIG = -1e30
N_CHIPS = 4
N_DEV = 8

ADAM_LR = 0.001
ADAM_B1 = 0.9
ADAM_B2 = 0.999
ADAM_EPS = 1e-08
ADAM_WD = 0.01
ADAM_STEP = 10

VMEM_LIMIT_BYTES = 56 * 1024 * 1024
LANES = 128
STRIP = 16
GELU_C0 = math.sqrt(2.0 / math.pi)
GELU_C1 = 0.044715
ALIBI_SLOPES = tuple(2.0 ** (-8.0 * (h + 1.0) / N_Q_HEADS) for h in range(N_Q_HEADS))

WEIGHTS = ['ln_mix_pre', 'w_in', 'attn_sinks', 'gm_ln_g', 'gm_ln_b', 'gm_ws', 'gm_bs', 'g_attn_out',
           'g_gm_out', 'w_out', 'ln_mix_post', 'ln_ffn_pre', 'w_ffn_gate', 'w_ffn_up', 'w_ffn_down',
           'ln_ffn_post', 'w_ple', 'ln_ple_gate', 'w_ple_gate']
BIG = {'w_in': 2, 'w_out': 1, 'w_ffn_gate': 2, 'w_ffn_up': 2, 'w_ffn_down': 1, 'w_ple': 2, 'w_ple_gate': 1}
BIG_NAMES = list(BIG)
TRANSPOSED = ('w_in', 'w_ffn_gate', 'w_ffn_up')
SMALL_NAMES = [n for n in WEIGHTS if n not in BIG]
SMALL_PAD = 1024
GATHER_GROUPS = (('w_in',), ('w_out', 'w_ffn_gate', 'w_ffn_up'), ('w_ffn_down', 'w_ple_gate', 'w_ple'))
GROUPED_LAYERS = (0, 1)
SCATTER_GROUPS = (('w_ple', 'w_ple_gate', 'w_ffn_down', 'w_ffn_gate', 'w_ffn_up'), ('w_out',), ('w_in',))
SMALL_LATE = ('ln_mix_pre',)
SMALL_EARLY = tuple(n for n in SMALL_NAMES if n not in SMALL_LATE)


def _nt(a, b):
    return lax.dot_general(a, b, (((1,), (1,)), ((), ())), preferred_element_type=F32)


def _tn(a, b):
    return lax.dot_general(a, b, (((0,), (0,)), ((), ())), preferred_element_type=F32)


def _mm(a, b):
    return jnp.dot(a, b, preferred_element_type=F32)


def _rms(x, g):
    r = lax.rsqrt(jnp.mean(x * x, axis=-1, keepdims=True) + NORM_EPS)
    return x * r * g


def _rms_bwd(dy, x, g):
    r = lax.rsqrt(jnp.mean(x * x, axis=-1, keepdims=True) + NORM_EPS)
    xh = x * r
    dg = jnp.sum(dy * xh, axis=0, keepdims=True)
    dxh = dy * g
    dx = r * (dxh - xh * jnp.mean(dxh * xh, axis=-1, keepdims=True))
    return dx, dg


def _gelu(x):
    return 0.5 * x * (1.0 + jnp.tanh(GELU_C0 * (x + GELU_C1 * x * x * x)))


def _sigmoid(x):
    return 0.5 * jnp.tanh(0.5 * x) + 0.5


def _rows(tm, n):
    return pl.BlockSpec((tm, n), lambda i: (i, 0))


def _whole(shape):
    return pl.BlockSpec(shape, lambda i: (0,) * len(shape))


def _accumulate(ref, val):
    @pl.when(pl.program_id(0) == 0)
    def _():
        ref[...] = jnp.zeros_like(ref)

    ref[...] += val


def _params(n_axes=1):
    return pltpu.CompilerParams(dimension_semantics=("arbitrary",) * n_axes,
                                vmem_limit_bytes=VMEM_LIMIT_BYTES)


def _sds(shape, dtype):
    return jax.ShapeDtypeStruct(shape, dtype)


def _tile(t, want):
    return min(t, want)


def _f1_norm_in(h, g, w_t):
    t = h.shape[0]
    tm = _tile(t, 512)

    def body(h_ref, g_ref, w_ref, z_ref, a_ref):
        a = _rms(h_ref[...], g_ref[...]).astype(BF16)
        a_ref[...] = a
        z_ref[...] = _nt(a, w_ref[...]).astype(BF16)

    return pl.pallas_call(
        body, name="f1_norm_in", grid=(t // tm,),
        in_specs=[_rows(tm, D_MODEL), _whole((1, D_MODEL)), _whole((D_IN, D_MODEL))],
        out_specs=[_rows(tm, D_IN), _rows(tm, D_MODEL)],
        out_shape=[_sds((t, D_IN), BF16), _sds((t, D_MODEL), BF16)],
        compiler_params=_params())(h, g, w_t)


def _alibi_bias():
    ti = jnp.arange(BLK)[:, None]
    ji = jnp.arange(2 * BLK)[None, :]
    dist = ti + BLK - ji
    band = (dist >= 0) & (dist < BLK)
    bias = -jnp.asarray(ALIBI_SLOPES, F32)[:, None, None] * dist.astype(F32)[None]
    return jnp.stack([jnp.where((band & (ji >= BLK))[None], bias, NEG_BIG), jnp.where(band[None], bias, NEG_BIG)])


def _strips():
    return [slice(r * STRIP, (r + 1) * STRIP) for r in range(BLK // STRIP)]


def _bias_spec(transposed=False):
    tile = (2 * BLK, BLK) if transposed else (BLK, 2 * BLK)
    return pl.BlockSpec((None, N_Q_HEADS) + tile, lambda i: (jnp.minimum(i, 1), 0, 0, 0))


def _softmax_strip(s_ref, bias_ref, hq, rows, sink):
    sc = s_ref[hq, rows, :] + bias_ref[hq, rows, :]
    m = jnp.maximum(jnp.max(sc, axis=1, keepdims=True), sink)
    e = jnp.exp(sc - m)
    es = jnp.exp(sink - m)
    inv = 1.0 / (jnp.sum(e, axis=1, keepdims=True) + es)
    return e * inv, es * inv


def _kv_window(z_ref, kp_ref, vp_ref):
    kcat = jnp.concatenate([kp_ref[...], z_ref[:, ATTN_W:ATTN_W + KV_W]], axis=0).astype(F32)
    vcat = jnp.concatenate([vp_ref[...], z_ref[:, ATTN_W + KV_W:ATTN_W + 2 * KV_W]], axis=0).astype(F32)
    kswap = pltpu.roll(kcat, HEAD_DIM, 1)
    vswap = pltpu.roll(vcat, HEAD_DIM, 1)
    return kcat.astype(BF16), kswap.astype(BF16), vcat, vswap


def _gelu_and_grad(x):
    t = jnp.tanh(GELU_C0 * (x + GELU_C1 * x * x * x))
    return 0.5 * x * (1.0 + t), 0.5 * (1.0 + t) + 0.5 * x * (1.0 - t * t) * GELU_C0 * (1.0 + 3.0 * GELU_C1 * x * x)


def _layernorm_strip(gv, lng_ref, lnb_ref):
    xc = gv - jnp.mean(gv, axis=-1, keepdims=True)
    rstd = lax.rsqrt(jnp.mean(xc * xc, axis=-1, keepdims=True) + NORM_EPS)
    xhat = xc * rstd
    return xhat * lng_ref[...] + lnb_ref[...], xhat, rstd


def _tril_w(ws_ref, h):
    ti = lax.broadcasted_iota(jnp.int32, (BLK, BLK), 0)
    si = lax.broadcasted_iota(jnp.int32, (BLK, BLK), 1)
    causal = si <= ti
    return jnp.where(causal, ws_ref[h], 0.0).astype(BF16), causal


def _gm_mixed(vn, ws_ref, bs_ref, lo, hi):
    slabs = []
    for s in range(GM_W // LANES):
        vs = vn[:, s * LANES:(s + 1) * LANES]
        w0, _ = _tril_w(ws_ref, 2 * s)
        w1, _ = _tril_w(ws_ref, 2 * s + 1)
        mixed = (_mm(w0, jnp.where(lo, vs, 0.0).astype(BF16))
                 + _mm(w1, jnp.where(hi, vs, 0.0).astype(BF16))
                 + bs_ref[:, s * LANES:(s + 1) * LANES])
        slabs.append(mixed)
    return slabs


def _block_specs_z(nb):
    prev = lambda i: (jnp.maximum(i - 1, 0), ATTN_W // KV_W)
    prev_v = lambda i: (jnp.maximum(i - 1, 0), ATTN_W // KV_W + 1)
    return [_rows(BLK, D_IN), pl.BlockSpec((BLK, KV_W), prev), pl.BlockSpec((BLK, KV_W), prev_v)]


def _heads():
    return [(2 * s + half, s, half, s // 2 == half) for s in range(ATTN_W // LANES) for half in range(2)]


def _f2_attn_gm(z, sinks, bias, ln_g, ln_b, ws, bs_full):
    t = z.shape[0]
    nb = t // BLK

    def body(z_ref, kp_ref, vp_ref, sink_ref, bias_ref, lng_ref, lnb_ref, ws_ref, bs_ref, am_ref,
             s_ref, p_ref, u_ref, vn_ref):
        lane = lax.broadcasted_iota(jnp.int32, (1, LANES), 1)
        lo = lane < HEAD_DIM
        hi = lane >= HEAD_DIM
        kc, ks, vcat, vswap = _kv_window(z_ref, kp_ref, vp_ref)
        for hq, s, half, same in _heads():
            qs = z_ref[:, s * LANES:(s + 1) * LANES].astype(F32) * (HEAD_DIM ** -0.5)
            qm = jnp.where(lo if half == 0 else hi, qs, 0.0).astype(BF16)
            s_ref[hq] = _nt(qm, kc if same else ks)
        for hq in range(N_Q_HEADS):
            for rows in _strips():
                pr, _ = _softmax_strip(s_ref, bias_ref, hq, rows, sink_ref[hq])
                p_ref[hq, rows, :] = pr.astype(BF16)
        for s in range(ATTN_W // LANES):
            o = jnp.zeros((BLK, LANES), F32)
            for hq, hs, half, same in _heads():
                if hs == s:
                    vm = jnp.where(lo if half == 0 else hi, vcat if same else vswap, 0.0).astype(BF16)
                    o = o + _mm(p_ref[hq], vm)
            am_ref[:, s * LANES:(s + 1) * LANES] = o.astype(BF16)

        for rows in _strips():
            u_ref[rows, :] = _gelu(z_ref[rows, ATTN_W + 2 * KV_W:ATTN_W + 2 * KV_W + GM_W].astype(F32))
            vn_ref[rows, :], _, _ = _layernorm_strip(
                _gelu(z_ref[rows, ATTN_W + 2 * KV_W + GM_W:D_IN].astype(F32)), lng_ref, lnb_ref)
        mixed = _gm_mixed(vn_ref, ws_ref, bs_ref, lo, hi)
        for s in range(GM_W // LANES):
            am_ref[:, ATTN_W + s * LANES:ATTN_W + (s + 1) * LANES] = (
                u_ref[:, s * LANES:(s + 1) * LANES] * mixed[s]).astype(BF16)

    return pl.pallas_call(
        body, name="f2_attn_gm", grid=(nb,),
        in_specs=_block_specs_z(nb) + [
            pl.BlockSpec(memory_space=pltpu.SMEM), _bias_spec(), _whole((1, GM_W)),
            _whole((1, GM_W)), _whole((N_Q_HEADS, BLK, BLK)), _whole((BLK, GM_W))],
        out_specs=_rows(BLK, ATTN_W + GM_W),
        out_shape=_sds((t, ATTN_W + GM_W), BF16),
        scratch_shapes=[pltpu.VMEM((N_Q_HEADS, BLK, 2 * BLK), F32), pltpu.VMEM((N_Q_HEADS, BLK, 2 * BLK), BF16),
                        pltpu.VMEM((BLK, GM_W), F32), pltpu.VMEM((BLK, GM_W), F32)],
        compiler_params=_params())(z, z, z, sinks, bias, ln_g, ln_b, ws, bs_full)


def _f3_mix_out(am, h, ga, gg, w, gpost):
    t = h.shape[0]
    tm = _tile(t, 512)

    def body(am_ref, h_ref, ga_ref, gg_ref, w_ref, gp_ref, heads_ref, mix_ref, h1_ref):
        heads = jnp.concatenate([_rms(am_ref[:, :ATTN_W].astype(F32), ga_ref[...]),
                                 _rms(am_ref[:, ATTN_W:].astype(F32), gg_ref[...])], axis=1).astype(BF16)
        heads_ref[...] = heads
        mix = _mm(heads, w_ref[...])
        mix_ref[...] = mix.astype(BF16)
        h1_ref[...] = h_ref[...] + _rms(mix, gp_ref[...])

    return pl.pallas_call(
        body, name="f3_mix_out", grid=(t // tm,),
        in_specs=[_rows(tm, D_MODEL), _rows(tm, D_MODEL), _whole((1, ATTN_W)), _whole((1, GM_W)),
                  _whole((D_MODEL, D_MODEL)), _whole((1, D_MODEL))],
        out_specs=[_rows(tm, D_MODEL)] * 3,
        out_shape=[_sds((t, D_MODEL), BF16), _sds((t, D_MODEL), BF16), _sds((t, D_MODEL), F32)],
        compiler_params=_params())(am, h, ga, gg, w, gpost)


def _f4a_ffn_in(h1, gf, wg_t, wu_t):
    t = h1.shape[0]
    tm = _tile(t, 256)

    def body(h_ref, g_ref, wg_ref, wu_ref, f_ref, gt_ref, up_ref):
        f = _rms(h_ref[...], g_ref[...]).astype(BF16)
        f_ref[...] = f
        gt_ref[...] = _nt(f, wg_ref[...]).astype(BF16)
        up_ref[...] = _nt(f, wu_ref[...]).astype(BF16)

    return pl.pallas_call(
        body, name="f4a_ffn_in", grid=(t // tm,),
        in_specs=[_rows(tm, D_MODEL), _whole((1, D_MODEL)), _whole((D_FF, D_MODEL)), _whole((D_FF, D_MODEL))],
        out_specs=[_rows(tm, D_MODEL), _rows(tm, D_FF), _rows(tm, D_FF)],
        out_shape=[_sds((t, D_MODEL), BF16), _sds((t, D_FF), BF16), _sds((t, D_FF), BF16)],
        compiler_params=_params())(h1, gf, wg_t, wu_t)


def _f4b_ffn_out(gt, up, wd, h1, gfp):
    t = h1.shape[0]
    tm = _tile(t, 256)

    def body(gt_ref, up_ref, wd_ref, h_ref, g_ref, dn_ref, h2_ref):
        gt = gt_ref[...].astype(F32)
        act = (gt * _sigmoid(gt) * up_ref[...].astype(F32)).astype(BF16)
        dn = _mm(act, wd_ref[...])
        dn_ref[...] = dn.astype(BF16)
        h2_ref[...] = h_ref[...] + _rms(dn, g_ref[...])

    return pl.pallas_call(
        body, name="f4b_ffn_out", grid=(t // tm,),
        in_specs=[_rows(tm, D_FF), _rows(tm, D_FF), _whole((D_FF, D_MODEL)), _rows(tm, D_MODEL),
                  _whole((1, D_MODEL))],
        out_specs=[_rows(tm, D_MODEL)] * 2,
        out_shape=[_sds((t, D_MODEL), BF16), _sds((t, D_MODEL), F32)],
        compiler_params=_params())(gt, up, wd, h1, gfp)


def _f5_ple(h2, gpl, wpg, p, wple, target=None):
    t = h2.shape[0]
    tm = _tile(t, 512)
    with_loss = target is not None

    def body(h_ref, g_ref, wpg_ref, p_ref, wple_ref, *rest):
        r_ref, pg_ref, pe_ref, last_ref = rest[-5:-1] if with_loss else rest[-4:]
        h = h_ref[...]
        r = _rms(h, g_ref[...]).astype(BF16)
        r_ref[...] = r
        pg = _mm(r, wpg_ref[...])
        pe = _mm(p_ref[...].astype(BF16), wple_ref[...])
        pg_ref[...] = pg.astype(BF16)
        pe_ref[...] = pe.astype(BF16)
        h3 = h + pe * _sigmoid(pg)
        if with_loss:
            err = h3 - rest[0][...]
            last_ref[...] = err * (1.0 / D_MODEL)
            _accumulate(rest[-1], jnp.sum(err * err, keepdims=True))
        else:
            last_ref[...] = h3

    return pl.pallas_call(
        body, name="f5_ple_loss" if with_loss else "f5_ple", grid=(t // tm,),
        in_specs=[_rows(tm, D_MODEL), _whole((1, D_MODEL)), _whole((D_MODEL, D_MODEL)), _rows(tm, PLE_DIM),
                  _whole((PLE_DIM, D_MODEL))] + [_rows(tm, D_MODEL)] * with_loss,
        out_specs=[_rows(tm, D_MODEL)] * 4 + [_whole((1, LANES))] * with_loss,
        out_shape=[_sds((t, D_MODEL), BF16)] * 3 + [_sds((t, D_MODEL), F32)] + [_sds((1, LANES), F32)] * with_loss,
        compiler_params=_params())(h2, gpl, wpg, p, wple, *([target] if with_loss else []))


def _b5_ple(dh3, h2, pg, pe, gpl, wpg):
    t = h2.shape[0]
    tm = _tile(t, 512)

    def body(dh_ref, h_ref, pg_ref, pe_ref, g_ref, w_ref, dpe_ref, dpg_ref, dh2_ref, dg_ref):
        dh = dh_ref[...]
        s = _sigmoid(pg_ref[...].astype(F32))
        dpe_ref[...] = (dh * s).astype(BF16)
        dpg = (dh * pe_ref[...].astype(F32) * s * (1.0 - s)).astype(BF16)
        dpg_ref[...] = dpg
        dx, dg = _rms_bwd(_nt(dpg, w_ref[...]), h_ref[...], g_ref[...])
        dh2_ref[...] = dh + dx
        _accumulate(dg_ref, dg)

    return pl.pallas_call(
        body, name="b5_ple", grid=(t // tm,),
        in_specs=[_rows(tm, D_MODEL)] * 4 + [_whole((1, D_MODEL)), _whole((D_MODEL, D_MODEL))],
        out_specs=[_rows(tm, D_MODEL)] * 3 + [_whole((1, D_MODEL))],
        out_shape=[_sds((t, D_MODEL), BF16)] * 2 + [_sds((t, D_MODEL), F32), _sds((1, D_MODEL), F32)],
        compiler_params=_params())(dh3, h2, pg, pe, gpl, wpg)


def _b4a_ffn_out(dh2, dn, gfp, wd, gt, up):
    t = dh2.shape[0]
    tm = _tile(t, 256)

    def body(dh_ref, dn_ref, g_ref, wd_ref, gt_ref, up_ref, ddn_ref, act_ref, dgt_ref, dup_ref, dg_ref):
        ddn, dg = _rms_bwd(dh_ref[...], dn_ref[...].astype(F32), g_ref[...])
        _accumulate(dg_ref, dg)
        ddn = ddn.astype(BF16)
        ddn_ref[...] = ddn
        dact = _nt(ddn, wd_ref[...])
        gt = gt_ref[...].astype(F32)
        up = up_ref[...].astype(F32)
        sg = _sigmoid(gt)
        silu = gt * sg
        act_ref[...] = (silu * up).astype(BF16)
        dup_ref[...] = (dact * silu).astype(BF16)
        dgt_ref[...] = (dact * up * (sg * (1.0 + gt * (1.0 - sg)))).astype(BF16)

    return pl.pallas_call(
        body, name="b4a_ffn_out", grid=(t // tm,),
        in_specs=[_rows(tm, D_MODEL), _rows(tm, D_MODEL), _whole((1, D_MODEL)), _whole((D_FF, D_MODEL)),
                  _rows(tm, D_FF), _rows(tm, D_FF)],
        out_specs=[_rows(tm, D_MODEL), _rows(tm, D_FF), _rows(tm, D_FF), _rows(tm, D_FF), _whole((1, D_MODEL))],
        out_shape=[_sds((t, D_MODEL), BF16)] + [_sds((t, D_FF), BF16)] * 3 + [_sds((1, D_MODEL), F32)],
        compiler_params=_params())(dh2, dn, gfp, wd, gt, up)


def _b4b_ffn_in(dgt, dup, wg_t, wu_t, h1, gf, dh2):
    t = h1.shape[0]
    tm = _tile(t, 512)

    def body(dgt_ref, dup_ref, wg_ref, wu_ref, h_ref, g_ref, dh_ref, dh1_ref, dg_ref):
        df = _mm(dgt_ref[...], wg_ref[...]) + _mm(dup_ref[...], wu_ref[...])
        dx, dg = _rms_bwd(df, h_ref[...], g_ref[...])
        dh1_ref[...] = dh_ref[...] + dx
        _accumulate(dg_ref, dg)

    return pl.pallas_call(
        body, name="b4b_ffn_in", grid=(t // tm,),
        in_specs=[_rows(tm, D_FF), _rows(tm, D_FF), _whole((D_FF, D_MODEL)), _whole((D_FF, D_MODEL)),
                  _rows(tm, D_MODEL), _whole((1, D_MODEL)), _rows(tm, D_MODEL)],
        out_specs=[_rows(tm, D_MODEL), _whole((1, D_MODEL))],
        out_shape=[_sds((t, D_MODEL), F32), _sds((1, D_MODEL), F32)],
        compiler_params=_params())(dgt, dup, wg_t, wu_t, h1, gf, dh2)


def _b3_mix_out(dh1, mix, gpost, w, am, ga, gg):
    t = dh1.shape[0]
    tm = _tile(t, 512)

    def body(dh_ref, mix_ref, gp_ref, w_ref, am_ref, ga_ref, gg_ref, dmix_ref, dam_ref, dgp_ref, dga_ref, dgg_ref):
        dmix, dgp = _rms_bwd(dh_ref[...], mix_ref[...].astype(F32), gp_ref[...])
        _accumulate(dgp_ref, dgp)
        dmix = dmix.astype(BF16)
        dmix_ref[...] = dmix
        dheads = _nt(dmix, w_ref[...])
        dat, dga = _rms_bwd(dheads[:, :ATTN_W], am_ref[:, :ATTN_W].astype(F32), ga_ref[...])
        dgm, dgg = _rms_bwd(dheads[:, ATTN_W:], am_ref[:, ATTN_W:].astype(F32), gg_ref[...])
        dam_ref[:, :ATTN_W] = dat.astype(BF16)
        dam_ref[:, ATTN_W:] = dgm.astype(BF16)
        _accumulate(dga_ref, dga)
        _accumulate(dgg_ref, dgg)

    return pl.pallas_call(
        body, name="b3_mix_out", grid=(t // tm,),
        in_specs=[_rows(tm, D_MODEL), _rows(tm, D_MODEL), _whole((1, D_MODEL)), _whole((D_MODEL, D_MODEL)),
                  _rows(tm, D_MODEL), _whole((1, ATTN_W)), _whole((1, GM_W))],
        out_specs=[_rows(tm, D_MODEL), _rows(tm, D_MODEL), _whole((1, D_MODEL)), _whole((1, ATTN_W)),
                   _whole((1, GM_W))],
        out_shape=[_sds((t, D_MODEL), BF16), _sds((t, D_MODEL), BF16), _sds((1, D_MODEL), F32),
                   _sds((1, ATTN_W), F32), _sds((1, GM_W), F32)],
        compiler_params=_params())(dh1, mix, gpost, w, am, ga, gg)


def _b2_attn_gm(dam, z, sinks, bias, ln_g, ln_b, ws, bs_full):
    t = z.shape[0]
    nb = t // BLK

    def body(dam_ref, z_ref, kp_ref, vp_ref, sink_ref, bias_ref, lng_ref, lnb_ref, ws_ref, wst_ref, bs_ref,
             dzq_ref, dkv_ref, dzuv_ref, dsink_ref, dlng_ref, dlnb_ref, dws_ref, dbs_ref,
             sc_ref, dp_ref, p_ref, ds_ref, u_ref, du_ref, dv_ref, vn_ref, xhat_ref, rstd_ref, dvn_ref):
        n = pl.program_id(0)
        lane = lax.broadcasted_iota(jnp.int32, (1, LANES), 1)
        lo = lane < HEAD_DIM
        hi = lane >= HEAD_DIM
        sub = lax.broadcasted_iota(jnp.int32, (LANES, 1), 0)
        kc, ks, vcat, vswap = _kv_window(z_ref, kp_ref, vp_ref)
        vc = vcat.astype(BF16)
        vs_ = vswap.astype(BF16)
        kc_t = kc.T
        ks_t = ks.T

        def operands(s, half):
            mask = lo if half == 0 else hi
            qs = z_ref[:, s * LANES:(s + 1) * LANES].astype(F32) * (HEAD_DIM ** -0.5)
            qm = jnp.where(mask, qs, 0.0).astype(BF16)
            dom = jnp.where(mask, dam_ref[:, s * LANES:(s + 1) * LANES], 0.0).astype(BF16)
            return qm, dom

        for hq, s, half, same in _heads():
            qm, dom = operands(s, half)
            sc_ref[hq] = _nt(kc if same else ks, qm)
            dp_ref[hq] = _nt(vc if same else vs_, dom)
        dsink = jnp.zeros((1, LANES), F32)
        for hq in range(N_Q_HEADS):
            sink = sink_ref[hq]
            sc = sc_ref[hq] + bias_ref[hq]
            m = jnp.maximum(jnp.max(sc, axis=0, keepdims=True), sink)
            e = jnp.exp(sc - m)
            es = jnp.exp(sink - m)
            inv = 1.0 / (jnp.sum(e, axis=0, keepdims=True) + es)
            pr = e * inv
            dpr = dp_ref[hq]
            row = jnp.sum(dpr * pr, axis=0, keepdims=True)
            p_ref[hq] = pr.astype(BF16)
            ds_ref[hq] = (pr * (dpr - row)).astype(BF16)
            dsink = dsink + jnp.where(lane == hq, -jnp.sum(es * inv * row, keepdims=True), 0.0)
        dk_acc = [jnp.zeros((2 * BLK, LANES), F32), jnp.zeros((2 * BLK, LANES), F32)]
        dv_acc = [jnp.zeros((2 * BLK, LANES), F32), jnp.zeros((2 * BLK, LANES), F32)]
        for s in range(ATTN_W // LANES):
            dq_t = jnp.zeros((LANES, BLK), F32)
            for hq, hs, half, same in _heads():
                if hs != s:
                    continue
                qm, dom = operands(s, half)
                ds = ds_ref[hq]
                dk_acc[same] = dk_acc[same] + _mm(ds, qm)
                dv_acc[same] = dv_acc[same] + _mm(p_ref[hq], dom)
                in_half = (sub < HEAD_DIM) if half == 0 else (sub >= HEAD_DIM)
                dq_t = dq_t + jnp.where(in_half, _mm(kc_t if same else ks_t, ds), 0.0)
            dzq_ref[:, s * LANES:(s + 1) * LANES] = (dq_t.T * (HEAD_DIM ** -0.5)).astype(BF16)
        dk_acc = dk_acc[True] + pltpu.roll(dk_acc[False], HEAD_DIM, 1)
        dv_acc = dv_acc[True] + pltpu.roll(dv_acc[False], HEAD_DIM, 1)
        cur = pl.multiple_of(n * BLK, BLK)
        dkv_ref[pl.ds(cur, BLK), 0:KV_W] = dk_acc[BLK:, :]
        dkv_ref[pl.ds(cur, BLK), KV_W:2 * KV_W] = dv_acc[BLK:, :]

        @pl.when(n > 0)
        def _():
            prv = pl.multiple_of((n - 1) * BLK, BLK)
            dkv_ref[pl.ds(prv, BLK), 0:KV_W] += dk_acc[:BLK, :]
            dkv_ref[pl.ds(prv, BLK), KV_W:2 * KV_W] += dv_acc[:BLK, :]

        _accumulate(dsink_ref, dsink)

        for rows in _strips():
            u_ref[rows, :], du_ref[rows, :] = _gelu_and_grad(
                z_ref[rows, ATTN_W + 2 * KV_W:ATTN_W + 2 * KV_W + GM_W].astype(F32))
            gv, dv_ref[rows, :] = _gelu_and_grad(z_ref[rows, ATTN_W + 2 * KV_W + GM_W:D_IN].astype(F32))
            vn_ref[rows, :], xhat_ref[rows, :], rstd = _layernorm_strip(gv, lng_ref, lnb_ref)
            rstd_ref[rows, :] = jnp.broadcast_to(rstd, (STRIP, LANES))
        mixed = _gm_mixed(vn_ref, ws_ref, bs_ref, lo, hi)

        @pl.when(n == 0)
        def _():
            dws_ref[...] = jnp.zeros_like(dws_ref)

        dbs = jnp.zeros((BLK, LANES), F32)
        for s in range(GM_W // LANES):
            slab = slice(s * LANES, (s + 1) * LANES)
            dgm = dam_ref[:, ATTN_W + s * LANES:ATTN_W + (s + 1) * LANES]
            dzuv_ref[:, slab] = (dgm * mixed[s] * du_ref[:, slab]).astype(BF16)
            dmx = dgm * u_ref[:, slab]
            vsb = vn_ref[:, slab].astype(BF16)
            dvn = jnp.zeros((BLK, LANES), F32)
            for half in range(2):
                h = 2 * s + half
                mask = lo if half == 0 else hi
                dmm = jnp.where(mask, dmx, 0.0)
                dbs = dbs + jnp.where(lane == h, jnp.sum(dmm, axis=1, keepdims=True), 0.0)
                dmm = dmm.astype(BF16)
                _, causal = _tril_w(ws_ref, h)
                ti = lax.broadcasted_iota(jnp.int32, (BLK, BLK), 0)
                si = lax.broadcasted_iota(jnp.int32, (BLK, BLK), 1)
                wt_t = jnp.where(ti <= si, wst_ref[h], 0.0).astype(BF16)
                dvn = dvn + jnp.where(mask, _mm(wt_t, dmm), 0.0)
                dws_ref[h] += jnp.where(causal, _nt(dmm, vsb), 0.0)
            dvn_ref[:, slab] = dvn
        _accumulate(dbs_ref, dbs)
        dlnb = jnp.zeros((STRIP, GM_W), F32)
        dlng = jnp.zeros((STRIP, GM_W), F32)
        for rows in _strips():
            dvn = dvn_ref[rows, :]
            xhat = xhat_ref[rows, :]
            dlnb = dlnb + dvn
            dlng = dlng + dvn * xhat
            dxh = dvn * lng_ref[...]
            dgv = rstd_ref[rows, 0:1] * (dxh - jnp.mean(dxh, axis=-1, keepdims=True)
                                         - xhat * jnp.mean(dxh * xhat, axis=-1, keepdims=True))
            dzuv_ref[rows, GM_W:] = (dgv * dv_ref[rows, :]).astype(BF16)
        _accumulate(dlnb_ref, jnp.sum(dlnb, axis=0, keepdims=True))
        _accumulate(dlng_ref, jnp.sum(dlng, axis=0, keepdims=True))

    return pl.pallas_call(
        body, name="b2_attn_gm", grid=(nb,),
        in_specs=[_rows(BLK, ATTN_W + GM_W)] + _block_specs_z(nb) + [
            pl.BlockSpec(memory_space=pltpu.SMEM), _bias_spec(transposed=True), _whole((1, GM_W)),
            _whole((1, GM_W)), _whole((N_Q_HEADS, BLK, BLK)), _whole((N_Q_HEADS, BLK, BLK)), _whole((BLK, GM_W))],
        out_specs=[_rows(BLK, ATTN_W), _whole((t, 2 * KV_W)), _rows(BLK, 2 * GM_W), _whole((1, LANES)),
                   _whole((1, GM_W)), _whole((1, GM_W)), _whole((N_Q_HEADS, BLK, BLK)), _whole((BLK, LANES))],
        out_shape=[_sds((t, ATTN_W), BF16), _sds((t, 2 * KV_W), F32), _sds((t, 2 * GM_W), BF16),
                   _sds((1, LANES), F32), _sds((1, GM_W), F32), _sds((1, GM_W), F32),
                   _sds((N_Q_HEADS, BLK, BLK), F32), _sds((BLK, LANES), F32)],
        scratch_shapes=[pltpu.VMEM((N_Q_HEADS, 2 * BLK, BLK), F32)] * 2
        + [pltpu.VMEM((N_Q_HEADS, 2 * BLK, BLK), BF16)] * 2 + [pltpu.VMEM((BLK, GM_W), F32)] * 5
        + [pltpu.VMEM((BLK, LANES), F32), pltpu.VMEM((BLK, GM_W), F32)],
        compiler_params=_params())(dam, z, z, z, sinks, jnp.swapaxes(bias, 2, 3), ln_g, ln_b, ws,
                                   jnp.swapaxes(ws, 1, 2), bs_full)


def _b1_norm_in(dzq, dkv, dzuv, w_t, h, g, dh1):
    t = h.shape[0]
    tm = _tile(t, 512)

    def body(dzq_ref, dkv_ref, dzuv_ref, w_ref, h_ref, g_ref, dh_ref, dh0_ref, dg_ref):
        dz = jnp.concatenate([dzq_ref[...], dkv_ref[...].astype(BF16), dzuv_ref[...]], axis=1)
        dx, dg = _rms_bwd(_mm(dz, w_ref[...]), h_ref[...], g_ref[...])
        dh0_ref[...] = dh_ref[...] + dx
        _accumulate(dg_ref, dg)

    return pl.pallas_call(
        body, name="b1_norm_in", grid=(t // tm,),
        in_specs=[_rows(tm, ATTN_W), _rows(tm, 2 * KV_W), _rows(tm, 2 * GM_W), _whole((D_IN, D_MODEL)),
                  _rows(tm, D_MODEL), _whole((1, D_MODEL)), _rows(tm, D_MODEL)],
        out_specs=[_rows(tm, D_MODEL), _whole((1, D_MODEL))],
        out_shape=[_sds((t, D_MODEL), F32), _sds((1, D_MODEL), F32)],
        compiler_params=_params())(dzq, dkv, dzuv, w_t, h, g, dh1)


def _weight_grad(x, dy, name):
    xs = tuple(x) if isinstance(x, (tuple, list)) else (x,)
    t = xs[0].shape[0]
    k = sum(a.shape[1] for a in xs)
    n = dy.shape[1]
    tm = _tile(t, 1024)
    steps = t // tm

    def body(*refs):
        x_refs, dy_ref, dw_ref, acc_ref = refs[:len(xs)], refs[-3], refs[-2], refs[-1]
        i = pl.program_id(0)

        def product():
            cols = [r[...].astype(BF16) for r in x_refs]
            return _tn(cols[0] if len(cols) == 1 else jnp.concatenate(cols, axis=1), dy_ref[...])

        @pl.when(i == 0)
        def _():
            acc_ref[...] = product()

        @pl.when(i > 0)
        def _():
            acc_ref[...] += product()

        @pl.when(i == steps - 1)
        def _():
            dw_ref[...] = acc_ref[...].astype(BF16)

    return pl.pallas_call(
        body, name=name, grid=(steps,),
        in_specs=[_rows(tm, a.shape[1]) for a in xs] + [_rows(tm, n)],
        out_specs=_whole((k, n)),
        out_shape=_sds((k, n), BF16),
        scratch_shapes=[pltpu.VMEM((k, n), F32)],
        compiler_params=_params())(*xs, dy)


_ANY = pl.BlockSpec(memory_space=pl.ANY)
_HBM = pl.BlockSpec(memory_space=pltpu.HBM)
_SEM = pl.BlockSpec(memory_space=pltpu.SEMAPHORE)
_EFFECT = pltpu.SideEffectType.DATAFLOW_SIDE_EFFECTING


def _mesh_pos():
    return lax.axis_index("x"), lax.axis_index("y"), lax.axis_index("c")


def _other_chips(x, y):
    return [(1 - x, y), (x, 1 - y), (1 - x, 1 - y)]


def _peers(mode, x, y, c):
    if mode == "devices":
        peers = []
        for j in range(1, N_DEV):
            px = 1 - x if (j >> 2) & 1 else x
            py = 1 - y if (j >> 1) & 1 else y
            pc = 1 - c if j & 1 else c
            peers.append(((px, py, pc), 4 * px + 2 * py + pc))
        return peers, 4 * x + 2 * y + c
    if mode == "sibling":
        return [((x, y, 1 - c), None)], None
    return [((px, py, c), 2 * px + py) for px, py in _other_chips(x, y)], 2 * x + y


def _descriptors(mode, srcs, lands, send_sems, recv_sems, with_incoming=True):
    x, y, c = _mesh_pos()
    peers, me = _peers(mode, x, y, c)
    scatter = mode == "scatter"
    outgoing, incoming = [], []
    for i, (src, land) in enumerate(zip(srcs, lands)):
        for j, (dev, slot) in enumerate(peers):
            sem = i * len(peers) + j
            common = dict(send_sem=send_sems.at[sem], recv_sem=recv_sems.at[sem], device_id=dev,
                          device_id_type=pl.DeviceIdType.MESH)
            whole = mode == "sibling"
            outgoing.append(pltpu.make_async_remote_copy(
                src_ref=src.at[slot] if scatter else src, dst_ref=land if whole else land.at[me], **common))
            if with_incoming:
                incoming.append(pltpu.make_async_remote_copy(
                    src_ref=src.at[me] if scatter else src, dst_ref=land if whole else land.at[slot], **common))
    return outgoing, incoming


def _n_sems(mode, k):
    return k * {"devices": N_DEV - 1, "sibling": 1}.get(mode, N_CHIPS - 1)


def _exchange_start(srcs, lands, mode, name, after=None):
    k = len(srcs)
    arrs = [*srcs, *lands]

    def body(*refs):
        skip = 1 if after is not None else 0
        send_sems, recv_sems = refs[2 * k + skip], refs[2 * k + skip + 1]
        outgoing, _ = _descriptors(mode, refs[:k], refs[k:2 * k], send_sems, recv_sems, with_incoming=False)
        for cp in outgoing:
            cp.start()
        refs[-1][...] = jnp.zeros_like(refs[-1])

    operands = [pltpu.with_memory_space_constraint(a, pltpu.HBM) for a in arrs]
    in_specs = [_HBM] * (2 * k)
    if after is not None:
        operands.append(after)
        in_specs.append(_ANY)
    sems = pltpu.SemaphoreType.DMA((_n_sems(mode, k),))
    res = pl.pallas_call(
        body, name=name, in_specs=in_specs,
        out_shape=(sems, sems, *[pltpu.HBM(a.shape, a.dtype) for a in arrs], _sds((8, LANES), F32)),
        out_specs=(_SEM, _SEM, *[_HBM] * (2 * k), pl.BlockSpec(memory_space=pltpu.VMEM)),
        input_output_aliases={i: 2 + i for i in range(2 * k)},
        compiler_params=pltpu.CompilerParams(has_side_effects=_EFFECT))(*operands)
    return dict(mode=mode, send=res[0], recv=res[1], srcs=res[2:2 + k], lands=res[2 + k:2 + 2 * k], token=res[-1])


def _exchange_wait(flight, name, after):
    mode, k = flight["mode"], len(flight["srcs"])
    arrs = [*flight["srcs"], *flight["lands"]]

    def body(*refs):
        outgoing, incoming = _descriptors(mode, refs[:k], refs[k:2 * k], refs[2 * k], refs[2 * k + 1])
        for cp in outgoing:
            cp.wait_send()
        for cp in incoming:
            cp.wait_recv()

    res = pl.pallas_call(
        body, name=name, in_specs=[_HBM] * (2 * k) + [_SEM, _SEM, _ANY],
        out_shape=tuple(pltpu.HBM(a.shape, a.dtype) for a in arrs), out_specs=tuple([_HBM] * (2 * k)),
        input_output_aliases={i: i for i in range(2 * k)},
        compiler_params=pltpu.CompilerParams(has_side_effects=_EFFECT))(*arrs, flight["send"], flight["recv"], after)
    return res[:k], res[k:]


def _adamw(w, g, m, v):
    m = ADAM_B1 * m + (1.0 - ADAM_B1) * g
    v = ADAM_B2 * v + (1.0 - ADAM_B2) * (g * g)
    m_hat = m / (1.0 - ADAM_B1 ** ADAM_STEP)
    v_hat = v / (1.0 - ADAM_B2 ** ADAM_STEP)
    delta = -ADAM_LR * (m_hat / (jnp.sqrt(v_hat) + ADAM_EPS) + ADAM_WD * w)
    return delta, m, v


def _row_tile(rows, cols, n_arrays):
    budget = VMEM_LIMIT_BYTES // 2
    padded = -(-cols // LANES) * LANES
    cap = min(rows, max(16, budget // (2 * n_arrays * padded * 4)))
    return max(tr for tr in range(16, cap + 1, 16) if rows % tr == 0)


def _sum_chips(landing, name):
    _, r, c = landing.shape
    tr = _row_tile(r, c, 4)

    def body(l_ref, s_ref):
        acc = l_ref[0].astype(F32)
        for s in range(1, N_CHIPS):
            acc = acc + l_ref[s].astype(F32)
        s_ref[...] = acc

    return pl.pallas_call(
        body, name=name, grid=(r // tr,),
        in_specs=[pl.BlockSpec((N_CHIPS, tr, c), lambda i: (0, i, 0))],
        out_specs=_rows(tr, c), out_shape=_sds((r, c), F32),
        compiler_params=_params())(landing)


def _adamw_big(mine, sibling, w, m, v, prev, layer, name):
    _, r, c = w.shape
    tr = _row_tile(r, c, 9)
    stacked = pl.BlockSpec((None, tr, c), lambda i: (layer, i, 0))

    def body(a_ref, b_ref, w_ref, m_ref, v_ref, *rest):
        g_out, d_out, m_out, v_out = rest[-4:]
        g = a_ref[...] + b_ref[...]
        g_out[...] = g
        d_out[...], m_out[...], v_out[...] = _adamw(w_ref[...], g, m_ref[...], v_ref[...])

    prev = list(prev) if prev is not None else []
    return pl.pallas_call(
        body, name=name, grid=(r // tr,),
        in_specs=[_rows(tr, c)] * 2 + [stacked] * 3 + [_ANY] * len(prev),
        out_specs=[stacked] * 4, out_shape=[_sds(w.shape, F32)] * 4,
        input_output_aliases={5 + j: j for j in range(len(prev))},
        compiler_params=_params())(mine, sibling, w, m, v, *prev)


def _adamw_small(gathered, w, m, v):
    r = w.shape[0]

    def body(a_ref, w_ref, m_ref, v_ref, g_out, d_out, m_out, v_out):
        g = a_ref[0]
        for d in range(1, N_DEV):
            g = g + a_ref[d]
        g_out[...] = g
        d_out[...], m_out[...], v_out[...] = _adamw(w_ref[...], g, m_ref[...], v_ref[...])

    return pl.pallas_call(
        body, name="adamw_small", grid=(1,),
        in_specs=[_whole((N_DEV, r, LANES))] + [_whole((r, LANES))] * 3,
        out_specs=[_whole((r, LANES))] * 4, out_shape=[_sds((r, LANES), F32)] * 4,
        compiler_params=_params())(gathered, w, m, v)


def _shard_view(name, stacked):
    return jnp.swapaxes(stacked, 1, 2) if name in TRANSPOSED else stacked


def _row_split(name):
    return name in TRANSPOSED or BIG[name] == 1


def _assemble(name, landed):
    _, r, c = landed.shape
    if _row_split(name):
        return landed.reshape(N_CHIPS * r, c)
    return landed.transpose(1, 0, 2).reshape(r, N_CHIPS * c)


def _split(name, whole):
    r, c = whole.shape
    if _row_split(name):
        return whole.reshape(N_CHIPS, r // N_CHIPS, c)
    return whole.reshape(r, N_CHIPS, c // N_CHIPS).transpose(1, 0, 2)


def _with_own_slot(landed, own, slot):
    return lax.dynamic_update_index_in_dim(landed, own, slot, 0)


def _pack_small(params, names):
    pieces = []
    for name in names:
        flat = params[name].reshape(-1)
        pieces.append(jnp.pad(flat, (0, -flat.shape[0] % SMALL_PAD)))
    return jnp.concatenate(pieces).reshape(-1, LANES)


def _unpack_small(packed, like, names):
    flat = packed.reshape(-1)
    out, off = {}, 0
    for name in names:
        size = like[name].size
        out[name] = flat[off:off + size].reshape(like[name].shape)
        off += size + (-size % SMALL_PAD)
    return out


def kernel(x, p, ln_mix_pre, w_in, attn_sinks, gm_ln_g, gm_ln_b, gm_ws, gm_bs, g_attn_out, g_gm_out, w_out, ln_mix_post, ln_ffn_pre, w_ffn_gate, w_ffn_up, w_ffn_down, ln_ffn_post, w_ple, ln_ple_gate, w_ple_gate, loss_target, m_ln_mix_pre, m_w_in, m_attn_sinks, m_gm_ln_g, m_gm_ln_b, m_gm_ws, m_gm_bs, m_g_attn_out, m_g_gm_out, m_w_out, m_ln_mix_post, m_ln_ffn_pre, m_w_ffn_gate, m_w_ffn_up, m_w_ffn_down, m_ln_ffn_post, m_w_ple, m_ln_ple_gate, m_w_ple_gate, v_ln_mix_pre, v_w_in, v_attn_sinks, v_gm_ln_g, v_gm_ln_b, v_gm_ws, v_gm_bs, v_g_attn_out, v_g_gm_out, v_w_out, v_ln_mix_post, v_ln_ffn_pre, v_w_ffn_gate, v_w_ffn_up, v_w_ffn_down, v_ln_ffn_post, v_w_ple, v_ln_ple_gate, v_w_ple_gate):
    given = dict(locals())
    wts = {n: given[n] for n in WEIGHTS}
    mom = {n: given["m_" + n] for n in WEIGHTS}
    var = {n: given["v_" + n] for n in WEIGHTS}
    depth = w_in.shape[0]
    h = x[0]
    target = loss_target[0]
    chip = 2 * lax.axis_index("x") + lax.axis_index("y")
    device = 2 * chip + lax.axis_index("c")
    row = lambda a, i: a[i][None, :]
    bs_full = [jnp.repeat(gm_bs[i].T, HEAD_DIM, axis=1) for i in range(depth)]
    bias = _alibi_bias()
    kinds = ("grad", "delta", "m", "v")
    wview = {n: _shard_view(n, wts[n]) for n in BIG_NAMES}
    mview = {n: _shard_view(n, mom[n]) for n in BIG_NAMES}
    vview = {n: _shard_view(n, var[n]) for n in BIG_NAMES}

    zero = lambda flight: flight["token"][0:1, 0:1]

    def start_gather(i, names, after, tag):
        shards = [wview[n][i].astype(BF16) for n in names]
        lands = [lax.empty((N_CHIPS,) + s.shape, BF16) for s in shards]
        return _exchange_start(shards, lands, "gather", f"gather_weights_start_{i}{tag}", after)

    def finish_gather(flight, names, i, after, tag):
        shards, lands = _exchange_wait(flight, f"gather_weights_wait_{i}{tag}", after)
        return {n: _assemble(n, _with_own_slot(l, s, chip)) for n, s, l in zip(names, shards, lands)}

    def start_groups(i, after):
        flights = []
        for k, names in enumerate(GATHER_GROUPS):
            flights.append(start_gather(i, names, after, "abc"[k]))
            after = flights[-1]["token"]
        return flights

    grouped = {0: start_groups(0, None)}
    full = [finish_gather(grouped[0][0], GATHER_GROUPS[0], 0, grouped[0][-1]["token"], "a")] + [None] * (depth - 1)
    saved = []
    for i in range(depth):
        w = full[i]
        g_in = row(ln_mix_pre, i)
        if i + 1 in GROUPED_LAYERS:
            grouped[i + 1] = start_groups(i + 1, w['w_in'])
            g_in = g_in + zero(grouped[i + 1][-1])
        elif i + 1 < depth:
            flight = start_gather(i + 1, BIG_NAMES, w['w_in'], "")
            g_in = g_in + zero(flight)
        z, a = _f1_norm_in(h, g_in, w['w_in'])
        am = _f2_attn_gm(z, attn_sinks[i], bias, row(gm_ln_g, i), row(gm_ln_b, i), gm_ws[i], bs_full[i])
        if i in grouped:
            w.update(finish_gather(grouped[i][1], GATHER_GROUPS[1], i, am, "b"))
        heads, mix, h1 = _f3_mix_out(am, h, row(g_attn_out, i), row(g_gm_out, i), w['w_out'], row(ln_mix_post, i))
        f, gt, up = _f4a_ffn_in(h1, row(ln_ffn_pre, i), w['w_ffn_gate'], w['w_ffn_up'])
        if i in grouped:
            w.update(finish_gather(grouped[i][2], GATHER_GROUPS[2], i, gt, "c"))
        dn, h2 = _f4b_ffn_out(gt, up, w['w_ffn_down'], h1, row(ln_ffn_post, i))
        ple = _f5_ple(h2, row(ln_ple_gate, i), w['w_ple_gate'], p[i, 0], w['w_ple'],
                      target if i + 1 == depth else None)
        r, pg, pe = ple[:3]
        saved.append(dict(h=h, z=z, a=a, am=am, heads=heads, mix=mix, h1=h1, f=f, gt=gt, up=up, dn=dn, h2=h2,
                          r=r, pg=pg, pe=pe))
        if i + 1 < depth:
            h = ple[3]
            if i + 1 in grouped:
                full[i + 1] = finish_gather(grouped[i + 1][0], GATHER_GROUPS[0], i + 1, h, "a")
            else:
                full[i + 1] = finish_gather(flight, BIG_NAMES, i + 1, h, "")

    dh, sq = ple[3], ple[4]
    loss = lax.psum(0.5 / D_MODEL * sq[0, 0], ("x", "y", "c"))

    chain = {n: None for n in BIG_NAMES}
    owed = []
    small_out = [{k: {} for k in kinds} for _ in range(depth)]

    def start_scatter(i, names, dws, after, tag):
        parts = [_split(n, dws[n]) for n in names]
        lands = [lax.empty(q.shape, BF16) for q in parts]
        return _exchange_start(parts, lands, "scatter", f"scatter_grads_start_{i}{tag}", after)

    def finish_scatter(i, names, flight, after, tag):
        parts, lands = _exchange_wait(flight, f"scatter_grads_wait_{i}{tag}", after)
        lands = [_with_own_slot(l, lax.dynamic_index_in_dim(q, chip, 0, keepdims=False), chip)
                 for q, l in zip(parts, lands)]
        partial = [_sum_chips(l, "sum_chips_" + n) for n, l in zip(names, lands)]
        flight = _exchange_start(partial, [lax.empty(q.shape, F32) for q in partial], "sibling",
                                 f"sibling_grads_start_{i}{tag}", after)

        def update(later):
            mine, theirs = _exchange_wait(flight, f"sibling_grads_wait_{i}{tag}", later)
            for n, a, b in zip(names, mine, theirs):
                chain[n] = _adamw_big(a, b, wview[n], mview[n], vview[n], chain[n], i, "adamw_" + n)

        owed.append(update)
        return flight["token"]

    def settle(later):
        while owed:
            owed.pop(0)(later)

    def start_small(i, names, small, after, tag):
        packed = _pack_small(small, names)
        land = lax.empty((N_DEV,) + packed.shape, F32)
        return _exchange_start([packed], [land], "devices", f"gather_small_grads_start_{i}{tag}", after)

    def finish_small(i, names, flight, after, tag):
        (packed,), (gathered,) = _exchange_wait(flight, f"gather_small_grads_wait_{i}{tag}", after)
        gathered = _with_own_slot(gathered, packed, device)
        layer = lambda d: _pack_small({n: d[n][i] for n in names}, names)
        res = _adamw_small(gathered, layer(wts), layer(mom), layer(var))
        for k, a in zip(kinds, res):
            small_out[i][k].update(_unpack_small(a, {n: wts[n][i] for n in names}, names))
        return gathered

    pending, done, behind = [], None, None
    for i in reversed(range(depth)):
        s, w = saved[i], full[i]
        last = i == 0
        dws, small = {}, {}
        gain = row(ln_ple_gate, i)
        if behind is not None:
            gain = gain + behind
        dpe, dpg, dh2, dg = _b5_ple(dh, s['h2'], s['pg'], s['pe'], gain, w['w_ple_gate'])
        small['ln_ple_gate'] = dg[0]
        dws['w_ple'] = _weight_grad(p[i, 0], dpe, "dw_ple")
        dws['w_ple_gate'] = _weight_grad(s['r'], dpg, "dw_ple_gate")

        ddn, act, dgt, dup, dg = _b4a_ffn_out(dh2, s['dn'], row(ln_ffn_post, i), w['w_ffn_down'], s['gt'], s['up'])
        small['ln_ffn_post'] = dg[0]
        dws['w_ffn_down'] = _weight_grad(act, ddn, "dw_ffn_down")
        dws['w_ffn_gate'] = _weight_grad(dgt, s['f'], "dw_ffn_gate")
        dws['w_ffn_up'] = _weight_grad(dup, s['f'], "dw_ffn_up")
        gain = row(ln_ffn_pre, i)
        if last:
            flight_a = start_scatter(i, SCATTER_GROUPS[0], dws, None, "a")
            gain = gain + zero(flight_a)
        dh1, dg = _b4b_ffn_in(dgt, dup, w['w_ffn_gate'], w['w_ffn_up'], s['h1'], gain, dh2)
        small['ln_ffn_pre'] = dg[0]

        dmix, dam, dgp, dga, dgg = _b3_mix_out(dh1, s['mix'], row(ln_mix_post, i), w['w_out'], s['am'],
                                               row(g_attn_out, i), row(g_gm_out, i))
        small['ln_mix_post'] = dgp[0]
        small['g_attn_out'] = dga[0]
        small['g_gm_out'] = dgg[0]
        dws['w_out'] = _weight_grad(s['heads'], dmix, "dw_out")
        gain = row(gm_ln_g, i)
        if last:
            flight_b = start_scatter(i, SCATTER_GROUPS[1], dws, flight_a["token"], "b")
            gain = gain + zero(flight_b)

        dzq, dkv, dzuv, dsink, dlng, dlnb, dgws, dbs = _b2_attn_gm(
            dam, s['z'], attn_sinks[i], bias, gain, row(gm_ln_b, i), gm_ws[i], bs_full[i])
        small['attn_sinks'] = dsink[0, :N_Q_HEADS]
        small['gm_ln_g'] = dlng[0]
        small['gm_ln_b'] = dlnb[0]
        small['gm_ws'] = dgws
        small['gm_bs'] = dbs[:, :N_Q_HEADS].T
        gain = row(ln_mix_pre, i)
        if last:
            flight_s = start_small(i, SMALL_EARLY, small, flight_b["token"], "a")
            gain = gain + zero(flight_s)

        dh, dg = _b1_norm_in(dzq, dkv, dzuv, w['w_in'], s['h'], gain, dh1)
        small['ln_mix_pre'] = dg[0]
        settle(dh)
        for finish in pending:
            done = finish(dh)
        pending = []
        if last:
            done = finish_scatter(i, SCATTER_GROUPS[0], flight_a, dh, "a")
            done = finish_scatter(i, SCATTER_GROUPS[1], flight_b, done, "b")
        dws['w_in'] = _weight_grad((dzq, dkv, dzuv), s['a'], "dw_in")
        if last:
            flight_c = start_scatter(i, SCATTER_GROUPS[2], dws, done, "c")
            flight_t = start_small(i, SMALL_LATE, small, flight_c["token"], "b")
            finish_small(i, SMALL_EARLY, flight_s, flight_t["token"], "a")
            done = finish_scatter(i, SCATTER_GROUPS[2], flight_c, flight_t["token"], "c")
            finish_small(i, SMALL_LATE, flight_t, flight_t["token"], "b")
            settle(done)
        else:
            flight_c = start_scatter(i, BIG_NAMES, dws, done, "")
            flight_s = start_small(i, SMALL_NAMES, small, flight_c["token"], "")
            behind = zero(flight_s)
            pending = [functools.partial(finish_small, i, SMALL_NAMES, flight_s, tag=""),
                       functools.partial(finish_scatter, i, BIG_NAMES, flight_c, tag="")]
    grad_x = dh[None]

    out = {k: {n: _shard_view(n, chain[n][j]) for n in BIG_NAMES} for j, k in enumerate(kinds)}
    for k in kinds:
        out[k].update({n: jnp.stack([small_out[i][k][n] for i in range(depth)]) for n in SMALL_NAMES})

    return (loss, grad_x, *[out["grad"][n] for n in WEIGHTS], *[out["delta"][n] for n in WEIGHTS],
            *[out["m"][n] for n in WEIGHTS], *[out["v"][n] for n in WEIGHTS])
```

```python
import functools
import math

import jax
import jax.numpy as jnp
from jax import lax
from jax.experimental import pallas as pl
from jax.experimental.pallas import tpu as pltpu

F32 = jnp.float32
BF16 = jnp.bfloat16

D_MODEL = 1024
HEAD_DIM = 64
N_Q_HEADS = 8
BLK = 128
ATTN_W = 512
KV_W = 128
GM_W = 512
D_IN = ATTN_W + 2 * KV_W + 2 * GM_W
D_FF = 2816
PLE_DIM = 256
DEPTH = 4
NORM_EPS = 1e-6
NEG_BIG = -1e30
N_CHIPS = 4
N_DEV = 8

ADAM_LR = 0.001
ADAM_B1 = 0.9
ADAM_B2 = 0.999
ADAM_EPS = 1e-08
ADAM_WD = 0.01
ADAM_STEP = 10

VMEM_LIMIT_BYTES = 56 * 1024 * 1024
LANES = 128
STRIP = 16
FF_CHUNK = 256
GELU_C0 = math.sqrt(2.0 / math.pi)
GELU_C1 = 0.044715
ALIBI_SLOPES = tuple(2.0 ** (-8.0 * (h + 1.0) / N_Q_HEADS) for h in range(N_Q_HEADS))

WEIGHTS = ['ln_mix_pre', 'w_in', 'attn_sinks', 'gm_ln_g', 'gm_ln_b', 'gm_ws', 'gm_bs', 'g_attn_out',
           'g_gm_out', 'w_out', 'ln_mix_post', 'ln_ffn_pre', 'w_ffn_gate', 'w_ffn_up', 'w_ffn_down',
           'ln_ffn_post', 'w_ple', 'ln_ple_gate', 'w_ple_gate']
BIG = {'w_in': 2, 'w_out': 1, 'w_ffn_gate': 2, 'w_ffn_up': 2, 'w_ffn_down': 1, 'w_ple': 2, 'w_ple_gate': 1}
BIG_NAMES = list(BIG)
TRANSPOSED = ('w_in', 'w_ffn_gate', 'w_ffn_up')
SMALL_NAMES = [n for n in WEIGHTS if n not in BIG]
SMALL_PAD = 1024
GATHER_GROUPS = (('w_in',), ('w_out', 'w_ffn_gate', 'w_ffn_up'), ('w_ffn_down', 'w_ple_gate', 'w_ple'))
GROUPED_LAYERS = (0, 1)
SCATTER_GROUPS = (('w_ple', 'w_ple_gate', 'w_ffn_down', 'w_ffn_gate', 'w_ffn_up'), ('w_out',), ('w_in',))
SMALL_LATE = ('ln_mix_pre',)
SMALL_EARLY = tuple(n for n in SMALL_NAMES if n not in SMALL_LATE)


def _nt(a, b):
    return lax.dot_general(a, b, (((1,), (1,)), ((), ())), preferred_element_type=F32)


def _tn(a, b):
    return lax.dot_general(a, b, (((0,), (0,)), ((), ())), preferred_element_type=F32)


def _mm(a, b):
    return jnp.dot(a, b, preferred_element_type=F32)


def _rms(x, g):
    r = lax.rsqrt(jnp.mean(x * x, axis=-1, keepdims=True) + NORM_EPS)
    return x * r * g


def _rms_bwd(dy, x, g):
    r = lax.rsqrt(jnp.mean(x * x, axis=-1, keepdims=True) + NORM_EPS)
    xh = x * r
    dg = jnp.sum(dy * xh, axis=0, keepdims=True)
    dxh = dy * g
    dx = r * (dxh - xh * jnp.mean(dxh * xh, axis=-1, keepdims=True))
    return dx, dg


def _gelu(x):
    return 0.5 * x * (1.0 + jnp.tanh(GELU_C0 * (x + GELU_C1 * x * x * x)))


def _sigmoid(x):
    return 0.5 * jnp.tanh(0.5 * x) + 0.5


def _rows(tm, n):
    return pl.BlockSpec((tm, n), lambda i: (i, 0))


def _whole(shape):
    return pl.BlockSpec(shape, lambda i: (0,) * len(shape))


def _accumulate(ref, val):
    @pl.when(pl.program_id(0) == 0)
    def _():
        ref[...] = jnp.zeros_like(ref)

    ref[...] += val


def _params(n_axes=1):
    return pltpu.CompilerParams(dimension_semantics=("arbitrary",) * n_axes,
                                vmem_limit_bytes=VMEM_LIMIT_BYTES)


def _sds(shape, dtype):
    return jax.ShapeDtypeStruct(shape, dtype)


def _tile(t, want):
    return min(t, want)


def _f1_norm_in(h, g, w_t):
    t = h.shape[0]
    tm = _tile(t, 512)

    def body(h_ref, g_ref, w_ref, z_ref, a_ref):
        a = _rms(h_ref[...], g_ref[...]).astype(BF16)
        a_ref[...] = a
        z_ref[...] = _nt(a, w_ref[...]).astype(BF16)

    return pl.pallas_call(
        body, name="f1_norm_in", grid=(t // tm,),
        in_specs=[_rows(tm, D_MODEL), _whole((1, D_MODEL)), _whole((D_IN, D_MODEL))],
        out_specs=[_rows(tm, D_IN), _rows(tm, D_MODEL)],
        out_shape=[_sds((t, D_IN), BF16), _sds((t, D_MODEL), BF16)],
        compiler_params=_params())(h, g, w_t)


def _alibi_bias():
    ti = jnp.arange(BLK)[:, None]
    ji = jnp.arange(2 * BLK)[None, :]
    dist = ti + BLK - ji
    band = (dist >= 0) & (dist < BLK)
    bias = -jnp.asarray(ALIBI_SLOPES, F32)[:, None, None] * dist.astype(F32)[None]
    return jnp.stack([jnp.where((band & (ji >= BLK))[None], bias, NEG_BIG), jnp.where(band[None], bias, NEG_BIG)])


def _strips():
    return [slice(r * STRIP, (r + 1) * STRIP) for r in range(BLK // STRIP)]


def _bias_spec(transposed=False):
    tile = (2 * BLK, BLK) if transposed else (BLK, 2 * BLK)
    return pl.BlockSpec((None, N_Q_HEADS) + tile, lambda i: (jnp.minimum(i, 1), 0, 0, 0))


def _softmax_strip(s_ref, bias_ref, hq, rows, sink):
    sc = s_ref[hq, rows, :] + bias_ref[hq, rows, :]
    m = jnp.maximum(jnp.max(sc, axis=1, keepdims=True), sink)
    e = jnp.exp(sc - m)
    es = jnp.exp(sink - m)
    inv = 1.0 / (jnp.sum(e, axis=1, keepdims=True) + es)
    return e * inv, es * inv


def _kv_window(z_ref, kp_ref, vp_ref):
    kcat = jnp.concatenate([kp_ref[...], z_ref[:, ATTN_W:ATTN_W + KV_W]], axis=0).astype(F32)
    vcat = jnp.concatenate([vp_ref[...], z_ref[:, ATTN_W + KV_W:ATTN_W + 2 * KV_W]], axis=0).astype(F32)
    kswap = pltpu.roll(kcat, HEAD_DIM, 1)
    vswap = pltpu.roll(vcat, HEAD_DIM, 1)
    return kcat.astype(BF16), kswap.astype(BF16), vcat, vswap


def _gelu_and_grad(x):
    t = jnp.tanh(GELU_C0 * (x + GELU_C1 * x * x * x))
    return 0.5 * x * (1.0 + t), 0.5 * (1.0 + t) + 0.5 * x * (1.0 - t * t) * GELU_C0 * (1.0 + 3.0 * GELU_C1 * x * x)


def _layernorm_strip(gv, lng_ref, lnb_ref):
    xc = gv - jnp.mean(gv, axis=-1, keepdims=True)
    rstd = lax.rsqrt(jnp.mean(xc * xc, axis=-1, keepdims=True) + NORM_EPS)
    xhat = xc * rstd
    return xhat * lng_ref[...] + lnb_ref[...], xhat, rstd


def _tril_w(ws_ref, h):
    ti = lax.broadcasted_iota(jnp.int32, (BLK, BLK), 0)
    si = lax.broadcasted_iota(jnp.int32, (BLK, BLK), 1)
    causal = si <= ti
    return jnp.where(causal, ws_ref[h], 0.0).astype(BF16), causal


def _gm_mixed(vn, ws_ref, bs_ref, lo, hi):
    slabs = []
    for s in range(GM_W // LANES):
        vs = vn[:, s * LANES:(s + 1) * LANES]
        w0, _ = _tril_w(ws_ref, 2 * s)
        w1, _ = _tril_w(ws_ref, 2 * s + 1)
        mixed = (_mm(w0, jnp.where(lo, vs, 0.0).astype(BF16))
                 + _mm(w1, jnp.where(hi, vs, 0.0).astype(BF16))
                 + bs_ref[:, s * LANES:(s + 1) * LANES])
        slabs.append(mixed)
    return slabs


def _block_specs_z(nb):
    prev = lambda i: (jnp.maximum(i - 1, 0), ATTN_W // KV_W)
    prev_v = lambda i: (jnp.maximum(i - 1, 0), ATTN_W // KV_W + 1)
    return [_rows(BLK, D_IN), pl.BlockSpec((BLK, KV_W), prev), pl.BlockSpec((BLK, KV_W), prev_v)]


def _heads():
    return [(2 * s + half, s, half, s // 2 == half) for s in range(ATTN_W // LANES) for half in range(2)]


def _f2_attn_gm(z, sinks, bias, ln_g, ln_b, ws, bs_full):
    t = z.shape[0]
    nb = t // BLK

    def body(z_ref, kp_ref, vp_ref, sink_ref, bias_ref, lng_ref, lnb_ref, ws_ref, bs_ref, am_ref,
             s_ref, p_ref, u_ref, vn_ref):
        lane = lax.broadcasted_iota(jnp.int32, (1, LANES), 1)
        lo = lane < HEAD_DIM
        hi = lane >= HEAD_DIM
        kc, ks, vcat, vswap = _kv_window(z_ref, kp_ref, vp_ref)
        for hq, s, half, same in _heads():
            qs = z_ref[:, s * LANES:(s + 1) * LANES].astype(F32) * (HEAD_DIM ** -0.5)
            qm = jnp.where(lo if half == 0 else hi, qs, 0.0).astype(BF16)
            s_ref[hq] = _nt(qm, kc if same else ks)
        for hq in range(N_Q_HEADS):
            for rows in _strips():
                pr, _ = _softmax_strip(s_ref, bias_ref, hq, rows, sink_ref[hq])
                p_ref[hq, rows, :] = pr.astype(BF16)
        for s in range(ATTN_W // LANES):
            o = jnp.zeros((BLK, LANES), F32)
            for hq, hs, half, same in _heads():
                if hs == s:
                    vm = jnp.where(lo if half == 0 else hi, vcat if same else vswap, 0.0).astype(BF16)
                    o = o + _mm(p_ref[hq], vm)
            am_ref[:, s * LANES:(s + 1) * LANES] = o.astype(BF16)

        for rows in _strips():
            u_ref[rows, :] = _gelu(z_ref[rows, ATTN_W + 2 * KV_W:ATTN_W + 2 * KV_W + GM_W].astype(F32))
            vn_ref[rows, :], _, _ = _layernorm_strip(
                _gelu(z_ref[rows, ATTN_W + 2 * KV_W + GM_W:D_IN].astype(F32)), lng_ref, lnb_ref)
        mixed = _gm_mixed(vn_ref, ws_ref, bs_ref, lo, hi)
        for s in range(GM_W // LANES):
            am_ref[:, ATTN_W + s * LANES:ATTN_W + (s + 1) * LANES] = (
                u_ref[:, s * LANES:(s + 1) * LANES] * mixed[s]).astype(BF16)

    return pl.pallas_call(
        body, name="f2_attn_gm", grid=(nb,),
        in_specs=_block_specs_z(nb) + [
            pl.BlockSpec(memory_space=pltpu.SMEM), _bias_spec(), _whole((1, GM_W)),
            _whole((1, GM_W)), _whole((N_Q_HEADS, BLK, BLK)), _whole((BLK, GM_W))],
        out_specs=_rows(BLK, ATTN_W + GM_W),
        out_shape=_sds((t, ATTN_W + GM_W), BF16),
        scratch_shapes=[pltpu.VMEM((N_Q_HEADS, BLK, 2 * BLK), F32), pltpu.VMEM((N_Q_HEADS, BLK, 2 * BLK), BF16),
                        pltpu.VMEM((BLK, GM_W), F32), pltpu.VMEM((BLK, GM_W), F32)],
        compiler_params=_params())(z, z, z, sinks, bias, ln_g, ln_b, ws, bs_full)


def _f3_mix_out(am, h, ga, gg, w, gpost):
    t = h.shape[0]
    tm = _tile(t, 512)

    def body(am_ref, h_ref, ga_ref, gg_ref, w_ref, gp_ref, heads_ref, mix_ref, h1_ref):
        heads = jnp.concatenate([_rms(am_ref[:, :ATTN_W].astype(F32), ga_ref[...]),
                                 _rms(am_ref[:, ATTN_W:].astype(F32), gg_ref[...])], axis=1).astype(BF16)
        heads_ref[...] = heads
        mix = _mm(heads, w_ref[...])
        mix_ref[...] = mix.astype(BF16)
        h1_ref[...] = h_ref[...] + _rms(mix, gp_ref[...])

    return pl.pallas_call(
        body, name="f3_mix_out", grid=(t // tm,),
        in_specs=[_rows(tm, D_MODEL), _rows(tm, D_MODEL), _whole((1, ATTN_W)), _whole((1, GM_W)),
                  _whole((D_MODEL, D_MODEL)), _whole((1, D_MODEL))],
        out_specs=[_rows(tm, D_MODEL)] * 3,
        out_shape=[_sds((t, D_MODEL), BF16), _sds((t, D_MODEL), BF16), _sds((t, D_MODEL), F32)],
        compiler_params=_params())(am, h, ga, gg, w, gpost)


def _f4a_ffn_in(h1, gf, wg_t, wu_t):
    t = h1.shape[0]
    tm = _tile(t, 256)

    def body(h_ref, g_ref, wg_ref, wu_ref, f_ref, gt_ref, up_ref):
        f = _rms(h_ref[...], g_ref[...]).astype(BF16)
        f_ref[...] = f
        gt_ref[...] = _nt(f, wg_ref[...]).astype(BF16)
        up_ref[...] = _nt(f, wu_ref[...]).astype(BF16)

    return pl.pallas_call(
        body, name="f4a_ffn_in", grid=(t // tm,),
        in_specs=[_rows(tm, D_MODEL), _whole((1, D_MODEL)), _whole((D_FF, D_MODEL)), _whole((D_FF, D_MODEL))],
        out_specs=[_rows(tm, D_MODEL), _rows(tm, D_FF), _rows(tm, D_FF)],
        out_shape=[_sds((t, D_MODEL), BF16), _sds((t, D_FF), BF16), _sds((t, D_FF), BF16)],
        compiler_params=_params())(h1, gf, wg_t, wu_t)


def _f4b_ffn_out(gt, up, wd, h1, gfp):
    t = h1.shape[0]
    tm = _tile(t, 256)

    def body(gt_ref, up_ref, wd_ref, h_ref, g_ref, dn_ref, h2_ref):
        gt = gt_ref[...].astype(F32)
        act = (gt * _sigmoid(gt) * up_ref[...].astype(F32)).astype(BF16)
        dn = _mm(act, wd_ref[...])
        dn_ref[...] = dn.astype(BF16)
        h2_ref[...] = h_ref[...] + _rms(dn, g_ref[...])

    return pl.pallas_call(
        body, name="f4b_ffn_out", grid=(t // tm,),
        in_specs=[_rows(tm, D_FF), _rows(tm, D_FF), _whole((D_FF, D_MODEL)), _rows(tm, D_MODEL),
                  _whole((1, D_MODEL))],
        out_specs=[_rows(tm, D_MODEL)] * 2,
        out_shape=[_sds((t, D_MODEL), BF16), _sds((t, D_MODEL), F32)],
        compiler_params=_params())(gt, up, wd, h1, gfp)


def _f5_ple(h2, gpl, wpg, p, wple, target=None):
    t = h2.shape[0]
    tm = _tile(t, 512)
    with_loss = target is not None

    def body(h_ref, g_ref, wpg_ref, p_ref, wple_ref, *rest):
        r_ref, pg_ref, pe_ref, last_ref = rest[-5:-1] if with_loss else rest[-4:]
        h = h_ref[...]
        r = _rms(h, g_ref[...]).astype(BF16)
        r_ref[...] = r
        pg = _mm(r, wpg_ref[...])
        pe = _mm(p_ref[...].astype(BF16), wple_ref[...])
        pg_ref[...] = pg.astype(BF16)
        pe_ref[...] = pe.astype(BF16)
        h3 = h + pe * _sigmoid(pg)
        if with_loss:
            err = h3 - rest[0][...]
            last_ref[...] = err * (1.0 / D_MODEL)
            _accumulate(rest[-1], jnp.sum(err * err, keepdims=True))
        else:
            last_ref[...] = h3

    return pl.pallas_call(
        body, name="f5_ple_loss" if with_loss else "f5_ple", grid=(t // tm,),
        in_specs=[_rows(tm, D_MODEL), _whole((1, D_MODEL)), _whole((D_MODEL, D_MODEL)), _rows(tm, PLE_DIM),
                  _whole((PLE_DIM, D_MODEL))] + [_rows(tm, D_MODEL)] * with_loss,
        out_specs=[_rows(tm, D_MODEL)] * 4 + [_whole((1, LANES))] * with_loss,
        out_shape=[_sds((t, D_MODEL), BF16)] * 3 + [_sds((t, D_MODEL), F32)] + [_sds((1, LANES), F32)] * with_loss,
        compiler_params=_params())(h2, gpl, wpg, p, wple, *([target] if with_loss else []))


def _b5_ple(dh3, h2, pg, pe, gpl, wpg):
    t = h2.shape[0]
    tm = _tile(t, 512)

    def body(dh_ref, h_ref, pg_ref, pe_ref, g_ref, w_ref, dpe_ref, dpg_ref, dh2_ref, dg_ref):
        dh = dh_ref[...]
        s = _sigmoid(pg_ref[...].astype(F32))
        dpe_ref[...] = (dh * s).astype(BF16)
        dpg = (dh * pe_ref[...].astype(F32) * s * (1.0 - s)).astype(BF16)
        dpg_ref[...] = dpg
        dx, dg = _rms_bwd(_nt(dpg, w_ref[...]), h_ref[...], g_ref[...])
        dh2_ref[...] = dh + dx
        _accumulate(dg_ref, dg)

    return pl.pallas_call(
        body, name="b5_ple", grid=(t // tm,),
        in_specs=[_rows(tm, D_MODEL)] * 4 + [_whole((1, D_MODEL)), _whole((D_MODEL, D_MODEL))],
        out_specs=[_rows(tm, D_MODEL)] * 3 + [_whole((1, D_MODEL))],
        out_shape=[_sds((t, D_MODEL), BF16)] * 2 + [_sds((t, D_MODEL), F32), _sds((1, D_MODEL), F32)],
        compiler_params=_params())(dh3, h2, pg, pe, gpl, wpg)


def _b4a_ffn_out(dh2, dn, gfp, wd, gt, up):
    t = dh2.shape[0]
    tm = _tile(t, 256)

    def body(dh_ref, dn_ref, g_ref, wd_ref, gt_ref, up_ref, ddn_ref, act_ref, dgt_ref, dup_ref, dg_ref):
        ddn, dg = _rms_bwd(dh_ref[...], dn_ref[...].astype(F32), g_ref[...])
        _accumulate(dg_ref, dg)
        ddn = ddn.astype(BF16)
        ddn_ref[...] = ddn
        for c in range(D_FF // FF_CHUNK):
            cols = slice(c * FF_CHUNK, (c + 1) * FF_CHUNK)
            dact = _nt(ddn, wd_ref[cols, :])
            gt = gt_ref[:, cols].astype(F32)
            up = up_ref[:, cols].astype(F32)
            sg = _sigmoid(gt)
            silu = gt * sg
            act_ref[:, cols] = (silu * up).astype(BF16)
            dup_ref[:, cols] = (dact * silu).astype(BF16)
            dgt_ref[:, cols] = (dact * up * (sg * (1.0 + gt * (1.0 - sg)))).astype(BF16)

    return pl.pallas_call(
        body, name="b4a_ffn_out", grid=(t // tm,),
        in_specs=[_rows(tm, D_MODEL), _rows(tm, D_MODEL), _whole((1, D_MODEL)), _whole((D_FF, D_MODEL)),
                  _rows(tm, D_FF), _rows(tm, D_FF)],
        out_specs=[_rows(tm, D_MODEL), _rows(tm, D_FF), _rows(tm, D_FF), _rows(tm, D_FF), _whole((1, D_MODEL))],
        out_shape=[_sds((t, D_MODEL), BF16)] + [_sds((t, D_FF), BF16)] * 3 + [_sds((1, D_MODEL), F32)],
        compiler_params=_params())(dh2, dn, gfp, wd, gt, up)


def _b4b_ffn_in(dgt, dup, wg_t, wu_t, h1, gf, dh2):
    t = h1.shape[0]
    tm = _tile(t, 512)

    def body(dgt_ref, dup_ref, wg_ref, wu_ref, h_ref, g_ref, dh_ref, dh1_ref, dg_ref):
        df = _mm(dgt_ref[...], wg_ref[...]) + _mm(dup_ref[...], wu_ref[...])
        dx, dg = _rms_bwd(df, h_ref[...], g_ref[...])
        dh1_ref[...] = dh_ref[...] + dx
        _accumulate(dg_ref, dg)

    return pl.pallas_call(
        body, name="b4b_ffn_in", grid=(t // tm,),
        in_specs=[_rows(tm, D_FF), _rows(tm, D_FF), _whole((D_FF, D_MODEL)), _whole((D_FF, D_MODEL)),
                  _rows(tm, D_MODEL), _whole((1, D_MODEL)), _rows(tm, D_MODEL)],
        out_specs=[_rows(tm, D_MODEL), _whole((1, D_MODEL))],
        out_shape=[_sds((t, D_MODEL), F32), _sds((1, D_MODEL), F32)],
        compiler_params=_params())(dgt, dup, wg_t, wu_t, h1, gf, dh2)


def _b3_mix_out(dh1, mix, gpost, w, am, ga, gg):
    t = dh1.shape[0]
    tm = _tile(t, 512)

    def body(dh_ref, mix_ref, gp_ref, w_ref, am_ref, ga_ref, gg_ref, dmix_ref, dam_ref, dgp_ref, dga_ref, dgg_ref):
        dmix, dgp = _rms_bwd(dh_ref[...], mix_ref[...].astype(F32), gp_ref[...])
        _accumulate(dgp_ref, dgp)
        dmix = dmix.astype(BF16)
        dmix_ref[...] = dmix
        dheads = _nt(dmix, w_ref[...])
        dat, dga = _rms_bwd(dheads[:, :ATTN_W], am_ref[:, :ATTN_W].astype(F32), ga_ref[...])
        dgm, dgg = _rms_bwd(dheads[:, ATTN_W:], am_ref[:, ATTN_W:].astype(F32), gg_ref[...])
        dam_ref[:, :ATTN_W] = dat.astype(BF16)
        dam_ref[:, ATTN_W:] = dgm.astype(BF16)
        _accumulate(dga_ref, dga)
        _accumulate(dgg_ref, dgg)

    return pl.pallas_call(
        body, name="b3_mix_out", grid=(t // tm,),
        in_specs=[_rows(tm, D_MODEL), _rows(tm, D_MODEL), _whole((1, D_MODEL)), _whole((D_MODEL, D_MODEL)),
                  _rows(tm, D_MODEL), _whole((1, ATTN_W)), _whole((1, GM_W))],
        out_specs=[_rows(tm, D_MODEL), _rows(tm, D_MODEL), _whole((1, D_MODEL)), _whole((1, ATTN_W)),
                   _whole((1, GM_W))],
        out_shape=[_sds((t, D_MODEL), BF16), _sds((t, D_MODEL), BF16), _sds((1, D_MODEL), F32),
                   _sds((1, ATTN_W), F32), _sds((1, GM_W), F32)],
        compiler_params=_params())(dh1, mix, gpost, w, am, ga, gg)


def _b2_attn_gm(dam, z, sinks, bias, ln_g, ln_b, ws, bs_full):
    t = z.shape[0]
    nb = t // BLK

    def body(dam_ref, z_ref, kp_ref, vp_ref, sink_ref, bias_ref, lng_ref, lnb_ref, ws_ref, wst_ref, bs_ref,
             dzq_ref, dkv_ref, dzuv_ref, dsink_ref, dlng_ref, dlnb_ref, dws_ref, dbs_ref,
             sc_ref, dp_ref, p_ref, ds_ref, u_ref, du_ref, dv_ref, vn_ref, xhat_ref, rstd_ref, dvn_ref):
        n = pl.program_id(0)
        lane = lax.broadcasted_iota(jnp.int32, (1, LANES), 1)
        lo = lane < HEAD_DIM
        hi = lane >= HEAD_DIM
        sub = lax.broadcasted_iota(jnp.int32, (LANES, 1), 0)
        kc, ks, vcat, vswap = _kv_window(z_ref, kp_ref, vp_ref)
        vc = vcat.astype(BF16)
        vs_ = vswap.astype(BF16)
        kc_t = kc.T
        ks_t = ks.T

        def operands(s, half):
            mask = lo if half == 0 else hi
            qs = z_ref[:, s * LANES:(s + 1) * LANES].astype(F32) * (HEAD_DIM ** -0.5)
            qm = jnp.where(mask, qs, 0.0).astype(BF16)
            dom = jnp.where(mask, dam_ref[:, s * LANES:(s + 1) * LANES], 0.0).astype(BF16)
            return qm, dom

        for hq, s, half, same in _heads():
            qm, dom = operands(s, half)
            sc_ref[hq] = _nt(kc if same else ks, qm)
            dp_ref[hq] = _nt(vc if same else vs_, dom)
        dsink = jnp.zeros((1, LANES), F32)
        for hq in range(N_Q_HEADS):
            sink = sink_ref[hq]
            sc = sc_ref[hq] + bias_ref[hq]
            m = jnp.maximum(jnp.max(sc, axis=0, keepdims=True), sink)
            e = jnp.exp(sc - m)
            es = jnp.exp(sink - m)
            inv = 1.0 / (jnp.sum(e, axis=0, keepdims=True) + es)
            pr = e * inv
            dpr = dp_ref[hq]
            row = jnp.sum(dpr * pr, axis=0, keepdims=True)
            p_ref[hq] = pr.astype(BF16)
            ds_ref[hq] = (pr * (dpr - row)).astype(BF16)
            dsink = dsink + jnp.where(lane == hq, -jnp.sum(es * inv * row, keepdims=True), 0.0)
        dk_acc = [jnp.zeros((2 * BLK, LANES), F32), jnp.zeros((2 * BLK, LANES), F32)]
        dv_acc = [jnp.zeros((2 * BLK, LANES), F32), jnp.zeros((2 * BLK, LANES), F32)]
        for s in range(ATTN_W // LANES):
            dq_t = jnp.zeros((LANES, BLK), F32)
            for hq, hs, half, same in _heads():
                if hs != s:
                    continue
                qm, dom = operands(s, half)
                ds = ds_ref[hq]
                dk_acc[same] = dk_acc[same] + _mm(ds, qm)
                dv_acc[same] = dv_acc[same] + _mm(p_ref[hq], dom)
                in_half = (sub < HEAD_DIM) if half == 0 else (sub >= HEAD_DIM)
                dq_t = dq_t + jnp.where(in_half, _mm(kc_t if same else ks_t, ds), 0.0)
            dzq_ref[:, s * LANES:(s + 1) * LANES] = (dq_t.T * (HEAD_DIM ** -0.5)).astype(BF16)
        dk_acc = dk_acc[True] + pltpu.roll(dk_acc[False], HEAD_DIM, 1)
        dv_acc = dv_acc[True] + pltpu.roll(dv_acc[False], HEAD_DIM, 1)
        cur = pl.multiple_of(n * BLK, BLK)
        dkv_ref[pl.ds(cur, BLK), 0:KV_W] = dk_acc[BLK:, :]
        dkv_ref[pl.ds(cur, BLK), KV_W:2 * KV_W] = dv_acc[BLK:, :]

        @pl.when(n > 0)
        def _():
            prv = pl.multiple_of((n - 1) * BLK, BLK)
            dkv_ref[pl.ds(prv, BLK), 0:KV_W] += dk_acc[:BLK, :]
            dkv_ref[pl.ds(prv, BLK), KV_W:2 * KV_W] += dv_acc[:BLK, :]

        _accumulate(dsink_ref, dsink)

        for rows in _strips():
            u_ref[rows, :], du_ref[rows, :] = _gelu_and_grad(
                z_ref[rows, ATTN_W + 2 * KV_W:ATTN_W + 2 * KV_W + GM_W].astype(F32))
            gv, dv_ref[rows, :] = _gelu_and_grad(z_ref[rows, ATTN_W + 2 * KV_W + GM_W:D_IN].astype(F32))
            vn_ref[rows, :], xhat_ref[rows, :], rstd = _layernorm_strip(gv, lng_ref, lnb_ref)
            rstd_ref[rows, :] = jnp.broadcast_to(rstd, (STRIP, LANES))
        mixed = _gm_mixed(vn_ref, ws_ref, bs_ref, lo, hi)

        @pl.when(n == 0)
        def _():
            dws_ref[...] = jnp.zeros_like(dws_ref)

        dbs = jnp.zeros((BLK, LANES), F32)
        for s in range(GM_W // LANES):
            slab = slice(s * LANES, (s + 1) * LANES)
            dgm = dam_ref[:, ATTN_W + s * LANES:ATTN_W + (s + 1) * LANES]
            dzuv_ref[:, slab] = (dgm * mixed[s] * du_ref[:, slab]).astype(BF16)
            dmx = dgm * u_ref[:, slab]
            vsb = vn_ref[:, slab].astype(BF16)
            dvn = jnp.zeros((BLK, LANES), F32)
            for half in range(2):
                h = 2 * s + half
                mask = lo if half == 0 else hi
                dmm = jnp.where(mask, dmx, 0.0)
                dbs = dbs + jnp.where(lane == h, jnp.sum(dmm, axis=1, keepdims=True), 0.0)
                dmm = dmm.astype(BF16)
                _, causal = _tril_w(ws_ref, h)
                ti = lax.broadcasted_iota(jnp.int32, (BLK, BLK), 0)
                si = lax.broadcasted_iota(jnp.int32, (BLK, BLK), 1)
                wt_t = jnp.where(ti <= si, wst_ref[h], 0.0).astype(BF16)
                dvn = dvn + jnp.where(mask, _mm(wt_t, dmm), 0.0)
                dws_ref[h] += jnp.where(causal, _nt(dmm, vsb), 0.0)
            dvn_ref[:, slab] = dvn
        _accumulate(dbs_ref, dbs)
        dlnb = jnp.zeros((STRIP, GM_W), F32)
        dlng = jnp.zeros((STRIP, GM_W), F32)
        for rows in _strips():
            dvn = dvn_ref[rows, :]
            xhat = xhat_ref[rows, :]
            dlnb = dlnb + dvn
            dlng = dlng + dvn * xhat
            dxh = dvn * lng_ref[...]
            dgv = rstd_ref[rows, 0:1] * (dxh - jnp.mean(dxh, axis=-1, keepdims=True)
                                         - xhat * jnp.mean(dxh * xhat, axis=-1, keepdims=True))
            dzuv_ref[rows, GM_W:] = (dgv * dv_ref[rows, :]).astype(BF16)
        _accumulate(dlnb_ref, jnp.sum(dlnb, axis=0, keepdims=True))
        _accumulate(dlng_ref, jnp.sum(dlng, axis=0, keepdims=True))

    return pl.pallas_call(
        body, name="b2_attn_gm", grid=(nb,),
        in_specs=[_rows(BLK, ATTN_W + GM_W)] + _block_specs_z(nb) + [
            pl.BlockSpec(memory_space=pltpu.SMEM), _bias_spec(transposed=True), _whole((1, GM_W)),
            _whole((1, GM_W)), _whole((N_Q_HEADS, BLK, BLK)), _whole((N_Q_HEADS, BLK, BLK)), _whole((BLK, GM_W))],
        out_specs=[_rows(BLK, ATTN_W), _whole((t, 2 * KV_W)), _rows(BLK, 2 * GM_W), _whole((1, LANES)),
                   _whole((1, GM_W)), _whole((1, GM_W)), _whole((N_Q_HEADS, BLK, BLK)), _whole((BLK, LANES))],
        out_shape=[_sds((t, ATTN_W), BF16), _sds((t, 2 * KV_W), F32), _sds((t, 2 * GM_W), BF16),
                   _sds((1, LANES), F32), _sds((1, GM_W), F32), _sds((1, GM_W), F32),
                   _sds((N_Q_HEADS, BLK, BLK), F32), _sds((BLK, LANES), F32)],
        scratch_shapes=[pltpu.VMEM((N_Q_HEADS, 2 * BLK, BLK), F32)] * 2
        + [pltpu.VMEM((N_Q_HEADS, 2 * BLK, BLK), BF16)] * 2 + [pltpu.VMEM((BLK, GM_W), F32)] * 5
        + [pltpu.VMEM((BLK, LANES), F32), pltpu.VMEM((BLK, GM_W), F32)],
        compiler_params=_params())(dam, z, z, z, sinks, jnp.swapaxes(bias, 2, 3), ln_g, ln_b, ws,
                                   jnp.swapaxes(ws, 1, 2), bs_full)


def _b1_norm_in(dzq, dkv, dzuv, w_t, h, g, dh1):
    t = h.shape[0]
    tm = _tile(t, 512)

    def body(dzq_ref, dkv_ref, dzuv_ref, w_ref, h_ref, g_ref, dh_ref, dh0_ref, dg_ref):
        dz = jnp.concatenate([dzq_ref[...], dkv_ref[...].astype(BF16), dzuv_ref[...]], axis=1)
        dx, dg = _rms_bwd(_mm(dz, w_ref[...]), h_ref[...], g_ref[...])
        dh0_ref[...] = dh_ref[...] + dx
        _accumulate(dg_ref, dg)

    return pl.pallas_call(
        body, name="b1_norm_in", grid=(t // tm,),
        in_specs=[_rows(tm, ATTN_W), _rows(tm, 2 * KV_W), _rows(tm, 2 * GM_W), _whole((D_IN, D_MODEL)),
                  _rows(tm, D_MODEL), _whole((1, D_MODEL)), _rows(tm, D_MODEL)],
        out_specs=[_rows(tm, D_MODEL), _whole((1, D_MODEL))],
        out_shape=[_sds((t, D_MODEL), F32), _sds((1, D_MODEL), F32)],
        compiler_params=_params())(dzq, dkv, dzuv, w_t, h, g, dh1)


def _weight_grad(x, dy, name):
    xs = tuple(x) if isinstance(x, (tuple, list)) else (x,)
    t = xs[0].shape[0]
    k = sum(a.shape[1] for a in xs)
    n = dy.shape[1]
    tm = _tile(t, 1024)
    steps = t // tm

    def body(*refs):
        x_refs, dy_ref, dw_ref, acc_ref = refs[:len(xs)], refs[-3], refs[-2], refs[-1]
        i = pl.program_id(0)

        def product():
            cols = [r[...].astype(BF16) for r in x_refs]
            return _tn(cols[0] if len(cols) == 1 else jnp.concatenate(cols, axis=1), dy_ref[...])

        @pl.when(i == 0)
        def _():
            acc_ref[...] = product()

        @pl.when(i > 0)
        def _():
            acc_ref[...] += product()

        @pl.when(i == steps - 1)
        def _():
            dw_ref[...] = acc_ref[...].astype(BF16)

    return pl.pallas_call(
        body, name=name, grid=(steps,),
        in_specs=[_rows(tm, a.shape[1]) for a in xs] + [_rows(tm, n)],
        out_specs=_whole((k, n)),
        out_shape=_sds((k, n), BF16),
        scratch_shapes=[pltpu.VMEM((k, n), F32)],
        compiler_params=_params())(*xs, dy)


_ANY = pl.BlockSpec(memory_space=pl.ANY)
_HBM = pl.BlockSpec(memory_space=pltpu.HBM)
_SEM = pl.BlockSpec(memory_space=pltpu.SEMAPHORE)
_EFFECT = pltpu.SideEffectType.DATAFLOW_SIDE_EFFECTING


def _mesh_pos():
    return lax.axis_index("x"), lax.axis_index("y"), lax.axis_index("c")


def _other_chips(x, y):
    return [(1 - x, y), (x, 1 - y), (1 - x, 1 - y)]


def _peers(mode, x, y, c):
    if mode == "devices":
        peers = []
        for j in range(1, N_DEV):
            px = 1 - x if (j >> 2) & 1 else x
            py = 1 - y if (j >> 1) & 1 else y
            pc = 1 - c if j & 1 else c
            peers.append(((px, py, pc), 4 * px + 2 * py + pc))
        return peers, 4 * x + 2 * y + c
    if mode == "sibling":
        return [((x, y, 1 - c), None)], None
    return [((px, py, c), 2 * px + py) for px, py in _other_chips(x, y)], 2 * x + y


def _descriptors(mode, srcs, lands, send_sems, recv_sems, with_incoming=True):
    x, y, c = _mesh_pos()
    peers, me = _peers(mode, x, y, c)
    scatter = mode == "scatter"
    outgoing, incoming = [], []
    for i, (src, land) in enumerate(zip(srcs, lands)):
        for j, (dev, slot) in enumerate(peers):
            sem = i * len(peers) + j
            common = dict(send_sem=send_sems.at[sem], recv_sem=recv_sems.at[sem], device_id=dev,
                          device_id_type=pl.DeviceIdType.MESH)
            whole = mode == "sibling"
            outgoing.append(pltpu.make_async_remote_copy(
                src_ref=src.at[slot] if scatter else src, dst_ref=land if whole else land.at[me], **common))
            if with_incoming:
                incoming.append(pltpu.make_async_remote_copy(
                    src_ref=src.at[me] if scatter else src, dst_ref=land if whole else land.at[slot], **common))
    return outgoing, incoming


def _n_sems(mode, k):
    return k * {"devices": N_DEV - 1, "sibling": 1}.get(mode, N_CHIPS - 1)


def _exchange_start(srcs, lands, mode, name, after=None):
    k = len(srcs)
    arrs = [*srcs, *lands]

    def body(*refs):
        skip = 1 if after is not None else 0
        send_sems, recv_sems = refs[2 * k + skip], refs[2 * k + skip + 1]
        outgoing, _ = _descriptors(mode, refs[:k], refs[k:2 * k], send_sems, recv_sems, with_incoming=False)
        for cp in outgoing:
            cp.start()
        refs[-1][...] = jnp.zeros_like(refs[-1])

    operands = [pltpu.with_memory_space_constraint(a, pltpu.HBM) for a in arrs]
    in_specs = [_HBM] * (2 * k)
    if after is not None:
        operands.append(after)
        in_specs.append(_ANY)
    sems = pltpu.SemaphoreType.DMA((_n_sems(mode, k),))
    res = pl.pallas_call(
        body, name=name, in_specs=in_specs,
        out_shape=(sems, sems, *[pltpu.HBM(a.shape, a.dtype) for a in arrs], _sds((8, LANES), F32)),
        out_specs=(_SEM, _SEM, *[_HBM] * (2 * k), pl.BlockSpec(memory_space=pltpu.VMEM)),
        input_output_aliases={i: 2 + i for i in range(2 * k)},
        compiler_params=pltpu.CompilerParams(has_side_effects=_EFFECT))(*operands)
    return dict(mode=mode, send=res[0], recv=res[1], srcs=res[2:2 + k], lands=res[2 + k:2 + 2 * k], token=res[-1])


def _exchange_wait(flight, name, after):
    mode, k = flight["mode"], len(flight["srcs"])
    arrs = [*flight["srcs"], *flight["lands"]]

    def body(*refs):
        outgoing, incoming = _descriptors(mode, refs[:k], refs[k:2 * k], refs[2 * k], refs[2 * k + 1])
        for cp in outgoing:
            cp.wait_send()
        for cp in incoming:
            cp.wait_recv()

    res = pl.pallas_call(
        body, name=name, in_specs=[_HBM] * (2 * k) + [_SEM, _SEM, _ANY],
        out_shape=tuple(pltpu.HBM(a.shape, a.dtype) for a in arrs), out_specs=tuple([_HBM] * (2 * k)),
        input_output_aliases={i: i for i in range(2 * k)},
        compiler_params=pltpu.CompilerParams(has_side_effects=_EFFECT))(*arrs, flight["send"], flight["recv"], after)
    return res[:k], res[k:]


def _adamw(w, g, m, v):
    m = ADAM_B1 * m + (1.0 - ADAM_B1) * g
    v = ADAM_B2 * v + (1.0 - ADAM_B2) * (g * g)
    m_hat = m / (1.0 - ADAM_B1 ** ADAM_STEP)
    v_hat = v / (1.0 - ADAM_B2 ** ADAM_STEP)
    delta = -ADAM_LR * (m_hat / (jnp.sqrt(v_hat) + ADAM_EPS) + ADAM_WD * w)
    return delta, m, v


def _row_tile(rows, cols, n_arrays):
    budget = VMEM_LIMIT_BYTES // 2
    padded = -(-cols // LANES) * LANES
    cap = min(rows, max(16, budget // (2 * n_arrays * padded * 4)))
    return max(tr for tr in range(16, cap + 1, 16) if rows % tr == 0)


def _sum_chips(landing, name):
    _, r, c = landing.shape
    tr = _row_tile(r, c, 4)

    def body(l_ref, s_ref):
        acc = l_ref[0].astype(F32)
        for s in range(1, N_CHIPS):
            acc = acc + l_ref[s].astype(F32)
        s_ref[...] = acc

    return pl.pallas_call(
        body, name=name, grid=(r // tr,),
        in_specs=[pl.BlockSpec((N_CHIPS, tr, c), lambda i: (0, i, 0))],
        out_specs=_rows(tr, c), out_shape=_sds((r, c), F32),
        compiler_params=_params())(landing)


def _adamw_big(mine, sibling, w, m, v, prev, layer, name):
    _, r, c = w.shape
    tr = _row_tile(r, c, 9)
    stacked = pl.BlockSpec((None, tr, c), lambda i: (layer, i, 0))

    def body(a_ref, b_ref, w_ref, m_ref, v_ref, *rest):
        g_out, d_out, m_out, v_out = rest[-4:]
        g = a_ref[...] + b_ref[...]
        g_out[...] = g
        d_out[...], m_out[...], v_out[...] = _adamw(w_ref[...], g, m_ref[...], v_ref[...])

    prev = list(prev) if prev is not None else []
    return pl.pallas_call(
        body, name=name, grid=(r // tr,),
        in_specs=[_rows(tr, c)] * 2 + [stacked] * 3 + [_ANY] * len(prev),
        out_specs=[stacked] * 4, out_shape=[_sds(w.shape, F32)] * 4,
        input_output_aliases={5 + j: j for j in range(len(prev))},
        compiler_params=_params())(mine, sibling, w, m, v, *prev)


def _adamw_small(gathered, w, m, v):
    r = w.shape[0]

    def body(a_ref, w_ref, m_ref, v_ref, g_out, d_out, m_out, v_out):
        g = a_ref[0]
        for d in range(1, N_DEV):
            g = g + a_ref[d]
        g_out[...] = g
        d_out[...], m_out[...], v_out[...] = _adamw(w_ref[...], g, m_ref[...], v_ref[...])

    return pl.pallas_call(
        body, name="adamw_small", grid=(1,),
        in_specs=[_whole((N_DEV, r, LANES))] + [_whole((r, LANES))] * 3,
        out_specs=[_whole((r, LANES))] * 4, out_shape=[_sds((r, LANES), F32)] * 4,
        compiler_params=_params())(gathered, w, m, v)


def _shard_view(name, stacked):
    return jnp.swapaxes(stacked, 1, 2) if name in TRANSPOSED else stacked


def _row_split(name):
    return name in TRANSPOSED or BIG[name] == 1


def _assemble(name, landed):
    _, r, c = landed.shape
    if _row_split(name):
        return landed.reshape(N_CHIPS * r, c)
    return landed.transpose(1, 0, 2).reshape(r, N_CHIPS * c)


def _split(name, whole):
    r, c = whole.shape
    if _row_split(name):
        return whole.reshape(N_CHIPS, r // N_CHIPS, c)
    return whole.reshape(r, N_CHIPS, c // N_CHIPS).transpose(1, 0, 2)


def _with_own_slot(landed, own, slot):
    return lax.dynamic_update_index_in_dim(landed, own, slot, 0)


def _pack_small(params, names):
    pieces = []
    for name in names:
        flat = params[name].reshape(-1)
        pieces.append(jnp.pad(flat, (0, -flat.shape[0] % SMALL_PAD)))
    return jnp.concatenate(pieces).reshape(-1, LANES)


def _unpack_small(packed, like, names):
    flat = packed.reshape(-1)
    out, off = {}, 0
    for name in names:
        size = like[name].size
        out[name] = flat[off:off + size].reshape(like[name].shape)
        off += size + (-size % SMALL_PAD)
    return out


def kernel(x, p, ln_mix_pre, w_in, attn_sinks, gm_ln_g, gm_ln_b, gm_ws, gm_bs, g_attn_out, g_gm_out, w_out, ln_mix_post, ln_ffn_pre, w_ffn_gate, w_ffn_up, w_ffn_down, ln_ffn_post, w_ple, ln_ple_gate, w_ple_gate, loss_target, m_ln_mix_pre, m_w_in, m_attn_sinks, m_gm_ln_g, m_gm_ln_b, m_gm_ws, m_gm_bs, m_g_attn_out, m_g_gm_out, m_w_out, m_ln_mix_post, m_ln_ffn_pre, m_w_ffn_gate, m_w_ffn_up, m_w_ffn_down, m_ln_ffn_post, m_w_ple, m_ln_ple_gate, m_w_ple_gate, v_ln_mix_pre, v_w_in, v_attn_sinks, v_gm_ln_g, v_gm_ln_b, v_gm_ws, v_gm_bs, v_g_attn_out, v_g_gm_out, v_w_out, v_ln_mix_post, v_ln_ffn_pre, v_w_ffn_gate, v_w_ffn_up, v_w_ffn_down, v_ln_ffn_post, v_w_ple, v_ln_ple_gate, v_w_ple_gate):
    given = dict(locals())
    wts = {n: given[n] for n in WEIGHTS}
    mom = {n: given["m_" + n] for n in WEIGHTS}
    var = {n: given["v_" + n] for n in WEIGHTS}
    depth = w_in.shape[0]
    h = x[0]
    target = loss_target[0]
    chip = 2 * lax.axis_index("x") + lax.axis_index("y")
    device = 2 * chip + lax.axis_index("c")
    row = lambda a, i: a[i][None, :]
    bs_full = [jnp.repeat(gm_bs[i].T, HEAD_DIM, axis=1) for i in range(depth)]
    bias = _alibi_bias()
    kinds = ("grad", "delta", "m", "v")
    wview = {n: _shard_view(n, wts[n]) for n in BIG_NAMES}
    mview = {n: _shard_view(n, mom[n]) for n in BIG_NAMES}
    vview = {n: _shard_view(n, var[n]) for n in BIG_NAMES}

    zero = lambda flight: flight["token"][0:1, 0:1]

    def start_gather(i, names, after, tag):
        shards = [wview[n][i].astype(BF16) for n in names]
        lands = [lax.empty((N_CHIPS,) + s.shape, BF16) for s in shards]
        return _exchange_start(shards, lands, "gather", f"gather_weights_start_{i}{tag}", after)

    def finish_gather(flight, names, i, after, tag):
        shards, lands = _exchange_wait(flight, f"gather_weights_wait_{i}{tag}", after)
        return {n: _assemble(n, _with_own_slot(l, s, chip)) for n, s, l in zip(names, shards, lands)}

    def start_groups(i, after):
        flights = []
        for k, names in enumerate(GATHER_GROUPS):
            flights.append(start_gather(i, names, after, "abc"[k]))
            after = flights[-1]["token"]
        return flights

    grouped = {0: start_groups(0, None)}
    full = [finish_gather(grouped[0][0], GATHER_GROUPS[0], 0, grouped[0][-1]["token"], "a")] + [None] * (depth - 1)
    saved = []
    for i in range(depth):
        w = full[i]
        g_in = row(ln_mix_pre, i)
        if i + 1 in GROUPED_LAYERS:
            grouped[i + 1] = start_groups(i + 1, w['w_in'])
            g_in = g_in + zero(grouped[i + 1][-1])
        elif i + 1 < depth:
            flight = start_gather(i + 1, BIG_NAMES, w['w_in'], "")
            g_in = g_in + zero(flight)
        z, a = _f1_norm_in(h, g_in, w['w_in'])
        am = _f2_attn_gm(z, attn_sinks[i], bias, row(gm_ln_g, i), row(gm_ln_b, i), gm_ws[i], bs_full[i])
        if i in grouped:
            w.update(finish_gather(grouped[i][1], GATHER_GROUPS[1], i, am, "b"))
        heads, mix, h1 = _f3_mix_out(am, h, row(g_attn_out, i), row(g_gm_out, i), w['w_out'], row(ln_mix_post, i))
        f, gt, up = _f4a_ffn_in(h1, row(ln_ffn_pre, i), w['w_ffn_gate'], w['w_ffn_up'])
        if i in grouped:
            w.update(finish_gather(grouped[i][2], GATHER_GROUPS[2], i, gt, "c"))
        dn, h2 = _f4b_ffn_out(gt, up, w['w_ffn_down'], h1, row(ln_ffn_post, i))
        ple = _f5_ple(h2, row(ln_ple_gate, i), w['w_ple_gate'], p[i, 0], w['w_ple'],
                      target if i + 1 == depth else None)
        r, pg, pe = ple[:3]
        saved.append(dict(h=h, z=z, a=a, am=am, heads=heads, mix=mix, h1=h1, f=f, gt=gt, up=up, dn=dn, h2=h2,
                          r=r, pg=pg, pe=pe))
        if i + 1 < depth:
            h = ple[3]
            if i + 1 in grouped:
                full[i + 1] = finish_gather(grouped[i + 1][0], GATHER_GROUPS[0], i + 1, h, "a")
            else:
                full[i + 1] = finish_gather(flight, BIG_NAMES, i + 1, h, "")

    dh, sq = ple[3], ple[4]
    loss = lax.psum(0.5 / D_MODEL * sq[0, 0], ("x", "y", "c"))

    chain = {n: None for n in BIG_NAMES}
    owed = []
    small_out = [{k: {} for k in kinds} for _ in range(depth)]

    def start_scatter(i, names, dws, after, tag):
        parts = [_split(n, dws[n]) for n in names]
        lands = [lax.empty(q.shape, BF16) for q in parts]
        return _exchange_start(parts, lands, "scatter", f"scatter_grads_start_{i}{tag}", after)

    def finish_scatter(i, names, flight, after, tag):
        parts, lands = _exchange_wait(flight, f"scatter_grads_wait_{i}{tag}", after)
        lands = [_with_own_slot(l, lax.dynamic_index_in_dim(q, chip, 0, keepdims=False), chip)
                 for q, l in zip(parts, lands)]
        partial = [_sum_chips(l, "sum_chips_" + n) for n, l in zip(names, lands)]
        flight = _exchange_start(partial, [lax.empty(q.shape, F32) for q in partial], "sibling",
                                 f"sibling_grads_start_{i}{tag}", after)

        def update(later):
            mine, theirs = _exchange_wait(flight, f"sibling_grads_wait_{i}{tag}", later)
            for n, a, b in zip(names, mine, theirs):
                chain[n] = _adamw_big(a, b, wview[n], mview[n], vview[n], chain[n], i, "adamw_" + n)

        owed.append(update)
        return flight["token"]

    def settle(later):
        while owed:
            owed.pop(0)(later)

    def start_small(i, names, small, after, tag):
        packed = _pack_small(small, names)
        land = lax.empty((N_DEV,) + packed.shape, F32)
        return _exchange_start([packed], [land], "devices", f"gather_small_grads_start_{i}{tag}", after)

    def finish_small(i, names, flight, after, tag):
        (packed,), (gathered,) = _exchange_wait(flight, f"gather_small_grads_wait_{i}{tag}", after)
        gathered = _with_own_slot(gathered, packed, device)
        layer = lambda d: _pack_small({n: d[n][i] for n in names}, names)
        res = _adamw_small(gathered, layer(wts), layer(mom), layer(var))
        for k, a in zip(kinds, res):
            small_out[i][k].update(_unpack_small(a, {n: wts[n][i] for n in names}, names))
        return gathered

    pending, done, behind = [], None, None
    for i in reversed(range(depth)):
        s, w = saved[i], full[i]
        last = i == 0
        dws, small = {}, {}
        gain = row(ln_ple_gate, i)
        if behind is not None:
            gain = gain + behind
        dpe, dpg, dh2, dg = _b5_ple(dh, s['h2'], s['pg'], s['pe'], gain, w['w_ple_gate'])
        small['ln_ple_gate'] = dg[0]
        dws['w_ple'] = _weight_grad(p[i, 0], dpe, "dw_ple")
        dws['w_ple_gate'] = _weight_grad(s['r'], dpg, "dw_ple_gate")

        ddn, act, dgt, dup, dg = _b4a_ffn_out(dh2, s['dn'], row(ln_ffn_post, i), w['w_ffn_down'], s['gt'], s['up'])
        small['ln_ffn_post'] = dg[0]
        dws['w_ffn_down'] = _weight_grad(act, ddn, "dw_ffn_down")
        dws['w_ffn_gate'] = _weight_grad(dgt, s['f'], "dw_ffn_gate")
        dws['w_ffn_up'] = _weight_grad(dup, s['f'], "dw_ffn_up")
        gain = row(ln_ffn_pre, i)
        if last:
            flight_a = start_scatter(i, SCATTER_GROUPS[0], dws, None, "a")
            gain = gain + zero(flight_a)
        dh1, dg = _b4b_ffn_in(dgt, dup, w['w_ffn_gate'], w['w_ffn_up'], s['h1'], gain, dh2)
        small['ln_ffn_pre'] = dg[0]

        dmix, dam, dgp, dga, dgg = _b3_mix_out(dh1, s['mix'], row(ln_mix_post, i), w['w_out'], s['am'],
                                               row(g_attn_out, i), row(g_gm_out, i))
        small['ln_mix_post'] = dgp[0]
        small['g_attn_out'] = dga[0]
        small['g_gm_out'] = dgg[0]
        dws['w_out'] = _weight_grad(s['heads'], dmix, "dw_out")
        gain = row(gm_ln_g, i)
        if last:
            flight_b = start_scatter(i, SCATTER_GROUPS[1], dws, flight_a["token"], "b")
            gain = gain + zero(flight_b)

        dzq, dkv, dzuv, dsink, dlng, dlnb, dgws, dbs = _b2_attn_gm(
            dam, s['z'], attn_sinks[i], bias, gain, row(gm_ln_b, i), gm_ws[i], bs_full[i])
        small['attn_sinks'] = dsink[0, :N_Q_HEADS]
        small['gm_ln_g'] = dlng[0]
        small['gm_ln_b'] = dlnb[0]
        small['gm_ws'] = dgws
        small['gm_bs'] = dbs[:, :N_Q_HEADS].T
        gain = row(ln_mix_pre, i)
        if last:
            flight_s = start_small(i, SMALL_EARLY, small, flight_b["token"], "a")
            gain = gain + zero(flight_s)

        dh, dg = _b1_norm_in(dzq, dkv, dzuv, w['w_in'], s['h'], gain, dh1)
        small['ln_mix_pre'] = dg[0]
        settle(dh)
        for finish in pending:
            done = finish(dh)
        pending = []
        if last:
            done = finish_scatter(i, SCATTER_GROUPS[0], flight_a, dh, "a")
            done = finish_scatter(i, SCATTER_GROUPS[1], flight_b, done, "b")
        dws['w_in'] = _weight_grad((dzq, dkv, dzuv), s['a'], "dw_in")
        if last:
            flight_c = start_scatter(i, SCATTER_GROUPS[2], dws, done, "c")
            flight_t = start_small(i, SMALL_LATE, small, flight_c["token"], "b")
            finish_small(i, SMALL_EARLY, flight_s, flight_t["token"], "a")
            done = finish_scatter(i, SCATTER_GROUPS[2], flight_c, flight_t["token"], "c")
            finish_small(i, SMALL_LATE, flight_t, flight_t["token"], "b")
            settle(done)
        else:
            flight_c = start_scatter(i, BIG_NAMES, dws, done, "")
            flight_s = start_small(i, SMALL_NAMES, small, flight_c["token"], "")
            behind = zero(flight_s)
            pending = [functools.partial(finish_small, i, SMALL_NAMES, flight_s, tag=""),
                       functools.partial(finish_scatter, i, BIG_NAMES, flight_c, tag="")]
    grad_x = dh[None]

    out = {k: {n: _shard_view(n, chain[n][j]) for n in BIG_NAMES} for j, k in enumerate(kinds)}
    for k in kinds:
        out[k].update({n: jnp.stack([small_out[i][k][n] for i in range(depth)]) for n in SMALL_NAMES})

    return (loss, grad_x, *[out["grad"][n] for n in WEIGHTS], *[out["delta"][n] for n in WEIGHTS],
            *[out["m"][n] for n in WEIGHTS], *[out["v"][n] for n in WEIGHTS])
```
